```python
import math
import jax, jax.numpy as jnp
from jax import lax
import numpy as np

D_MODEL = 2048
BATCH = 1
SEQ = 8192
DEPTH = 2

HEAD_DIM = 64
MIX_WIDTH = D_MODEL
SWA_WIDTH = 3 * MIX_WIDTH // 8
SWA_HEADS = SWA_WIDTH // HEAD_DIM
SWA_KV_HEADS = SWA_HEADS // 3
SWA_GROUP = SWA_HEADS // SWA_KV_HEADS
WINDOW = 128
SWA_BLOCK = 128
REL_BUCKETS = 32
REL_MAX_DIST = 128
GLA_WIDTH = 3 * MIX_WIDTH // 8
GLA_HEADS = 4
GLA_DV = GLA_WIDTH // GLA_HEADS
GLA_DK = GLA_DV // 2
GLA_LOWRANK = 16
GLA_CHUNK = 64
GATE_NORMALIZER = 16.0
GATE_LOG_MIN = -1.0
SB_WIDTH = MIX_WIDTH - SWA_WIDTH - GLA_WIDTH
SB_HEADS = SB_WIDTH // HEAD_DIM
SB_BLOCK = 128
D_FF = 4 * D_MODEL
RMS_EPS = 1e-6
NEG_INF = -1e30

PROJ_SIZES = (
    SWA_WIDTH, SWA_KV_HEADS * HEAD_DIM, SWA_KV_HEADS * HEAD_DIM,
    GLA_HEADS * GLA_DK, GLA_HEADS * GLA_DK, GLA_WIDTH, GLA_WIDTH, GLA_LOWRANK,
    SB_WIDTH, SB_WIDTH, SB_WIDTH,
)
PROJ_WIDTH = sum(PROJ_SIZES)

kernel_name = 'hybrid_swa_gla_stickbreak_block'


def rmsnorm(x, gain, eps=RMS_EPS):
    xf = x.astype(jnp.float32)
    y = xf * lax.rsqrt(jnp.mean(xf * xf, axis=-1, keepdims=True) + eps)
    return (y * gain.astype(jnp.float32)).astype(x.dtype)


def t5_causal_bucket(dist):
    max_exact = REL_BUCKETS // 2
    is_small = dist < max_exact
    ratio = jnp.log(jnp.maximum(dist, 1).astype(jnp.float32) / max_exact) / math.log(REL_MAX_DIST / max_exact)
    large = max_exact + (ratio * (REL_BUCKETS - max_exact)).astype(jnp.int32)
    large = jnp.minimum(large, REL_BUCKETS - 1)
    return jnp.where(is_small, dist, large)


def swa_sink_attention(q, k, v, sinks, rel_bias):
    B, S = q.shape[:2]
    nb = S // SWA_BLOCK
    qb = q.reshape(B, nb, SWA_BLOCK, SWA_KV_HEADS, SWA_GROUP, HEAD_DIM)
    kb = k.reshape(B, nb, SWA_BLOCK, SWA_KV_HEADS, HEAD_DIM)
    vb = v.reshape(B, nb, SWA_BLOCK, SWA_KV_HEADS, HEAD_DIM)
    pad = ((0, 0), (1, 0), (0, 0), (0, 0), (0, 0))
    kw = jnp.concatenate([jnp.pad(kb, pad)[:, :-1], kb], axis=2)
    vw = jnp.concatenate([jnp.pad(vb, pad)[:, :-1], vb], axis=2)
    logits = jnp.einsum('bnqhgd,bnkhd->bnhgqk', qb, kw).astype(jnp.float32) * (HEAD_DIM ** -0.5)
    qpos = jnp.arange(SWA_BLOCK) + SWA_BLOCK
    kpos = jnp.arange(2 * SWA_BLOCK)
    dist = qpos[:, None] - kpos[None, :]
    in_window = (dist >= 0) & (dist < WINDOW)
    key_abs = jnp.arange(nb)[:, None] * SWA_BLOCK - SWA_BLOCK + kpos[None, :]
    mask = in_window[None, :, :] & (key_abs >= 0)[:, None, :]
    bias = rel_bias.astype(jnp.float32)[t5_causal_bucket(jnp.maximum(dist, 0))]
    bias = bias.transpose(2, 0, 1).reshape(SWA_KV_HEADS, SWA_GROUP, SWA_BLOCK, 2 * SWA_BLOCK)
    logits = jnp.where(mask[None, :, None, None], logits + bias[None, None], NEG_INF)
    sink_col = jnp.broadcast_to(
        sinks.astype(jnp.float32).reshape(1, 1, SWA_KV_HEADS, SWA_GROUP, 1, 1),
        logits.shape[:-1] + (1,))
    probs = jax.nn.softmax(jnp.concatenate([logits, sink_col], axis=-1), axis=-1)[..., :-1]
    out = jnp.einsum('bnhgqk,bnkhd->bnqhgd', probs.astype(v.dtype), vw)
    return out.reshape(B, S, SWA_HEADS * HEAD_DIM)


def gla_chunked(q, k, v, log_gate):
    B, S, H, dk = q.shape
    dv = v.shape[-1]
    n = S // GLA_CHUNK
    f32 = jnp.float32
    q = (q.astype(f32) * dk ** -0.5).reshape(B, n, GLA_CHUNK, H, dk)
    k = k.astype(f32).reshape(B, n, GLA_CHUNK, H, dk)
    v = v.astype(f32).reshape(B, n, GLA_CHUNK, H, dv)
    b = jnp.cumsum(log_gate.reshape(B, n, GLA_CHUNK, H, dk), axis=2)
    b_last = b[:, :, -1:]
    q_dec = q * jnp.exp(b)
    k_inv = k * jnp.exp(-b)
    k_end = k * jnp.exp(b_last - b)
    causal = jnp.tril(jnp.ones((GLA_CHUNK, GLA_CHUNK), dtype=bool))
    scores = jnp.where(causal, jnp.einsum('bnthk,bnshk->bnhts', q_dec, k_inv), 0.0)
    o_intra = jnp.einsum('bnhts,bnshv->bnthv', scores, v)
    d_state = jnp.einsum('bnshk,bnshv->bnhkv', k_end, v)
    decay = jnp.exp(b_last[:, :, 0])

    def step(state, inp):
        dec, ds = inp
        return dec[..., None] * state + ds, state

    init = jnp.zeros((B, H, dk, dv), f32)
    _, s_prev = lax.scan(step, init, (decay.swapaxes(0, 1), d_state.swapaxes(0, 1)))
    s_prev = s_prev.swapaxes(0, 1)
    o_inter = jnp.einsum('bnthk,bnhkv->bnthv', q_dec, s_prev)
    return (o_intra + o_inter).reshape(B, S, H, dv)


def stick_breaking_attention(q, k, v):
    B, S, H, D = q.shape
    nb = S // SB_BLOCK
    qb = q.reshape(B, nb, SB_BLOCK, H, D).swapaxes(0, 1)
    kpos = jnp.arange(S)
    scale = D ** -0.5

    def block(args):
        q_blk, i = args
        z = jnp.einsum('bqhd,bshd->bhqs', q_blk, k).astype(jnp.float32) * scale
        qpos = i * SB_BLOCK + jnp.arange(SB_BLOCK)
        strict = kpos[None, :] < qpos[:, None]
        log_beta = jax.nn.log_sigmoid(z)
        log_1m_beta = jnp.where(strict, jax.nn.log_sigmoid(-z), 0.0)
        between = lax.cumsum(log_1m_beta, axis=3, reverse=True) - log_1m_beta
        w = jnp.where(strict, jnp.exp(log_beta + between), 0.0)
        return jnp.einsum('bhqs,bshd->bqhd', w.astype(v.dtype), v)

    out = lax.map(block, (qb, jnp.arange(nb)))
    return out.swapaxes(0, 1).reshape(B, S, H * D)


def token_mixing(h, w_in, sinks, rel_bias, gla_gate_w, gla_gate_b, gla_norm,
                 swa_out_norm, sb_out_norm, w_out):
    B, S, _ = h.shape
    proj = h @ w_in
    split_at = np.cumsum(PROJ_SIZES)[:-1].tolist()
    aq, ak, av, bq, bk, bv, br, blr, cq, ck, cv = jnp.split(proj, split_at, axis=-1)
    out_a = swa_sink_attention(
        aq.reshape(B, S, SWA_HEADS, HEAD_DIM),
        ak.reshape(B, S, SWA_KV_HEADS, HEAD_DIM),
        av.reshape(B, S, SWA_KV_HEADS, HEAD_DIM), sinks, rel_bias)
    out_a = rmsnorm(out_a, swa_out_norm)
    gate_pre = (blr @ gla_gate_w + gla_gate_b).astype(jnp.float32)
    log_gate = jnp.maximum(jax.nn.log_sigmoid(gate_pre) / GATE_NORMALIZER, GATE_LOG_MIN)
    o_b = gla_chunked(
        bq.reshape(B, S, GLA_HEADS, GLA_DK),
        bk.reshape(B, S, GLA_HEADS, GLA_DK),
        bv.reshape(B, S, GLA_HEADS, GLA_DV),
        log_gate.reshape(B, S, GLA_HEADS, GLA_DK))
    o_b = rmsnorm(o_b, gla_norm).reshape(B, S, GLA_WIDTH).astype(h.dtype)
    out_b = o_b * jax.nn.silu(br)
    out_c = stick_breaking_attention(
        cq.reshape(B, S, SB_HEADS, HEAD_DIM),
        ck.reshape(B, S, SB_HEADS, HEAD_DIM),
        cv.reshape(B, S, SB_HEADS, HEAD_DIM))
    out_c = rmsnorm(out_c, sb_out_norm)
    mixed = jnp.concatenate([out_a, out_b, out_c], axis=-1)
    return mixed @ w_out


def squared_relu_mlp(h, w1, w2):
    return jnp.square(jax.nn.relu(h @ w1)) @ w2


def setup_inputs(seed: int = 0) -> dict:
    key = jax.random.key(seed)
    ks = jax.random.split(key, 15)
    f32 = jnp.float32

    def nrm(k, shape, scale):
        return jax.random.normal(k, shape, f32) * scale

    def gain(k, shape):
        return 1.0 + 0.05 * jax.random.normal(k, shape, f32)

    return {
        'x': nrm(ks[0], (BATCH, SEQ, D_MODEL), 1.0),
        'norm_mix': gain(ks[1], (DEPTH, D_MODEL)),
        'w_in': nrm(ks[2], (DEPTH, D_MODEL, PROJ_WIDTH), D_MODEL ** -0.5),
        'swa_sinks': nrm(ks[3], (DEPTH, SWA_HEADS), 0.5),
        'rel_bias': nrm(ks[4], (REL_BUCKETS, SWA_HEADS), 0.5),
        'gla_gate_w': nrm(ks[5], (DEPTH, GLA_LOWRANK, GLA_HEADS * GLA_DK), GLA_LOWRANK ** -0.5),
        'gla_gate_b': nrm(ks[6], (DEPTH, GLA_HEADS * GLA_DK), 0.1),
        'gla_norm': gain(ks[7], (DEPTH, GLA_DV)),
        'swa_out_norm': gain(ks[8], (DEPTH, SWA_WIDTH)),
        'sb_out_norm': gain(ks[9], (DEPTH, SB_WIDTH)),
        'w_out': nrm(ks[10], (DEPTH, MIX_WIDTH, D_MODEL), MIX_WIDTH ** -0.5),
        'norm_mlp': gain(ks[11], (DEPTH, D_MODEL)),
        'w_mlp_in': nrm(ks[12], (DEPTH, D_MODEL, D_FF), D_MODEL ** -0.5),
        'w_mlp_out': nrm(ks[13], (DEPTH, D_FF, D_MODEL), D_FF ** -0.5),
        'norm_final': gain(ks[14], (D_MODEL,)),
    }


def reference(x, norm_mix, w_in, swa_sinks, rel_bias, gla_gate_w, gla_gate_b, gla_norm,
              swa_out_norm, sb_out_norm, w_out, norm_mlp, w_mlp_in, w_mlp_out, norm_final):
    for l in range(DEPTH):
        h = rmsnorm(x, norm_mix[l])
        x = x + token_mixing(h, w_in[l], swa_sinks[l], rel_bias, gla_gate_w[l], gla_gate_b[l],
                             gla_norm[l], swa_out_norm[l], sb_out_norm[l], w_out[l])
        h = rmsnorm(x, norm_mlp[l])
        x = x + squared_relu_mlp(h, w_mlp_in[l], w_mlp_out[l])
    return rmsnorm(x, norm_final)
```

```python
import functools
import math

import numpy as np
import jax
import jax.numpy as jnp
from jax import lax
from jax.experimental import pallas as pl
from jax.experimental.pallas import tpu as pltpu

F32 = jnp.float32
BF16 = jnp.bfloat16

D_MODEL = 2048
HEAD_DIM = 64
SWA_HEADS = 12
SWA_KV_HEADS = 4
SWA_WIDTH = SWA_HEADS * HEAD_DIM
SWA_KV_WIDTH = SWA_KV_HEADS * HEAD_DIM
SWA_BLOCK = 128
WINDOW = 128
REL_BUCKETS = 32
REL_MAX_DIST = 128
GLA_HEADS = 4
GLA_DV = 192
GLA_DK = 96
GLA_DK_PAD = 128
GLA_DV_PAD = 256
GLA_LOWRANK = 16
GLA_LR_PAD = 128
GLA_CHUNK = 64
GATE_NORMALIZER = 16.0
GATE_LOG_MIN = -1.0
SB_HEADS = 8
SB_WIDTH = SB_HEADS * HEAD_DIM
D_FF = 4 * D_MODEL
RMS_EPS = 1e-6
NEG_INF = -1e30

LANES = 128
VMEM_LIMIT = 56 * 1024 * 1024

OFF_BV = 0
OFF_BR = 1024
OFF_BQ = 2048
OFF_BK = 2560
OFF_CQ = 3072
OFF_CK = 3584
OFF_CV = 4096
OFF_AQ = 4608
OFF_AK = 5376
OFF_AV = 5632
OFF_BLR = 5888
PROJ_PAD_WIDTH = 6144

SWA_Q_PERM = (0, 3, 1, 4, 2, 5, 6, 9, 7, 10, 8, 11)

_O_AQ, _O_AK, _O_AV = 0, 768, 1024
_O_BQ, _O_BK, _O_BV, _O_BR, _O_BLR = 1280, 1664, 2048, 2816, 3584
_O_CQ, _O_CK, _O_CV = 3600, 4112, 4624


def _cparams(sem):
    return pltpu.CompilerParams(dimension_semantics=sem, vmem_limit_bytes=VMEM_LIMIT)


def _rms(x, gain):
    ms = jnp.mean(x * x, axis=-1, keepdims=True)
    return x * lax.rsqrt(ms + RMS_EPS) * gain


def _split_bf16(x):
    hi = x.astype(BF16)
    lo = (x - hi.astype(F32)).astype(BF16)
    return hi, lo


def _rmsnorm_kernel(x_ref, g_ref, o_ref):
    o_ref[...] = _rms(x_ref[...], g_ref[...]).astype(o_ref.dtype)


def rmsnorm_bf16(x, gain, tm=512):
    s, d = x.shape
    return pl.pallas_call(
        _rmsnorm_kernel,
        grid=(s // tm,),
        in_specs=[pl.BlockSpec((tm, d), lambda i: (i, 0)),
                  pl.BlockSpec((1, d), lambda i: (0, 0))],
        out_specs=pl.BlockSpec((tm, d), lambda i: (i, 0)),
        out_shape=jax.ShapeDtypeStruct((s, d), BF16),
        compiler_params=_cparams(("parallel",)),
        name="rmsnorm",
    )(x, gain.reshape(1, d))


def _matmul_kernel(h_ref, w_ref, o_ref):
    o_ref[...] = jnp.dot(h_ref[...], w_ref[...],
                         preferred_element_type=F32).astype(o_ref.dtype)


def in_projection(h, w, tm=1024, tn=512):
    s, d = h.shape
    n = w.shape[1]
    tm = min(tm, s)
    return pl.pallas_call(
        _matmul_kernel,
        grid=(s // tm, n // tn),
        in_specs=[pl.BlockSpec((tm, d), lambda i, j: (i, 0)),
                  pl.BlockSpec((d, tn), lambda i, j: (0, j))],
        out_specs=pl.BlockSpec((tm, tn), lambda i, j: (i, j)),
        out_shape=jax.ShapeDtypeStruct((s, n), BF16),
        compiler_params=_cparams(("parallel", "arbitrary")),
        name="in_proj",
    )(h, w)


def _swa_kernel(sink_ref, q_ref, kp_ref, kc_ref, vp_ref, vc_ref, bias_ref, o_ref):
    i = pl.program_id(0)
    blk = SWA_BLOCK
    k = jnp.concatenate([kp_ref[...], kc_ref[...]], axis=0)
    v = jnp.concatenate([vp_ref[...], vc_ref[...]], axis=0)
    lane = lax.broadcasted_iota(jnp.int32, (blk, LANES), 1)
    col = lax.broadcasted_iota(jnp.int32, (blk, 2 * blk), 1)
    kill = jnp.logical_and(col < blk, i == 0)
    low = lane < HEAD_DIM
    for b in range(SWA_HEADS // 2):
        qb = q_ref[:, LANES * b:LANES * (b + 1)]
        kv_lo = LANES * (b // 3)
        kb = k[:, kv_lo:kv_lo + LANES]
        vb = v[:, kv_lo:kv_lo + LANES]
        outs = []
        for half in range(2):
            hq = 2 * b + half
            keep = low if half == 0 else jnp.logical_not(low)
            qm = jnp.where(keep, qb, jnp.zeros_like(qb))
            s = lax.dot_general(qm, kb, (((1,), (1,)), ((), ())),
                                preferred_element_type=F32)
            s = jnp.where(kill, NEG_INF, s + bias_ref[hq])
            sink = sink_ref[hq]
            m = jnp.maximum(jnp.max(s, axis=1, keepdims=True), sink)
            p = jnp.exp(s - m)
            denom = jnp.sum(p, axis=1, keepdims=True) + jnp.exp(sink - m)
            o = jnp.dot(p.astype(BF16), vb, preferred_element_type=F32)
            outs.append(o / denom)
        o_ref[:, LANES * b:LANES * (b + 1)] = jnp.where(low, outs[0], outs[1]).astype(o_ref.dtype)


def swa_attention(proj, sinks_perm, bias_tab):
    s = proj.shape[0]
    blk = SWA_BLOCK
    qi = OFF_AQ // SWA_WIDTH
    ki = OFF_AK // SWA_KV_WIDTH
    vi = OFF_AV // SWA_KV_WIDTH
    prev = lambda i: jnp.maximum(i - 1, 0)
    return pl.pallas_call(
        _swa_kernel,
        grid=(s // blk,),
        in_specs=[pl.BlockSpec(memory_space=pltpu.SMEM),
                  pl.BlockSpec((blk, SWA_WIDTH), lambda i: (i, qi)),
                  pl.BlockSpec((blk, SWA_KV_WIDTH), lambda i: (prev(i), ki)),
                  pl.BlockSpec((blk, SWA_KV_WIDTH), lambda i: (i, ki)),
                  pl.BlockSpec((blk, SWA_KV_WIDTH), lambda i: (prev(i), vi)),
                  pl.BlockSpec((blk, SWA_KV_WIDTH), lambda i: (i, vi)),
                  pl.BlockSpec((SWA_HEADS, blk, 2 * blk), lambda i: (0, 0, 0))],
        out_specs=pl.BlockSpec((blk, SWA_WIDTH), lambda i: (i, 0)),
        out_shape=jax.ShapeDtypeStruct((s, SWA_WIDTH), BF16),
        compiler_params=_cparams(("parallel",)),
        name="swa",
    )(sinks_perm, proj, proj, proj, proj, proj, bias_tab)


def _t5_causal_bucket(dist):
    max_exact = REL_BUCKETS // 2
    is_small = dist < max_exact
    ratio = (jnp.log(jnp.maximum(dist, 1).astype(F32) / max_exact)
             / math.log(REL_MAX_DIST / max_exact))
    large = max_exact + (ratio * (REL_BUCKETS - max_exact)).astype(jnp.int32)
    large = jnp.minimum(large, REL_BUCKETS - 1)
    return jnp.where(is_small, dist, large)


def swa_bias_table(rel_bias):
    blk = SWA_BLOCK
    qpos = jnp.arange(blk) + blk
    kpos = jnp.arange(2 * blk)
    dist = qpos[:, None] - kpos[None, :]
    in_window = (dist >= 0) & (dist < WINDOW)
    bias = rel_bias.astype(F32)[_t5_causal_bucket(jnp.maximum(dist, 0))]
    bias = bias.transpose(2, 0, 1)[jnp.array(SWA_Q_PERM)]
    return jnp.where(in_window[None], bias, NEG_INF)


def _gla_kernel(q_ref, k_ref, v_ref, r_ref, lr_ref, gw_ref, gb_ref, gn_ref, o_ref,
                st_ref, *, rows):
    t = pl.program_id(1)
    ch = GLA_CHUNK

    @pl.when(t == 0)
    def _():
        st_ref[...] = jnp.zeros_like(st_ref)

    gw = gw_ref[...]
    gb = gb_ref[...]
    gn = gn_ref[...]
    row = lax.broadcasted_iota(jnp.int32, (ch, ch), 0)
    colm = lax.broadcasted_iota(jnp.int32, (ch, ch), 1)
    causal = colm <= row
    ltri = jnp.where(causal, 1.0, 0.0).astype(BF16)
    st = st_ref[...]
    nt = (((1,), (1,)), ((), ()))
    tn = (((0,), (0,)), ((), ()))
    for c in range(rows // ch):
        sl = pl.ds(c * ch, ch)
        gp = jnp.dot(lr_ref[sl, :], gw, preferred_element_type=F32) + gb
        log_sig = jnp.minimum(gp, 0.0) - jnp.log(1.0 + jnp.exp(-jnp.abs(gp)))
        lg = jnp.maximum(log_sig * (1.0 / GATE_NORMALIZER), GATE_LOG_MIN)
        hi, lo = _split_bf16(lg)
        b = (jnp.dot(ltri, hi, preferred_element_type=F32)
             + jnp.dot(ltri, lo, preferred_element_type=F32))
        b_last = b[ch - 1:ch, :]
        q = q_ref[sl, :].astype(F32) * (GLA_DK ** -0.5)
        kk = k_ref[sl, :].astype(F32)
        q_dec = (q * jnp.exp(b)).astype(BF16)
        k_inv = (kk * jnp.exp(-b)).astype(BF16)
        k_end = (kk * jnp.exp(b_last - b)).astype(BF16)
        v = v_ref[sl, :]
        sc = lax.dot_general(q_dec, k_inv, nt, preferred_element_type=F32)
        sc = jnp.where(causal, sc, 0.0).astype(BF16)
        o = (jnp.dot(sc, v, preferred_element_type=F32)
             + lax.dot_general(q_dec, st.astype(BF16), nt, preferred_element_type=F32))
        d_st = lax.dot_general(v, k_end, tn, preferred_element_type=F32)
        st = st * jnp.exp(b_last) + d_st
        ms = jnp.sum(o * o, axis=1, keepdims=True) * (1.0 / GLA_DV)
        y = o * lax.rsqrt(ms + RMS_EPS) * gn
        rr = r_ref[sl, :].astype(F32)
        o_ref[sl, :] = (y * (rr / (1.0 + jnp.exp(-rr)))).astype(o_ref.dtype)
    st_ref[...] = st


def gla_attention(proj, gw_pad, gb_pad, gn_pad, rows=256):
    s = proj.shape[0]
    rows = min(rows, s)
    qi = OFF_BQ // GLA_DK_PAD
    ki = OFF_BK // GLA_DK_PAD
    vi = OFF_BV // GLA_DV_PAD
    ri = OFF_BR // GLA_DV_PAD
    li = OFF_BLR // GLA_LR_PAD
    return pl.pallas_call(
        functools.partial(_gla_kernel, rows=rows),
        grid=(GLA_HEADS, s // rows),
        in_specs=[pl.BlockSpec((rows, GLA_DK_PAD), lambda h, t: (t, qi + h)),
                  pl.BlockSpec((rows, GLA_DK_PAD), lambda h, t: (t, ki + h)),
                  pl.BlockSpec((rows, GLA_DV_PAD), lambda h, t: (t, vi + h)),
                  pl.BlockSpec((rows, GLA_DV_PAD), lambda h, t: (t, ri + h)),
                  pl.BlockSpec((rows, GLA_LR_PAD), lambda h, t: (t, li)),
                  pl.BlockSpec((None, GLA_LR_PAD, GLA_DK_PAD), lambda h, t: (h, 0, 0)),
                  pl.BlockSpec((None, 1, GLA_DK_PAD), lambda h, t: (h, 0, 0)),
                  pl.BlockSpec((1, GLA_DV_PAD), lambda h, t: (0, 0))],
        out_specs=pl.BlockSpec((rows, GLA_DV_PAD), lambda h, t: (t, h)),
        out_shape=jax.ShapeDtypeStruct((s, GLA_HEADS * GLA_DV_PAD), BF16),
        scratch_shapes=[pltpu.VMEM((GLA_DV_PAD, GLA_DK_PAD), F32)],
        compiler_params=_cparams(("parallel", "arbitrary")),
        name="gla",
    )(proj, proj, proj, proj, proj, gw_pad, gb_pad, gn_pad)


def _sb_kernel(q_ref, k_ref, v_ref, u_ref, o_ref, acc_ref, carry_ref, *, bq):
    i = pl.program_id(1)
    lane = lax.broadcasted_iota(jnp.int32, (bq, LANES), 1)
    low = lane < HEAD_DIM
    q = q_ref[...]
    zq = jnp.zeros_like(q)
    qm = (jnp.where(low, q, zq), jnp.where(low, zq, q))
    u = u_ref[...]
    acc_ref[...] = jnp.zeros_like(acc_ref)
    carry_ref[...] = jnp.zeros_like(carry_ref)
    nt = (((1,), (1,)), ((), ()))

    def chunk(c, diagonal):
        start = pl.multiple_of(c * bq, bq)
        kc = k_ref[pl.ds(start, bq), :]
        vc = v_ref[pl.ds(start, bq), :]
        if diagonal:
            rowi = lax.broadcasted_iota(jnp.int32, (bq, bq), 0)
            coli = lax.broadcasted_iota(jnp.int32, (bq, bq), 1)
            strict = coli < rowi
        for h in range(2):
            z = lax.dot_general(qm[h], kc, nt, preferred_element_type=F32)
            nl = jnp.maximum(z, 0.0) + jnp.log(1.0 + jnp.exp(-jnp.abs(z)))
            if diagonal:
                nl = jnp.where(strict, nl, 0.0)
            hi, lo = _split_bf16(nl)
            cs = (jnp.dot(hi, u, preferred_element_type=F32)
                  + jnp.dot(lo, u, preferred_element_type=F32))
            car = carry_ref[h]
            w = jnp.exp(z - nl - cs - jnp.concatenate([car, car], axis=1))
            if diagonal:
                w = jnp.where(strict, w, 0.0)
            acc_ref[h] += jnp.dot(w.astype(BF16), vc, preferred_element_type=F32)
            tot = cs[:, 0:1] + nl[:, 0:1]
            carry_ref[h] = car + jnp.broadcast_to(tot, (bq, LANES))

    chunk(i, True)

    def body(j, carry):
        chunk(i - 1 - j, False)
        return carry

    lax.fori_loop(0, i, body, 0)
    o_ref[...] = jnp.where(low, acc_ref[0], acc_ref[1]).astype(o_ref.dtype)


def sb_attention(proj, bq=256):
    s = proj.shape[0]
    bq = min(bq, s)
    qi = OFF_CQ // LANES
    ki = OFF_CK // LANES
    vi = OFF_CV // LANES
    u = (np.arange(bq)[:, None] > np.arange(bq)[None, :]).astype(np.float32)
    u = jnp.asarray(u, dtype=BF16)
    return pl.pallas_call(
        functools.partial(_sb_kernel, bq=bq),
        grid=(SB_HEADS // 2, s // bq),
        in_specs=[pl.BlockSpec((bq, LANES), lambda p, i: (i, qi + p)),
                  pl.BlockSpec((s, LANES), lambda p, i: (0, ki + p)),
                  pl.BlockSpec((s, LANES), lambda p, i: (0, vi + p)),
                  pl.BlockSpec((bq, bq), lambda p, i: (0, 0))],
        out_specs=pl.BlockSpec((bq, LANES), lambda p, i: (i, p)),
        out_shape=jax.ShapeDtypeStruct((s, SB_WIDTH), BF16),
        scratch_shapes=[pltpu.VMEM((2, bq, LANES), F32),
                        pltpu.VMEM((2, bq, LANES), F32)],
        compiler_params=_cparams(("parallel", "arbitrary")),
        name="stickbreak",
    )(proj, proj, proj, u)


def _oproj_kernel(a_ref, b_ref, c_ref, wa_ref, wb_ref, wc_ref, ga_ref, gc_ref,
                  x_ref, gn_ref, xo_ref, ho_ref):
    an = _rms(a_ref[...].astype(F32), ga_ref[...]).astype(BF16)
    cn = _rms(c_ref[...].astype(F32), gc_ref[...]).astype(BF16)
    y = (jnp.dot(an, wa_ref[...], preferred_element_type=F32)
         + jnp.dot(b_ref[...], wb_ref[...], preferred_element_type=F32)
         + jnp.dot(cn, wc_ref[...], preferred_element_type=F32))
    xn = x_ref[...] + y
    xo_ref[...] = xn
    ho_ref[...] = _rms(xn, gn_ref[...]).astype(ho_ref.dtype)


def out_projection(a, b, c, wa, wb, wc, ga, gc, x, gn, tm=256):
    s, d = x.shape
    tm = min(tm, s)
    row = lambda i: (i, 0)
    fixed = lambda i: (0, 0)
    return pl.pallas_call(
        _oproj_kernel,
        grid=(s // tm,),
        in_specs=[pl.BlockSpec((tm, a.shape[1]), row),
                  pl.BlockSpec((tm, b.shape[1]), row),
                  pl.BlockSpec((tm, c.shape[1]), row),
                  pl.BlockSpec(wa.shape, fixed),
                  pl.BlockSpec(wb.shape, fixed),
                  pl.BlockSpec(wc.shape, fixed),
                  pl.BlockSpec((1, a.shape[1]), fixed),
                  pl.BlockSpec((1, c.shape[1]), fixed),
                  pl.BlockSpec((tm, d), row),
                  pl.BlockSpec((1, d), fixed)],
        out_specs=[pl.BlockSpec((tm, d), row), pl.BlockSpec((tm, d), row)],
        out_shape=[jax.ShapeDtypeStruct((s, d), F32), jax.ShapeDtypeStruct((s, d), BF16)],
        compiler_params=_cparams(("parallel",)),
        name="out_proj",
    )(a, b, c, wa, wb, wc, ga, gc, x, gn)


def _mlp_kernel(h_ref, x_ref, w1_ref, w2_ref, gn_ref, *refs, final):
    if final:
        ho_ref, acc_ref = refs
    else:
        xo_ref, ho_ref, acc_ref = refs
    j = pl.program_id(1)
    u = jnp.dot(h_ref[...], w1_ref[...], preferred_element_type=F32)
    act = jnp.square(jnp.maximum(u, 0.0)).astype(BF16)
    contrib = jnp.dot(act, w2_ref[...], preferred_element_type=F32)

    @pl.when(j == 0)
    def _():
        acc_ref[...] = contrib

    @pl.when(j > 0)
    def _():
        acc_ref[...] += contrib

    @pl.when(j == pl.num_programs(1) - 1)
    def _():
        xn = x_ref[...] + acc_ref[...]
        if not final:
            xo_ref[...] = xn
        ho_ref[...] = _rms(xn, gn_ref[...]).astype(ho_ref.dtype)


def mlp_block(h, x, w1, w2, gn, final, tm=512, tf=512):
    s, d = x.shape
    ff = w1.shape[1]
    tm = min(tm, s)
    row = lambda i, j: (i, 0)
    out_specs = [pl.BlockSpec((tm, d), row)]
    out_shape = [jax.ShapeDtypeStruct((s, d), F32 if final else BF16)]
    if not final:
        out_specs = [pl.BlockSpec((tm, d), row)] + out_specs
        out_shape = [jax.ShapeDtypeStruct((s, d), F32)] + out_shape
    return pl.pallas_call(
        functools.partial(_mlp_kernel, final=final),
        grid=(s // tm, ff // tf),
        in_specs=[pl.BlockSpec((tm, d), row),
                  pl.BlockSpec((tm, d), row),
                  pl.BlockSpec((d, tf), lambda i, j: (0, j)),
                  pl.BlockSpec((tf, d), lambda i, j: (j, 0)),
                  pl.BlockSpec((1, d), lambda i, j: (0, 0))],
        out_specs=out_specs,
        out_shape=out_shape,
        scratch_shapes=[pltpu.VMEM((tm, d), F32)],
        compiler_params=_cparams(("parallel", "arbitrary")),
        name="mlp",
    )(h, x, w1, w2, gn)


def _pad_heads(w, heads, width, padded):
    lead = w.shape[:-1]
    w = w.reshape(lead + (heads, width))
    w = jnp.pad(w, [(0, 0)] * len(lead) + [(0, 0), (0, padded - width)])
    return w.reshape(lead + (heads * padded,))


def _permute_w_in(w):
    d = w.shape[0]
    seg = lambda off, width: w[:, off:off + width]
    aq = seg(_O_AQ, SWA_WIDTH).reshape(d, SWA_HEADS, HEAD_DIM)[:, np.array(SWA_Q_PERM)]
    aq = aq.reshape(d, SWA_WIDTH) * (HEAD_DIM ** -0.5)
    pieces = [
        _pad_heads(seg(_O_BV, GLA_HEADS * GLA_DV), GLA_HEADS, GLA_DV, GLA_DV_PAD),
        _pad_heads(seg(_O_BR, GLA_HEADS * GLA_DV), GLA_HEADS, GLA_DV, GLA_DV_PAD),
        _pad_heads(seg(_O_BQ, GLA_HEADS * GLA_DK), GLA_HEADS, GLA_DK, GLA_DK_PAD),
        _pad_heads(seg(_O_BK, GLA_HEADS * GLA_DK), GLA_HEADS, GLA_DK, GLA_DK_PAD),
        seg(_O_CQ, SB_WIDTH) * (HEAD_DIM ** -0.5),
        seg(_O_CK, SB_WIDTH),
        seg(_O_CV, SB_WIDTH),
        aq,
        seg(_O_AK, SWA_KV_WIDTH),
        seg(_O_AV, SWA_KV_WIDTH),
        jnp.pad(seg(_O_BLR, GLA_LOWRANK), ((0, 0), (0, GLA_LR_PAD - GLA_LOWRANK))),
        jnp.zeros((d, PROJ_PAD_WIDTH - OFF_BLR - GLA_LR_PAD), w.dtype),
    ]
    return jnp.concatenate(pieces, axis=1).astype(BF16)


def _permute_swa_rows(w):
    rest = w.shape[1:]
    return w.reshape((SWA_HEADS, HEAD_DIM) + rest)[np.array(SWA_Q_PERM)].reshape(w.shape)


def kernel(x, norm_mix, w_in, swa_sinks, rel_bias, gla_gate_w, gla_gate_b, gla_norm,
           swa_out_norm, sb_out_norm, w_out, norm_mlp, w_mlp_in, w_mlp_out, norm_final):
    depth = w_in.shape[0]
    xs = x[0]
    s, d = xs.shape
    bias_tab = swa_bias_table(rel_bias)
    h = rmsnorm_bf16(xs, norm_mix[0])
    out = None
    for l in range(depth):
        w_in_p = _permute_w_in(w_in[l])
        wo = w_out[l]
        wa = _permute_swa_rows(wo[:SWA_WIDTH]).astype(BF16)
        wb = wo[SWA_WIDTH:SWA_WIDTH + GLA_HEADS * GLA_DV].reshape(GLA_HEADS, GLA_DV, d)
        wb = jnp.pad(wb, ((0, 0), (0, GLA_DV_PAD - GLA_DV), (0, 0)))
        wb = wb.reshape(GLA_HEADS * GLA_DV_PAD, d).astype(BF16)
        wc = wo[SWA_WIDTH + GLA_HEADS * GLA_DV:].astype(BF16)
        ga = _permute_swa_rows(swa_out_norm[l]).reshape(1, SWA_WIDTH)
        gc = sb_out_norm[l].reshape(1, SB_WIDTH)
        sinks_p = swa_sinks[l][np.array(SWA_Q_PERM)]
        gw = _pad_heads(gla_gate_w[l], GLA_HEADS, GLA_DK, GLA_DK_PAD)
        gw = jnp.pad(gw, ((0, GLA_LR_PAD - GLA_LOWRANK), (0, 0)))
        gw = gw.reshape(GLA_LR_PAD, GLA_HEADS, GLA_DK_PAD).transpose(1, 0, 2).astype(BF16)
        gb = _pad_heads(gla_gate_b[l], GLA_HEADS, GLA_DK, GLA_DK_PAD)
        gb = gb.reshape(GLA_HEADS, 1, GLA_DK_PAD)
        gn = jnp.pad(gla_norm[l], (0, GLA_DV_PAD - GLA_DV)).reshape(1, GLA_DV_PAD)

        proj = in_projection(h, w_in_p)
        a = swa_attention(proj, sinks_p, bias_tab)
        b = gla_attention(proj, gw, gb, gn)
        c = sb_attention(proj)
        xs, hm = out_projection(a, b, c, wa, wb, wc, ga, gc, xs, norm_mlp[l].reshape(1, d))
        w1 = w_mlp_in[l].astype(BF16)
        w2 = w_mlp_out[l].astype(BF16)
        if l + 1 < depth:
            xs, h = mlp_block(hm, xs, w1, w2, norm_mix[l + 1].reshape(1, d), final=False)
        else:
            (out,) = mlp_block(hm, xs, w1, w2, norm_final.reshape(1, d), final=True)
    return out[None]
```

```python
import functools
import math

import numpy as np
import jax
import jax.numpy as jnp
from jax import lax
from jax.experimental import pallas as pl
from jax.experimental.pallas import tpu as pltpu

F32 = jnp.float32
BF16 = jnp.bfloat16

D_MODEL = 2048
HEAD_DIM = 64
SWA_HEADS = 12
SWA_KV_HEADS = 4
SWA_WIDTH = SWA_HEADS * HEAD_DIM
SWA_KV_WIDTH = SWA_KV_HEADS * HEAD_DIM
SWA_BLOCK = 128
WINDOW = 128
REL_BUCKETS = 32
REL_MAX_DIST = 128
GLA_HEADS = 4
GLA_DV = 192
GLA_DK = 96
GLA_DK_PAD = 128
GLA_DV_PAD = 256
GLA_LOWRANK = 16
GLA_LR_PAD = 128
GLA_CHUNK = 64
GATE_NORMALIZER = 16.0
GATE_LOG_MIN = -1.0
SB_HEADS = 8
SB_WIDTH = SB_HEADS * HEAD_DIM
D_FF = 4 * D_MODEL
RMS_EPS = 1e-6
NEG_INF = -1e30

LANES = 128
VMEM_LIMIT = 56 * 1024 * 1024

OFF_BV = 0
OFF_BR = 1024
OFF_BQ = 2048
OFF_BK = 2560
OFF_CQ = 3072
OFF_CK = 3584
OFF_CV = 4096
OFF_AQ = 4608
OFF_AK = 5376
OFF_AV = 5632
OFF_BLR = 5888
PROJ_PAD_WIDTH = 6144

SWA_Q_PERM = (0, 3, 1, 4, 2, 5, 6, 9, 7, 10, 8, 11)

_O_AQ, _O_AK, _O_AV = 0, 768, 1024
_O_BQ, _O_BK, _O_BV, _O_BR, _O_BLR = 1280, 1664, 2048, 2816, 3584
_O_CQ, _O_CK, _O_CV = 3600, 4112, 4624


def _cparams(sem):
    return pltpu.CompilerParams(dimension_semantics=sem, vmem_limit_bytes=VMEM_LIMIT)


def _rms(x, gain):
    ms = jnp.mean(x * x, axis=-1, keepdims=True)
    return x * lax.rsqrt(ms + RMS_EPS) * gain


def _split_bf16(x):
    hi = x.astype(BF16)
    lo = (x - hi.astype(F32)).astype(BF16)
    return hi, lo


def _rmsnorm_kernel(x_ref, g_ref, o_ref):
    o_ref[...] = _rms(x_ref[...], g_ref[...]).astype(o_ref.dtype)


def rmsnorm_bf16(x, gain, tm=512):
    s, d = x.shape
    return pl.pallas_call(
        _rmsnorm_kernel,
        grid=(s // tm,),
        in_specs=[pl.BlockSpec((tm, d), lambda i: (i, 0)),
                  pl.BlockSpec((1, d), lambda i: (0, 0))],
        out_specs=pl.BlockSpec((tm, d), lambda i: (i, 0)),
        out_shape=jax.ShapeDtypeStruct((s, d), BF16),
        compiler_params=_cparams(("parallel",)),
        name="rmsnorm",
    )(x, gain.reshape(1, d))


def _matmul_kernel(h_ref, w_ref, o_ref):
    o_ref[...] = jnp.dot(h_ref[...], w_ref[...],
                         preferred_element_type=F32).astype(o_ref.dtype)


def in_projection(h, w, tm=1024, tn=512):
    s, d = h.shape
    n = w.shape[1]
    tm = min(tm, s)
    return pl.pallas_call(
        _matmul_kernel,
        grid=(s // tm, n // tn),
        in_specs=[pl.BlockSpec((tm, d), lambda i, j: (i, 0)),
                  pl.BlockSpec((d, tn), lambda i, j: (0, j))],
        out_specs=pl.BlockSpec((tm, tn), lambda i, j: (i, j)),
        out_shape=jax.ShapeDtypeStruct((s, n), BF16),
        compiler_params=_cparams(("parallel", "arbitrary")),
        name="in_proj",
    )(h, w)


def _swa_kernel(sink_ref, q_ref, kp_ref, kc_ref, vp_ref, vc_ref, bias_ref, o_ref):
    i = pl.program_id(0)
    blk = SWA_BLOCK
    k = jnp.concatenate([kp_ref[...], kc_ref[...]], axis=0)
    v = jnp.concatenate([vp_ref[...], vc_ref[...]], axis=0)
    lane = lax.broadcasted_iota(jnp.int32, (blk, LANES), 1)
    col = lax.broadcasted_iota(jnp.int32, (blk, 2 * blk), 1)
    kill = jnp.logical_and(col < blk, i == 0)
    low = lane < HEAD_DIM
    for b in range(SWA_HEADS // 2):
        qb = q_ref[:, LANES * b:LANES * (b + 1)]
        kv_lo = LANES * (b // 3)
        kb = k[:, kv_lo:kv_lo + LANES]
        vb = v[:, kv_lo:kv_lo + LANES]
        outs = []
        for half in range(2):
            hq = 2 * b + half
            keep = low if half == 0 else jnp.logical_not(low)
            qm = jnp.where(keep, qb, jnp.zeros_like(qb))
            s = lax.dot_general(qm, kb, (((1,), (1,)), ((), ())),
                                preferred_element_type=F32)
            s = jnp.where(kill, NEG_INF, s + bias_ref[hq])
            sink = sink_ref[hq]
            m = jnp.maximum(jnp.max(s, axis=1, keepdims=True), sink)
            p = jnp.exp(s - m)
            denom = jnp.sum(p, axis=1, keepdims=True) + jnp.exp(sink - m)
            o = jnp.dot(p.astype(BF16), vb, preferred_element_type=F32)
            outs.append(o / denom)
        o_ref[:, LANES * b:LANES * (b + 1)] = jnp.where(low, outs[0], outs[1]).astype(o_ref.dtype)


def swa_attention(proj, sinks_perm, bias_tab):
    s = proj.shape[0]
    blk = SWA_BLOCK
    qi = OFF_AQ // SWA_WIDTH
    ki = OFF_AK // SWA_KV_WIDTH
    vi = OFF_AV // SWA_KV_WIDTH
    prev = lambda i: jnp.maximum(i - 1, 0)
    return pl.pallas_call(
        _swa_kernel,
        grid=(s // blk,),
        in_specs=[pl.BlockSpec(memory_space=pltpu.SMEM),
                  pl.BlockSpec((blk, SWA_WIDTH), lambda i: (i, qi)),
                  pl.BlockSpec((blk, SWA_KV_WIDTH), lambda i: (prev(i), ki)),
                  pl.BlockSpec((blk, SWA_KV_WIDTH), lambda i: (i, ki)),
                  pl.BlockSpec((blk, SWA_KV_WIDTH), lambda i: (prev(i), vi)),
                  pl.BlockSpec((blk, SWA_KV_WIDTH), lambda i: (i, vi)),
                  pl.BlockSpec((SWA_HEADS, blk, 2 * blk), lambda i: (0, 0, 0))],
        out_specs=pl.BlockSpec((blk, SWA_WIDTH), lambda i: (i, 0)),
        out_shape=jax.ShapeDtypeStruct((s, SWA_WIDTH), BF16),
        compiler_params=_cparams(("parallel",)),
        name="swa",
    )(sinks_perm, proj, proj, proj, proj, proj, bias_tab)


def _t5_causal_bucket(dist):
    max_exact = REL_BUCKETS // 2
    is_small = dist < max_exact
    ratio = (jnp.log(jnp.maximum(dist, 1).astype(F32) / max_exact)
             / math.log(REL_MAX_DIST / max_exact))
    large = max_exact + (ratio * (REL_BUCKETS - max_exact)).astype(jnp.int32)
    large = jnp.minimum(large, REL_BUCKETS - 1)
    return jnp.where(is_small, dist, large)


def swa_bias_table(rel_bias):
    blk = SWA_BLOCK
    qpos = jnp.arange(blk) + blk
    kpos = jnp.arange(2 * blk)
    dist = qpos[:, None] - kpos[None, :]
    in_window = (dist >= 0) & (dist < WINDOW)
    bias = rel_bias.astype(F32)[_t5_causal_bucket(jnp.maximum(dist, 0))]
    bias = bias.transpose(2, 0, 1)[jnp.array(SWA_Q_PERM)]
    return jnp.where(in_window[None], bias, NEG_INF)


def _gla_kernel(q_ref, k_ref, v_ref, r_ref, lr_ref, gw_ref, gb_ref, gn_ref, o_ref,
                st_ref, *, rows):
    t = pl.program_id(1)
    ch = GLA_CHUNK

    @pl.when(t == 0)
    def _():
        st_ref[...] = jnp.zeros_like(st_ref)

    gw = gw_ref[...]
    gb = gb_ref[...]
    gn = gn_ref[...]
    row = lax.broadcasted_iota(jnp.int32, (ch, ch), 0)
    colm = lax.broadcasted_iota(jnp.int32, (ch, ch), 1)
    causal = colm <= row
    ltri = jnp.where(causal, 1.0, 0.0).astype(BF16)
    st = st_ref[...]
    nt = (((1,), (1,)), ((), ()))
    tn = (((0,), (0,)), ((), ()))
    for c in range(rows // ch):
        sl = pl.ds(c * ch, ch)
        gp = jnp.dot(lr_ref[sl, :], gw, preferred_element_type=F32) + gb
        log_sig = jnp.minimum(gp, 0.0) - jnp.log(1.0 + jnp.exp(-jnp.abs(gp)))
        lg = jnp.maximum(log_sig * (1.0 / GATE_NORMALIZER), GATE_LOG_MIN)
        hi, lo = _split_bf16(lg)
        b = (jnp.dot(ltri, hi, preferred_element_type=F32)
             + jnp.dot(ltri, lo, preferred_element_type=F32))
        b_last = b[ch - 1:ch, :]
        q = q_ref[sl, :].astype(F32) * (GLA_DK ** -0.5)
        kk = k_ref[sl, :].astype(F32)
        q_dec = (q * jnp.exp(b)).astype(BF16)
        k_inv = (kk * jnp.exp(-b)).astype(BF16)
        k_end = (kk * jnp.exp(b_last - b)).astype(BF16)
        v = v_ref[sl, :]
        sc = lax.dot_general(q_dec, k_inv, nt, preferred_element_type=F32)
        sc = jnp.where(causal, sc, 0.0).astype(BF16)
        o = (jnp.dot(sc, v, preferred_element_type=F32)
             + lax.dot_general(q_dec, st.astype(BF16), nt, preferred_element_type=F32))
        d_st = lax.dot_general(v, k_end, tn, preferred_element_type=F32)
        st = st * jnp.exp(b_last) + d_st
        ms = jnp.sum(o * o, axis=1, keepdims=True) * (1.0 / GLA_DV)
        y = o * lax.rsqrt(ms + RMS_EPS) * gn
        rr = r_ref[sl, :].astype(F32)
        o_ref[sl, :] = (y * (rr / (1.0 + jnp.exp(-rr)))).astype(o_ref.dtype)
    st_ref[...] = st


def gla_attention(proj, gw_pad, gb_pad, gn_pad, rows=256):
    s = proj.shape[0]
    rows = min(rows, s)
    qi = OFF_BQ // GLA_DK_PAD
    ki = OFF_BK // GLA_DK_PAD
    vi = OFF_BV // GLA_DV_PAD
    ri = OFF_BR // GLA_DV_PAD
    li = OFF_BLR // GLA_LR_PAD
    return pl.pallas_call(
        functools.partial(_gla_kernel, rows=rows),
        grid=(GLA_HEADS, s // rows),
        in_specs=[pl.BlockSpec((rows, GLA_DK_PAD), lambda h, t: (t, qi + h)),
                  pl.BlockSpec((rows, GLA_DK_PAD), lambda h, t: (t, ki + h)),
                  pl.BlockSpec((rows, GLA_DV_PAD), lambda h, t: (t, vi + h)),
                  pl.BlockSpec((rows, GLA_DV_PAD), lambda h, t: (t, ri + h)),
                  pl.BlockSpec((rows, GLA_LR_PAD), lambda h, t: (t, li)),
                  pl.BlockSpec((None, GLA_LR_PAD, GLA_DK_PAD), lambda h, t: (h, 0, 0)),
                  pl.BlockSpec((None, 1, GLA_DK_PAD), lambda h, t: (h, 0, 0)),
                  pl.BlockSpec((1, GLA_DV_PAD), lambda h, t: (0, 0))],
        out_specs=pl.BlockSpec((rows, GLA_DV_PAD), lambda h, t: (t, h)),
        out_shape=jax.ShapeDtypeStruct((s, GLA_HEADS * GLA_DV_PAD), BF16),
        scratch_shapes=[pltpu.VMEM((GLA_DV_PAD, GLA_DK_PAD), F32)],
        compiler_params=_cparams(("parallel", "arbitrary")),
        name="gla",
    )(proj, proj, proj, proj, proj, gw_pad, gb_pad, gn_pad)


def _sb_kernel(q_ref, k_ref, v_ref, u_ref, o_ref, acc_ref, carry_ref, hl_ref, z_ref, *, bq):
    i = pl.program_id(1)
    lane = lax.broadcasted_iota(jnp.int32, (bq, LANES), 1)
    low = lane < HEAD_DIM
    q = q_ref[...]
    zq = jnp.zeros_like(q)
    qm = (jnp.where(low, q, zq), jnp.where(low, zq, q))
    u2 = u_ref[...]
    acc_ref[...] = jnp.zeros_like(acc_ref)
    carry_ref[...] = jnp.zeros_like(carry_ref)
    nt = (((1,), (1,)), ((), ()))

    def strict_mask():
        rowi = lax.broadcasted_iota(jnp.int32, (bq, bq), 0)
        coli = lax.broadcasted_iota(jnp.int32, (bq, bq), 1)
        return coli < rowi

    def stage1(c, slot, diagonal=False):
        start = pl.multiple_of(c * bq, bq)
        kc = k_ref[pl.ds(start, bq), :]
        for h in range(2):
            z = lax.dot_general(qm[h], kc, nt, preferred_element_type=F32)
            nl = jnp.maximum(z, 0.0) + jnp.log(1.0 + jnp.exp(-jnp.abs(z)))
            if diagonal:
                nl = jnp.where(strict_mask(), nl, 0.0)
            hi, lo = _split_bf16(nl)
            hl_ref[slot, h, :, :bq] = hi
            hl_ref[slot, h, :, bq:] = lo
            z_ref[slot, h] = z

    def stage2(c, slot, diagonal=False):
        start = pl.multiple_of(c * bq, bq)
        vc = v_ref[pl.ds(start, bq), :]
        for h in range(2):
            cs = jnp.dot(hl_ref[slot, h], u2, preferred_element_type=F32)
            car = carry_ref[h]
            w = jnp.exp(z_ref[slot, h] - cs - jnp.concatenate([car, car], axis=1))
            if diagonal:
                w = jnp.where(strict_mask(), w, 0.0)
            acc_ref[h] += jnp.dot(w.astype(BF16), vc, preferred_element_type=F32)
            carry_ref[h] = car + jnp.broadcast_to(cs[:, 0:1], (bq, LANES))

    stage1(i, 0, diagonal=True)
    stage1(jnp.maximum(i - 1, 0), 1)
    stage2(i, 0, diagonal=True)

    def pair(jj, carry):
        c = i - 1 - 2 * jj
        stage1(jnp.maximum(c - 1, 0), 0)
        stage2(c, 1)
        stage1(jnp.maximum(c - 2, 0), 1)
        stage2(c - 1, 0)
        return carry

    lax.fori_loop(0, i // 2, pair, 0)

    @pl.when(i % 2 == 1)
    def _():
        stage2(0, 1)

    o_ref[...] = jnp.where(low, acc_ref[0], acc_ref[1]).astype(o_ref.dtype)


def sb_attention(proj, bq=256):
    s = proj.shape[0]
    bq = min(bq, s)
    qi = OFF_CQ // LANES
    ki = OFF_CK // LANES
    vi = OFF_CV // LANES
    u = (np.arange(bq)[:, None] >= np.arange(bq)[None, :]).astype(np.float32)
    u2 = jnp.asarray(np.concatenate([u, u], axis=0), dtype=BF16)
    return pl.pallas_call(
        functools.partial(_sb_kernel, bq=bq),
        grid=(SB_HEADS // 2, s // bq),
        in_specs=[pl.BlockSpec((bq, LANES), lambda p, i: (i, qi + p)),
                  pl.BlockSpec((s, LANES), lambda p, i: (0, ki + p)),
                  pl.BlockSpec((s, LANES), lambda p, i: (0, vi + p)),
                  pl.BlockSpec((2 * bq, bq), lambda p, i: (0, 0))],
        out_specs=pl.BlockSpec((bq, LANES), lambda p, i: (i, p)),
        out_shape=jax.ShapeDtypeStruct((s, SB_WIDTH), BF16),
        scratch_shapes=[pltpu.VMEM((2, bq, LANES), F32),
                        pltpu.VMEM((2, bq, LANES), F32),
                        pltpu.VMEM((2, 2, bq, 2 * bq), BF16),
                        pltpu.VMEM((2, 2, bq, bq), F32)],
        compiler_params=_cparams(("parallel", "arbitrary")),
        name="stickbreak",
    )(proj, proj, proj, u2)


def _oproj_kernel(a_ref, b_ref, c_ref, wa_ref, wb_ref, wc_ref, ga_ref, gc_ref,
                  x_ref, gn_ref, xo_ref, ho_ref):
    an = _rms(a_ref[...].astype(F32), ga_ref[...]).astype(BF16)
    cn = _rms(c_ref[...].astype(F32), gc_ref[...]).astype(BF16)
    y = (jnp.dot(an, wa_ref[...], preferred_element_type=F32)
         + jnp.dot(b_ref[...], wb_ref[...], preferred_element_type=F32)
         + jnp.dot(cn, wc_ref[...], preferred_element_type=F32))
    xn = x_ref[...] + y
    xo_ref[...] = xn
    ho_ref[...] = _rms(xn, gn_ref[...]).astype(ho_ref.dtype)


def out_projection(a, b, c, wa, wb, wc, ga, gc, x, gn, tm=256):
    s, d = x.shape
    tm = min(tm, s)
    row = lambda i: (i, 0)
    fixed = lambda i: (0, 0)
    return pl.pallas_call(
        _oproj_kernel,
        grid=(s // tm,),
        in_specs=[pl.BlockSpec((tm, a.shape[1]), row),
                  pl.BlockSpec((tm, b.shape[1]), row),
                  pl.BlockSpec((tm, c.shape[1]), row),
                  pl.BlockSpec(wa.shape, fixed),
                  pl.BlockSpec(wb.shape, fixed),
                  pl.BlockSpec(wc.shape, fixed),
                  pl.BlockSpec((1, a.shape[1]), fixed),
                  pl.BlockSpec((1, c.shape[1]), fixed),
                  pl.BlockSpec((tm, d), row),
                  pl.BlockSpec((1, d), fixed)],
        out_specs=[pl.BlockSpec((tm, d), row), pl.BlockSpec((tm, d), row)],
        out_shape=[jax.ShapeDtypeStruct((s, d), F32), jax.ShapeDtypeStruct((s, d), BF16)],
        compiler_params=_cparams(("parallel",)),
        name="out_proj",
    )(a, b, c, wa, wb, wc, ga, gc, x, gn)


def _mlp_kernel(h_ref, x_ref, w1_ref, w2_ref, gn_ref, *refs, final):
    if final:
        ho_ref, acc_ref = refs
    else:
        xo_ref, ho_ref, acc_ref = refs
    j = pl.program_id(1)
    u = jnp.dot(h_ref[...], w1_ref[...], preferred_element_type=F32)
    act = jnp.square(jnp.maximum(u, 0.0)).astype(BF16)
    contrib = jnp.dot(act, w2_ref[...], preferred_element_type=F32)

    @pl.when(j == 0)
    def _():
        acc_ref[...] = contrib

    @pl.when(j > 0)
    def _():
        acc_ref[...] += contrib

    @pl.when(j == pl.num_programs(1) - 1)
    def _():
        xn = x_ref[...] + acc_ref[...]
        if not final:
            xo_ref[...] = xn
        ho_ref[...] = _rms(xn, gn_ref[...]).astype(ho_ref.dtype)


def mlp_block(h, x, w1, w2, gn, final, tm=512, tf=512):
    s, d = x.shape
    ff = w1.shape[1]
    tm = min(tm, s)
    row = lambda i, j: (i, 0)
    out_specs = [pl.BlockSpec((tm, d), row)]
    out_shape = [jax.ShapeDtypeStruct((s, d), F32 if final else BF16)]
    if not final:
        out_specs = [pl.BlockSpec((tm, d), row)] + out_specs
        out_shape = [jax.ShapeDtypeStruct((s, d), F32)] + out_shape
    return pl.pallas_call(
        functools.partial(_mlp_kernel, final=final),
        grid=(s // tm, ff // tf),
        in_specs=[pl.BlockSpec((tm, d), row),
                  pl.BlockSpec((tm, d), row),
                  pl.BlockSpec((d, tf), lambda i, j: (0, j)),
                  pl.BlockSpec((tf, d), lambda i, j: (j, 0)),
                  pl.BlockSpec((1, d), lambda i, j: (0, 0))],
        out_specs=out_specs,
        out_shape=out_shape,
        scratch_shapes=[pltpu.VMEM((tm, d), F32)],
        compiler_params=_cparams(("parallel", "arbitrary")),
        name="mlp",
    )(h, x, w1, w2, gn)


def _pad_heads(w, heads, width, padded):
    lead = w.shape[:-1]
    w = w.reshape(lead + (heads, width))
    w = jnp.pad(w, [(0, 0)] * len(lead) + [(0, 0), (0, padded - width)])
    return w.reshape(lead + (heads * padded,))


def _permute_w_in(w):
    d = w.shape[0]
    seg = lambda off, width: w[:, off:off + width]
    aq = seg(_O_AQ, SWA_WIDTH).reshape(d, SWA_HEADS, HEAD_DIM)[:, np.array(SWA_Q_PERM)]
    aq = aq.reshape(d, SWA_WIDTH) * (HEAD_DIM ** -0.5)
    pieces = [
        _pad_heads(seg(_O_BV, GLA_HEADS * GLA_DV), GLA_HEADS, GLA_DV, GLA_DV_PAD),
        _pad_heads(seg(_O_BR, GLA_HEADS * GLA_DV), GLA_HEADS, GLA_DV, GLA_DV_PAD),
        _pad_heads(seg(_O_BQ, GLA_HEADS * GLA_DK), GLA_HEADS, GLA_DK, GLA_DK_PAD),
        _pad_heads(seg(_O_BK, GLA_HEADS * GLA_DK), GLA_HEADS, GLA_DK, GLA_DK_PAD),
        seg(_O_CQ, SB_WIDTH) * (HEAD_DIM ** -0.5),
        seg(_O_CK, SB_WIDTH),
        seg(_O_CV, SB_WIDTH),
        aq,
        seg(_O_AK, SWA_KV_WIDTH),
        seg(_O_AV, SWA_KV_WIDTH),
        jnp.pad(seg(_O_BLR, GLA_LOWRANK), ((0, 0), (0, GLA_LR_PAD - GLA_LOWRANK))),
        jnp.zeros((d, PROJ_PAD_WIDTH - OFF_BLR - GLA_LR_PAD), w.dtype),
    ]
    return jnp.concatenate(pieces, axis=1).astype(BF16)


def _permute_swa_rows(w):
    rest = w.shape[1:]
    return w.reshape((SWA_HEADS, HEAD_DIM) + rest)[np.array(SWA_Q_PERM)].reshape(w.shape)


def kernel(x, norm_mix, w_in, swa_sinks, rel_bias, gla_gate_w, gla_gate_b, gla_norm,
           swa_out_norm, sb_out_norm, w_out, norm_mlp, w_mlp_in, w_mlp_out, norm_final):
    depth = w_in.shape[0]
    xs = x[0]
    s, d = xs.shape
    bias_tab = swa_bias_table(rel_bias)
    h = rmsnorm_bf16(xs, norm_mix[0])
    out = None
    for l in range(depth):
        w_in_p = _permute_w_in(w_in[l])
        wo = w_out[l]
        wa = _permute_swa_rows(wo[:SWA_WIDTH]).astype(BF16)
        wb = wo[SWA_WIDTH:SWA_WIDTH + GLA_HEADS * GLA_DV].reshape(GLA_HEADS, GLA_DV, d)
        wb = jnp.pad(wb, ((0, 0), (0, GLA_DV_PAD - GLA_DV), (0, 0)))
        wb = wb.reshape(GLA_HEADS * GLA_DV_PAD, d).astype(BF16)
        wc = wo[SWA_WIDTH + GLA_HEADS * GLA_DV:].astype(BF16)
        ga = _permute_swa_rows(swa_out_norm[l]).reshape(1, SWA_WIDTH)
        gc = sb_out_norm[l].reshape(1, SB_WIDTH)
        sinks_p = swa_sinks[l][np.array(SWA_Q_PERM)]
        gw = _pad_heads(gla_gate_w[l], GLA_HEADS, GLA_DK, GLA_DK_PAD)
        gw = jnp.pad(gw, ((0, GLA_LR_PAD - GLA_LOWRANK), (0, 0)))
        gw = gw.reshape(GLA_LR_PAD, GLA_HEADS, GLA_DK_PAD).transpose(1, 0, 2).astype(BF16)
        gb = _pad_heads(gla_gate_b[l], GLA_HEADS, GLA_DK, GLA_DK_PAD)
        gb = gb.reshape(GLA_HEADS, 1, GLA_DK_PAD)
        gn = jnp.pad(gla_norm[l], (0, GLA_DV_PAD - GLA_DV)).reshape(1, GLA_DV_PAD)

        proj = in_projection(h, w_in_p)
        a = swa_attention(proj, sinks_p, bias_tab)
        b = gla_attention(proj, gw, gb, gn)
        c = sb_attention(proj)
        xs, hm = out_projection(a, b, c, wa, wb, wc, ga, gc, xs, norm_mlp[l].reshape(1, d))
        w1 = w_mlp_in[l].astype(BF16)
        w2 = w_mlp_out[l].astype(BF16)
        if l + 1 < depth:
            xs, h = mlp_block(hm, xs, w1, w2, norm_mix[l + 1].reshape(1, d), final=False)
        else:
            (out,) = mlp_block(hm, xs, w1, w2, norm_final.reshape(1, d), final=True)
    return out[None]
```

```python
import functools
import math

import numpy as np
import jax
import jax.numpy as jnp
from jax import lax
from jax.experimental import pallas as pl
from jax.experimental.pallas import tpu as pltpu

F32 = jnp.float32
BF16 = jnp.bfloat16

D_MODEL = 2048
HEAD_DIM = 64
SWA_HEADS = 12
SWA_KV_HEADS = 4
SWA_WIDTH = SWA_HEADS * HEAD_DIM
SWA_KV_WIDTH = SWA_KV_HEADS * HEAD_DIM
SWA_BLOCK = 128
WINDOW = 128
REL_BUCKETS = 32
REL_MAX_DIST = 128
GLA_HEADS = 4
GLA_DV = 192
GLA_DK = 96
GLA_DK_PAD = 128
GLA_DV_PAD = 256
GLA_LOWRANK = 16
GLA_LR_PAD = 128
GLA_CHUNK = 64
GATE_NORMALIZER = 16.0
GATE_LOG_MIN = -1.0
SB_HEADS = 8
SB_WIDTH = SB_HEADS * HEAD_DIM
D_FF = 4 * D_MODEL
RMS_EPS = 1e-6
LOG2E = 1.4426950408889634
NEG_INF = -1e30

LANES = 128
VMEM_LIMIT = 56 * 1024 * 1024

OFF_BV = 0
OFF_BR = 1024
OFF_BQ = 2048
OFF_BK = 2560
OFF_CQ = 3072
OFF_CK = 3584
OFF_CV = 4096
OFF_AQ = 4608
OFF_AK = 5376
OFF_AV = 5632
OFF_BLR = 5888
PROJ_PAD_WIDTH = 6144

SWA_Q_PERM = (0, 3, 1, 4, 2, 5, 6, 9, 7, 10, 8, 11)

_O_AQ, _O_AK, _O_AV = 0, 768, 1024
_O_BQ, _O_BK, _O_BV, _O_BR, _O_BLR = 1280, 1664, 2048, 2816, 3584
_O_CQ, _O_CK, _O_CV = 3600, 4112, 4624


def _cparams(sem):
    return pltpu.CompilerParams(dimension_semantics=sem, vmem_limit_bytes=VMEM_LIMIT)


def _rms(x, gain):
    ms = jnp.mean(x * x, axis=-1, keepdims=True)
    return x * lax.rsqrt(ms + RMS_EPS) * gain


def _split_bf16(x):
    hi = x.astype(BF16)
    lo = (x - hi.astype(F32)).astype(BF16)
    return hi, lo


def _rmsnorm_kernel(x_ref, g_ref, o_ref):
    o_ref[...] = _rms(x_ref[...], g_ref[...]).astype(o_ref.dtype)


def rmsnorm_bf16(x, gain, tm=512):
    s, d = x.shape
    return pl.pallas_call(
        _rmsnorm_kernel,
        grid=(s // tm,),
        in_specs=[pl.BlockSpec((tm, d), lambda i: (i, 0)),
                  pl.BlockSpec((1, d), lambda i: (0, 0))],
        out_specs=pl.BlockSpec((tm, d), lambda i: (i, 0)),
        out_shape=jax.ShapeDtypeStruct((s, d), BF16),
        compiler_params=_cparams(("parallel",)),
        name="rmsnorm",
    )(x, gain.reshape(1, d))


def _matmul_kernel(h_ref, w_ref, o_ref):
    o_ref[...] = jnp.dot(h_ref[...], w_ref[...],
                         preferred_element_type=F32).astype(o_ref.dtype)


def in_projection(h, w, tm=1024, tn=512):
    s, d = h.shape
    n = w.shape[1]
    tm = min(tm, s)
    return pl.pallas_call(
        _matmul_kernel,
        grid=(s // tm, n // tn),
        in_specs=[pl.BlockSpec((tm, d), lambda i, j: (i, 0)),
                  pl.BlockSpec((d, tn), lambda i, j: (0, j))],
        out_specs=pl.BlockSpec((tm, tn), lambda i, j: (i, j)),
        out_shape=jax.ShapeDtypeStruct((s, n), BF16),
        compiler_params=_cparams(("parallel", "arbitrary")),
        name="in_proj",
    )(h, w)


def _swa_kernel(sink_ref, q_ref, kp_ref, kc_ref, vp_ref, vc_ref, bias_ref, o_ref):
    i = pl.program_id(0)
    blk = SWA_BLOCK
    k = jnp.concatenate([kp_ref[...], kc_ref[...]], axis=0)
    v = jnp.concatenate([vp_ref[...], vc_ref[...]], axis=0)
    lane = lax.broadcasted_iota(jnp.int32, (blk, LANES), 1)
    col = lax.broadcasted_iota(jnp.int32, (blk, 2 * blk), 1)
    kill = jnp.logical_and(col < blk, i == 0)
    low = lane < HEAD_DIM
    for b in range(SWA_HEADS // 2):
        qb = q_ref[:, LANES * b:LANES * (b + 1)]
        kv_lo = LANES * (b // 3)
        kb = k[:, kv_lo:kv_lo + LANES]
        vb = v[:, kv_lo:kv_lo + LANES]
        outs = []
        for half in range(2):
            hq = 2 * b + half
            keep = low if half == 0 else jnp.logical_not(low)
            qm = jnp.where(keep, qb, jnp.zeros_like(qb))
            s = lax.dot_general(qm, kb, (((1,), (1,)), ((), ())),
                                preferred_element_type=F32)
            s = jnp.where(kill, NEG_INF, s + bias_ref[hq])
            sink = sink_ref[hq]
            m = jnp.maximum(jnp.max(s, axis=1, keepdims=True), sink)
            p = jnp.exp(s - m)
            denom = jnp.sum(p, axis=1, keepdims=True) + jnp.exp(sink - m)
            o = jnp.dot(p.astype(BF16), vb, preferred_element_type=F32)
            outs.append(o / denom)
        o_ref[:, LANES * b:LANES * (b + 1)] = jnp.where(low, outs[0], outs[1]).astype(o_ref.dtype)


def swa_attention(proj, sinks_perm, bias_tab):
    s = proj.shape[0]
    blk = SWA_BLOCK
    qi = OFF_AQ // SWA_WIDTH
    ki = OFF_AK // SWA_KV_WIDTH
    vi = OFF_AV // SWA_KV_WIDTH
    prev = lambda i: jnp.maximum(i - 1, 0)
    return pl.pallas_call(
        _swa_kernel,
        grid=(s // blk,),
        in_specs=[pl.BlockSpec(memory_space=pltpu.SMEM),
                  pl.BlockSpec((blk, SWA_WIDTH), lambda i: (i, qi)),
                  pl.BlockSpec((blk, SWA_KV_WIDTH), lambda i: (prev(i), ki)),
                  pl.BlockSpec((blk, SWA_KV_WIDTH), lambda i: (i, ki)),
                  pl.BlockSpec((blk, SWA_KV_WIDTH), lambda i: (prev(i), vi)),
                  pl.BlockSpec((blk, SWA_KV_WIDTH), lambda i: (i, vi)),
                  pl.BlockSpec((SWA_HEADS, blk, 2 * blk), lambda i: (0, 0, 0))],
        out_specs=pl.BlockSpec((blk, SWA_WIDTH), lambda i: (i, 0)),
        out_shape=jax.ShapeDtypeStruct((s, SWA_WIDTH), BF16),
        compiler_params=_cparams(("parallel",)),
        name="swa",
    )(sinks_perm, proj, proj, proj, proj, proj, bias_tab)


def _t5_causal_bucket(dist):
    max_exact = REL_BUCKETS // 2
    is_small = dist < max_exact
    ratio = (jnp.log(jnp.maximum(dist, 1).astype(F32) / max_exact)
             / math.log(REL_MAX_DIST / max_exact))
    large = max_exact + (ratio * (REL_BUCKETS - max_exact)).astype(jnp.int32)
    large = jnp.minimum(large, REL_BUCKETS - 1)
    return jnp.where(is_small, dist, large)


def swa_bias_table(rel_bias):
    blk = SWA_BLOCK
    qpos = jnp.arange(blk) + blk
    kpos = jnp.arange(2 * blk)
    dist = qpos[:, None] - kpos[None, :]
    in_window = (dist >= 0) & (dist < WINDOW)
    bias = rel_bias.astype(F32)[_t5_causal_bucket(jnp.maximum(dist, 0))]
    bias = bias.transpose(2, 0, 1)[jnp.array(SWA_Q_PERM)]
    return jnp.where(in_window[None], bias, NEG_INF)


def _gla_kernel(q_ref, k_ref, v_ref, r_ref, lr_ref, gw_ref, gb_ref, gn_ref, o_ref,
                st_ref, *, rows):
    t = pl.program_id(1)
    ch = GLA_CHUNK

    @pl.when(t == 0)
    def _():
        st_ref[...] = jnp.zeros_like(st_ref)

    gw = gw_ref[...]
    gb = gb_ref[...]
    gn = gn_ref[...]
    row = lax.broadcasted_iota(jnp.int32, (ch, ch), 0)
    colm = lax.broadcasted_iota(jnp.int32, (ch, ch), 1)
    causal = colm <= row
    ltri = jnp.where(causal, 1.0, 0.0).astype(BF16)
    st = st_ref[...]
    nt = (((1,), (1,)), ((), ()))
    tn = (((0,), (0,)), ((), ()))
    for c in range(rows // ch):
        sl = pl.ds(c * ch, ch)
        gp = jnp.dot(lr_ref[sl, :], gw, preferred_element_type=F32) + gb
        log_sig = jnp.minimum(gp, 0.0) - jnp.log(1.0 + jnp.exp(-jnp.abs(gp)))
        lg = jnp.maximum(log_sig * (1.0 / GATE_NORMALIZER), GATE_LOG_MIN)
        hi, lo = _split_bf16(lg)
        b = (jnp.dot(ltri, hi, preferred_element_type=F32)
             + jnp.dot(ltri, lo, preferred_element_type=F32))
        b_last = b[ch - 1:ch, :]
        q = q_ref[sl, :].astype(F32) * (GLA_DK ** -0.5)
        kk = k_ref[sl, :].astype(F32)
        q_dec = (q * jnp.exp(b)).astype(BF16)
        k_inv = (kk * jnp.exp(-b)).astype(BF16)
        k_end = (kk * jnp.exp(b_last - b)).astype(BF16)
        v = v_ref[sl, :]
        sc = lax.dot_general(q_dec, k_inv, nt, preferred_element_type=F32)
        sc = jnp.where(causal, sc, 0.0).astype(BF16)
        o = (jnp.dot(sc, v, preferred_element_type=F32)
             + lax.dot_general(q_dec, st.astype(BF16), nt, preferred_element_type=F32))
        d_st = lax.dot_general(v, k_end, tn, preferred_element_type=F32)
        st = st * jnp.exp(b_last) + d_st
        ms = jnp.sum(o * o, axis=1, keepdims=True) * (1.0 / GLA_DV)
        y = o * lax.rsqrt(ms + RMS_EPS) * gn
        rr = r_ref[sl, :].astype(F32)
        o_ref[sl, :] = (y * (rr / (1.0 + jnp.exp(-rr)))).astype(o_ref.dtype)
    st_ref[...] = st


def gla_attention(proj, gw_pad, gb_pad, gn_pad, rows=256):
    s = proj.shape[0]
    rows = min(rows, s)
    qi = OFF_BQ // GLA_DK_PAD
    ki = OFF_BK // GLA_DK_PAD
    vi = OFF_BV // GLA_DV_PAD
    ri = OFF_BR // GLA_DV_PAD
    li = OFF_BLR // GLA_LR_PAD
    return pl.pallas_call(
        functools.partial(_gla_kernel, rows=rows),
        grid=(GLA_HEADS, s // rows),
        in_specs=[pl.BlockSpec((rows, GLA_DK_PAD), lambda h, t: (t, qi + h)),
                  pl.BlockSpec((rows, GLA_DK_PAD), lambda h, t: (t, ki + h)),
                  pl.BlockSpec((rows, GLA_DV_PAD), lambda h, t: (t, vi + h)),
                  pl.BlockSpec((rows, GLA_DV_PAD), lambda h, t: (t, ri + h)),
                  pl.BlockSpec((rows, GLA_LR_PAD), lambda h, t: (t, li)),
                  pl.BlockSpec((None, GLA_LR_PAD, GLA_DK_PAD), lambda h, t: (h, 0, 0)),
                  pl.BlockSpec((None, 1, GLA_DK_PAD), lambda h, t: (h, 0, 0)),
                  pl.BlockSpec((1, GLA_DV_PAD), lambda h, t: (0, 0))],
        out_specs=pl.BlockSpec((rows, GLA_DV_PAD), lambda h, t: (t, h)),
        out_shape=jax.ShapeDtypeStruct((s, GLA_HEADS * GLA_DV_PAD), BF16),
        scratch_shapes=[pltpu.VMEM((GLA_DV_PAD, GLA_DK_PAD), F32)],
        compiler_params=_cparams(("parallel", "arbitrary")),
        name="gla",
    )(proj, proj, proj, proj, proj, gw_pad, gb_pad, gn_pad)


def _sb_kernel(q_ref, k_ref, v_ref, u_ref, o_ref, acc_ref, car_ref,
               z0_ref, z1_ref, nl0_ref, nl1_ref, zc0_ref, zc1_ref, w0_ref, w1_ref,
               *, bq, nblk, unroll):
    lane = lax.broadcasted_iota(jnp.int32, (bq, LANES), 1)
    low = lane < HEAD_DIM
    nt = (((1,), (1,)), ((), ()))
    z_ref, nl_ref = (z0_ref, z1_ref), (nl0_ref, nl1_ref)
    zc_ref, w_ref = (zc0_ref, zc1_ref), (w0_ref, w1_ref)

    def rows(blk):
        if isinstance(blk, int):
            return pl.ds(blk * bq, bq)
        return pl.ds(pl.multiple_of(blk * bq, bq), bq)

    def strict_mask():
        rowi = lax.broadcasted_iota(jnp.int32, (bq, bq), 0)
        coli = lax.broadcasted_iota(jnp.int32, (bq, bq), 1)
        return coli < rowi

    def s1a(item, slot):
        i, c = item
        qb = q_ref[rows(i), :]
        kc = k_ref[rows(c), :]
        zq = jnp.zeros_like(qb)
        for h in range(2):
            qm = jnp.where(low, qb, zq) if h == 0 else jnp.where(low, zq, qb)
            z_ref[slot][h] = lax.dot_general(qm, kc, nt, preferred_element_type=F32)

    def s1b(slot, masked):
        for h in range(2):
            z = z_ref[slot][h]
            nl = jnp.maximum(z, 0.0) + jnp.log(1.0 + jnp.exp2(jnp.abs(z) * (-LOG2E)))
            if masked:
                nl = jnp.where(strict_mask(), nl, 0.0)
            nl_ref[slot][h] = nl.astype(BF16)
            zc_ref[slot][h] = z

    def s2(item, slot, masked, first):
        i, _ = item
        for h in range(2):
            cs = jnp.dot(nl_ref[slot][h], u_ref[...], preferred_element_type=F32)
            tot = jnp.broadcast_to(cs[:, 0:1], (bq, LANES))
            if first:
                arg = zc_ref[slot][h] - cs
                car_ref[h, rows(i), :] = tot
            else:
                car = car_ref[h, rows(i), :]
                arg = zc_ref[slot][h] - cs - jnp.concatenate([car, car], axis=1)
                car_ref[h, rows(i), :] = car + tot
            w = jnp.exp(arg)
            if masked:
                w = jnp.where(strict_mask(), w, 0.0)
            w_ref[slot][h] = w.astype(BF16)

    def s3(item, slot, first):
        i, c = item
        vc = v_ref[rows(c), :]
        for h in range(2):
            pv = jnp.dot(w_ref[slot][h], vc, preferred_element_type=F32)
            if first:
                acc_ref[h, rows(i), :] = pv
            else:
                acc_ref[h, rows(i), :] += pv

    def run_pipeline(items, step, masked, first):
        n = len(items)

        def tick(par, its, do):
            if do[1]:
                s1b(1 - par, masked)
            if do[0]:
                s1a(its[0], par)
            if do[2]:
                s2(its[2], par, masked, first)
            if do[3]:
                s3(its[3], 1 - par, first)

        def static_tick(t):
            do = [0 <= t - j < n for j in range(4)]
            its = [items[t - j] if do[j] else None for j in range(4)]
            tick(t % 2, its, do)

        lo_t, hi_t = 3, n - 1
        groups = max(hi_t - lo_t + 1, 0) // unroll
        if groups < 2:
            groups = 0
        for t in range(min(lo_t, n + 3)):
            static_tick(t)
        if groups:
            def body(_, carry):
                its = [(carry[2 * j], carry[2 * j + 1]) for j in range(4)]
                for k in range(unroll):
                    tick((lo_t + k) % 2, its, [True] * 4)
                    its = [step(*its[0])] + its[:3]
                return tuple(x for it in its for x in it)

            init = tuple(jnp.int32(x) for j in range(4) for x in items[lo_t - j])
            lax.fori_loop(0, groups, body, init)
        for t in range(lo_t + unroll * groups, n + 3):
            static_tick(t)

    diag = [(i, i) for i in range(nblk)]
    run_pipeline(diag, lambda i, c: (i + 1, c + 1), masked=True, first=True)

    off = [(i, c) for i in range(1, nblk) for c in range(i - 1, -1, -1)]

    def off_step(i, c):
        wrap = c == 0
        i2 = jnp.where(wrap, i + 1, i)
        return i2, jnp.where(wrap, i2 - 1, c - 1)

    if off:
        run_pipeline(off, off_step, masked=False, first=False)

    for blk in range(nblk):
        r = rows(blk)
        o_ref[r, :] = jnp.where(low, acc_ref[0, r, :], acc_ref[1, r, :]).astype(o_ref.dtype)


def sb_attention(proj, bq=256, unroll=4):
    s = proj.shape[0]
    bq = min(bq, s)
    nblk = s // bq
    qi = OFF_CQ // LANES
    ki = OFF_CK // LANES
    vi = OFF_CV // LANES
    u = (np.arange(bq)[:, None] >= np.arange(bq)[None, :]).astype(np.float32)
    u = jnp.asarray(u, dtype=BF16)
    return pl.pallas_call(
        functools.partial(_sb_kernel, bq=bq, nblk=nblk, unroll=unroll),
        grid=(SB_HEADS // 2,),
        in_specs=[pl.BlockSpec((s, LANES), lambda p: (0, qi + p)),
                  pl.BlockSpec((s, LANES), lambda p: (0, ki + p)),
                  pl.BlockSpec((s, LANES), lambda p: (0, vi + p)),
                  pl.BlockSpec((bq, bq), lambda p: (0, 0))],
        out_specs=pl.BlockSpec((s, LANES), lambda p: (0, p)),
        out_shape=jax.ShapeDtypeStruct((s, SB_WIDTH), BF16),
        scratch_shapes=[pltpu.VMEM((2, s, LANES), F32),
                        pltpu.VMEM((2, s, LANES), F32),
                        pltpu.VMEM((2, bq, bq), F32),
                        pltpu.VMEM((2, bq, bq), F32),
                        pltpu.VMEM((2, bq, bq), BF16),
                        pltpu.VMEM((2, bq, bq), BF16),
                        pltpu.VMEM((2, bq, bq), F32),
                        pltpu.VMEM((2, bq, bq), F32),
                        pltpu.VMEM((2, bq, bq), BF16),
                        pltpu.VMEM((2, bq, bq), BF16)],
        compiler_params=_cparams(("parallel",)),
        name="stickbreak",
    )(proj, proj, proj, u)


def _oproj_kernel(a_ref, b_ref, c_ref, wa_ref, wb_ref, wc_ref, ga_ref, gc_ref,
                  x_ref, gn_ref, xo_ref, ho_ref):
    an = _rms(a_ref[...].astype(F32), ga_ref[...]).astype(BF16)
    cn = _rms(c_ref[...].astype(F32), gc_ref[...]).astype(BF16)
    y = (jnp.dot(an, wa_ref[...], preferred_element_type=F32)
         + jnp.dot(b_ref[...], wb_ref[...], preferred_element_type=F32)
         + jnp.dot(cn, wc_ref[...], preferred_element_type=F32))
    xn = x_ref[...] + y
    xo_ref[...] = xn
    ho_ref[...] = _rms(xn, gn_ref[...]).astype(ho_ref.dtype)


def out_projection(a, b, c, wa, wb, wc, ga, gc, x, gn, tm=256):
    s, d = x.shape
    tm = min(tm, s)
    row = lambda i: (i, 0)
    fixed = lambda i: (0, 0)
    return pl.pallas_call(
        _oproj_kernel,
        grid=(s // tm,),
        in_specs=[pl.BlockSpec((tm, a.shape[1]), row),
                  pl.BlockSpec((tm, b.shape[1]), row),
                  pl.BlockSpec((tm, c.shape[1]), row),
                  pl.BlockSpec(wa.shape, fixed),
                  pl.BlockSpec(wb.shape, fixed),
                  pl.BlockSpec(wc.shape, fixed),
                  pl.BlockSpec((1, a.shape[1]), fixed),
                  pl.BlockSpec((1, c.shape[1]), fixed),
                  pl.BlockSpec((tm, d), row),
                  pl.BlockSpec((1, d), fixed)],
        out_specs=[pl.BlockSpec((tm, d), row), pl.BlockSpec((tm, d), row)],
        out_shape=[jax.ShapeDtypeStruct((s, d), F32), jax.ShapeDtypeStruct((s, d), BF16)],
        compiler_params=_cparams(("parallel",)),
        name="out_proj",
    )(a, b, c, wa, wb, wc, ga, gc, x, gn)


def _mlp_kernel(h_ref, x_ref, w1_ref, w2_ref, gn_ref, *refs, final):
    if final:
        ho_ref, acc_ref = refs
    else:
        xo_ref, ho_ref, acc_ref = refs
    j = pl.program_id(1)
    u = jnp.dot(h_ref[...], w1_ref[...], preferred_element_type=F32)
    act = jnp.square(jnp.maximum(u, 0.0)).astype(BF16)
    contrib = jnp.dot(act, w2_ref[...], preferred_element_type=F32)

    @pl.when(j == 0)
    def _():
        acc_ref[...] = contrib

    @pl.when(j > 0)
    def _():
        acc_ref[...] += contrib

    @pl.when(j == pl.num_programs(1) - 1)
    def _():
        xn = x_ref[...] + acc_ref[...]
        if not final:
            xo_ref[...] = xn
        ho_ref[...] = _rms(xn, gn_ref[...]).astype(ho_ref.dtype)


def mlp_block(h, x, w1, w2, gn, final, tm=512, tf=512):
    s, d = x.shape
    ff = w1.shape[1]
    tm = min(tm, s)
    row = lambda i, j: (i, 0)
    out_specs = [pl.BlockSpec((tm, d), row)]
    out_shape = [jax.ShapeDtypeStruct((s, d), F32 if final else BF16)]
    if not final:
        out_specs = [pl.BlockSpec((tm, d), row)] + out_specs
        out_shape = [jax.ShapeDtypeStruct((s, d), F32)] + out_shape
    return pl.pallas_call(
        functools.partial(_mlp_kernel, final=final),
        grid=(s // tm, ff // tf),
        in_specs=[pl.BlockSpec((tm, d), row),
                  pl.BlockSpec((tm, d), row),
                  pl.BlockSpec((d, tf), lambda i, j: (0, j)),
                  pl.BlockSpec((tf, d), lambda i, j: (j, 0)),
                  pl.BlockSpec((1, d), lambda i, j: (0, 0))],
        out_specs=out_specs,
        out_shape=out_shape,
        scratch_shapes=[pltpu.VMEM((tm, d), F32)],
        compiler_params=_cparams(("parallel", "arbitrary")),
        name="mlp",
    )(h, x, w1, w2, gn)


def _pad_heads(w, heads, width, padded):
    lead = w.shape[:-1]
    w = w.reshape(lead + (heads, width))
    w = jnp.pad(w, [(0, 0)] * len(lead) + [(0, 0), (0, padded - width)])
    return w.reshape(lead + (heads * padded,))


def _permute_w_in(w):
    d = w.shape[0]
    seg = lambda off, width: w[:, off:off + width]
    aq = seg(_O_AQ, SWA_WIDTH).reshape(d, SWA_HEADS, HEAD_DIM)[:, np.array(SWA_Q_PERM)]
    aq = aq.reshape(d, SWA_WIDTH) * (HEAD_DIM ** -0.5)
    pieces = [
        _pad_heads(seg(_O_BV, GLA_HEADS * GLA_DV), GLA_HEADS, GLA_DV, GLA_DV_PAD),
        _pad_heads(seg(_O_BR, GLA_HEADS * GLA_DV), GLA_HEADS, GLA_DV, GLA_DV_PAD),
        _pad_heads(seg(_O_BQ, GLA_HEADS * GLA_DK), GLA_HEADS, GLA_DK, GLA_DK_PAD),
        _pad_heads(seg(_O_BK, GLA_HEADS * GLA_DK), GLA_HEADS, GLA_DK, GLA_DK_PAD),
        seg(_O_CQ, SB_WIDTH) * (HEAD_DIM ** -0.5),
        seg(_O_CK, SB_WIDTH),
        seg(_O_CV, SB_WIDTH),
        aq,
        seg(_O_AK, SWA_KV_WIDTH),
        seg(_O_AV, SWA_KV_WIDTH),
        jnp.pad(seg(_O_BLR, GLA_LOWRANK), ((0, 0), (0, GLA_LR_PAD - GLA_LOWRANK))),
        jnp.zeros((d, PROJ_PAD_WIDTH - OFF_BLR - GLA_LR_PAD), w.dtype),
    ]
    return jnp.concatenate(pieces, axis=1).astype(BF16)


def _permute_swa_rows(w):
    rest = w.shape[1:]
    return w.reshape((SWA_HEADS, HEAD_DIM) + rest)[np.array(SWA_Q_PERM)].reshape(w.shape)


def kernel(x, norm_mix, w_in, swa_sinks, rel_bias, gla_gate_w, gla_gate_b, gla_norm,
           swa_out_norm, sb_out_norm, w_out, norm_mlp, w_mlp_in, w_mlp_out, norm_final):
    depth = w_in.shape[0]
    xs = x[0]
    s, d = xs.shape
    bias_tab = swa_bias_table(rel_bias)
    h = rmsnorm_bf16(xs, norm_mix[0])
    out = None
    for l in range(depth):
        w_in_p = _permute_w_in(w_in[l])
        wo = w_out[l]
        wa = _permute_swa_rows(wo[:SWA_WIDTH]).astype(BF16)
        wb = wo[SWA_WIDTH:SWA_WIDTH + GLA_HEADS * GLA_DV].reshape(GLA_HEADS, GLA_DV, d)
        wb = jnp.pad(wb, ((0, 0), (0, GLA_DV_PAD - GLA_DV), (0, 0)))
        wb = wb.reshape(GLA_HEADS * GLA_DV_PAD, d).astype(BF16)
        wc = wo[SWA_WIDTH + GLA_HEADS * GLA_DV:].astype(BF16)
        ga = _permute_swa_rows(swa_out_norm[l]).reshape(1, SWA_WIDTH)
        gc = sb_out_norm[l].reshape(1, SB_WIDTH)
        sinks_p = swa_sinks[l][np.array(SWA_Q_PERM)]
        gw = _pad_heads(gla_gate_w[l], GLA_HEADS, GLA_DK, GLA_DK_PAD)
        gw = jnp.pad(gw, ((0, GLA_LR_PAD - GLA_LOWRANK), (0, 0)))
        gw = gw.reshape(GLA_LR_PAD, GLA_HEADS, GLA_DK_PAD).transpose(1, 0, 2).astype(BF16)
        gb = _pad_heads(gla_gate_b[l], GLA_HEADS, GLA_DK, GLA_DK_PAD)
        gb = gb.reshape(GLA_HEADS, 1, GLA_DK_PAD)
        gn = jnp.pad(gla_norm[l], (0, GLA_DV_PAD - GLA_DV)).reshape(1, GLA_DV_PAD)

        proj = in_projection(h, w_in_p)
        a = swa_attention(proj, sinks_p, bias_tab)
        b = gla_attention(proj, gw, gb, gn)
        c = sb_attention(proj)
        xs, hm = out_projection(a, b, c, wa, wb, wc, ga, gc, xs, norm_mlp[l].reshape(1, d))
        w1 = w_mlp_in[l].astype(BF16)
        w2 = w_mlp_out[l].astype(BF16)
        if l + 1 < depth:
            xs, h = mlp_block(hm, xs, w1, w2, norm_mix[l + 1].reshape(1, d), final=False)
        else:
            (out,) = mlp_block(hm, xs, w1, w2, norm_final.reshape(1, d), final=True)
    return out[None]
```

```python
import functools
import math

import numpy as np
import jax
import jax.numpy as jnp
from jax import lax
from jax.experimental import pallas as pl
from jax.experimental.pallas import tpu as pltpu

F32 = jnp.float32
BF16 = jnp.bfloat16

D_MODEL = 2048
HEAD_DIM = 64
SWA_HEADS = 12
SWA_KV_HEADS = 4
SWA_WIDTH = SWA_HEADS * HEAD_DIM
SWA_KV_WIDTH = SWA_KV_HEADS * HEAD_DIM
SWA_BLOCK = 128
WINDOW = 128
REL_BUCKETS = 32
REL_MAX_DIST = 128
GLA_HEADS = 4
GLA_DV = 192
GLA_DK = 96
GLA_DK_PAD = 128
GLA_DV_PAD = 256
GLA_LOWRANK = 16
GLA_LR_PAD = 128
GLA_CHUNK = 64
GATE_NORMALIZER = 16.0
GATE_LOG_MIN = -1.0
SB_HEADS = 8
SB_WIDTH = SB_HEADS * HEAD_DIM
D_FF = 4 * D_MODEL
RMS_EPS = 1e-6
LOG2E = 1.4426950408889634
NEG_INF = -1e30

LANES = 128
VMEM_LIMIT = 56 * 1024 * 1024

OFF_BV = 0
OFF_BR = 1024
OFF_BQ = 2048
OFF_BK = 2560
OFF_CQ = 3072
OFF_CK = 3584
OFF_CV = 4096
OFF_AQ = 4608
OFF_AK = 5376
OFF_AV = 5632
OFF_BLR = 5888
PROJ_PAD_WIDTH = 6144

SWA_Q_PERM = (0, 3, 1, 4, 2, 5, 6, 9, 7, 10, 8, 11)

_O_AQ, _O_AK, _O_AV = 0, 768, 1024
_O_BQ, _O_BK, _O_BV, _O_BR, _O_BLR = 1280, 1664, 2048, 2816, 3584
_O_CQ, _O_CK, _O_CV = 3600, 4112, 4624


def _cparams(sem):
    return pltpu.CompilerParams(dimension_semantics=sem, vmem_limit_bytes=VMEM_LIMIT)


def _rms(x, gain):
    ms = jnp.mean(x * x, axis=-1, keepdims=True)
    return x * lax.rsqrt(ms + RMS_EPS) * gain


def _split_bf16(x):
    hi = x.astype(BF16)
    lo = (x - hi.astype(F32)).astype(BF16)
    return hi, lo


def _matmul_kernel(h_ref, w_ref, o_ref):
    o_ref[...] = jnp.dot(h_ref[...], w_ref[...],
                         preferred_element_type=F32).astype(o_ref.dtype)


def _norm_matmul_kernel(x_ref, g_ref, w_ref, o_ref, h_ref):
    @pl.when(pl.program_id(1) == 0)
    def _():
        h_ref[...] = _rms(x_ref[...], g_ref[...]).astype(h_ref.dtype)

    o_ref[...] = jnp.dot(h_ref[...], w_ref[...],
                         preferred_element_type=F32).astype(o_ref.dtype)


def in_projection(h, w_all, layer, gain=None, tm=1024, tn=512):
    s, d = h.shape
    n = w_all.shape[2]
    tm = min(tm, s)
    w_spec = pl.BlockSpec((None, d, tn), lambda i, j: (layer, 0, j))
    h_spec = pl.BlockSpec((tm, d), lambda i, j: (i, 0))
    common = dict(
        grid=(s // tm, n // tn),
        out_specs=pl.BlockSpec((tm, tn), lambda i, j: (i, j)),
        out_shape=jax.ShapeDtypeStruct((s, n), BF16),
        compiler_params=_cparams(("parallel", "arbitrary")),
        name="in_proj",
    )
    if gain is None:
        return pl.pallas_call(_matmul_kernel, in_specs=[h_spec, w_spec], **common)(h, w_all)
    return pl.pallas_call(
        _norm_matmul_kernel,
        in_specs=[h_spec, pl.BlockSpec((1, d), lambda i, j: (0, 0)), w_spec],
        scratch_shapes=[pltpu.VMEM((tm, d), BF16)],
        **common,
    )(h, gain.reshape(1, d), w_all)


def _swa_kernel(sink_ref, q_ref, kp_ref, kc_ref, vp_ref, vc_ref, bias_ref, o_ref, *, layer):
    i = pl.program_id(0)
    blk = SWA_BLOCK
    k = jnp.concatenate([kp_ref[...], kc_ref[...]], axis=0)
    v = jnp.concatenate([vp_ref[...], vc_ref[...]], axis=0)
    lane = lax.broadcasted_iota(jnp.int32, (blk, LANES), 1)
    col = lax.broadcasted_iota(jnp.int32, (blk, 2 * blk), 1)
    kill = jnp.logical_and(col < blk, i == 0)
    low = lane < HEAD_DIM
    for b in range(SWA_HEADS // 2):
        qb = q_ref[:, LANES * b:LANES * (b + 1)]
        kv_lo = LANES * (b // 3)
        kb = k[:, kv_lo:kv_lo + LANES]
        vb = v[:, kv_lo:kv_lo + LANES]
        outs = []
        for half in range(2):
            hq = 2 * b + half
            keep = low if half == 0 else jnp.logical_not(low)
            qm = jnp.where(keep, qb, jnp.zeros_like(qb))
            s = lax.dot_general(qm, kb, (((1,), (1,)), ((), ())),
                                preferred_element_type=F32)
            s = jnp.where(kill, NEG_INF, s + bias_ref[hq])
            sink = sink_ref[layer, hq]
            m = jnp.maximum(jnp.max(s, axis=1, keepdims=True), sink)
            p = jnp.exp(s - m)
            denom = jnp.sum(p, axis=1, keepdims=True) + jnp.exp(sink - m)
            o = jnp.dot(p.astype(BF16), vb, preferred_element_type=F32)
            outs.append(o / denom)
        o_ref[:, LANES * b:LANES * (b + 1)] = jnp.where(low, outs[0], outs[1]).astype(o_ref.dtype)


def swa_attention(proj, sinks_perm, bias_tab, layer):
    s = proj.shape[0]
    blk = SWA_BLOCK
    qi = OFF_AQ // SWA_WIDTH
    ki = OFF_AK // SWA_KV_WIDTH
    vi = OFF_AV // SWA_KV_WIDTH
    prev = lambda i: jnp.maximum(i - 1, 0)
    return pl.pallas_call(
        functools.partial(_swa_kernel, layer=layer),
        grid=(s // blk,),
        in_specs=[pl.BlockSpec(memory_space=pltpu.SMEM),
                  pl.BlockSpec((blk, SWA_WIDTH), lambda i: (i, qi)),
                  pl.BlockSpec((blk, SWA_KV_WIDTH), lambda i: (prev(i), ki)),
                  pl.BlockSpec((blk, SWA_KV_WIDTH), lambda i: (i, ki)),
                  pl.BlockSpec((blk, SWA_KV_WIDTH), lambda i: (prev(i), vi)),
                  pl.BlockSpec((blk, SWA_KV_WIDTH), lambda i: (i, vi)),
                  pl.BlockSpec((SWA_HEADS, blk, 2 * blk), lambda i: (0, 0, 0))],
        out_specs=pl.BlockSpec((blk, SWA_WIDTH), lambda i: (i, 0)),
        out_shape=jax.ShapeDtypeStruct((s, SWA_WIDTH), BF16),
        compiler_params=_cparams(("parallel",)),
        name="swa",
    )(sinks_perm, proj, proj, proj, proj, proj, bias_tab)


def _t5_causal_bucket(dist):
    max_exact = REL_BUCKETS // 2
    is_small = dist < max_exact
    ratio = (jnp.log(jnp.maximum(dist, 1).astype(F32) / max_exact)
             / math.log(REL_MAX_DIST / max_exact))
    large = max_exact + (ratio * (REL_BUCKETS - max_exact)).astype(jnp.int32)
    large = jnp.minimum(large, REL_BUCKETS - 1)
    return jnp.where(is_small, dist, large)


def swa_bias_table(rel_bias):
    blk = SWA_BLOCK
    qpos = jnp.arange(blk) + blk
    kpos = jnp.arange(2 * blk)
    dist = qpos[:, None] - kpos[None, :]
    in_window = (dist >= 0) & (dist < WINDOW)
    bucket = _t5_causal_bucket(jnp.maximum(dist, 0))
    table = rel_bias.astype(F32).T
    table = jnp.stack([table[h] for h in SWA_Q_PERM])
    bias = jnp.take(table, bucket, axis=1)
    return jnp.where(in_window[None], bias, NEG_INF)


def _gla_kernel(q_ref, k_ref, v_ref, r_ref, lr_ref, gw_ref, gb_ref, gn_ref, o_ref,
                st_ref, *, rows):
    t = pl.program_id(1)
    ch = GLA_CHUNK

    @pl.when(t == 0)
    def _():
        st_ref[...] = jnp.zeros_like(st_ref)

    gw = gw_ref[...]
    gb = gb_ref[...]
    gn = gn_ref[...]
    row = lax.broadcasted_iota(jnp.int32, (ch, ch), 0)
    colm = lax.broadcasted_iota(jnp.int32, (ch, ch), 1)
    causal = colm <= row
    ltri = jnp.where(causal, 1.0, 0.0).astype(BF16)
    st = st_ref[...]
    nt = (((1,), (1,)), ((), ()))
    tn = (((0,), (0,)), ((), ()))
    for c in range(rows // ch):
        sl = pl.ds(c * ch, ch)
        gp = jnp.dot(lr_ref[sl, :], gw, preferred_element_type=F32) + gb
        log_sig = jnp.minimum(gp, 0.0) - jnp.log(1.0 + jnp.exp(-jnp.abs(gp)))
        lg = jnp.maximum(log_sig * (1.0 / GATE_NORMALIZER), GATE_LOG_MIN)
        hi, lo = _split_bf16(lg)
        b = (jnp.dot(ltri, hi, preferred_element_type=F32)
             + jnp.dot(ltri, lo, preferred_element_type=F32))
        b_last = b[ch - 1:ch, :]
        q = q_ref[sl, :].astype(F32) * (GLA_DK ** -0.5)
        kk = k_ref[sl, :].astype(F32)
        q_dec = (q * jnp.exp(b)).astype(BF16)
        k_inv = (kk * jnp.exp(-b)).astype(BF16)
        k_end = (kk * jnp.exp(b_last - b)).astype(BF16)
        v = v_ref[sl, :]
        sc = lax.dot_general(q_dec, k_inv, nt, preferred_element_type=F32)
        sc = jnp.where(causal, sc, 0.0).astype(BF16)
        o = (jnp.dot(sc, v, preferred_element_type=F32)
             + lax.dot_general(q_dec, st.astype(BF16), nt, preferred_element_type=F32))
        d_st = lax.dot_general(v, k_end, tn, preferred_element_type=F32)
        st = st * jnp.exp(b_last) + d_st
        ms = jnp.sum(o * o, axis=1, keepdims=True) * (1.0 / GLA_DV)
        y = o * lax.rsqrt(ms + RMS_EPS) * gn
        rr = r_ref[sl, :].astype(F32)
        o_ref[sl, :] = (y * (rr / (1.0 + jnp.exp(-rr)))).astype(o_ref.dtype)
    st_ref[...] = st


def gla_attention(proj, gw_pad, gb_pad, gn_pad, layer, rows=256):
    s = proj.shape[0]
    rows = min(rows, s)
    qi = OFF_BQ // GLA_DK_PAD
    ki = OFF_BK // GLA_DK_PAD
    vi = OFF_BV // GLA_DV_PAD
    ri = OFF_BR // GLA_DV_PAD
    li = OFF_BLR // GLA_LR_PAD
    return pl.pallas_call(
        functools.partial(_gla_kernel, rows=rows),
        grid=(GLA_HEADS, s // rows),
        in_specs=[pl.BlockSpec((rows, GLA_DK_PAD), lambda h, t: (t, qi + h)),
                  pl.BlockSpec((rows, GLA_DK_PAD), lambda h, t: (t, ki + h)),
                  pl.BlockSpec((rows, GLA_DV_PAD), lambda h, t: (t, vi + h)),
                  pl.BlockSpec((rows, GLA_DV_PAD), lambda h, t: (t, ri + h)),
                  pl.BlockSpec((rows, GLA_LR_PAD), lambda h, t: (t, li)),
                  pl.BlockSpec((None, None, GLA_LR_PAD, GLA_DK_PAD),
                               lambda h, t: (layer, h, 0, 0)),
                  pl.BlockSpec((None, None, 1, GLA_DK_PAD), lambda h, t: (layer, h, 0, 0)),
                  pl.BlockSpec((None, 1, GLA_DV_PAD), lambda h, t: (layer, 0, 0))],
        out_specs=pl.BlockSpec((rows, GLA_DV_PAD), lambda h, t: (t, h)),
        out_shape=jax.ShapeDtypeStruct((s, GLA_HEADS * GLA_DV_PAD), BF16),
        scratch_shapes=[pltpu.VMEM((GLA_DV_PAD, GLA_DK_PAD), F32)],
        compiler_params=_cparams(("parallel", "arbitrary")),
        name="gla",
    )(proj, proj, proj, proj, proj, gw_pad, gb_pad, gn_pad)


def _sb_kernel(q_ref, k_ref, v_ref, u_ref, o_ref, acc_ref, car_ref,
               z0_ref, z1_ref, nl0_ref, nl1_ref, zc0_ref, zc1_ref, w0_ref, w1_ref,
               *, bq, nblk, unroll):
    lane = lax.broadcasted_iota(jnp.int32, (bq, LANES), 1)
    low = lane < HEAD_DIM
    nt = (((1,), (1,)), ((), ()))
    z_ref, nl_ref = (z0_ref, z1_ref), (nl0_ref, nl1_ref)
    zc_ref, w_ref = (zc0_ref, zc1_ref), (w0_ref, w1_ref)

    def rows(blk):
        if isinstance(blk, int):
            return pl.ds(blk * bq, bq)
        return pl.ds(pl.multiple_of(blk * bq, bq), bq)

    def strict_mask():
        rowi = lax.broadcasted_iota(jnp.int32, (bq, bq), 0)
        coli = lax.broadcasted_iota(jnp.int32, (bq, bq), 1)
        return coli < rowi

    def s1a(item, slot):
        i, c = item
        qb = q_ref[rows(i), :]
        kc = k_ref[rows(c), :]
        zq = jnp.zeros_like(qb)
        for h in range(2):
            qm = jnp.where(low, qb, zq) if h == 0 else jnp.where(low, zq, qb)
            z_ref[slot][h] = lax.dot_general(qm, kc, nt, preferred_element_type=F32)

    def s1b(slot, masked):
        for h in range(2):
            z = z_ref[slot][h]
            nl = jnp.maximum(z, 0.0) + jnp.log(1.0 + jnp.exp2(jnp.abs(z) * (-LOG2E)))
            if masked:
                nl = jnp.where(strict_mask(), nl, 0.0)
            nl_ref[slot][h] = nl.astype(BF16)
            zc_ref[slot][h] = z

    def s2(item, slot, masked, first):
        i, _ = item
        for h in range(2):
            cs = jnp.dot(nl_ref[slot][h], u_ref[...], preferred_element_type=F32)
            tot = jnp.broadcast_to(cs[:, 0:1], (bq, LANES))
            if first:
                arg = zc_ref[slot][h] - cs
                car_ref[h, rows(i), :] = tot
            else:
                car = car_ref[h, rows(i), :]
                arg = zc_ref[slot][h] - cs - jnp.concatenate([car, car], axis=1)
                car_ref[h, rows(i), :] = car + tot
            w = jnp.exp(arg)
            if masked:
                w = jnp.where(strict_mask(), w, 0.0)
            w_ref[slot][h] = w.astype(BF16)

    def s3(item, slot, first):
        i, c = item
        vc = v_ref[rows(c), :]
        for h in range(2):
            pv = jnp.dot(w_ref[slot][h], vc, preferred_element_type=F32)
            if first:
                acc_ref[h, rows(i), :] = pv
            else:
                acc_ref[h, rows(i), :] += pv

    def run_pipeline(items, step, masked, first):
        n = len(items)

        def tick(par, its, do):
            if do[1]:
                s1b(1 - par, masked)
            if do[0]:
                s1a(its[0], par)
            if do[2]:
                s2(its[2], par, masked, first)
            if do[3]:
                s3(its[3], 1 - par, first)

        def static_tick(t):
            do = [0 <= t - j < n for j in range(4)]
            its = [items[t - j] if do[j] else None for j in range(4)]
            tick(t % 2, its, do)

        lo_t, hi_t = 3, n - 1
        groups = max(hi_t - lo_t + 1, 0) // unroll
        if groups < 2:
            groups = 0
        for t in range(min(lo_t, n + 3)):
            static_tick(t)
        if groups:
            def body(_, carry):
                its = [(carry[2 * j], carry[2 * j + 1]) for j in range(4)]
                for k in range(unroll):
                    tick((lo_t + k) % 2, its, [True] * 4)
                    its = [step(*its[0])] + its[:3]
                return tuple(x for it in its for x in it)

            init = tuple(jnp.int32(x) for j in range(4) for x in items[lo_t - j])
            lax.fori_loop(0, groups, body, init)
        for t in range(lo_t + unroll * groups, n + 3):
            static_tick(t)

    diag = [(i, i) for i in range(nblk)]
    run_pipeline(diag, lambda i, c: (i + 1, c + 1), masked=True, first=True)

    off = [(i, c) for i in range(1, nblk) for c in range(i - 1, -1, -1)]

    def off_step(i, c):
        wrap = c == 0
        i2 = jnp.where(wrap, i + 1, i)
        return i2, jnp.where(wrap, i2 - 1, c - 1)

    if off:
        run_pipeline(off, off_step, masked=False, first=False)

    for blk in range(nblk):
        r = rows(blk)
        o_ref[r, :] = jnp.where(low, acc_ref[0, r, :], acc_ref[1, r, :]).astype(o_ref.dtype)


def sb_attention(proj, bq=256, unroll=4):
    s = proj.shape[0]
    bq = min(bq, s)
    nblk = s // bq
    qi = OFF_CQ // LANES
    ki = OFF_CK // LANES
    vi = OFF_CV // LANES
    u = (np.arange(bq)[:, None] >= np.arange(bq)[None, :]).astype(np.float32)
    u = jnp.asarray(u, dtype=BF16)
    return pl.pallas_call(
        functools.partial(_sb_kernel, bq=bq, nblk=nblk, unroll=unroll),
        grid=(SB_HEADS // 2,),
        in_specs=[pl.BlockSpec((s, LANES), lambda p: (0, qi + p)),
                  pl.BlockSpec((s, LANES), lambda p: (0, ki + p)),
                  pl.BlockSpec((s, LANES), lambda p: (0, vi + p)),
                  pl.BlockSpec((bq, bq), lambda p: (0, 0))],
        out_specs=pl.BlockSpec((s, LANES), lambda p: (0, p)),
        out_shape=jax.ShapeDtypeStruct((s, SB_WIDTH), BF16),
        scratch_shapes=[pltpu.VMEM((2, s, LANES), F32),
                        pltpu.VMEM((2, s, LANES), F32),
                        pltpu.VMEM((2, bq, bq), F32),
                        pltpu.VMEM((2, bq, bq), F32),
                        pltpu.VMEM((2, bq, bq), BF16),
                        pltpu.VMEM((2, bq, bq), BF16),
                        pltpu.VMEM((2, bq, bq), F32),
                        pltpu.VMEM((2, bq, bq), F32),
                        pltpu.VMEM((2, bq, bq), BF16),
                        pltpu.VMEM((2, bq, bq), BF16)],
        compiler_params=_cparams(("parallel",)),
        name="stickbreak",
    )(proj, proj, proj, u)


def _oproj_kernel(a_ref, b_ref, c_ref, wa_ref, wb_ref, wc_ref, ga_ref, gc_ref,
                  x_ref, gn_ref, xo_ref, ho_ref):
    an = _rms(a_ref[...].astype(F32), ga_ref[...]).astype(BF16)
    cn = _rms(c_ref[...].astype(F32), gc_ref[...]).astype(BF16)
    y = (jnp.dot(an, wa_ref[...], preferred_element_type=F32)
         + jnp.dot(b_ref[...], wb_ref[...], preferred_element_type=F32)
         + jnp.dot(cn, wc_ref[...], preferred_element_type=F32))
    xn = x_ref[...] + y
    xo_ref[...] = xn
    ho_ref[...] = _rms(xn, gn_ref[...]).astype(ho_ref.dtype)


def out_projection(a, b, c, wa, wb, wc, ga, gc, x, gn, layer, tm=256):
    s, d = x.shape
    tm = min(tm, s)
    row = lambda i: (i, 0)
    fixed = lambda i: (layer, 0, 0)
    per_layer = lambda arr: pl.BlockSpec((None,) + arr.shape[1:], fixed)
    return pl.pallas_call(
        _oproj_kernel,
        grid=(s // tm,),
        in_specs=[pl.BlockSpec((tm, a.shape[1]), row),
                  pl.BlockSpec((tm, b.shape[1]), row),
                  pl.BlockSpec((tm, c.shape[1]), row),
                  per_layer(wa), per_layer(wb), per_layer(wc), per_layer(ga), per_layer(gc),
                  pl.BlockSpec((tm, d), row),
                  per_layer(gn)],
        out_specs=[pl.BlockSpec((tm, d), row), pl.BlockSpec((tm, d), row)],
        out_shape=[jax.ShapeDtypeStruct((s, d), F32), jax.ShapeDtypeStruct((s, d), BF16)],
        compiler_params=_cparams(("parallel",)),
        name="out_proj",
    )(a, b, c, wa, wb, wc, ga, gc, x, gn)


def _mlp_kernel(h_ref, x_ref, w1_ref, w2_ref, gn_ref, *out_refs, final):
    acc_ref = out_refs[0]
    j = pl.program_id(1)

    @pl.when(j == 0)
    def _():
        acc_ref[...] = x_ref[...]

    u = jnp.dot(h_ref[...], w1_ref[...], preferred_element_type=F32)
    act = jnp.square(jnp.maximum(u, 0.0)).astype(BF16)
    acc_ref[...] += jnp.dot(act, w2_ref[...], preferred_element_type=F32)

    @pl.when(j == pl.num_programs(1) - 1)
    def _():
        y = _rms(acc_ref[...], gn_ref[...])
        if final:
            acc_ref[...] = y
        else:
            out_refs[1][...] = y.astype(out_refs[1].dtype)


def mlp_block(h, x, w1_all, w2_all, layer, gn, final, tm=1024, tf=512):
    s, d = x.shape
    ff = w1_all.shape[2]
    tm = min(tm, s)
    row = lambda i, j: (i, 0)
    once = pl.Buffered(1)
    out_specs = [pl.BlockSpec((tm, d), row)]
    out_shape = [jax.ShapeDtypeStruct((s, d), F32)]
    if not final:
        out_specs.append(pl.BlockSpec((tm, d), row))
        out_shape.append(jax.ShapeDtypeStruct((s, d), BF16))
    return pl.pallas_call(
        functools.partial(_mlp_kernel, final=final),
        grid=(s // tm, ff // tf),
        in_specs=[pl.BlockSpec((tm, d), row, pipeline_mode=once),
                  pl.BlockSpec((tm, d), row, pipeline_mode=once),
                  pl.BlockSpec((None, d, tf), lambda i, j: (layer, 0, j)),
                  pl.BlockSpec((None, tf, d), lambda i, j: (layer, j, 0)),
                  pl.BlockSpec((None, 1, d), lambda i, j: (layer, 0, 0))],
        out_specs=out_specs,
        out_shape=out_shape,
        compiler_params=_cparams(("parallel", "arbitrary")),
        name="mlp",
    )(h, x, w1_all, w2_all, gn)


def _zero_cols(w, width):
    return jnp.zeros(w.shape[:-1] + (width,), w.dtype)


def _head_cols(w, off, heads, width, padded):
    pieces = []
    for h in range(heads):
        pieces.append(w[..., off + h * width:off + (h + 1) * width])
        if padded > width:
            pieces.append(_zero_cols(w, padded - width))
    return pieces


def _head_rows(w, off, heads, width, padded):
    pieces = []
    for h in range(heads):
        pieces.append(w[..., off + h * width:off + (h + 1) * width, :])
        if padded > width:
            pieces.append(jnp.zeros(w.shape[:-2] + (padded - width, w.shape[-1]), w.dtype))
    return pieces


def _prep_w_in(w_in):
    seg = lambda off, width: w_in[..., off:off + width]
    scale = HEAD_DIM ** -0.5
    pieces = (
        _head_cols(w_in, _O_BV, GLA_HEADS, GLA_DV, GLA_DV_PAD)
        + _head_cols(w_in, _O_BR, GLA_HEADS, GLA_DV, GLA_DV_PAD)
        + _head_cols(w_in, _O_BQ, GLA_HEADS, GLA_DK, GLA_DK_PAD)
        + _head_cols(w_in, _O_BK, GLA_HEADS, GLA_DK, GLA_DK_PAD)
        + [seg(_O_CQ, SB_WIDTH) * scale, seg(_O_CK, SB_WIDTH), seg(_O_CV, SB_WIDTH)]
        + [seg(_O_AQ + HEAD_DIM * h, HEAD_DIM) * scale for h in SWA_Q_PERM]
        + [seg(_O_AK, SWA_KV_WIDTH), seg(_O_AV, SWA_KV_WIDTH), seg(_O_BLR, GLA_LOWRANK),
           _zero_cols(w_in, PROJ_PAD_WIDTH - OFF_BLR - GLA_LOWRANK)])
    return jnp.concatenate(pieces, axis=-1).astype(BF16)


def _swa_perm_cols(w):
    return jnp.concatenate([w[..., HEAD_DIM * h:HEAD_DIM * (h + 1)] for h in SWA_Q_PERM], axis=-1)


def kernel(x, norm_mix, w_in, swa_sinks, rel_bias, gla_gate_w, gla_gate_b, gla_norm,
           swa_out_norm, sb_out_norm, w_out, norm_mlp, w_mlp_in, w_mlp_out, norm_final):
    depth = w_in.shape[0]
    xs = x[0]
    gla_w = GLA_HEADS * GLA_DV
    w_in_p = _prep_w_in(w_in)
    w1 = w_mlp_in.astype(BF16)
    w2 = w_mlp_out.astype(BF16)
    wa = jnp.concatenate([w_out[:, HEAD_DIM * h:HEAD_DIM * (h + 1)] for h in SWA_Q_PERM],
                         axis=1).astype(BF16)
    wb = jnp.concatenate(_head_rows(w_out, SWA_WIDTH, GLA_HEADS, GLA_DV, GLA_DV_PAD),
                         axis=1).astype(BF16)
    wc = w_out[:, SWA_WIDTH + gla_w:].astype(BF16)
    ga = _swa_perm_cols(swa_out_norm)[:, None, :]
    gc = sb_out_norm[:, None, :]
    g_mlp = norm_mlp[:, None, :]
    g_next = jnp.concatenate([norm_mix[1:], norm_final[None]], axis=0)[:, None, :]
    sinks_p = jnp.concatenate([swa_sinks[:, h:h + 1] for h in SWA_Q_PERM], axis=1)
    bias_tab = swa_bias_table(rel_bias)
    gw = jnp.stack([jnp.pad(gla_gate_w[:, :, GLA_DK * h:GLA_DK * (h + 1)],
                            ((0, 0), (0, GLA_LR_PAD - GLA_LOWRANK), (0, GLA_DK_PAD - GLA_DK)))
                    for h in range(GLA_HEADS)], axis=1).astype(BF16)
    gb = jnp.stack([jnp.pad(gla_gate_b[:, GLA_DK * h:GLA_DK * (h + 1)],
                            ((0, 0), (0, GLA_DK_PAD - GLA_DK)))
                    for h in range(GLA_HEADS)], axis=1)[:, :, None, :]
    gn = jnp.pad(gla_norm, ((0, 0), (0, GLA_DV_PAD - GLA_DV)))[:, None, :]

    h = None
    for l in range(depth):
        if l == 0:
            proj = in_projection(xs, w_in_p, l, gain=norm_mix[0])
        else:
            proj = in_projection(h, w_in_p, l)
        a = swa_attention(proj, sinks_p, bias_tab, l)
        b = gla_attention(proj, gw, gb, gn, l)
        c = sb_attention(proj)
        xs, hm = out_projection(a, b, c, wa, wb, wc, ga, gc, xs, g_mlp, l)
        if l + 1 < depth:
            xs, h = mlp_block(hm, xs, w1, w2, l, g_next, final=False)
        else:
            (out,) = mlp_block(hm, xs, w1, w2, l, g_next, final=True)
    return out[None]
```

```python
import functools
import math

import numpy as np
import jax
import jax.numpy as jnp
from jax import lax
from jax.experimental import pallas as pl
from jax.experimental.pallas import tpu as pltpu

F32 = jnp.float32
BF16 = jnp.bfloat16

D_MODEL = 2048
HEAD_DIM = 64
SWA_HEADS = 12
SWA_KV_HEADS = 4
SWA_WIDTH = SWA_HEADS * HEAD_DIM
SWA_KV_WIDTH = SWA_KV_HEADS * HEAD_DIM
SWA_BLOCK = 128
WINDOW = 128
REL_BUCKETS = 32
REL_MAX_DIST = 128
GLA_HEADS = 4
GLA_DV = 192
GLA_DK = 96
GLA_DK_PAD = 128
GLA_DV_PAD = 256
GLA_LOWRANK = 16
GLA_LR_PAD = 128
GLA_CHUNK = 64
GATE_NORMALIZER = 16.0
GATE_LOG_MIN = -1.0
SB_HEADS = 8
SB_WIDTH = SB_HEADS * HEAD_DIM
D_FF = 4 * D_MODEL
RMS_EPS = 1e-6
LOG2E = 1.4426950408889634
NEG_INF = -1e30

LANES = 128
VMEM_LIMIT = 56 * 1024 * 1024

OFF_BV = 0
OFF_BR = 1024
OFF_BQ = 2048
OFF_BK = 2560
OFF_CQ = 3072
OFF_CK = 3584
OFF_CV = 4096
OFF_AQ = 4608
OFF_AK = 5376
OFF_AV = 5632
OFF_BLR = 5888
PROJ_PAD_WIDTH = 6144

SWA_Q_PERM = (0, 3, 1, 4, 2, 5, 6, 9, 7, 10, 8, 11)

_O_AQ, _O_AK, _O_AV = 0, 768, 1024
_O_BQ, _O_BK, _O_BV, _O_BR, _O_BLR = 1280, 1664, 2048, 2816, 3584
_O_CQ, _O_CK, _O_CV = 3600, 4112, 4624


def _cparams(sem):
    return pltpu.CompilerParams(dimension_semantics=sem, vmem_limit_bytes=VMEM_LIMIT)


def _rms(x, gain):
    ms = jnp.mean(x * x, axis=-1, keepdims=True)
    return x * lax.rsqrt(ms + RMS_EPS) * gain


def _split_bf16(x):
    hi = x.astype(BF16)
    lo = (x - hi.astype(F32)).astype(BF16)
    return hi, lo


def _matmul_kernel(h_ref, w_ref, o_ref):
    o_ref[...] = jnp.dot(h_ref[...], w_ref[...],
                         preferred_element_type=F32).astype(o_ref.dtype)


def _norm_matmul_kernel(x_ref, g_ref, w_ref, o_ref, h_ref):
    @pl.when(pl.program_id(1) == 0)
    def _():
        h_ref[...] = _rms(x_ref[...], g_ref[...]).astype(h_ref.dtype)

    o_ref[...] = jnp.dot(h_ref[...], w_ref[...],
                         preferred_element_type=F32).astype(o_ref.dtype)


def in_projection(h, w_all, layer, gain=None, tm=1024, tn=512):
    s, d = h.shape
    n = w_all.shape[2]
    tm = min(tm, s)
    w_spec = pl.BlockSpec((None, d, tn), lambda i, j: (layer, 0, j))
    h_spec = pl.BlockSpec((tm, d), lambda i, j: (i, 0))
    common = dict(
        grid=(s // tm, n // tn),
        out_specs=pl.BlockSpec((tm, tn), lambda i, j: (i, j)),
        out_shape=jax.ShapeDtypeStruct((s, n), BF16),
        compiler_params=_cparams(("parallel", "arbitrary")),
        name="in_proj",
    )
    if gain is None:
        return pl.pallas_call(_matmul_kernel, in_specs=[h_spec, w_spec], **common)(h, w_all)
    return pl.pallas_call(
        _norm_matmul_kernel,
        in_specs=[h_spec, pl.BlockSpec((1, d), lambda i, j: (0, 0)), w_spec],
        scratch_shapes=[pltpu.VMEM((tm, d), BF16)],
        **common,
    )(h, gain.reshape(1, d), w_all)


def _w_in_moves():
    scale = HEAD_DIM ** -0.5
    moves = []
    for h in range(GLA_HEADS):
        moves.append((_O_BV + GLA_DV * h, OFF_BV + GLA_DV_PAD * h, GLA_DV, 1.0))
        moves.append((_O_BR + GLA_DV * h, OFF_BR + GLA_DV_PAD * h, GLA_DV, 1.0))
        moves.append((_O_BQ + GLA_DK * h, OFF_BQ + GLA_DK_PAD * h, GLA_DK, 1.0))
        moves.append((_O_BK + GLA_DK * h, OFF_BK + GLA_DK_PAD * h, GLA_DK, 1.0))
    moves += [(_O_CQ, OFF_CQ, SB_WIDTH, scale), (_O_CK, OFF_CK, SB_WIDTH, 1.0),
              (_O_CV, OFF_CV, SB_WIDTH, 1.0)]
    for j, h in enumerate(SWA_Q_PERM):
        moves.append((_O_AQ + HEAD_DIM * h, OFF_AQ + HEAD_DIM * j, HEAD_DIM, scale))
    moves += [(_O_AK, OFF_AK, SWA_KV_WIDTH, 1.0), (_O_AV, OFF_AV, SWA_KV_WIDTH, 1.0),
              (_O_BLR, OFF_BLR, GLA_LOWRANK, 1.0)]
    return moves


def _w_in_prep_kernel(w_ref, o_ref):
    o_ref[...] = jnp.zeros_like(o_ref)
    for src, dst, width, scale in _w_in_moves():
        piece = w_ref[:, src:src + width]
        if scale != 1.0:
            piece = piece * scale
        o_ref[:, dst:dst + width] = piece.astype(o_ref.dtype)


def prep_w_in(w_in, tr=256):
    layers, d, n = w_in.shape
    return pl.pallas_call(
        _w_in_prep_kernel,
        grid=(layers, d // tr),
        in_specs=[pl.BlockSpec((None, tr, n), lambda l, i: (l, i, 0))],
        out_specs=pl.BlockSpec((None, tr, PROJ_PAD_WIDTH), lambda l, i: (l, i, 0)),
        out_shape=jax.ShapeDtypeStruct((layers, d, PROJ_PAD_WIDTH), BF16),
        compiler_params=_cparams(("parallel", "parallel")),
        name="w_in_prep",
    )(w_in)


def _swa_kernel(sink_ref, q_ref, kp_ref, kc_ref, vp_ref, vc_ref, bias_ref, o_ref, *, layer):
    i = pl.program_id(0)
    blk = SWA_BLOCK
    k = jnp.concatenate([kp_ref[...], kc_ref[...]], axis=0)
    v = jnp.concatenate([vp_ref[...], vc_ref[...]], axis=0)
    lane = lax.broadcasted_iota(jnp.int32, (blk, LANES), 1)
    col = lax.broadcasted_iota(jnp.int32, (blk, 2 * blk), 1)
    kill = jnp.logical_and(col < blk, i == 0)
    low = lane < HEAD_DIM
    for b in range(SWA_HEADS // 2):
        qb = q_ref[:, LANES * b:LANES * (b + 1)]
        kv_lo = LANES * (b // 3)
        kb = k[:, kv_lo:kv_lo + LANES]
        vb = v[:, kv_lo:kv_lo + LANES]
        outs = []
        for half in range(2):
            hq = 2 * b + half
            keep = low if half == 0 else jnp.logical_not(low)
            qm = jnp.where(keep, qb, jnp.zeros_like(qb))
            s = lax.dot_general(qm, kb, (((1,), (1,)), ((), ())),
                                preferred_element_type=F32)
            s = jnp.where(kill, NEG_INF, s + bias_ref[hq])
            sink = sink_ref[layer, hq]
            m = jnp.maximum(jnp.max(s, axis=1, keepdims=True), sink)
            p = jnp.exp(s - m)
            denom = jnp.sum(p, axis=1, keepdims=True) + jnp.exp(sink - m)
            o = jnp.dot(p.astype(BF16), vb, preferred_element_type=F32)
            outs.append(o / denom)
        o_ref[:, LANES * b:LANES * (b + 1)] = jnp.where(low, outs[0], outs[1]).astype(o_ref.dtype)


def swa_attention(proj, sinks_perm, bias_tab, layer):
    s = proj.shape[0]
    blk = SWA_BLOCK
    qi = OFF_AQ // SWA_WIDTH
    ki = OFF_AK // SWA_KV_WIDTH
    vi = OFF_AV // SWA_KV_WIDTH
    prev = lambda i: jnp.maximum(i - 1, 0)
    return pl.pallas_call(
        functools.partial(_swa_kernel, layer=layer),
        grid=(s // blk,),
        in_specs=[pl.BlockSpec(memory_space=pltpu.SMEM),
                  pl.BlockSpec((blk, SWA_WIDTH), lambda i: (i, qi)),
                  pl.BlockSpec((blk, SWA_KV_WIDTH), lambda i: (prev(i), ki)),
                  pl.BlockSpec((blk, SWA_KV_WIDTH), lambda i: (i, ki)),
                  pl.BlockSpec((blk, SWA_KV_WIDTH), lambda i: (prev(i), vi)),
                  pl.BlockSpec((blk, SWA_KV_WIDTH), lambda i: (i, vi)),
                  pl.BlockSpec((SWA_HEADS, blk, 2 * blk), lambda i: (0, 0, 0))],
        out_specs=pl.BlockSpec((blk, SWA_WIDTH), lambda i: (i, 0)),
        out_shape=jax.ShapeDtypeStruct((s, SWA_WIDTH), BF16),
        compiler_params=_cparams(("parallel",)),
        name="swa",
    )(sinks_perm, proj, proj, proj, proj, proj, bias_tab)


def _t5_causal_bucket(dist):
    max_exact = REL_BUCKETS // 2
    is_small = dist < max_exact
    ratio = (jnp.log(jnp.maximum(dist, 1).astype(F32) / max_exact)
             / math.log(REL_MAX_DIST / max_exact))
    large = max_exact + (ratio * (REL_BUCKETS - max_exact)).astype(jnp.int32)
    large = jnp.minimum(large, REL_BUCKETS - 1)
    return jnp.where(is_small, dist, large)


def swa_bias_table(rel_bias):
    blk = SWA_BLOCK
    qpos = jnp.arange(blk) + blk
    kpos = jnp.arange(2 * blk)
    dist = qpos[:, None] - kpos[None, :]
    in_window = (dist >= 0) & (dist < WINDOW)
    bucket = _t5_causal_bucket(jnp.maximum(dist, 0))
    table = rel_bias.astype(F32)
    table = jnp.stack([table[:, h] for h in SWA_Q_PERM])
    bias = jnp.full((SWA_HEADS, blk, 2 * blk), NEG_INF, F32)
    for bkt in range(REL_BUCKETS):
        hit = (in_window & (bucket == bkt))[None]
        bias = jnp.where(hit, table[:, bkt][:, None, None], bias)
    return bias


def _gla_kernel(q_ref, k_ref, v_ref, r_ref, lr_ref, gw_ref, gb_ref, gn_ref, o_ref,
                st_ref, *, rows):
    t = pl.program_id(1)
    ch = GLA_CHUNK

    @pl.when(t == 0)
    def _():
        st_ref[...] = jnp.zeros_like(st_ref)

    gw = gw_ref[...]
    gb = gb_ref[...]
    gn = gn_ref[...]
    row = lax.broadcasted_iota(jnp.int32, (ch, ch), 0)
    colm = lax.broadcasted_iota(jnp.int32, (ch, ch), 1)
    causal = colm <= row
    ltri = jnp.where(causal, 1.0, 0.0).astype(BF16)
    st = st_ref[...]
    nt = (((1,), (1,)), ((), ()))
    tn = (((0,), (0,)), ((), ()))
    for c in range(rows // ch):
        sl = pl.ds(c * ch, ch)
        gp = jnp.dot(lr_ref[sl, :], gw, preferred_element_type=F32) + gb
        log_sig = jnp.minimum(gp, 0.0) - jnp.log(1.0 + jnp.exp(-jnp.abs(gp)))
        lg = jnp.maximum(log_sig * (1.0 / GATE_NORMALIZER), GATE_LOG_MIN)
        hi, lo = _split_bf16(lg)
        b = (jnp.dot(ltri, hi, preferred_element_type=F32)
             + jnp.dot(ltri, lo, preferred_element_type=F32))
        b_last = b[ch - 1:ch, :]
        q = q_ref[sl, :].astype(F32) * (GLA_DK ** -0.5)
        kk = k_ref[sl, :].astype(F32)
        q_dec = (q * jnp.exp(b)).astype(BF16)
        k_inv = (kk * jnp.exp(-b)).astype(BF16)
        k_end = (kk * jnp.exp(b_last - b)).astype(BF16)
        v = v_ref[sl, :]
        sc = lax.dot_general(q_dec, k_inv, nt, preferred_element_type=F32)
        sc = jnp.where(causal, sc, 0.0).astype(BF16)
        o = (jnp.dot(sc, v, preferred_element_type=F32)
             + lax.dot_general(q_dec, st.astype(BF16), nt, preferred_element_type=F32))
        d_st = lax.dot_general(v, k_end, tn, preferred_element_type=F32)
        st = st * jnp.exp(b_last) + d_st
        ms = jnp.sum(o * o, axis=1, keepdims=True) * (1.0 / GLA_DV)
        y = o * lax.rsqrt(ms + RMS_EPS) * gn
        rr = r_ref[sl, :].astype(F32)
        o_ref[sl, :] = (y * (rr / (1.0 + jnp.exp(-rr)))).astype(o_ref.dtype)
    st_ref[...] = st


def gla_attention(proj, gw_pad, gb_pad, gn_pad, layer, rows=256):
    s = proj.shape[0]
    rows = min(rows, s)
    qi = OFF_BQ // GLA_DK_PAD
    ki = OFF_BK // GLA_DK_PAD
    vi = OFF_BV // GLA_DV_PAD
    ri = OFF_BR // GLA_DV_PAD
    li = OFF_BLR // GLA_LR_PAD
    return pl.pallas_call(
        functools.partial(_gla_kernel, rows=rows),
        grid=(GLA_HEADS, s // rows),
        in_specs=[pl.BlockSpec((rows, GLA_DK_PAD), lambda h, t: (t, qi + h)),
                  pl.BlockSpec((rows, GLA_DK_PAD), lambda h, t: (t, ki + h)),
                  pl.BlockSpec((rows, GLA_DV_PAD), lambda h, t: (t, vi + h)),
                  pl.BlockSpec((rows, GLA_DV_PAD), lambda h, t: (t, ri + h)),
                  pl.BlockSpec((rows, GLA_LR_PAD), lambda h, t: (t, li)),
                  pl.BlockSpec((None, None, GLA_LR_PAD, GLA_DK_PAD),
                               lambda h, t: (layer, h, 0, 0)),
                  pl.BlockSpec((None, None, 1, GLA_DK_PAD), lambda h, t: (layer, h, 0, 0)),
                  pl.BlockSpec((None, 1, GLA_DV_PAD), lambda h, t: (layer, 0, 0))],
        out_specs=pl.BlockSpec((rows, GLA_DV_PAD), lambda h, t: (t, h)),
        out_shape=jax.ShapeDtypeStruct((s, GLA_HEADS * GLA_DV_PAD), BF16),
        scratch_shapes=[pltpu.VMEM((GLA_DV_PAD, GLA_DK_PAD), F32)],
        compiler_params=_cparams(("parallel", "arbitrary")),
        name="gla",
    )(proj, proj, proj, proj, proj, gw_pad, gb_pad, gn_pad)


def _sb_kernel(q_ref, k_ref, v_ref, u_ref, o_ref, acc_ref, car_ref,
               z0_ref, z1_ref, nl0_ref, nl1_ref, zc0_ref, zc1_ref, w0_ref, w1_ref,
               *, bq, nblk, unroll):
    lane = lax.broadcasted_iota(jnp.int32, (bq, LANES), 1)
    low = lane < HEAD_DIM
    nt = (((1,), (1,)), ((), ()))
    z_ref, nl_ref = (z0_ref, z1_ref), (nl0_ref, nl1_ref)
    zc_ref, w_ref = (zc0_ref, zc1_ref), (w0_ref, w1_ref)

    def rows(blk):
        if isinstance(blk, int):
            return pl.ds(blk * bq, bq)
        return pl.ds(pl.multiple_of(blk * bq, bq), bq)

    def strict_mask():
        rowi = lax.broadcasted_iota(jnp.int32, (bq, bq), 0)
        coli = lax.broadcasted_iota(jnp.int32, (bq, bq), 1)
        return coli < rowi

    def s1a(item, slot):
        i, c = item
        qb = q_ref[rows(i), :]
        kc = k_ref[rows(c), :]
        zq = jnp.zeros_like(qb)
        for h in range(2):
            qm = jnp.where(low, qb, zq) if h == 0 else jnp.where(low, zq, qb)
            z_ref[slot][h] = lax.dot_general(qm, kc, nt, preferred_element_type=F32)

    def s1b(slot, masked):
        for h in range(2):
            z = z_ref[slot][h]
            nl = jnp.maximum(z, 0.0) + jnp.log(1.0 + jnp.exp2(jnp.abs(z) * (-LOG2E)))
            if masked:
                nl = jnp.where(strict_mask(), nl, 0.0)
            nl_ref[slot][h] = nl.astype(BF16)
            zc_ref[slot][h] = z

    def s2(item, slot, masked, first):
        i, _ = item
        for h in range(2):
            cs = jnp.dot(nl_ref[slot][h], u_ref[...], preferred_element_type=F32)
            tot = jnp.broadcast_to(cs[:, 0:1], (bq, LANES))
            if first:
                arg = zc_ref[slot][h] - cs
                car_ref[h, rows(i), :] = tot
            else:
                car = car_ref[h, rows(i), :]
                arg = zc_ref[slot][h] - cs - jnp.concatenate([car, car], axis=1)
                car_ref[h, rows(i), :] = car + tot
            w = jnp.exp(arg)
            if masked:
                w = jnp.where(strict_mask(), w, 0.0)
            w_ref[slot][h] = w.astype(BF16)

    def s3(item, slot, first):
        i, c = item
        vc = v_ref[rows(c), :]
        for h in range(2):
            pv = jnp.dot(w_ref[slot][h], vc, preferred_element_type=F32)
            if first:
                acc_ref[h, rows(i), :] = pv
            else:
                acc_ref[h, rows(i), :] += pv

    def run_pipeline(items, step, masked, first):
        n = len(items)

        def tick(par, its, do):
            if do[1]:
                s1b(1 - par, masked)
            if do[0]:
                s1a(its[0], par)
            if do[2]:
                s2(its[2], par, masked, first)
            if do[3]:
                s3(its[3], 1 - par, first)

        def static_tick(t):
            do = [0 <= t - j < n for j in range(4)]
            its = [items[t - j] if do[j] else None for j in range(4)]
            tick(t % 2, its, do)

        lo_t, hi_t = 3, n - 1
        groups = max(hi_t - lo_t + 1, 0) // unroll
        if groups < 2:
            groups = 0
        for t in range(min(lo_t, n + 3)):
            static_tick(t)
        if groups:
            def body(_, carry):
                its = [(carry[2 * j], carry[2 * j + 1]) for j in range(4)]
                for k in range(unroll):
                    tick((lo_t + k) % 2, its, [True] * 4)
                    its = [step(*its[0])] + its[:3]
                return tuple(x for it in its for x in it)

            init = tuple(jnp.int32(x) for j in range(4) for x in items[lo_t - j])
            lax.fori_loop(0, groups, body, init)
        for t in range(lo_t + unroll * groups, n + 3):
            static_tick(t)

    diag = [(i, i) for i in range(nblk)]
    run_pipeline(diag, lambda i, c: (i + 1, c + 1), masked=True, first=True)

    off = [(i, c) for i in range(1, nblk) for c in range(i - 1, -1, -1)]

    def off_step(i, c):
        wrap = c == 0
        i2 = jnp.where(wrap, i + 1, i)
        return i2, jnp.where(wrap, i2 - 1, c - 1)

    if off:
        run_pipeline(off, off_step, masked=False, first=False)

    for blk in range(nblk):
        r = rows(blk)
        o_ref[r, :] = jnp.where(low, acc_ref[0, r, :], acc_ref[1, r, :]).astype(o_ref.dtype)


def sb_attention(proj, bq=256, unroll=4):
    s = proj.shape[0]
    bq = min(bq, s)
    nblk = s // bq
    qi = OFF_CQ // LANES
    ki = OFF_CK // LANES
    vi = OFF_CV // LANES
    u = (np.arange(bq)[:, None] >= np.arange(bq)[None, :]).astype(np.float32)
    u = jnp.asarray(u, dtype=BF16)
    return pl.pallas_call(
        functools.partial(_sb_kernel, bq=bq, nblk=nblk, unroll=unroll),
        grid=(SB_HEADS // 2,),
        in_specs=[pl.BlockSpec((s, LANES), lambda p: (0, qi + p)),
                  pl.BlockSpec((s, LANES), lambda p: (0, ki + p)),
                  pl.BlockSpec((s, LANES), lambda p: (0, vi + p)),
                  pl.BlockSpec((bq, bq), lambda p: (0, 0))],
        out_specs=pl.BlockSpec((s, LANES), lambda p: (0, p)),
        out_shape=jax.ShapeDtypeStruct((s, SB_WIDTH), BF16),
        scratch_shapes=[pltpu.VMEM((2, s, LANES), F32),
                        pltpu.VMEM((2, s, LANES), F32),
                        pltpu.VMEM((2, bq, bq), F32),
                        pltpu.VMEM((2, bq, bq), F32),
                        pltpu.VMEM((2, bq, bq), BF16),
                        pltpu.VMEM((2, bq, bq), BF16),
                        pltpu.VMEM((2, bq, bq), F32),
                        pltpu.VMEM((2, bq, bq), F32),
                        pltpu.VMEM((2, bq, bq), BF16),
                        pltpu.VMEM((2, bq, bq), BF16)],
        compiler_params=_cparams(("parallel",)),
        name="stickbreak",
    )(proj, proj, proj, u)


def _oproj_kernel(a_ref, b_ref, c_ref, wa_ref, wb_ref, wc_ref, ga_ref, gc_ref,
                  x_ref, gn_ref, xo_ref, ho_ref):
    an = _rms(a_ref[...].astype(F32), ga_ref[...]).astype(BF16)
    cn = _rms(c_ref[...].astype(F32), gc_ref[...]).astype(BF16)
    y = (jnp.dot(an, wa_ref[...], preferred_element_type=F32)
         + jnp.dot(b_ref[...], wb_ref[...], preferred_element_type=F32)
         + jnp.dot(cn, wc_ref[...], preferred_element_type=F32))
    xn = x_ref[...] + y
    xo_ref[...] = xn
    ho_ref[...] = _rms(xn, gn_ref[...]).astype(ho_ref.dtype)


def out_projection(a, b, c, wa, wb, wc, ga, gc, x, gn, layer, tm=256):
    s, d = x.shape
    tm = min(tm, s)
    row = lambda i: (i, 0)
    fixed = lambda i: (layer, 0, 0)
    per_layer = lambda arr: pl.BlockSpec((None,) + arr.shape[1:], fixed)
    return pl.pallas_call(
        _oproj_kernel,
        grid=(s // tm,),
        in_specs=[pl.BlockSpec((tm, a.shape[1]), row),
                  pl.BlockSpec((tm, b.shape[1]), row),
                  pl.BlockSpec((tm, c.shape[1]), row),
                  per_layer(wa), per_layer(wb), per_layer(wc), per_layer(ga), per_layer(gc),
                  pl.BlockSpec((tm, d), row),
                  per_layer(gn)],
        out_specs=[pl.BlockSpec((tm, d), row), pl.BlockSpec((tm, d), row)],
        out_shape=[jax.ShapeDtypeStruct((s, d), F32), jax.ShapeDtypeStruct((s, d), BF16)],
        compiler_params=_cparams(("parallel",)),
        name="out_proj",
    )(a, b, c, wa, wb, wc, ga, gc, x, gn)


def _mlp_kernel(h_ref, x_ref, w1_ref, w2_ref, gn_ref, *out_refs, final):
    acc_ref = out_refs[0]
    j = pl.program_id(1)

    @pl.when(j == 0)
    def _():
        acc_ref[...] = x_ref[...]

    u = jnp.dot(h_ref[...], w1_ref[...].astype(BF16), preferred_element_type=F32)
    act = jnp.square(jnp.maximum(u, 0.0)).astype(BF16)
    acc_ref[...] += jnp.dot(act, w2_ref[...].astype(BF16), preferred_element_type=F32)

    @pl.when(j == pl.num_programs(1) - 1)
    def _():
        y = _rms(acc_ref[...], gn_ref[...])
        if final:
            acc_ref[...] = y
        else:
            out_refs[1][...] = y.astype(out_refs[1].dtype)


def mlp_block(h, x, w1_all, w2_all, layer, gn, final, tm=1024, tf=512):
    s, d = x.shape
    ff = w1_all.shape[2]
    tm = min(tm, s)
    row = lambda i, j: (i, 0)
    once = pl.Buffered(1)
    out_specs = [pl.BlockSpec((tm, d), row)]
    out_shape = [jax.ShapeDtypeStruct((s, d), F32)]
    if not final:
        out_specs.append(pl.BlockSpec((tm, d), row))
        out_shape.append(jax.ShapeDtypeStruct((s, d), BF16))
    return pl.pallas_call(
        functools.partial(_mlp_kernel, final=final),
        grid=(s // tm, ff // tf),
        in_specs=[pl.BlockSpec((tm, d), row, pipeline_mode=once),
                  pl.BlockSpec((tm, d), row, pipeline_mode=once),
                  pl.BlockSpec((None, d, tf), lambda i, j: (layer, 0, j)),
                  pl.BlockSpec((None, tf, d), lambda i, j: (layer, j, 0)),
                  pl.BlockSpec((None, 1, d), lambda i, j: (layer, 0, 0))],
        out_specs=out_specs,
        out_shape=out_shape,
        compiler_params=_cparams(("parallel", "arbitrary")),
        name="mlp",
    )(h, x, w1_all, w2_all, gn)


def _zero_cols(w, width):
    return jnp.zeros(w.shape[:-1] + (width,), w.dtype)


def _head_cols(w, off, heads, width, padded):
    pieces = []
    for h in range(heads):
        pieces.append(w[..., off + h * width:off + (h + 1) * width])
        if padded > width:
            pieces.append(_zero_cols(w, padded - width))
    return pieces


def _head_rows(w, off, heads, width, padded):
    pieces = []
    for h in range(heads):
        pieces.append(w[..., off + h * width:off + (h + 1) * width, :])
        if padded > width:
            pieces.append(jnp.zeros(w.shape[:-2] + (padded - width, w.shape[-1]), w.dtype))
    return pieces


def _prep_w_in(w_in):
    seg = lambda off, width: w_in[..., off:off + width]
    scale = HEAD_DIM ** -0.5
    pieces = (
        _head_cols(w_in, _O_BV, GLA_HEADS, GLA_DV, GLA_DV_PAD)
        + _head_cols(w_in, _O_BR, GLA_HEADS, GLA_DV, GLA_DV_PAD)
        + _head_cols(w_in, _O_BQ, GLA_HEADS, GLA_DK, GLA_DK_PAD)
        + _head_cols(w_in, _O_BK, GLA_HEADS, GLA_DK, GLA_DK_PAD)
        + [seg(_O_CQ, SB_WIDTH) * scale, seg(_O_CK, SB_WIDTH), seg(_O_CV, SB_WIDTH)]
        + [seg(_O_AQ + HEAD_DIM * h, HEAD_DIM) * scale for h in SWA_Q_PERM]
        + [seg(_O_AK, SWA_KV_WIDTH), seg(_O_AV, SWA_KV_WIDTH), seg(_O_BLR, GLA_LOWRANK),
           _zero_cols(w_in, PROJ_PAD_WIDTH - OFF_BLR - GLA_LOWRANK)])
    return jnp.concatenate(pieces, axis=-1).astype(BF16)


def _swa_perm_cols(w):
    return jnp.concatenate([w[..., HEAD_DIM * h:HEAD_DIM * (h + 1)] for h in SWA_Q_PERM], axis=-1)


def kernel(x, norm_mix, w_in, swa_sinks, rel_bias, gla_gate_w, gla_gate_b, gla_norm,
           swa_out_norm, sb_out_norm, w_out, norm_mlp, w_mlp_in, w_mlp_out, norm_final):
    depth = w_in.shape[0]
    xs = x[0]
    gla_w = GLA_HEADS * GLA_DV
    w_in_p = prep_w_in(w_in)
    wa = jnp.concatenate([w_out[:, HEAD_DIM * h:HEAD_DIM * (h + 1)] for h in SWA_Q_PERM],
                         axis=1).astype(BF16)
    wb = jnp.concatenate(_head_rows(w_out, SWA_WIDTH, GLA_HEADS, GLA_DV, GLA_DV_PAD),
                         axis=1).astype(BF16)
    wc = w_out[:, SWA_WIDTH + gla_w:].astype(BF16)
    ga = _swa_perm_cols(swa_out_norm)[:, None, :]
    gc = sb_out_norm[:, None, :]
    g_mlp = norm_mlp[:, None, :]
    g_next = jnp.concatenate([norm_mix[1:], norm_final[None]], axis=0)[:, None, :]
    sinks_p = jnp.concatenate([swa_sinks[:, h:h + 1] for h in SWA_Q_PERM], axis=1)
    bias_tab = swa_bias_table(rel_bias)
    gw = jnp.stack([jnp.pad(gla_gate_w[:, :, GLA_DK * h:GLA_DK * (h + 1)],
                            ((0, 0), (0, GLA_LR_PAD - GLA_LOWRANK), (0, GLA_DK_PAD - GLA_DK)))
                    for h in range(GLA_HEADS)], axis=1).astype(BF16)
    gb = jnp.stack([jnp.pad(gla_gate_b[:, GLA_DK * h:GLA_DK * (h + 1)],
                            ((0, 0), (0, GLA_DK_PAD - GLA_DK)))
                    for h in range(GLA_HEADS)], axis=1)[:, :, None, :]
    gn = jnp.pad(gla_norm, ((0, 0), (0, GLA_DV_PAD - GLA_DV)))[:, None, :]

    h = None
    for l in range(depth):
        if l == 0:
            proj = in_projection(xs, w_in_p, l, gain=norm_mix[0])
        else:
            proj = in_projection(h, w_in_p, l)
        a = swa_attention(proj, sinks_p, bias_tab, l)
        b = gla_attention(proj, gw, gb, gn, l)
        c = sb_attention(proj)
        xs, hm = out_projection(a, b, c, wa, wb, wc, ga, gc, xs, g_mlp, l)
        if l + 1 < depth:
            xs, h = mlp_block(hm, xs, w_mlp_in, w_mlp_out, l, g_next, final=False)
        else:
            (out,) = mlp_block(hm, xs, w_mlp_in, w_mlp_out, l, g_next, final=True)
    return out[None]
```

```python
import functools
import math

import numpy as np
import jax
import jax.numpy as jnp
from jax import lax
from jax.experimental import pallas as pl
from jax.experimental.pallas import tpu as pltpu

F32 = jnp.float32
BF16 = jnp.bfloat16

D_MODEL = 2048
HEAD_DIM = 64
SWA_HEADS = 12
SWA_KV_HEADS = 4
SWA_WIDTH = SWA_HEADS * HEAD_DIM
SWA_KV_WIDTH = SWA_KV_HEADS * HEAD_DIM
SWA_BLOCK = 128
WINDOW = 128
REL_BUCKETS = 32
REL_MAX_DIST = 128
GLA_HEADS = 4
GLA_DV = 192
GLA_DK = 96
GLA_DK_PAD = 128
GLA_DV_PAD = 256
GLA_LOWRANK = 16
GLA_LR_PAD = 128
GLA_CHUNK = 64
GATE_NORMALIZER = 16.0
GATE_LOG_MIN = -1.0
SB_HEADS = 8
SB_WIDTH = SB_HEADS * HEAD_DIM
D_FF = 4 * D_MODEL
RMS_EPS = 1e-6
LOG2E = 1.4426950408889634
NEG_INF = -1e30

LANES = 128
VMEM_LIMIT = 56 * 1024 * 1024

OFF_BV = 0
OFF_BR = 1024
OFF_BQ = 2048
OFF_BK = 2560
OFF_CQ = 3072
OFF_CK = 3584
OFF_CV = 4096
OFF_AQ = 4608
OFF_AK = 5376
OFF_AV = 5632
OFF_BLR = 5888
PROJ_PAD_WIDTH = 6144

SWA_Q_PERM = (0, 3, 1, 4, 2, 5, 6, 9, 7, 10, 8, 11)

_O_AQ, _O_AK, _O_AV = 0, 768, 1024
_O_BQ, _O_BK, _O_BV, _O_BR, _O_BLR = 1280, 1664, 2048, 2816, 3584
_O_CQ, _O_CK, _O_CV = 3600, 4112, 4624


def _cparams(sem):
    return pltpu.CompilerParams(dimension_semantics=sem, vmem_limit_bytes=VMEM_LIMIT)


def _rms(x, gain):
    ms = jnp.mean(x * x, axis=-1, keepdims=True)
    return x * lax.rsqrt(ms + RMS_EPS) * gain


def _split_bf16(x):
    hi = x.astype(BF16)
    lo = (x - hi.astype(F32)).astype(BF16)
    return hi, lo


_NT_DIMS = (((1,), (1,)), ((), ()))


def _matmul_kernel(h_ref, w_ref, o_ref):
    o_ref[...] = lax.dot_general(h_ref[...], w_ref[...], _NT_DIMS,
                                 preferred_element_type=F32).astype(o_ref.dtype)


def _norm_matmul_kernel(x_ref, g_ref, w_ref, o_ref, h_ref):
    @pl.when(pl.program_id(1) == 0)
    def _():
        h_ref[...] = _rms(x_ref[...], g_ref[...]).astype(h_ref.dtype)

    o_ref[...] = lax.dot_general(h_ref[...], w_ref[...], _NT_DIMS,
                                 preferred_element_type=F32).astype(o_ref.dtype)


def in_projection(h, w_all, layer, gain=None, tm=1024, tn=512):
    s, d = h.shape
    n = w_all.shape[1]
    tm = min(tm, s)
    w_spec = pl.BlockSpec((None, tn, d), lambda i, j: (layer, j, 0))
    h_spec = pl.BlockSpec((tm, d), lambda i, j: (i, 0))
    common = dict(
        grid=(s // tm, n // tn),
        out_specs=pl.BlockSpec((tm, tn), lambda i, j: (i, j)),
        out_shape=jax.ShapeDtypeStruct((s, n), BF16),
        compiler_params=_cparams(("parallel", "arbitrary")),
        name="in_proj",
    )
    if gain is None:
        return pl.pallas_call(_matmul_kernel, in_specs=[h_spec, w_spec], **common)(h, w_all)
    return pl.pallas_call(
        _norm_matmul_kernel,
        in_specs=[h_spec, pl.BlockSpec((1, d), lambda i, j: (0, 0)), w_spec],
        scratch_shapes=[pltpu.VMEM((tm, d), BF16)],
        **common,
    )(h, gain.reshape(1, d), w_all)


def _w_in_moves():
    scale = HEAD_DIM ** -0.5
    moves = []
    for h in range(GLA_HEADS):
        moves.append((_O_BV + GLA_DV * h, OFF_BV + GLA_DV_PAD * h, GLA_DV, 1.0))
        moves.append((_O_BR + GLA_DV * h, OFF_BR + GLA_DV_PAD * h, GLA_DV, 1.0))
        moves.append((_O_BQ + GLA_DK * h, OFF_BQ + GLA_DK_PAD * h, GLA_DK, 1.0))
        moves.append((_O_BK + GLA_DK * h, OFF_BK + GLA_DK_PAD * h, GLA_DK, 1.0))
    moves += [(_O_CQ, OFF_CQ, SB_WIDTH, scale), (_O_CK, OFF_CK, SB_WIDTH, 1.0),
              (_O_CV, OFF_CV, SB_WIDTH, 1.0)]
    for j, h in enumerate(SWA_Q_PERM):
        moves.append((_O_AQ + HEAD_DIM * h, OFF_AQ + HEAD_DIM * j, HEAD_DIM, scale))
    moves += [(_O_AK, OFF_AK, SWA_KV_WIDTH, 1.0), (_O_AV, OFF_AV, SWA_KV_WIDTH, 1.0),
              (_O_BLR, OFF_BLR, GLA_LOWRANK, 1.0)]
    return moves


def _w_in_prep_kernel(w_ref, o_ref):
    o_ref[...] = jnp.zeros_like(o_ref)
    for src, dst, width, scale in _w_in_moves():
        piece = w_ref[src:src + width, :]
        if scale != 1.0:
            piece = piece * scale
        o_ref[dst:dst + width, :] = piece.astype(o_ref.dtype)


def prep_w_in(w_in_t, tc=256):
    layers, n, d = w_in_t.shape
    return pl.pallas_call(
        _w_in_prep_kernel,
        grid=(layers, d // tc),
        in_specs=[pl.BlockSpec((None, n, tc), lambda l, i: (l, 0, i))],
        out_specs=pl.BlockSpec((None, PROJ_PAD_WIDTH, tc), lambda l, i: (l, 0, i)),
        out_shape=jax.ShapeDtypeStruct((layers, PROJ_PAD_WIDTH, d), BF16),
        compiler_params=_cparams(("parallel", "parallel")),
        name="w_in_prep",
    )(w_in_t)


def _swa_kernel(sink_ref, q_ref, kp_ref, kc_ref, vp_ref, vc_ref, bias_ref, o_ref, *, layer):
    i = pl.program_id(0)
    blk = SWA_BLOCK
    k = jnp.concatenate([kp_ref[...], kc_ref[...]], axis=0)
    v = jnp.concatenate([vp_ref[...], vc_ref[...]], axis=0)
    lane = lax.broadcasted_iota(jnp.int32, (blk, LANES), 1)
    col = lax.broadcasted_iota(jnp.int32, (blk, 2 * blk), 1)
    kill = jnp.logical_and(col < blk, i == 0)
    low = lane < HEAD_DIM
    for b in range(SWA_HEADS // 2):
        qb = q_ref[:, LANES * b:LANES * (b + 1)]
        kv_lo = LANES * (b // 3)
        kb = k[:, kv_lo:kv_lo + LANES]
        vb = v[:, kv_lo:kv_lo + LANES]
        outs = []
        for half in range(2):
            hq = 2 * b + half
            keep = low if half == 0 else jnp.logical_not(low)
            qm = jnp.where(keep, qb, jnp.zeros_like(qb))
            s = lax.dot_general(qm, kb, (((1,), (1,)), ((), ())),
                                preferred_element_type=F32)
            s = jnp.where(kill, NEG_INF, s + bias_ref[hq])
            sink = sink_ref[layer, hq]
            m = jnp.maximum(jnp.max(s, axis=1, keepdims=True), sink)
            p = jnp.exp(s - m)
            denom = jnp.sum(p, axis=1, keepdims=True) + jnp.exp(sink - m)
            o = jnp.dot(p.astype(BF16), vb, preferred_element_type=F32)
            outs.append(o / denom)
        o_ref[:, LANES * b:LANES * (b + 1)] = jnp.where(low, outs[0], outs[1]).astype(o_ref.dtype)


def swa_attention(proj, sinks_perm, bias_tab, layer):
    s = proj.shape[0]
    blk = SWA_BLOCK
    qi = OFF_AQ // SWA_WIDTH
    ki = OFF_AK // SWA_KV_WIDTH
    vi = OFF_AV // SWA_KV_WIDTH
    prev = lambda i: jnp.maximum(i - 1, 0)
    return pl.pallas_call(
        functools.partial(_swa_kernel, layer=layer),
        grid=(s // blk,),
        in_specs=[pl.BlockSpec(memory_space=pltpu.SMEM),
                  pl.BlockSpec((blk, SWA_WIDTH), lambda i: (i, qi)),
                  pl.BlockSpec((blk, SWA_KV_WIDTH), lambda i: (prev(i), ki)),
                  pl.BlockSpec((blk, SWA_KV_WIDTH), lambda i: (i, ki)),
                  pl.BlockSpec((blk, SWA_KV_WIDTH), lambda i: (prev(i), vi)),
                  pl.BlockSpec((blk, SWA_KV_WIDTH), lambda i: (i, vi)),
                  pl.BlockSpec((SWA_HEADS, blk, 2 * blk), lambda i: (0, 0, 0))],
        out_specs=pl.BlockSpec((blk, SWA_WIDTH), lambda i: (i, 0)),
        out_shape=jax.ShapeDtypeStruct((s, SWA_WIDTH), BF16),
        compiler_params=_cparams(("parallel",)),
        name="swa",
    )(sinks_perm, proj, proj, proj, proj, proj, bias_tab)


def _t5_causal_bucket(dist):
    max_exact = REL_BUCKETS // 2
    is_small = dist < max_exact
    ratio = (jnp.log(jnp.maximum(dist, 1).astype(F32) / max_exact)
             / math.log(REL_MAX_DIST / max_exact))
    large = max_exact + (ratio * (REL_BUCKETS - max_exact)).astype(jnp.int32)
    large = jnp.minimum(large, REL_BUCKETS - 1)
    return jnp.where(is_small, dist, large)


def swa_bias_table(rel_bias):
    blk = SWA_BLOCK
    qpos = jnp.arange(blk) + blk
    kpos = jnp.arange(2 * blk)
    dist = qpos[:, None] - kpos[None, :]
    in_window = (dist >= 0) & (dist < WINDOW)
    bucket = _t5_causal_bucket(jnp.maximum(dist, 0))
    table = rel_bias.astype(F32)
    table = jnp.stack([table[:, h] for h in SWA_Q_PERM])
    bias = jnp.full((SWA_HEADS, blk, 2 * blk), NEG_INF, F32)
    for bkt in range(REL_BUCKETS):
        hit = (in_window & (bucket == bkt))[None]
        bias = jnp.where(hit, table[:, bkt][:, None, None], bias)
    return bias


def _gla_kernel(q_ref, k_ref, v_ref, r_ref, lr_ref, gw_ref, gb_ref, gn_ref, lm_ref, o_ref,
                st_ref, *, rows, hps):
    t = pl.program_id(1)
    ch = GLA_CHUNK
    nch = rows // ch
    heads = range(hps)

    @pl.when(t == 0)
    def _():
        st_ref[...] = jnp.zeros_like(st_ref)

    nt = (((1,), (1,)), ((), ()))
    tn = (((0,), (0,)), ((), ()))
    rowi = lax.broadcasted_iota(jnp.int32, (rows, rows), 0)
    coli = lax.broadcasted_iota(jnp.int32, (rows, rows), 1)
    same_chunk_causal = jnp.logical_and(coli <= rowi, (rowi // ch) == (coli // ch))
    lm = lm_ref[...]
    lr = lr_ref[...]
    gn = gn_ref[...]

    gp = [jnp.dot(lr, gw_ref[h], preferred_element_type=F32) + gb_ref[h] for h in heads]
    lg = []
    for h in heads:
        log_sig = jnp.minimum(gp[h], 0.0) - jnp.log(1.0 + jnp.exp(-jnp.abs(gp[h])))
        lg.append(jnp.maximum(log_sig * (1.0 / GATE_NORMALIZER), GATE_LOG_MIN))
    bb = []
    for h in heads:
        hi, lo = _split_bf16(lg[h])
        bb.append(jnp.dot(lm, hi, preferred_element_type=F32)
                  + jnp.dot(lm, lo, preferred_element_type=F32))
    q_dec, k_inv, k_end, dec = [], [], [], []
    for h in heads:
        b, b_last = bb[h][:rows], bb[h][rows:]
        q = q_ref[:, GLA_DK_PAD * h:GLA_DK_PAD * (h + 1)].astype(F32) * (GLA_DK ** -0.5)
        kk = k_ref[:, GLA_DK_PAD * h:GLA_DK_PAD * (h + 1)].astype(F32)
        q_dec.append((q * jnp.exp(b)).astype(BF16))
        k_inv.append((kk * jnp.exp(-b)).astype(BF16))
        k_end.append((kk * jnp.exp(b_last - b)).astype(BF16))
        dec.append(jnp.exp(b_last))
    v = [v_ref[:, GLA_DV_PAD * h:GLA_DV_PAD * (h + 1)] for h in heads]
    sc = []
    for h in heads:
        s_h = lax.dot_general(q_dec[h], k_inv[h], nt, preferred_element_type=F32)
        sc.append(jnp.where(same_chunk_causal, s_h, 0.0).astype(BF16))
    o_intra = [jnp.dot(sc[h], v[h], preferred_element_type=F32) for h in heads]
    d_st = [[lax.dot_general(v[h][c * ch:(c + 1) * ch], k_end[h][c * ch:(c + 1) * ch], tn,
                             preferred_element_type=F32) for c in range(nch)] for h in heads]
    st_in = []
    for h in heads:
        st = st_ref[h]
        states = []
        for c in range(nch):
            states.append(st.astype(BF16))
            st = st * dec[h][c * ch:c * ch + 1, :] + d_st[h][c]
        st_ref[h] = st
        st_in.append(states)
    for h in heads:
        o_inter = [lax.dot_general(q_dec[h][c * ch:(c + 1) * ch], st_in[h][c], nt,
                                   preferred_element_type=F32) for c in range(nch)]
        o = o_intra[h] + jnp.concatenate(o_inter, axis=0)
        ms = jnp.sum(o * o, axis=1, keepdims=True) * (1.0 / GLA_DV)
        y = o * lax.rsqrt(ms + RMS_EPS) * gn
        rr = r_ref[:, GLA_DV_PAD * h:GLA_DV_PAD * (h + 1)].astype(F32)
        o_ref[:, GLA_DV_PAD * h:GLA_DV_PAD * (h + 1)] = (
            y * (rr / (1.0 + jnp.exp(-rr)))).astype(o_ref.dtype)


def gla_attention(proj, gw_pad, gb_pad, gn_pad, layer, rows=256, hps=2):
    s = proj.shape[0]
    rows = min(rows, s)
    ch = GLA_CHUNK
    dkw, dvw = hps * GLA_DK_PAD, hps * GLA_DV_PAD
    qi, ki = OFF_BQ // dkw, OFF_BK // dkw
    vi, ri = OFF_BV // dvw, OFF_BR // dvw
    li = OFF_BLR // GLA_LR_PAD
    idx = np.arange(rows)
    same = (idx[:, None] // ch) == (idx[None, :] // ch)
    prefix = same & (idx[None, :] <= idx[:, None])
    lm = jnp.asarray(np.concatenate([prefix, same], axis=0).astype(np.float32), dtype=BF16)
    return pl.pallas_call(
        functools.partial(_gla_kernel, rows=rows, hps=hps),
        grid=(GLA_HEADS // hps, s // rows),
        in_specs=[pl.BlockSpec((rows, dkw), lambda g, t: (t, qi + g)),
                  pl.BlockSpec((rows, dkw), lambda g, t: (t, ki + g)),
                  pl.BlockSpec((rows, dvw), lambda g, t: (t, vi + g)),
                  pl.BlockSpec((rows, dvw), lambda g, t: (t, ri + g)),
                  pl.BlockSpec((rows, GLA_LR_PAD), lambda g, t: (t, li)),
                  pl.BlockSpec((None, hps, GLA_LR_PAD, GLA_DK_PAD),
                               lambda g, t: (layer, g, 0, 0)),
                  pl.BlockSpec((None, hps, 1, GLA_DK_PAD), lambda g, t: (layer, g, 0, 0)),
                  pl.BlockSpec((None, 1, GLA_DV_PAD), lambda g, t: (layer, 0, 0)),
                  pl.BlockSpec((2 * rows, rows), lambda g, t: (0, 0))],
        out_specs=pl.BlockSpec((rows, dvw), lambda g, t: (t, g)),
        out_shape=jax.ShapeDtypeStruct((s, GLA_HEADS * GLA_DV_PAD), BF16),
        scratch_shapes=[pltpu.VMEM((hps, GLA_DV_PAD, GLA_DK_PAD), F32)],
        compiler_params=_cparams(("parallel", "arbitrary")),
        name="gla",
    )(proj, proj, proj, proj, proj, gw_pad, gb_pad, gn_pad, lm)


def _sb_kernel(q_ref, k_ref, v_ref, u_ref, o_ref, acc_ref, car_ref,
               z0_ref, z1_ref, nl0_ref, nl1_ref, zc0_ref, zc1_ref, w0_ref, w1_ref,
               *, bq, nblk, unroll):
    lane = lax.broadcasted_iota(jnp.int32, (bq, LANES), 1)
    low = lane < HEAD_DIM
    nt = (((1,), (1,)), ((), ()))
    z_ref, nl_ref = (z0_ref, z1_ref), (nl0_ref, nl1_ref)
    zc_ref, w_ref = (zc0_ref, zc1_ref), (w0_ref, w1_ref)

    def rows(blk):
        if isinstance(blk, int):
            return pl.ds(blk * bq, bq)
        return pl.ds(pl.multiple_of(blk * bq, bq), bq)

    def strict_mask():
        rowi = lax.broadcasted_iota(jnp.int32, (bq, bq), 0)
        coli = lax.broadcasted_iota(jnp.int32, (bq, bq), 1)
        return coli < rowi

    def s1a(item, slot):
        i, c = item
        qb = q_ref[rows(i), :]
        kc = k_ref[rows(c), :]
        zq = jnp.zeros_like(qb)
        for h in range(2):
            qm = jnp.where(low, qb, zq) if h == 0 else jnp.where(low, zq, qb)
            z_ref[slot][h] = lax.dot_general(qm, kc, nt, preferred_element_type=F32)

    def s1b(slot, masked):
        for h in range(2):
            z = z_ref[slot][h]
            nl = jnp.maximum(z, 0.0) + jnp.log(1.0 + jnp.exp2(jnp.abs(z) * (-LOG2E)))
            if masked:
                nl = jnp.where(strict_mask(), nl, 0.0)
            nl_ref[slot][h] = nl.astype(BF16)
            zc_ref[slot][h] = z

    def s2(item, slot, masked, first):
        i, _ = item
        for h in range(2):
            cs = jnp.dot(nl_ref[slot][h], u_ref[...], preferred_element_type=F32)
            tot = jnp.broadcast_to(cs[:, 0:1], (bq, LANES))
            if first:
                arg = zc_ref[slot][h] - cs
                car_ref[h, rows(i), :] = tot
            else:
                car = car_ref[h, rows(i), :]
                arg = zc_ref[slot][h] - cs - jnp.concatenate([car, car], axis=1)
                car_ref[h, rows(i), :] = car + tot
            w = jnp.exp(arg)
            if masked:
                w = jnp.where(strict_mask(), w, 0.0)
            w_ref[slot][h] = w.astype(BF16)

    def s3(item, slot, first):
        i, c = item
        vc = v_ref[rows(c), :]
        for h in range(2):
            pv = jnp.dot(w_ref[slot][h], vc, preferred_element_type=F32)
            if first:
                acc_ref[h, rows(i), :] = pv
            else:
                acc_ref[h, rows(i), :] += pv

    def run_pipeline(items, step, masked, first):
        n = len(items)

        def tick(par, its, do):
            if do[1]:
                s1b(1 - par, masked)
            if do[0]:
                s1a(its[0], par)
            if do[2]:
                s2(its[2], par, masked, first)
            if do[3]:
                s3(its[3], 1 - par, first)

        def static_tick(t):
            do = [0 <= t - j < n for j in range(4)]
            its = [items[t - j] if do[j] else None for j in range(4)]
            tick(t % 2, its, do)

        lo_t, hi_t = 3, n - 1
        groups = max(hi_t - lo_t + 1, 0) // unroll
        if groups < 2:
            groups = 0
        for t in range(min(lo_t, n + 3)):
            static_tick(t)
        if groups:
            def body(_, carry):
                its = [(carry[2 * j], carry[2 * j + 1]) for j in range(4)]
                for k in range(unroll):
                    tick((lo_t + k) % 2, its, [True] * 4)
                    its = [step(*its[0])] + its[:3]
                return tuple(x for it in its for x in it)

            init = tuple(jnp.int32(x) for j in range(4) for x in items[lo_t - j])
            lax.fori_loop(0, groups, body, init)
        for t in range(lo_t + unroll * groups, n + 3):
            static_tick(t)

    diag = [(i, i) for i in range(nblk)]
    run_pipeline(diag, lambda i, c: (i + 1, c + 1), masked=True, first=True)

    off = [(i, c) for i in range(1, nblk) for c in range(i - 1, -1, -1)]

    def off_step(i, c):
        wrap = c == 0
        i2 = jnp.where(wrap, i + 1, i)
        return i2, jnp.where(wrap, i2 - 1, c - 1)

    if off:
        run_pipeline(off, off_step, masked=False, first=False)

    for blk in range(nblk):
        r = rows(blk)
        o_ref[r, :] = jnp.where(low, acc_ref[0, r, :], acc_ref[1, r, :]).astype(o_ref.dtype)


def sb_attention(proj, bq=256, unroll=4):
    s = proj.shape[0]
    bq = min(bq, s)
    nblk = s // bq
    qi = OFF_CQ // LANES
    ki = OFF_CK // LANES
    vi = OFF_CV // LANES
    u = (np.arange(bq)[:, None] >= np.arange(bq)[None, :]).astype(np.float32)
    u = jnp.asarray(u, dtype=BF16)
    return pl.pallas_call(
        functools.partial(_sb_kernel, bq=bq, nblk=nblk, unroll=unroll),
        grid=(SB_HEADS // 2,),
        in_specs=[pl.BlockSpec((s, LANES), lambda p: (0, qi + p)),
                  pl.BlockSpec((s, LANES), lambda p: (0, ki + p)),
                  pl.BlockSpec((s, LANES), lambda p: (0, vi + p)),
                  pl.BlockSpec((bq, bq), lambda p: (0, 0))],
        out_specs=pl.BlockSpec((s, LANES), lambda p: (0, p)),
        out_shape=jax.ShapeDtypeStruct((s, SB_WIDTH), BF16),
        scratch_shapes=[pltpu.VMEM((2, s, LANES), F32),
                        pltpu.VMEM((2, s, LANES), F32),
                        pltpu.VMEM((2, bq, bq), F32),
                        pltpu.VMEM((2, bq, bq), F32),
                        pltpu.VMEM((2, bq, bq), BF16),
                        pltpu.VMEM((2, bq, bq), BF16),
                        pltpu.VMEM((2, bq, bq), F32),
                        pltpu.VMEM((2, bq, bq), F32),
                        pltpu.VMEM((2, bq, bq), BF16),
                        pltpu.VMEM((2, bq, bq), BF16)],
        compiler_params=_cparams(("parallel",)),
        name="stickbreak",
    )(proj, proj, proj, u)


def _oproj_kernel(a_ref, b_ref, c_ref, wa_ref, wb_ref, wc_ref, ga_ref, gc_ref,
                  x_ref, gn_ref, xo_ref, ho_ref):
    an = _rms(a_ref[...].astype(F32), ga_ref[...]).astype(BF16)
    cn = _rms(c_ref[...].astype(F32), gc_ref[...]).astype(BF16)
    y = (jnp.dot(an, wa_ref[...], preferred_element_type=F32)
         + jnp.dot(b_ref[...], wb_ref[...], preferred_element_type=F32)
         + jnp.dot(cn, wc_ref[...], preferred_element_type=F32))
    xn = x_ref[...] + y
    xo_ref[...] = xn
    ho_ref[...] = _rms(xn, gn_ref[...]).astype(ho_ref.dtype)


def out_projection(a, b, c, wa, wb, wc, ga, gc, x, gn, layer, tm=256):
    s, d = x.shape
    tm = min(tm, s)
    row = lambda i: (i, 0)
    fixed = lambda i: (layer, 0, 0)
    per_layer = lambda arr: pl.BlockSpec((None,) + arr.shape[1:], fixed)
    return pl.pallas_call(
        _oproj_kernel,
        grid=(s // tm,),
        in_specs=[pl.BlockSpec((tm, a.shape[1]), row),
                  pl.BlockSpec((tm, b.shape[1]), row),
                  pl.BlockSpec((tm, c.shape[1]), row),
                  per_layer(wa), per_layer(wb), per_layer(wc), per_layer(ga), per_layer(gc),
                  pl.BlockSpec((tm, d), row),
                  per_layer(gn)],
        out_specs=[pl.BlockSpec((tm, d), row), pl.BlockSpec((tm, d), row)],
        out_shape=[jax.ShapeDtypeStruct((s, d), F32), jax.ShapeDtypeStruct((s, d), BF16)],
        compiler_params=_cparams(("parallel",)),
        name="out_proj",
    )(a, b, c, wa, wb, wc, ga, gc, x, gn)


def _mlp_kernel(h_ref, x_ref, w1_ref, w2_ref, gn_ref, *out_refs, final):
    acc_ref = out_refs[0]
    j = pl.program_id(1)

    @pl.when(j == 0)
    def _():
        acc_ref[...] = x_ref[...]

    u = jnp.dot(h_ref[...], w1_ref[...].astype(BF16), preferred_element_type=F32)
    act = jnp.square(jnp.maximum(u, 0.0)).astype(BF16)
    acc_ref[...] += jnp.dot(act, w2_ref[...].astype(BF16), preferred_element_type=F32)

    @pl.when(j == pl.num_programs(1) - 1)
    def _():
        y = _rms(acc_ref[...], gn_ref[...])
        if final:
            acc_ref[...] = y
        else:
            out_refs[1][...] = y.astype(out_refs[1].dtype)


def mlp_block(h, x, w1_all, w2_all, layer, gn, final, tm=1024, tf=512):
    s, d = x.shape
    ff = w1_all.shape[2]
    tm = min(tm, s)
    row = lambda i, j: (i, 0)
    once = pl.Buffered(1)
    out_specs = [pl.BlockSpec((tm, d), row)]
    out_shape = [jax.ShapeDtypeStruct((s, d), F32)]
    if not final:
        out_specs.append(pl.BlockSpec((tm, d), row))
        out_shape.append(jax.ShapeDtypeStruct((s, d), BF16))
    return pl.pallas_call(
        functools.partial(_mlp_kernel, final=final),
        grid=(s // tm, ff // tf),
        in_specs=[pl.BlockSpec((tm, d), row, pipeline_mode=once),
                  pl.BlockSpec((tm, d), row, pipeline_mode=once),
                  pl.BlockSpec((None, d, tf), lambda i, j: (layer, 0, j)),
                  pl.BlockSpec((None, tf, d), lambda i, j: (layer, j, 0)),
                  pl.BlockSpec((None, 1, d), lambda i, j: (layer, 0, 0))],
        out_specs=out_specs,
        out_shape=out_shape,
        compiler_params=_cparams(("parallel", "arbitrary")),
        name="mlp",
    )(h, x, w1_all, w2_all, gn)


def _zero_cols(w, width):
    return jnp.zeros(w.shape[:-1] + (width,), w.dtype)


def _head_cols(w, off, heads, width, padded):
    pieces = []
    for h in range(heads):
        pieces.append(w[..., off + h * width:off + (h + 1) * width])
        if padded > width:
            pieces.append(_zero_cols(w, padded - width))
    return pieces


def _head_rows(w, off, heads, width, padded):
    pieces = []
    for h in range(heads):
        pieces.append(w[..., off + h * width:off + (h + 1) * width, :])
        if padded > width:
            pieces.append(jnp.zeros(w.shape[:-2] + (padded - width, w.shape[-1]), w.dtype))
    return pieces


def _prep_w_in(w_in):
    seg = lambda off, width: w_in[..., off:off + width]
    scale = HEAD_DIM ** -0.5
    pieces = (
        _head_cols(w_in, _O_BV, GLA_HEADS, GLA_DV, GLA_DV_PAD)
        + _head_cols(w_in, _O_BR, GLA_HEADS, GLA_DV, GLA_DV_PAD)
        + _head_cols(w_in, _O_BQ, GLA_HEADS, GLA_DK, GLA_DK_PAD)
        + _head_cols(w_in, _O_BK, GLA_HEADS, GLA_DK, GLA_DK_PAD)
        + [seg(_O_CQ, SB_WIDTH) * scale, seg(_O_CK, SB_WIDTH), seg(_O_CV, SB_WIDTH)]
        + [seg(_O_AQ + HEAD_DIM * h, HEAD_DIM) * scale for h in SWA_Q_PERM]
        + [seg(_O_AK, SWA_KV_WIDTH), seg(_O_AV, SWA_KV_WIDTH), seg(_O_BLR, GLA_LOWRANK),
           _zero_cols(w_in, PROJ_PAD_WIDTH - OFF_BLR - GLA_LOWRANK)])
    return jnp.concatenate(pieces, axis=-1).astype(BF16)


def _swa_perm_cols(w):
    return jnp.concatenate([w[..., HEAD_DIM * h:HEAD_DIM * (h + 1)] for h in SWA_Q_PERM], axis=-1)


def kernel(x, norm_mix, w_in, swa_sinks, rel_bias, gla_gate_w, gla_gate_b, gla_norm,
           swa_out_norm, sb_out_norm, w_out, norm_mlp, w_mlp_in, w_mlp_out, norm_final):
    depth = w_in.shape[0]
    xs = x[0]
    gla_w = GLA_HEADS * GLA_DV
    w_in_p = prep_w_in(jnp.swapaxes(w_in, 1, 2))
    wa = jnp.concatenate([w_out[:, HEAD_DIM * h:HEAD_DIM * (h + 1)] for h in SWA_Q_PERM],
                         axis=1).astype(BF16)
    wb = jnp.concatenate(_head_rows(w_out, SWA_WIDTH, GLA_HEADS, GLA_DV, GLA_DV_PAD),
                         axis=1).astype(BF16)
    wc = w_out[:, SWA_WIDTH + gla_w:].astype(BF16)
    ga = _swa_perm_cols(swa_out_norm)[:, None, :]
    gc = sb_out_norm[:, None, :]
    g_mlp = norm_mlp[:, None, :]
    g_next = jnp.concatenate([norm_mix[1:], norm_final[None]], axis=0)[:, None, :]
    sinks_p = jnp.concatenate([swa_sinks[:, h:h + 1] for h in SWA_Q_PERM], axis=1)
    bias_tab = swa_bias_table(rel_bias)
    gw = jnp.stack([jnp.pad(gla_gate_w[:, :, GLA_DK * h:GLA_DK * (h + 1)],
                            ((0, 0), (0, GLA_LR_PAD - GLA_LOWRANK), (0, GLA_DK_PAD - GLA_DK)))
                    for h in range(GLA_HEADS)], axis=1).astype(BF16)
    gb = jnp.stack([jnp.pad(gla_gate_b[:, GLA_DK * h:GLA_DK * (h + 1)],
                            ((0, 0), (0, GLA_DK_PAD - GLA_DK)))
                    for h in range(GLA_HEADS)], axis=1)[:, :, None, :]
    gn = jnp.pad(gla_norm, ((0, 0), (0, GLA_DV_PAD - GLA_DV)))[:, None, :]

    h = None
    for l in range(depth):
        if l == 0:
            proj = in_projection(xs, w_in_p, l, gain=norm_mix[0])
        else:
            proj = in_projection(h, w_in_p, l)
        a = swa_attention(proj, sinks_p, bias_tab, l)
        b = gla_attention(proj, gw, gb, gn, l)
        c = sb_attention(proj)
        xs, hm = out_projection(a, b, c, wa, wb, wc, ga, gc, xs, g_mlp, l)
        if l + 1 < depth:
            xs, h = mlp_block(hm, xs, w_mlp_in, w_mlp_out, l, g_next, final=False)
        else:
            (out,) = mlp_block(hm, xs, w_mlp_in, w_mlp_out, l, g_next, final=True)
    return out[None]
```

```python
import functools
import math

import numpy as np
import jax
import jax.numpy as jnp
from jax import lax
from jax.experimental import pallas as pl
from jax.experimental.pallas import tpu as pltpu

F32 = jnp.float32
BF16 = jnp.bfloat16

D_MODEL = 2048
HEAD_DIM = 64
SWA_HEADS = 12
SWA_KV_HEADS = 4
SWA_WIDTH = SWA_HEADS * HEAD_DIM
SWA_KV_WIDTH = SWA_KV_HEADS * HEAD_DIM
SWA_BLOCK = 128
WINDOW = 128
REL_BUCKETS = 32
REL_MAX_DIST = 128
GLA_HEADS = 4
GLA_DV = 192
GLA_DK = 96
GLA_DK_PAD = 128
GLA_DV_PAD = 256
GLA_LOWRANK = 16
GLA_LR_PAD = 128
GLA_CHUNK = 64
GATE_NORMALIZER = 16.0
GATE_LOG_MIN = -1.0
SB_HEADS = 8
SB_WIDTH = SB_HEADS * HEAD_DIM
D_FF = 4 * D_MODEL
RMS_EPS = 1e-6
LOG2E = 1.4426950408889634
NEG_INF = -1e30

LANES = 128
VMEM_LIMIT = 56 * 1024 * 1024

OFF_BV = 0
OFF_BR = 1024
OFF_BQ = 2048
OFF_BK = 2560
OFF_CQ = 3072
OFF_CK = 3584
OFF_CV = 4096
OFF_AQ = 4608
OFF_AK = 5376
OFF_AV = 5632
OFF_BLR = 5888
PROJ_PAD_WIDTH = 6144

SWA_Q_PERM = (0, 3, 1, 4, 2, 5, 6, 9, 7, 10, 8, 11)

_O_AQ, _O_AK, _O_AV = 0, 768, 1024
_O_BQ, _O_BK, _O_BV, _O_BR, _O_BLR = 1280, 1664, 2048, 2816, 3584
_O_CQ, _O_CK, _O_CV = 3600, 4112, 4624


def _cparams(sem):
    return pltpu.CompilerParams(dimension_semantics=sem, vmem_limit_bytes=VMEM_LIMIT)


def _rms(x, gain):
    ms = jnp.mean(x * x, axis=-1, keepdims=True)
    return x * lax.rsqrt(ms + RMS_EPS) * gain


def _split_bf16(x):
    hi = x.astype(BF16)
    lo = (x - hi.astype(F32)).astype(BF16)
    return hi, lo


_NT_DIMS = (((1,), (1,)), ((), ()))


def _matmul_kernel(h_ref, w_ref, o_ref):
    o_ref[...] = lax.dot_general(h_ref[...], w_ref[...], _NT_DIMS,
                                 preferred_element_type=F32).astype(o_ref.dtype)


def _norm_matmul_kernel(x_ref, g_ref, w_ref, o_ref, h_ref):
    @pl.when(pl.program_id(1) == 0)
    def _():
        h_ref[...] = _rms(x_ref[...], g_ref[...]).astype(h_ref.dtype)

    o_ref[...] = lax.dot_general(h_ref[...], w_ref[...], _NT_DIMS,
                                 preferred_element_type=F32).astype(o_ref.dtype)


def in_projection(h, w_all, layer, gain=None, tm=1024, tn=512):
    s, d = h.shape
    n = w_all.shape[1]
    tm = min(tm, s)
    w_spec = pl.BlockSpec((None, tn, d), lambda i, j: (layer, j, 0))
    h_spec = pl.BlockSpec((tm, d), lambda i, j: (i, 0))
    common = dict(
        grid=(s // tm, n // tn),
        out_specs=pl.BlockSpec((tm, tn), lambda i, j: (i, j)),
        out_shape=jax.ShapeDtypeStruct((s, n), BF16),
        compiler_params=_cparams(("parallel", "arbitrary")),
        name="in_proj",
    )
    if gain is None:
        return pl.pallas_call(_matmul_kernel, in_specs=[h_spec, w_spec], **common)(h, w_all)
    return pl.pallas_call(
        _norm_matmul_kernel,
        in_specs=[h_spec, pl.BlockSpec((1, d), lambda i, j: (0, 0)), w_spec],
        scratch_shapes=[pltpu.VMEM((tm, d), BF16)],
        **common,
    )(h, gain.reshape(1, d), w_all)


def _w_in_moves():
    scale = HEAD_DIM ** -0.5
    moves = []
    for h in range(GLA_HEADS):
        moves.append((_O_BV + GLA_DV * h, OFF_BV + GLA_DV_PAD * h, GLA_DV, 1.0))
        moves.append((_O_BR + GLA_DV * h, OFF_BR + GLA_DV_PAD * h, GLA_DV, 1.0))
        moves.append((_O_BQ + GLA_DK * h, OFF_BQ + GLA_DK_PAD * h, GLA_DK, 1.0))
        moves.append((_O_BK + GLA_DK * h, OFF_BK + GLA_DK_PAD * h, GLA_DK, 1.0))
    moves += [(_O_CQ, OFF_CQ, SB_WIDTH, scale), (_O_CK, OFF_CK, SB_WIDTH, 1.0),
              (_O_CV, OFF_CV, SB_WIDTH, 1.0)]
    for j, h in enumerate(SWA_Q_PERM):
        moves.append((_O_AQ + HEAD_DIM * h, OFF_AQ + HEAD_DIM * j, HEAD_DIM, scale))
    moves += [(_O_AK, OFF_AK, SWA_KV_WIDTH, 1.0), (_O_AV, OFF_AV, SWA_KV_WIDTH, 1.0),
              (_O_BLR, OFF_BLR, GLA_LOWRANK, 1.0)]
    return moves


def _w_in_prep_kernel(w_ref, o_ref):
    o_ref[...] = jnp.zeros_like(o_ref)
    for src, dst, width, scale in _w_in_moves():
        piece = w_ref[src:src + width, :]
        if scale != 1.0:
            piece = piece * scale
        o_ref[dst:dst + width, :] = piece.astype(o_ref.dtype)


def prep_w_in(w_in_t, tc=256):
    layers, n, d = w_in_t.shape
    return pl.pallas_call(
        _w_in_prep_kernel,
        grid=(layers, d // tc),
        in_specs=[pl.BlockSpec((None, n, tc), lambda l, i: (l, 0, i))],
        out_specs=pl.BlockSpec((None, PROJ_PAD_WIDTH, tc), lambda l, i: (l, 0, i)),
        out_shape=jax.ShapeDtypeStruct((layers, PROJ_PAD_WIDTH, d), BF16),
        compiler_params=_cparams(("parallel", "parallel")),
        name="w_in_prep",
    )(w_in_t)


def _swa_kernel(sink_ref, q_ref, kp_ref, kc_ref, vp_ref, vc_ref, bias_ref, o_ref, *, layer):
    i = pl.program_id(0)
    blk = SWA_BLOCK
    k = jnp.concatenate([kp_ref[...], kc_ref[...]], axis=0)
    v = jnp.concatenate([vp_ref[...], vc_ref[...]], axis=0)
    lane = lax.broadcasted_iota(jnp.int32, (blk, LANES), 1)
    col = lax.broadcasted_iota(jnp.int32, (blk, 2 * blk), 1)
    kill = jnp.logical_and(col < blk, i == 0)
    low = lane < HEAD_DIM
    for b in range(SWA_HEADS // 2):
        qb = q_ref[:, LANES * b:LANES * (b + 1)]
        kv_lo = LANES * (b // 3)
        kb = k[:, kv_lo:kv_lo + LANES]
        vb = v[:, kv_lo:kv_lo + LANES]
        outs = []
        for half in range(2):
            hq = 2 * b + half
            keep = low if half == 0 else jnp.logical_not(low)
            qm = jnp.where(keep, qb, jnp.zeros_like(qb))
            s = lax.dot_general(qm, kb, (((1,), (1,)), ((), ())),
                                preferred_element_type=F32)
            s = jnp.where(kill, NEG_INF, s + bias_ref[hq])
            sink = sink_ref[layer, hq]
            m = jnp.maximum(jnp.max(s, axis=1, keepdims=True), sink)
            p = jnp.exp(s - m)
            denom = jnp.sum(p, axis=1, keepdims=True) + jnp.exp(sink - m)
            o = jnp.dot(p.astype(BF16), vb, preferred_element_type=F32)
            outs.append(o / denom)
        o_ref[:, LANES * b:LANES * (b + 1)] = jnp.where(low, outs[0], outs[1]).astype(o_ref.dtype)


def swa_attention(proj, sinks_perm, bias_tab, layer):
    s = proj.shape[0]
    blk = SWA_BLOCK
    qi = OFF_AQ // SWA_WIDTH
    ki = OFF_AK // SWA_KV_WIDTH
    vi = OFF_AV // SWA_KV_WIDTH
    prev = lambda i: jnp.maximum(i - 1, 0)
    return pl.pallas_call(
        functools.partial(_swa_kernel, layer=layer),
        grid=(s // blk,),
        in_specs=[pl.BlockSpec(memory_space=pltpu.SMEM),
                  pl.BlockSpec((blk, SWA_WIDTH), lambda i: (i, qi)),
                  pl.BlockSpec((blk, SWA_KV_WIDTH), lambda i: (prev(i), ki)),
                  pl.BlockSpec((blk, SWA_KV_WIDTH), lambda i: (i, ki)),
                  pl.BlockSpec((blk, SWA_KV_WIDTH), lambda i: (prev(i), vi)),
                  pl.BlockSpec((blk, SWA_KV_WIDTH), lambda i: (i, vi)),
                  pl.BlockSpec((SWA_HEADS, blk, 2 * blk), lambda i: (0, 0, 0))],
        out_specs=pl.BlockSpec((blk, SWA_WIDTH), lambda i: (i, 0)),
        out_shape=jax.ShapeDtypeStruct((s, SWA_WIDTH), BF16),
        compiler_params=_cparams(("parallel",)),
        name="swa",
    )(sinks_perm, proj, proj, proj, proj, proj, bias_tab)


def _t5_causal_bucket(dist):
    max_exact = REL_BUCKETS // 2
    is_small = dist < max_exact
    ratio = (jnp.log(jnp.maximum(dist, 1).astype(F32) / max_exact)
             / math.log(REL_MAX_DIST / max_exact))
    large = max_exact + (ratio * (REL_BUCKETS - max_exact)).astype(jnp.int32)
    large = jnp.minimum(large, REL_BUCKETS - 1)
    return jnp.where(is_small, dist, large)


def swa_bias_table(rel_bias):
    blk = SWA_BLOCK
    qpos = jnp.arange(blk) + blk
    kpos = jnp.arange(2 * blk)
    dist = qpos[:, None] - kpos[None, :]
    in_window = (dist >= 0) & (dist < WINDOW)
    bucket = _t5_causal_bucket(jnp.maximum(dist, 0))
    table = rel_bias.astype(F32)
    table = jnp.stack([table[:, h] for h in SWA_Q_PERM])
    bias = jnp.full((SWA_HEADS, blk, 2 * blk), NEG_INF, F32)
    for bkt in range(REL_BUCKETS):
        hit = (in_window & (bucket == bkt))[None]
        bias = jnp.where(hit, table[:, bkt][:, None, None], bias)
    return bias


def _gla_kernel(q_ref, k_ref, v_ref, r_ref, lr_ref, gw_ref, gb_ref, gn_ref, lm_ref, o_ref,
                st_ref, *, rows, hps):
    t = pl.program_id(1)
    ch = GLA_CHUNK
    nch = rows // ch
    heads = range(hps)

    @pl.when(t == 0)
    def _():
        st_ref[...] = jnp.zeros_like(st_ref)

    nt = (((1,), (1,)), ((), ()))
    tn = (((0,), (0,)), ((), ()))
    rowi = lax.broadcasted_iota(jnp.int32, (rows, rows), 0)
    coli = lax.broadcasted_iota(jnp.int32, (rows, rows), 1)
    same_chunk_causal = jnp.logical_and(coli <= rowi, (rowi // ch) == (coli // ch))
    lm = lm_ref[...]
    lr = lr_ref[...]
    gn = gn_ref[...]

    gp = [jnp.dot(lr, gw_ref[h], preferred_element_type=F32) + gb_ref[h] for h in heads]
    lg = []
    for h in heads:
        log_sig = jnp.minimum(gp[h], 0.0) - jnp.log(1.0 + jnp.exp(-jnp.abs(gp[h])))
        lg.append(jnp.maximum(log_sig * (1.0 / GATE_NORMALIZER), GATE_LOG_MIN))
    bb = []
    for h in heads:
        hi, lo = _split_bf16(lg[h])
        bb.append(jnp.dot(lm, hi, preferred_element_type=F32)
                  + jnp.dot(lm, lo, preferred_element_type=F32))
    q_dec, k_inv, k_end, dec = [], [], [], []
    for h in heads:
        b, b_last = bb[h][:rows], bb[h][rows:]
        q = q_ref[:, GLA_DK_PAD * h:GLA_DK_PAD * (h + 1)].astype(F32) * (GLA_DK ** -0.5)
        kk = k_ref[:, GLA_DK_PAD * h:GLA_DK_PAD * (h + 1)].astype(F32)
        q_dec.append((q * jnp.exp(b)).astype(BF16))
        k_inv.append((kk * jnp.exp(-b)).astype(BF16))
        k_end.append((kk * jnp.exp(b_last - b)).astype(BF16))
        dec.append(jnp.exp(b_last))
    v = [v_ref[:, GLA_DV_PAD * h:GLA_DV_PAD * (h + 1)] for h in heads]
    sc = []
    for h in heads:
        s_h = lax.dot_general(q_dec[h], k_inv[h], nt, preferred_element_type=F32)
        sc.append(jnp.where(same_chunk_causal, s_h, 0.0).astype(BF16))
    o_intra = [jnp.dot(sc[h], v[h], preferred_element_type=F32) for h in heads]
    d_st = [[lax.dot_general(v[h][c * ch:(c + 1) * ch], k_end[h][c * ch:(c + 1) * ch], tn,
                             preferred_element_type=F32) for c in range(nch)] for h in heads]
    st_in = []
    for h in heads:
        st = st_ref[h]
        states = []
        for c in range(nch):
            states.append(st.astype(BF16))
            st = st * dec[h][c * ch:c * ch + 1, :] + d_st[h][c]
        st_ref[h] = st
        st_in.append(states)
    for h in heads:
        o_inter = [lax.dot_general(q_dec[h][c * ch:(c + 1) * ch], st_in[h][c], nt,
                                   preferred_element_type=F32) for c in range(nch)]
        o = o_intra[h] + jnp.concatenate(o_inter, axis=0)
        ms = jnp.sum(o * o, axis=1, keepdims=True) * (1.0 / GLA_DV)
        y = o * lax.rsqrt(ms + RMS_EPS) * gn
        rr = r_ref[:, GLA_DV_PAD * h:GLA_DV_PAD * (h + 1)].astype(F32)
        o_ref[:, GLA_DV_PAD * h:GLA_DV_PAD * (h + 1)] = (
            y * (rr / (1.0 + jnp.exp(-rr)))).astype(o_ref.dtype)


def gla_attention(proj, gw_pad, gb_pad, gn_pad, layer, rows=256, hps=2):
    s = proj.shape[0]
    rows = min(rows, s)
    ch = GLA_CHUNK
    dkw, dvw = hps * GLA_DK_PAD, hps * GLA_DV_PAD
    qi, ki = OFF_BQ // dkw, OFF_BK // dkw
    vi, ri = OFF_BV // dvw, OFF_BR // dvw
    li = OFF_BLR // GLA_LR_PAD
    idx = np.arange(rows)
    same = (idx[:, None] // ch) == (idx[None, :] // ch)
    prefix = same & (idx[None, :] <= idx[:, None])
    lm = jnp.asarray(np.concatenate([prefix, same], axis=0).astype(np.float32), dtype=BF16)
    return pl.pallas_call(
        functools.partial(_gla_kernel, rows=rows, hps=hps),
        grid=(GLA_HEADS // hps, s // rows),
        in_specs=[pl.BlockSpec((rows, dkw), lambda g, t: (t, qi + g)),
                  pl.BlockSpec((rows, dkw), lambda g, t: (t, ki + g)),
                  pl.BlockSpec((rows, dvw), lambda g, t: (t, vi + g)),
                  pl.BlockSpec((rows, dvw), lambda g, t: (t, ri + g)),
                  pl.BlockSpec((rows, GLA_LR_PAD), lambda g, t: (t, li)),
                  pl.BlockSpec((None, hps, GLA_LR_PAD, GLA_DK_PAD),
                               lambda g, t: (layer, g, 0, 0)),
                  pl.BlockSpec((None, hps, 1, GLA_DK_PAD), lambda g, t: (layer, g, 0, 0)),
                  pl.BlockSpec((None, 1, GLA_DV_PAD), lambda g, t: (layer, 0, 0)),
                  pl.BlockSpec((2 * rows, rows), lambda g, t: (0, 0))],
        out_specs=pl.BlockSpec((rows, dvw), lambda g, t: (t, g)),
        out_shape=jax.ShapeDtypeStruct((s, GLA_HEADS * GLA_DV_PAD), BF16),
        scratch_shapes=[pltpu.VMEM((hps, GLA_DV_PAD, GLA_DK_PAD), F32)],
        compiler_params=_cparams(("parallel", "arbitrary")),
        name="gla",
    )(proj, proj, proj, proj, proj, gw_pad, gb_pad, gn_pad, lm)


def _sb_kernel(q_ref, k_ref, v_ref, u_ref, o_ref, acc_ref, car_ref, qm_ref,
               z0_ref, z1_ref, nl0_ref, nl1_ref, zc0_ref, zc1_ref, w0_ref, w1_ref,
               *, bq, nblk, unroll):
    lane = lax.broadcasted_iota(jnp.int32, (bq, LANES), 1)
    low = lane < HEAD_DIM
    nt = (((1,), (1,)), ((), ()))
    z_ref, nl_ref = (z0_ref, z1_ref), (nl0_ref, nl1_ref)
    zc_ref, w_ref = (zc0_ref, zc1_ref), (w0_ref, w1_ref)

    def rows(blk):
        if isinstance(blk, int):
            return pl.ds(blk * bq, bq)
        return pl.ds(pl.multiple_of(blk * bq, bq), bq)

    def strict_mask():
        rowi = lax.broadcasted_iota(jnp.int32, (bq, bq), 0)
        coli = lax.broadcasted_iota(jnp.int32, (bq, bq), 1)
        return coli < rowi

    def s1a(item, slot):
        i, c = item
        kc = k_ref[rows(c), :]
        for h in range(2):
            z_ref[slot][h] = lax.dot_general(qm_ref[h, rows(i), :], kc, nt,
                                             preferred_element_type=F32)

    def s1b(slot, masked):
        for h in range(2):
            z = z_ref[slot][h]
            nl = jnp.maximum(z, 0.0) + jnp.log(1.0 + jnp.exp2(jnp.abs(z) * (-LOG2E)))
            if masked:
                nl = jnp.where(strict_mask(), nl, 0.0)
            nl_ref[slot][h] = nl.astype(BF16)
            zc_ref[slot][h] = z

    def s2(item, slot, masked, first):
        i, _ = item
        for h in range(2):
            cs = jnp.dot(nl_ref[slot][h], u_ref[...], preferred_element_type=F32)
            tot = jnp.broadcast_to(cs[:, 0:1], (bq, LANES))
            if first:
                arg = zc_ref[slot][h] - cs
                car_ref[h, rows(i), :] = tot
            else:
                car = car_ref[h, rows(i), :]
                arg = zc_ref[slot][h] - cs - jnp.concatenate([car, car], axis=1)
                car_ref[h, rows(i), :] = car + tot
            w = jnp.exp(arg)
            if masked:
                w = jnp.where(strict_mask(), w, 0.0)
            w_ref[slot][h] = w.astype(BF16)

    def s3(item, slot, first):
        i, c = item
        vc = v_ref[rows(c), :]
        for h in range(2):
            pv = jnp.dot(w_ref[slot][h], vc, preferred_element_type=F32)
            if first:
                acc_ref[h, rows(i), :] = pv
            else:
                acc_ref[h, rows(i), :] += pv

    def run_pipeline(items, step, masked, first):
        n = len(items)

        def tick(par, its, do):
            if do[1]:
                s1b(1 - par, masked)
            if do[0]:
                s1a(its[0], par)
            if do[2]:
                s2(its[2], par, masked, first)
            if do[3]:
                s3(its[3], 1 - par, first)

        def static_tick(t):
            do = [0 <= t - j < n for j in range(4)]
            its = [items[t - j] if do[j] else None for j in range(4)]
            tick(t % 2, its, do)

        lo_t, hi_t = 3, n - 1
        groups = max(hi_t - lo_t + 1, 0) // unroll
        if groups < 2:
            groups = 0
        for t in range(min(lo_t, n + 3)):
            static_tick(t)
        if groups:
            def body(_, carry):
                its = [(carry[2 * j], carry[2 * j + 1]) for j in range(4)]
                for k in range(unroll):
                    tick((lo_t + k) % 2, its, [True] * 4)
                    its = [step(*its[0])] + its[:3]
                return tuple(x for it in its for x in it)

            init = tuple(jnp.int32(x) for j in range(4) for x in items[lo_t - j])
            lax.fori_loop(0, groups, body, init)
        for t in range(lo_t + unroll * groups, n + 3):
            static_tick(t)

    for blk in range(nblk):
        r = rows(blk)
        qb = q_ref[r, :]
        zq = jnp.zeros_like(qb)
        qm_ref[0, r, :] = jnp.where(low, qb, zq)
        qm_ref[1, r, :] = jnp.where(low, zq, qb)

    diag = [(i, i) for i in range(nblk)]
    run_pipeline(diag, lambda i, c: (i + 1, c + 1), masked=True, first=True)

    off = [(i, c) for i in range(1, nblk) for c in range(i - 1, -1, -1)]

    def off_step(i, c):
        wrap = c == 0
        i2 = jnp.where(wrap, i + 1, i)
        return i2, jnp.where(wrap, i2 - 1, c - 1)

    if off:
        run_pipeline(off, off_step, masked=False, first=False)

    for blk in range(nblk):
        r = rows(blk)
        o_ref[r, :] = jnp.where(low, acc_ref[0, r, :], acc_ref[1, r, :]).astype(o_ref.dtype)


def sb_attention(proj, bq=256, unroll=8):
    s = proj.shape[0]
    bq = min(bq, s)
    nblk = s // bq
    qi = OFF_CQ // LANES
    ki = OFF_CK // LANES
    vi = OFF_CV // LANES
    u = (np.arange(bq)[:, None] >= np.arange(bq)[None, :]).astype(np.float32)
    u = jnp.asarray(u, dtype=BF16)
    return pl.pallas_call(
        functools.partial(_sb_kernel, bq=bq, nblk=nblk, unroll=unroll),
        grid=(SB_HEADS // 2,),
        in_specs=[pl.BlockSpec((s, LANES), lambda p: (0, qi + p)),
                  pl.BlockSpec((s, LANES), lambda p: (0, ki + p)),
                  pl.BlockSpec((s, LANES), lambda p: (0, vi + p)),
                  pl.BlockSpec((bq, bq), lambda p: (0, 0))],
        out_specs=pl.BlockSpec((s, LANES), lambda p: (0, p)),
        out_shape=jax.ShapeDtypeStruct((s, SB_WIDTH), BF16),
        scratch_shapes=[pltpu.VMEM((2, s, LANES), F32),
                        pltpu.VMEM((2, s, LANES), F32),
                        pltpu.VMEM((2, s, LANES), BF16),
                        pltpu.VMEM((2, bq, bq), F32),
                        pltpu.VMEM((2, bq, bq), F32),
                        pltpu.VMEM((2, bq, bq), BF16),
                        pltpu.VMEM((2, bq, bq), BF16),
                        pltpu.VMEM((2, bq, bq), F32),
                        pltpu.VMEM((2, bq, bq), F32),
                        pltpu.VMEM((2, bq, bq), BF16),
                        pltpu.VMEM((2, bq, bq), BF16)],
        compiler_params=_cparams(("parallel",)),
        name="stickbreak",
    )(proj, proj, proj, u)


def _oproj_kernel(a_ref, b_ref, c_ref, wa_ref, wb_ref, wc_ref, ga_ref, gc_ref,
                  x_ref, gn_ref, xo_ref, ho_ref):
    an = _rms(a_ref[...].astype(F32), ga_ref[...]).astype(BF16)
    cn = _rms(c_ref[...].astype(F32), gc_ref[...]).astype(BF16)
    y = (jnp.dot(an, wa_ref[...], preferred_element_type=F32)
         + jnp.dot(b_ref[...], wb_ref[...], preferred_element_type=F32)
         + jnp.dot(cn, wc_ref[...], preferred_element_type=F32))
    xn = x_ref[...] + y
    xo_ref[...] = xn
    ho_ref[...] = _rms(xn, gn_ref[...]).astype(ho_ref.dtype)


def out_projection(a, b, c, wa, wb, wc, ga, gc, x, gn, layer, tm=256):
    s, d = x.shape
    tm = min(tm, s)
    row = lambda i: (i, 0)
    fixed = lambda i: (layer, 0, 0)
    per_layer = lambda arr: pl.BlockSpec((None,) + arr.shape[1:], fixed)
    return pl.pallas_call(
        _oproj_kernel,
        grid=(s // tm,),
        in_specs=[pl.BlockSpec((tm, a.shape[1]), row),
                  pl.BlockSpec((tm, b.shape[1]), row),
                  pl.BlockSpec((tm, c.shape[1]), row),
                  per_layer(wa), per_layer(wb), per_layer(wc), per_layer(ga), per_layer(gc),
                  pl.BlockSpec((tm, d), row),
                  per_layer(gn)],
        out_specs=[pl.BlockSpec((tm, d), row), pl.BlockSpec((tm, d), row)],
        out_shape=[jax.ShapeDtypeStruct((s, d), F32), jax.ShapeDtypeStruct((s, d), BF16)],
        compiler_params=_cparams(("parallel",)),
        name="out_proj",
    )(a, b, c, wa, wb, wc, ga, gc, x, gn)


def _mlp_kernel(h_ref, x_ref, w1_ref, w2_ref, gn_ref, *out_refs, final):
    acc_ref = out_refs[0]
    j = pl.program_id(1)

    @pl.when(j == 0)
    def _():
        acc_ref[...] = x_ref[...]

    u = jnp.dot(h_ref[...], w1_ref[...].astype(BF16), preferred_element_type=F32)
    act = jnp.square(jnp.maximum(u, 0.0)).astype(BF16)
    acc_ref[...] += jnp.dot(act, w2_ref[...].astype(BF16), preferred_element_type=F32)

    @pl.when(j == pl.num_programs(1) - 1)
    def _():
        y = _rms(acc_ref[...], gn_ref[...])
        if final:
            acc_ref[...] = y
        else:
            out_refs[1][...] = y.astype(out_refs[1].dtype)


def mlp_block(h, x, w1_all, w2_all, layer, gn, final, tm=1024, tf=512):
    s, d = x.shape
    ff = w1_all.shape[2]
    tm = min(tm, s)
    row = lambda i, j: (i, 0)
    once = pl.Buffered(1)
    out_specs = [pl.BlockSpec((tm, d), row)]
    out_shape = [jax.ShapeDtypeStruct((s, d), F32)]
    if not final:
        out_specs.append(pl.BlockSpec((tm, d), row))
        out_shape.append(jax.ShapeDtypeStruct((s, d), BF16))
    return pl.pallas_call(
        functools.partial(_mlp_kernel, final=final),
        grid=(s // tm, ff // tf),
        in_specs=[pl.BlockSpec((tm, d), row, pipeline_mode=once),
                  pl.BlockSpec((tm, d), row, pipeline_mode=once),
                  pl.BlockSpec((None, d, tf), lambda i, j: (layer, 0, j)),
                  pl.BlockSpec((None, tf, d), lambda i, j: (layer, j, 0)),
                  pl.BlockSpec((None, 1, d), lambda i, j: (layer, 0, 0))],
        out_specs=out_specs,
        out_shape=out_shape,
        compiler_params=_cparams(("parallel", "arbitrary")),
        name="mlp",
    )(h, x, w1_all, w2_all, gn)


def _zero_cols(w, width):
    return jnp.zeros(w.shape[:-1] + (width,), w.dtype)


def _head_cols(w, off, heads, width, padded):
    pieces = []
    for h in range(heads):
        pieces.append(w[..., off + h * width:off + (h + 1) * width])
        if padded > width:
            pieces.append(_zero_cols(w, padded - width))
    return pieces


def _head_rows(w, off, heads, width, padded):
    pieces = []
    for h in range(heads):
        pieces.append(w[..., off + h * width:off + (h + 1) * width, :])
        if padded > width:
            pieces.append(jnp.zeros(w.shape[:-2] + (padded - width, w.shape[-1]), w.dtype))
    return pieces


def _prep_w_in(w_in):
    seg = lambda off, width: w_in[..., off:off + width]
    scale = HEAD_DIM ** -0.5
    pieces = (
        _head_cols(w_in, _O_BV, GLA_HEADS, GLA_DV, GLA_DV_PAD)
        + _head_cols(w_in, _O_BR, GLA_HEADS, GLA_DV, GLA_DV_PAD)
        + _head_cols(w_in, _O_BQ, GLA_HEADS, GLA_DK, GLA_DK_PAD)
        + _head_cols(w_in, _O_BK, GLA_HEADS, GLA_DK, GLA_DK_PAD)
        + [seg(_O_CQ, SB_WIDTH) * scale, seg(_O_CK, SB_WIDTH), seg(_O_CV, SB_WIDTH)]
        + [seg(_O_AQ + HEAD_DIM * h, HEAD_DIM) * scale for h in SWA_Q_PERM]
        + [seg(_O_AK, SWA_KV_WIDTH), seg(_O_AV, SWA_KV_WIDTH), seg(_O_BLR, GLA_LOWRANK),
           _zero_cols(w_in, PROJ_PAD_WIDTH - OFF_BLR - GLA_LOWRANK)])
    return jnp.concatenate(pieces, axis=-1).astype(BF16)


def _swa_perm_cols(w):
    return jnp.concatenate([w[..., HEAD_DIM * h:HEAD_DIM * (h + 1)] for h in SWA_Q_PERM], axis=-1)


def kernel(x, norm_mix, w_in, swa_sinks, rel_bias, gla_gate_w, gla_gate_b, gla_norm,
           swa_out_norm, sb_out_norm, w_out, norm_mlp, w_mlp_in, w_mlp_out, norm_final):
    depth = w_in.shape[0]
    xs = x[0]
    gla_w = GLA_HEADS * GLA_DV
    w_in_p = prep_w_in(jnp.swapaxes(w_in, 1, 2))
    wa = jnp.concatenate([w_out[:, HEAD_DIM * h:HEAD_DIM * (h + 1)] for h in SWA_Q_PERM],
                         axis=1).astype(BF16)
    wb = jnp.concatenate(_head_rows(w_out, SWA_WIDTH, GLA_HEADS, GLA_DV, GLA_DV_PAD),
                         axis=1).astype(BF16)
    wc = w_out[:, SWA_WIDTH + gla_w:].astype(BF16)
    ga = _swa_perm_cols(swa_out_norm)[:, None, :]
    gc = sb_out_norm[:, None, :]
    g_mlp = norm_mlp[:, None, :]
    g_next = jnp.concatenate([norm_mix[1:], norm_final[None]], axis=0)[:, None, :]
    sinks_p = jnp.concatenate([swa_sinks[:, h:h + 1] for h in SWA_Q_PERM], axis=1)
    bias_tab = swa_bias_table(rel_bias)
    gw = jnp.stack([jnp.pad(gla_gate_w[:, :, GLA_DK * h:GLA_DK * (h + 1)],
                            ((0, 0), (0, GLA_LR_PAD - GLA_LOWRANK), (0, GLA_DK_PAD - GLA_DK)))
                    for h in range(GLA_HEADS)], axis=1).astype(BF16)
    gb = jnp.stack([jnp.pad(gla_gate_b[:, GLA_DK * h:GLA_DK * (h + 1)],
                            ((0, 0), (0, GLA_DK_PAD - GLA_DK)))
                    for h in range(GLA_HEADS)], axis=1)[:, :, None, :]
    gn = jnp.pad(gla_norm, ((0, 0), (0, GLA_DV_PAD - GLA_DV)))[:, None, :]

    h = None
    for l in range(depth):
        if l == 0:
            proj = in_projection(xs, w_in_p, l, gain=norm_mix[0])
        else:
            proj = in_projection(h, w_in_p, l)
        a = swa_attention(proj, sinks_p, bias_tab, l)
        b = gla_attention(proj, gw, gb, gn, l)
        c = sb_attention(proj)
        xs, hm = out_projection(a, b, c, wa, wb, wc, ga, gc, xs, g_mlp, l)
        if l + 1 < depth:
            xs, h = mlp_block(hm, xs, w_mlp_in, w_mlp_out, l, g_next, final=False)
        else:
            (out,) = mlp_block(hm, xs, w_mlp_in, w_mlp_out, l, g_next, final=True)
    return out[None]
```

```python
import functools
import math

import numpy as np
import jax
import jax.numpy as jnp
from jax import lax
from jax.experimental import pallas as pl
from jax.experimental.pallas import tpu as pltpu

F32 = jnp.float32
BF16 = jnp.bfloat16

D_MODEL = 2048
HEAD_DIM = 64
SWA_HEADS = 12
SWA_KV_HEADS = 4
SWA_WIDTH = SWA_HEADS * HEAD_DIM
SWA_KV_WIDTH = SWA_KV_HEADS * HEAD_DIM
SWA_BLOCK = 128
WINDOW = 128
REL_BUCKETS = 32
REL_MAX_DIST = 128
GLA_HEADS = 4
GLA_DV = 192
GLA_DK = 96
GLA_DK_PAD = 128
GLA_DV_PAD = 256
GLA_LOWRANK = 16
GLA_LR_PAD = 128
GLA_CHUNK = 64
GATE_NORMALIZER = 16.0
GATE_LOG_MIN = -1.0
SB_HEADS = 8
SB_WIDTH = SB_HEADS * HEAD_DIM
D_FF = 4 * D_MODEL
RMS_EPS = 1e-6
LOG2E = 1.4426950408889634
NEG_INF = -1e30

LANES = 128
VMEM_LIMIT = 56 * 1024 * 1024

OFF_BV = 0
OFF_BR = 1024
OFF_BQ = 2048
OFF_BK = 2560
OFF_CQ = 3072
OFF_CK = 3584
OFF_CV = 4096
OFF_AQ = 4608
OFF_AK = 5376
OFF_AV = 5632
OFF_BLR = 5888
PROJ_PAD_WIDTH = 6144

SWA_Q_PERM = (0, 3, 1, 4, 2, 5, 6, 9, 7, 10, 8, 11)

_O_AQ, _O_AK, _O_AV = 0, 768, 1024
_O_BQ, _O_BK, _O_BV, _O_BR, _O_BLR = 1280, 1664, 2048, 2816, 3584
_O_CQ, _O_CK, _O_CV = 3600, 4112, 4624


def _cparams(sem):
    return pltpu.CompilerParams(dimension_semantics=sem, vmem_limit_bytes=VMEM_LIMIT)


def _rms(x, gain):
    ms = jnp.mean(x * x, axis=-1, keepdims=True)
    return x * lax.rsqrt(ms + RMS_EPS) * gain


def _split_bf16(x):
    hi = x.astype(BF16)
    lo = (x - hi.astype(F32)).astype(BF16)
    return hi, lo


_NT_DIMS = (((1,), (1,)), ((), ()))


def _matmul_kernel(h_ref, w_ref, o_ref):
    o_ref[...] = lax.dot_general(h_ref[...], w_ref[...], _NT_DIMS,
                                 preferred_element_type=F32).astype(o_ref.dtype)


def _norm_matmul_kernel(x_ref, g_ref, w_ref, o_ref, h_ref):
    @pl.when(pl.program_id(1) == 0)
    def _():
        h_ref[...] = _rms(x_ref[...], g_ref[...]).astype(h_ref.dtype)

    o_ref[...] = lax.dot_general(h_ref[...], w_ref[...], _NT_DIMS,
                                 preferred_element_type=F32).astype(o_ref.dtype)


def in_projection(h, w_all, layer, gain=None, tm=1024, tn=1024):
    s, d = h.shape
    n = w_all.shape[1]
    tm = min(tm, s)
    w_spec = pl.BlockSpec((None, tn, d), lambda i, j: (layer, j, 0))
    h_spec = pl.BlockSpec((tm, d), lambda i, j: (i, 0))
    common = dict(
        grid=(s // tm, n // tn),
        out_specs=pl.BlockSpec((tm, tn), lambda i, j: (i, j)),
        out_shape=jax.ShapeDtypeStruct((s, n), BF16),
        compiler_params=_cparams(("parallel", "arbitrary")),
        name="in_proj",
    )
    if gain is None:
        return pl.pallas_call(_matmul_kernel, in_specs=[h_spec, w_spec], **common)(h, w_all)
    return pl.pallas_call(
        _norm_matmul_kernel,
        in_specs=[h_spec, pl.BlockSpec((1, d), lambda i, j: (0, 0)), w_spec],
        scratch_shapes=[pltpu.VMEM((tm, d), BF16)],
        **common,
    )(h, gain.reshape(1, d), w_all)


def _w_in_moves():
    scale = HEAD_DIM ** -0.5
    moves = []
    for h in range(GLA_HEADS):
        moves.append((_O_BV + GLA_DV * h, OFF_BV + GLA_DV_PAD * h, GLA_DV, 1.0))
        moves.append((_O_BR + GLA_DV * h, OFF_BR + GLA_DV_PAD * h, GLA_DV, 1.0))
        moves.append((_O_BQ + GLA_DK * h, OFF_BQ + GLA_DK_PAD * h, GLA_DK, 1.0))
        moves.append((_O_BK + GLA_DK * h, OFF_BK + GLA_DK_PAD * h, GLA_DK, 1.0))
    moves += [(_O_CQ, OFF_CQ, SB_WIDTH, scale), (_O_CK, OFF_CK, SB_WIDTH, 1.0),
              (_O_CV, OFF_CV, SB_WIDTH, 1.0)]
    for j, h in enumerate(SWA_Q_PERM):
        moves.append((_O_AQ + HEAD_DIM * h, OFF_AQ + HEAD_DIM * j, HEAD_DIM, scale))
    moves += [(_O_AK, OFF_AK, SWA_KV_WIDTH, 1.0), (_O_AV, OFF_AV, SWA_KV_WIDTH, 1.0),
              (_O_BLR, OFF_BLR, GLA_LOWRANK, 1.0)]
    return moves


def _w_in_prep_kernel(w_ref, o_ref):
    o_ref[...] = jnp.zeros_like(o_ref)
    for src, dst, width, scale in _w_in_moves():
        piece = w_ref[src:src + width, :]
        if scale != 1.0:
            piece = piece * scale
        o_ref[dst:dst + width, :] = piece.astype(o_ref.dtype)


def prep_w_in(w_in_t, tc=256):
    layers, n, d = w_in_t.shape
    return pl.pallas_call(
        _w_in_prep_kernel,
        grid=(layers, d // tc),
        in_specs=[pl.BlockSpec((None, n, tc), lambda l, i: (l, 0, i))],
        out_specs=pl.BlockSpec((None, PROJ_PAD_WIDTH, tc), lambda l, i: (l, 0, i)),
        out_shape=jax.ShapeDtypeStruct((layers, PROJ_PAD_WIDTH, d), BF16),
        compiler_params=_cparams(("parallel", "parallel")),
        name="w_in_prep",
    )(w_in_t)


def _swa_kernel(sink_ref, q_ref, kp_ref, kc_ref, vp_ref, vc_ref, bias_ref, o_ref, *, layer):
    i = pl.program_id(0)
    blk = SWA_BLOCK
    k = jnp.concatenate([kp_ref[...], kc_ref[...]], axis=0)
    v = jnp.concatenate([vp_ref[...], vc_ref[...]], axis=0)
    lane = lax.broadcasted_iota(jnp.int32, (blk, LANES), 1)
    col = lax.broadcasted_iota(jnp.int32, (blk, 2 * blk), 1)
    kill = jnp.logical_and(col < blk, i == 0)
    low = lane < HEAD_DIM
    for b in range(SWA_HEADS // 2):
        qb = q_ref[:, LANES * b:LANES * (b + 1)]
        kv_lo = LANES * (b // 3)
        kb = k[:, kv_lo:kv_lo + LANES]
        vb = v[:, kv_lo:kv_lo + LANES]
        outs = []
        for half in range(2):
            hq = 2 * b + half
            keep = low if half == 0 else jnp.logical_not(low)
            qm = jnp.where(keep, qb, jnp.zeros_like(qb))
            s = lax.dot_general(qm, kb, (((1,), (1,)), ((), ())),
                                preferred_element_type=F32)
            s = jnp.where(kill, NEG_INF, s + bias_ref[hq])
            sink = sink_ref[layer, hq]
            m = jnp.maximum(jnp.max(s, axis=1, keepdims=True), sink)
            p = jnp.exp(s - m)
            denom = jnp.sum(p, axis=1, keepdims=True) + jnp.exp(sink - m)
            o = jnp.dot(p.astype(BF16), vb, preferred_element_type=F32)
            outs.append(o / denom)
        o_ref[:, LANES * b:LANES * (b + 1)] = jnp.where(low, outs[0], outs[1]).astype(o_ref.dtype)


def swa_attention(proj, sinks_perm, bias_tab, layer):
    s = proj.shape[0]
    blk = SWA_BLOCK
    qi = OFF_AQ // SWA_WIDTH
    ki = OFF_AK // SWA_KV_WIDTH
    vi = OFF_AV // SWA_KV_WIDTH
    prev = lambda i: jnp.maximum(i - 1, 0)
    return pl.pallas_call(
        functools.partial(_swa_kernel, layer=layer),
        grid=(s // blk,),
        in_specs=[pl.BlockSpec(memory_space=pltpu.SMEM),
                  pl.BlockSpec((blk, SWA_WIDTH), lambda i: (i, qi)),
                  pl.BlockSpec((blk, SWA_KV_WIDTH), lambda i: (prev(i), ki)),
                  pl.BlockSpec((blk, SWA_KV_WIDTH), lambda i: (i, ki)),
                  pl.BlockSpec((blk, SWA_KV_WIDTH), lambda i: (prev(i), vi)),
                  pl.BlockSpec((blk, SWA_KV_WIDTH), lambda i: (i, vi)),
                  pl.BlockSpec((SWA_HEADS, blk, 2 * blk), lambda i: (0, 0, 0))],
        out_specs=pl.BlockSpec((blk, SWA_WIDTH), lambda i: (i, 0)),
        out_shape=jax.ShapeDtypeStruct((s, SWA_WIDTH), BF16),
        compiler_params=_cparams(("parallel",)),
        name="swa",
    )(sinks_perm, proj, proj, proj, proj, proj, bias_tab)


def _t5_causal_bucket(dist):
    max_exact = REL_BUCKETS // 2
    is_small = dist < max_exact
    ratio = (jnp.log(jnp.maximum(dist, 1).astype(F32) / max_exact)
             / math.log(REL_MAX_DIST / max_exact))
    large = max_exact + (ratio * (REL_BUCKETS - max_exact)).astype(jnp.int32)
    large = jnp.minimum(large, REL_BUCKETS - 1)
    return jnp.where(is_small, dist, large)


def swa_bias_table(rel_bias):
    blk = SWA_BLOCK
    qpos = jnp.arange(blk) + blk
    kpos = jnp.arange(2 * blk)
    dist = qpos[:, None] - kpos[None, :]
    in_window = (dist >= 0) & (dist < WINDOW)
    bucket = _t5_causal_bucket(jnp.maximum(dist, 0))
    table = rel_bias.astype(F32)
    table = jnp.stack([table[:, h] for h in SWA_Q_PERM])
    bias = jnp.full((SWA_HEADS, blk, 2 * blk), NEG_INF, F32)
    for bkt in range(REL_BUCKETS):
        hit = (in_window & (bucket == bkt))[None]
        bias = jnp.where(hit, table[:, bkt][:, None, None], bias)
    return bias


def _gla_kernel(q_ref, k_ref, v_ref, r_ref, lr_ref, gw_ref, gb_ref, gn_ref, lm_ref, o_ref,
                st_ref, *, rows, hps):
    t = pl.program_id(1)
    ch = GLA_CHUNK
    nch = rows // ch
    heads = range(hps)

    @pl.when(t == 0)
    def _():
        st_ref[...] = jnp.zeros_like(st_ref)

    nt = (((1,), (1,)), ((), ()))
    tn = (((0,), (0,)), ((), ()))
    rowi = lax.broadcasted_iota(jnp.int32, (rows, rows), 0)
    coli = lax.broadcasted_iota(jnp.int32, (rows, rows), 1)
    same_chunk_causal = jnp.logical_and(coli <= rowi, (rowi // ch) == (coli // ch))
    lm = lm_ref[...]
    lr = lr_ref[...]
    gn = gn_ref[...]

    gp = [jnp.dot(lr, gw_ref[h], preferred_element_type=F32) + gb_ref[h] for h in heads]
    lg = []
    for h in heads:
        log_sig = jnp.minimum(gp[h], 0.0) - jnp.log(1.0 + jnp.exp(-jnp.abs(gp[h])))
        lg.append(jnp.maximum(log_sig * (1.0 / GATE_NORMALIZER), GATE_LOG_MIN))
    bb = []
    for h in heads:
        hi, lo = _split_bf16(lg[h])
        bb.append(jnp.dot(lm, hi, preferred_element_type=F32)
                  + jnp.dot(lm, lo, preferred_element_type=F32))
    q_dec, k_inv, k_end, dec = [], [], [], []
    for h in heads:
        b, b_last = bb[h][:rows], bb[h][rows:]
        q = q_ref[:, GLA_DK_PAD * h:GLA_DK_PAD * (h + 1)].astype(F32) * (GLA_DK ** -0.5)
        kk = k_ref[:, GLA_DK_PAD * h:GLA_DK_PAD * (h + 1)].astype(F32)
        q_dec.append((q * jnp.exp(b)).astype(BF16))
        k_inv.append((kk * jnp.exp(-b)).astype(BF16))
        k_end.append((kk * jnp.exp(b_last - b)).astype(BF16))
        dec.append(jnp.exp(b_last))
    v = [v_ref[:, GLA_DV_PAD * h:GLA_DV_PAD * (h + 1)] for h in heads]
    sc = []
    for h in heads:
        s_h = lax.dot_general(q_dec[h], k_inv[h], nt, preferred_element_type=F32)
        sc.append(jnp.where(same_chunk_causal, s_h, 0.0).astype(BF16))
    o_intra = [jnp.dot(sc[h], v[h], preferred_element_type=F32) for h in heads]
    d_st = [[lax.dot_general(v[h][c * ch:(c + 1) * ch], k_end[h][c * ch:(c + 1) * ch], tn,
                             preferred_element_type=F32) for c in range(nch)] for h in heads]
    st_in = []
    for h in heads:
        st = st_ref[h]
        states = []
        for c in range(nch):
            states.append(st.astype(BF16))
            st = st * dec[h][c * ch:c * ch + 1, :] + d_st[h][c]
        st_ref[h] = st
        st_in.append(states)
    for h in heads:
        o_inter = [lax.dot_general(q_dec[h][c * ch:(c + 1) * ch], st_in[h][c], nt,
                                   preferred_element_type=F32) for c in range(nch)]
        o = o_intra[h] + jnp.concatenate(o_inter, axis=0)
        ms = jnp.sum(o * o, axis=1, keepdims=True) * (1.0 / GLA_DV)
        y = o * lax.rsqrt(ms + RMS_EPS) * gn
        rr = r_ref[:, GLA_DV_PAD * h:GLA_DV_PAD * (h + 1)].astype(F32)
        o_ref[:, GLA_DV_PAD * h:GLA_DV_PAD * (h + 1)] = (
            y * (rr / (1.0 + jnp.exp(-rr)))).astype(o_ref.dtype)


def gla_attention(proj, gw_pad, gb_pad, gn_pad, layer, rows=256, hps=4):
    s = proj.shape[0]
    rows = min(rows, s)
    ch = GLA_CHUNK
    dkw, dvw = hps * GLA_DK_PAD, hps * GLA_DV_PAD
    qi, ki = OFF_BQ // dkw, OFF_BK // dkw
    vi, ri = OFF_BV // dvw, OFF_BR // dvw
    li = OFF_BLR // GLA_LR_PAD
    idx = np.arange(rows)
    same = (idx[:, None] // ch) == (idx[None, :] // ch)
    prefix = same & (idx[None, :] <= idx[:, None])
    lm = jnp.asarray(np.concatenate([prefix, same], axis=0).astype(np.float32), dtype=BF16)
    return pl.pallas_call(
        functools.partial(_gla_kernel, rows=rows, hps=hps),
        grid=(GLA_HEADS // hps, s // rows),
        in_specs=[pl.BlockSpec((rows, dkw), lambda g, t: (t, qi + g)),
                  pl.BlockSpec((rows, dkw), lambda g, t: (t, ki + g)),
                  pl.BlockSpec((rows, dvw), lambda g, t: (t, vi + g)),
                  pl.BlockSpec((rows, dvw), lambda g, t: (t, ri + g)),
                  pl.BlockSpec((rows, GLA_LR_PAD), lambda g, t: (t, li)),
                  pl.BlockSpec((None, hps, GLA_LR_PAD, GLA_DK_PAD),
                               lambda g, t: (layer, g, 0, 0)),
                  pl.BlockSpec((None, hps, 1, GLA_DK_PAD), lambda g, t: (layer, g, 0, 0)),
                  pl.BlockSpec((None, 1, GLA_DV_PAD), lambda g, t: (layer, 0, 0)),
                  pl.BlockSpec((2 * rows, rows), lambda g, t: (0, 0))],
        out_specs=pl.BlockSpec((rows, dvw), lambda g, t: (t, g)),
        out_shape=jax.ShapeDtypeStruct((s, GLA_HEADS * GLA_DV_PAD), BF16),
        scratch_shapes=[pltpu.VMEM((hps, GLA_DV_PAD, GLA_DK_PAD), F32)],
        compiler_params=_cparams(("parallel", "arbitrary")),
        name="gla",
    )(proj, proj, proj, proj, proj, gw_pad, gb_pad, gn_pad, lm)


def _sb_kernel(q_ref, k_ref, v_ref, u_ref, o_ref, acc_ref, car_ref, qm_ref,
               z0_ref, z1_ref, nl0_ref, nl1_ref, zc0_ref, zc1_ref, w0_ref, w1_ref,
               *, bq, nblk, unroll):
    lane = lax.broadcasted_iota(jnp.int32, (bq, LANES), 1)
    low = lane < HEAD_DIM
    nt = (((1,), (1,)), ((), ()))
    z_ref, nl_ref = (z0_ref, z1_ref), (nl0_ref, nl1_ref)
    zc_ref, w_ref = (zc0_ref, zc1_ref), (w0_ref, w1_ref)

    def rows(blk):
        if isinstance(blk, int):
            return pl.ds(blk * bq, bq)
        return pl.ds(pl.multiple_of(blk * bq, bq), bq)

    def strict_mask():
        rowi = lax.broadcasted_iota(jnp.int32, (bq, bq), 0)
        coli = lax.broadcasted_iota(jnp.int32, (bq, bq), 1)
        return coli < rowi

    def s1a(item, slot):
        i, c = item
        kc = k_ref[rows(c), :]
        for h in range(2):
            z_ref[slot][h] = lax.dot_general(qm_ref[h, rows(i), :], kc, nt,
                                             preferred_element_type=F32)

    def s1b(slot, masked):
        for h in range(2):
            z = z_ref[slot][h]
            nl = jnp.maximum(z, 0.0) + jnp.log(1.0 + jnp.exp2(jnp.abs(z) * (-LOG2E)))
            if masked:
                nl = jnp.where(strict_mask(), nl, 0.0)
            nl_ref[slot][h] = nl.astype(BF16)
            zc_ref[slot][h] = z

    def s2(item, slot, masked, first):
        i, _ = item
        for h in range(2):
            cs = jnp.dot(nl_ref[slot][h], u_ref[...], preferred_element_type=F32)
            tot = jnp.broadcast_to(cs[:, 0:1], (bq, LANES))
            if first:
                arg = zc_ref[slot][h] - cs
                car_ref[h, rows(i), :] = tot
            else:
                car = car_ref[h, rows(i), :]
                arg = zc_ref[slot][h] - cs - jnp.concatenate([car, car], axis=1)
                car_ref[h, rows(i), :] = car + tot
            w = jnp.exp(arg)
            if masked:
                w = jnp.where(strict_mask(), w, 0.0)
            w_ref[slot][h] = w.astype(BF16)

    def s3(item, slot, first):
        i, c = item
        vc = v_ref[rows(c), :]
        for h in range(2):
            pv = jnp.dot(w_ref[slot][h], vc, preferred_element_type=F32)
            if first:
                acc_ref[h, rows(i), :] = pv
            else:
                acc_ref[h, rows(i), :] += pv

    def run_pipeline(items, step, masked, first):
        n = len(items)

        def tick(par, its, do):
            if do[1]:
                s1b(1 - par, masked)
            if do[0]:
                s1a(its[0], par)
            if do[2]:
                s2(its[2], par, masked, first)
            if do[3]:
                s3(its[3], 1 - par, first)

        def static_tick(t):
            do = [0 <= t - j < n for j in range(4)]
            its = [items[t - j] if do[j] else None for j in range(4)]
            tick(t % 2, its, do)

        lo_t, hi_t = 3, n - 1
        groups = max(hi_t - lo_t + 1, 0) // unroll
        if groups < 2:
            groups = 0
        for t in range(min(lo_t, n + 3)):
            static_tick(t)
        if groups:
            def body(_, carry):
                its = [(carry[2 * j], carry[2 * j + 1]) for j in range(4)]
                for k in range(unroll):
                    tick((lo_t + k) % 2, its, [True] * 4)
                    its = [step(*its[0])] + its[:3]
                return tuple(x for it in its for x in it)

            init = tuple(jnp.int32(x) for j in range(4) for x in items[lo_t - j])
            lax.fori_loop(0, groups, body, init)
        for t in range(lo_t + unroll * groups, n + 3):
            static_tick(t)

    for blk in range(nblk):
        r = rows(blk)
        qb = q_ref[r, :]
        zq = jnp.zeros_like(qb)
        qm_ref[0, r, :] = jnp.where(low, qb, zq)
        qm_ref[1, r, :] = jnp.where(low, zq, qb)

    diag = [(i, i) for i in range(nblk)]
    run_pipeline(diag, lambda i, c: (i + 1, c + 1), masked=True, first=True)

    off = [(i, c) for i in range(1, nblk) for c in range(i - 1, -1, -1)]

    def off_step(i, c):
        wrap = c == 0
        i2 = jnp.where(wrap, i + 1, i)
        return i2, jnp.where(wrap, i2 - 1, c - 1)

    if off:
        run_pipeline(off, off_step, masked=False, first=False)

    for blk in range(nblk):
        r = rows(blk)
        o_ref[r, :] = jnp.where(low, acc_ref[0, r, :], acc_ref[1, r, :]).astype(o_ref.dtype)


def sb_attention(proj, bq=256, unroll=8):
    s = proj.shape[0]
    bq = min(bq, s)
    nblk = s // bq
    qi = OFF_CQ // LANES
    ki = OFF_CK // LANES
    vi = OFF_CV // LANES
    u = (np.arange(bq)[:, None] >= np.arange(bq)[None, :]).astype(np.float32)
    u = jnp.asarray(u, dtype=BF16)
    return pl.pallas_call(
        functools.partial(_sb_kernel, bq=bq, nblk=nblk, unroll=unroll),
        grid=(SB_HEADS // 2,),
        in_specs=[pl.BlockSpec((s, LANES), lambda p: (0, qi + p)),
                  pl.BlockSpec((s, LANES), lambda p: (0, ki + p)),
                  pl.BlockSpec((s, LANES), lambda p: (0, vi + p)),
                  pl.BlockSpec((bq, bq), lambda p: (0, 0))],
        out_specs=pl.BlockSpec((s, LANES), lambda p: (0, p)),
        out_shape=jax.ShapeDtypeStruct((s, SB_WIDTH), BF16),
        scratch_shapes=[pltpu.VMEM((2, s, LANES), F32),
                        pltpu.VMEM((2, s, LANES), F32),
                        pltpu.VMEM((2, s, LANES), BF16),
                        pltpu.VMEM((2, bq, bq), F32),
                        pltpu.VMEM((2, bq, bq), F32),
                        pltpu.VMEM((2, bq, bq), BF16),
                        pltpu.VMEM((2, bq, bq), BF16),
                        pltpu.VMEM((2, bq, bq), F32),
                        pltpu.VMEM((2, bq, bq), F32),
                        pltpu.VMEM((2, bq, bq), BF16),
                        pltpu.VMEM((2, bq, bq), BF16)],
        compiler_params=_cparams(("parallel",)),
        name="stickbreak",
    )(proj, proj, proj, u)


def _oproj_kernel(a_ref, b_ref, c_ref, wa_ref, wb_ref, wc_ref, ga_ref, gc_ref,
                  x_ref, gn_ref, xo_ref, ho_ref):
    an = _rms(a_ref[...].astype(F32), ga_ref[...]).astype(BF16)
    cn = _rms(c_ref[...].astype(F32), gc_ref[...]).astype(BF16)
    y = (jnp.dot(an, wa_ref[...], preferred_element_type=F32)
         + jnp.dot(b_ref[...], wb_ref[...], preferred_element_type=F32)
         + jnp.dot(cn, wc_ref[...], preferred_element_type=F32))
    xn = x_ref[...] + y
    xo_ref[...] = xn
    ho_ref[...] = _rms(xn, gn_ref[...]).astype(ho_ref.dtype)


def out_projection(a, b, c, wa, wb, wc, ga, gc, x, gn, layer, tm=512):
    s, d = x.shape
    tm = min(tm, s)
    row = lambda i: (i, 0)
    fixed = lambda i: (layer, 0, 0)
    per_layer = lambda arr: pl.BlockSpec((None,) + arr.shape[1:], fixed)
    return pl.pallas_call(
        _oproj_kernel,
        grid=(s // tm,),
        in_specs=[pl.BlockSpec((tm, a.shape[1]), row),
                  pl.BlockSpec((tm, b.shape[1]), row),
                  pl.BlockSpec((tm, c.shape[1]), row),
                  per_layer(wa), per_layer(wb), per_layer(wc), per_layer(ga), per_layer(gc),
                  pl.BlockSpec((tm, d), row),
                  per_layer(gn)],
        out_specs=[pl.BlockSpec((tm, d), row), pl.BlockSpec((tm, d), row)],
        out_shape=[jax.ShapeDtypeStruct((s, d), F32), jax.ShapeDtypeStruct((s, d), BF16)],
        compiler_params=_cparams(("parallel",)),
        name="out_proj",
    )(a, b, c, wa, wb, wc, ga, gc, x, gn)


def _mlp_kernel(h_ref, x_ref, w1_ref, w2_ref, gn_ref, *out_refs, final):
    acc_ref = out_refs[0]
    j = pl.program_id(1)

    @pl.when(j == 0)
    def _():
        acc_ref[...] = x_ref[...]

    u = jnp.dot(h_ref[...], w1_ref[...].astype(BF16), preferred_element_type=F32)
    act = jnp.square(jnp.maximum(u, 0.0)).astype(BF16)
    acc_ref[...] += jnp.dot(act, w2_ref[...].astype(BF16), preferred_element_type=F32)

    @pl.when(j == pl.num_programs(1) - 1)
    def _():
        y = _rms(acc_ref[...], gn_ref[...])
        if final:
            acc_ref[...] = y
        else:
            out_refs[1][...] = y.astype(out_refs[1].dtype)


def mlp_block(h, x, w1_all, w2_all, layer, gn, final, tm=1024, tf=512):
    s, d = x.shape
    ff = w1_all.shape[2]
    tm = min(tm, s)
    row = lambda i, j: (i, 0)
    once = pl.Buffered(1)
    out_specs = [pl.BlockSpec((tm, d), row)]
    out_shape = [jax.ShapeDtypeStruct((s, d), F32)]
    if not final:
        out_specs.append(pl.BlockSpec((tm, d), row))
        out_shape.append(jax.ShapeDtypeStruct((s, d), BF16))
    return pl.pallas_call(
        functools.partial(_mlp_kernel, final=final),
        grid=(s // tm, ff // tf),
        in_specs=[pl.BlockSpec((tm, d), row, pipeline_mode=once),
                  pl.BlockSpec((tm, d), row, pipeline_mode=once),
                  pl.BlockSpec((None, d, tf), lambda i, j: (layer, 0, j)),
                  pl.BlockSpec((None, tf, d), lambda i, j: (layer, j, 0)),
                  pl.BlockSpec((None, 1, d), lambda i, j: (layer, 0, 0))],
        out_specs=out_specs,
        out_shape=out_shape,
        compiler_params=_cparams(("parallel", "arbitrary")),
        name="mlp",
    )(h, x, w1_all, w2_all, gn)


def _zero_cols(w, width):
    return jnp.zeros(w.shape[:-1] + (width,), w.dtype)


def _head_cols(w, off, heads, width, padded):
    pieces = []
    for h in range(heads):
        pieces.append(w[..., off + h * width:off + (h + 1) * width])
        if padded > width:
            pieces.append(_zero_cols(w, padded - width))
    return pieces


def _head_rows(w, off, heads, width, padded):
    pieces = []
    for h in range(heads):
        pieces.append(w[..., off + h * width:off + (h + 1) * width, :])
        if padded > width:
            pieces.append(jnp.zeros(w.shape[:-2] + (padded - width, w.shape[-1]), w.dtype))
    return pieces


def _prep_w_in(w_in):
    seg = lambda off, width: w_in[..., off:off + width]
    scale = HEAD_DIM ** -0.5
    pieces = (
        _head_cols(w_in, _O_BV, GLA_HEADS, GLA_DV, GLA_DV_PAD)
        + _head_cols(w_in, _O_BR, GLA_HEADS, GLA_DV, GLA_DV_PAD)
        + _head_cols(w_in, _O_BQ, GLA_HEADS, GLA_DK, GLA_DK_PAD)
        + _head_cols(w_in, _O_BK, GLA_HEADS, GLA_DK, GLA_DK_PAD)
        + [seg(_O_CQ, SB_WIDTH) * scale, seg(_O_CK, SB_WIDTH), seg(_O_CV, SB_WIDTH)]
        + [seg(_O_AQ + HEAD_DIM * h, HEAD_DIM) * scale for h in SWA_Q_PERM]
        + [seg(_O_AK, SWA_KV_WIDTH), seg(_O_AV, SWA_KV_WIDTH), seg(_O_BLR, GLA_LOWRANK),
           _zero_cols(w_in, PROJ_PAD_WIDTH - OFF_BLR - GLA_LOWRANK)])
    return jnp.concatenate(pieces, axis=-1).astype(BF16)


def _swa_perm_cols(w):
    return jnp.concatenate([w[..., HEAD_DIM * h:HEAD_DIM * (h + 1)] for h in SWA_Q_PERM], axis=-1)


def kernel(x, norm_mix, w_in, swa_sinks, rel_bias, gla_gate_w, gla_gate_b, gla_norm,
           swa_out_norm, sb_out_norm, w_out, norm_mlp, w_mlp_in, w_mlp_out, norm_final):
    depth = w_in.shape[0]
    xs = x[0]
    gla_w = GLA_HEADS * GLA_DV
    w_in_p = prep_w_in(jnp.swapaxes(w_in, 1, 2))
    wa = jnp.concatenate([w_out[:, HEAD_DIM * h:HEAD_DIM * (h + 1)] for h in SWA_Q_PERM],
                         axis=1).astype(BF16)
    wb = jnp.concatenate(_head_rows(w_out, SWA_WIDTH, GLA_HEADS, GLA_DV, GLA_DV_PAD),
                         axis=1).astype(BF16)
    wc = w_out[:, SWA_WIDTH + gla_w:].astype(BF16)
    ga = _swa_perm_cols(swa_out_norm)[:, None, :]
    gc = sb_out_norm[:, None, :]
    g_mlp = norm_mlp[:, None, :]
    g_next = jnp.concatenate([norm_mix[1:], norm_final[None]], axis=0)[:, None, :]
    sinks_p = jnp.concatenate([swa_sinks[:, h:h + 1] for h in SWA_Q_PERM], axis=1)
    bias_tab = swa_bias_table(rel_bias)
    gw = jnp.stack([jnp.pad(gla_gate_w[:, :, GLA_DK * h:GLA_DK * (h + 1)],
                            ((0, 0), (0, GLA_LR_PAD - GLA_LOWRANK), (0, GLA_DK_PAD - GLA_DK)))
                    for h in range(GLA_HEADS)], axis=1).astype(BF16)
    gb = jnp.stack([jnp.pad(gla_gate_b[:, GLA_DK * h:GLA_DK * (h + 1)],
                            ((0, 0), (0, GLA_DK_PAD - GLA_DK)))
                    for h in range(GLA_HEADS)], axis=1)[:, :, None, :]
    gn = jnp.pad(gla_norm, ((0, 0), (0, GLA_DV_PAD - GLA_DV)))[:, None, :]

    h = None
    for l in range(depth):
        if l == 0:
            proj = in_projection(xs, w_in_p, l, gain=norm_mix[0])
        else:
            proj = in_projection(h, w_in_p, l)
        a = swa_attention(proj, sinks_p, bias_tab, l)
        b = gla_attention(proj, gw, gb, gn, l)
        c = sb_attention(proj)
        xs, hm = out_projection(a, b, c, wa, wb, wc, ga, gc, xs, g_mlp, l)
        if l + 1 < depth:
            xs, h = mlp_block(hm, xs, w_mlp_in, w_mlp_out, l, g_next, final=False)
        else:
            (out,) = mlp_block(hm, xs, w_mlp_in, w_mlp_out, l, g_next, final=True)
    return out[None]
```

```python
import functools
import math

import numpy as np
import jax
import jax.numpy as jnp
from jax import lax
from jax.experimental import pallas as pl
from jax.experimental.pallas import tpu as pltpu

F32 = jnp.float32
BF16 = jnp.bfloat16

D_MODEL = 2048
HEAD_DIM = 64
SWA_HEADS = 12
SWA_KV_HEADS = 4
SWA_WIDTH = SWA_HEADS * HEAD_DIM
SWA_KV_WIDTH = SWA_KV_HEADS * HEAD_DIM
SWA_BLOCK = 128
WINDOW = 128
REL_BUCKETS = 32
REL_MAX_DIST = 128
GLA_HEADS = 4
GLA_DV = 192
GLA_DK = 96
GLA_DK_PAD = 128
GLA_DV_PAD = 256
GLA_LOWRANK = 16
GLA_LR_PAD = 128
GLA_CHUNK = 64
GATE_NORMALIZER = 16.0
GATE_LOG_MIN = -1.0
SB_HEADS = 8
SB_WIDTH = SB_HEADS * HEAD_DIM
D_FF = 4 * D_MODEL
RMS_EPS = 1e-6
LOG2E = 1.4426950408889634
NEG_INF = -1e30

LANES = 128
VMEM_LIMIT = 56 * 1024 * 1024

OFF_BV = 0
OFF_BR = 1024
OFF_BQ = 2048
OFF_BK = 2560
OFF_CQ = 3072
OFF_CK = 3584
OFF_CV = 4096
OFF_AQ = 4608
OFF_AK = 5376
OFF_AV = 5632
OFF_BLR = 5888
PROJ_PAD_WIDTH = 6144

SWA_Q_PERM = (0, 3, 1, 4, 2, 5, 6, 9, 7, 10, 8, 11)

_O_AQ, _O_AK, _O_AV = 0, 768, 1024
_O_BQ, _O_BK, _O_BV, _O_BR, _O_BLR = 1280, 1664, 2048, 2816, 3584
_O_CQ, _O_CK, _O_CV = 3600, 4112, 4624


def _cparams(sem):
    return pltpu.CompilerParams(dimension_semantics=sem, vmem_limit_bytes=VMEM_LIMIT)


def _rms(x, gain):
    ms = jnp.mean(x * x, axis=-1, keepdims=True)
    return x * lax.rsqrt(ms + RMS_EPS) * gain


def _split_bf16(x):
    hi = x.astype(BF16)
    lo = (x - hi.astype(F32)).astype(BF16)
    return hi, lo


_NT_DIMS = (((1,), (1,)), ((), ()))


def _matmul_kernel(h_ref, w_ref, o_ref):
    o_ref[...] = lax.dot_general(h_ref[...], w_ref[...], _NT_DIMS,
                                 preferred_element_type=F32).astype(o_ref.dtype)


def _norm_matmul_kernel(x_ref, g_ref, w_ref, o_ref, h_ref):
    @pl.when(pl.program_id(1) == 0)
    def _():
        h_ref[...] = _rms(x_ref[...], g_ref[...]).astype(h_ref.dtype)

    o_ref[...] = lax.dot_general(h_ref[...], w_ref[...], _NT_DIMS,
                                 preferred_element_type=F32).astype(o_ref.dtype)


def in_projection(h, w_all, layer, gain=None, tm=1024, tn=1024):
    s, d = h.shape
    n = w_all.shape[1]
    tm = min(tm, s)
    w_spec = pl.BlockSpec((None, tn, d), lambda i, j: (layer, j, 0))
    h_spec = pl.BlockSpec((tm, d), lambda i, j: (i, 0))
    common = dict(
        grid=(s // tm, n // tn),
        out_specs=pl.BlockSpec((tm, tn), lambda i, j: (i, j)),
        out_shape=jax.ShapeDtypeStruct((s, n), BF16),
        compiler_params=_cparams(("parallel", "arbitrary")),
        name="in_proj",
    )
    if gain is None:
        return pl.pallas_call(_matmul_kernel, in_specs=[h_spec, w_spec], **common)(h, w_all)
    return pl.pallas_call(
        _norm_matmul_kernel,
        in_specs=[h_spec, pl.BlockSpec((1, d), lambda i, j: (0, 0)), w_spec],
        scratch_shapes=[pltpu.VMEM((tm, d), BF16)],
        **common,
    )(h, gain.reshape(1, d), w_all)


def _w_in_moves():
    scale = HEAD_DIM ** -0.5
    moves = []
    for h in range(GLA_HEADS):
        moves.append((_O_BV + GLA_DV * h, OFF_BV + GLA_DV_PAD * h, GLA_DV, 1.0))
        moves.append((_O_BR + GLA_DV * h, OFF_BR + GLA_DV_PAD * h, GLA_DV, 1.0))
        moves.append((_O_BQ + GLA_DK * h, OFF_BQ + GLA_DK_PAD * h, GLA_DK, 1.0))
        moves.append((_O_BK + GLA_DK * h, OFF_BK + GLA_DK_PAD * h, GLA_DK, 1.0))
    moves += [(_O_CQ, OFF_CQ, SB_WIDTH, scale), (_O_CK, OFF_CK, SB_WIDTH, 1.0),
              (_O_CV, OFF_CV, SB_WIDTH, 1.0)]
    for j, h in enumerate(SWA_Q_PERM):
        moves.append((_O_AQ + HEAD_DIM * h, OFF_AQ + HEAD_DIM * j, HEAD_DIM, scale))
    moves += [(_O_AK, OFF_AK, SWA_KV_WIDTH, 1.0), (_O_AV, OFF_AV, SWA_KV_WIDTH, 1.0),
              (_O_BLR, OFF_BLR, GLA_LOWRANK, 1.0)]
    return moves


def _w_in_prep_kernel(w_ref, o_ref):
    o_ref[...] = jnp.zeros_like(o_ref)
    for src, dst, width, scale in _w_in_moves():
        piece = w_ref[src:src + width, :]
        if scale != 1.0:
            piece = piece * scale
        o_ref[dst:dst + width, :] = piece.astype(o_ref.dtype)


def prep_w_in(w_in_t, tc=256):
    layers, n, d = w_in_t.shape
    return pl.pallas_call(
        _w_in_prep_kernel,
        grid=(layers, d // tc),
        in_specs=[pl.BlockSpec((None, n, tc), lambda l, i: (l, 0, i))],
        out_specs=pl.BlockSpec((None, PROJ_PAD_WIDTH, tc), lambda l, i: (l, 0, i)),
        out_shape=jax.ShapeDtypeStruct((layers, PROJ_PAD_WIDTH, d), BF16),
        compiler_params=_cparams(("parallel", "parallel")),
        name="w_in_prep",
    )(w_in_t)


def _swa_kernel(sink_ref, q_ref, kp_ref, kc_ref, vp_ref, vc_ref, bias_ref, o_ref, *, layer):
    i = pl.program_id(0)
    blk = SWA_BLOCK
    k = jnp.concatenate([kp_ref[...], kc_ref[...]], axis=0)
    v = jnp.concatenate([vp_ref[...], vc_ref[...]], axis=0)
    lane = lax.broadcasted_iota(jnp.int32, (blk, LANES), 1)
    col = lax.broadcasted_iota(jnp.int32, (blk, 2 * blk), 1)
    kill = jnp.logical_and(col < blk, i == 0)
    low = lane < HEAD_DIM
    for b in range(SWA_HEADS // 2):
        qb = q_ref[:, LANES * b:LANES * (b + 1)]
        kv_lo = LANES * (b // 3)
        kb = k[:, kv_lo:kv_lo + LANES]
        vb = v[:, kv_lo:kv_lo + LANES]
        outs = []
        for half in range(2):
            hq = 2 * b + half
            keep = low if half == 0 else jnp.logical_not(low)
            qm = jnp.where(keep, qb, jnp.zeros_like(qb))
            s = lax.dot_general(qm, kb, (((1,), (1,)), ((), ())),
                                preferred_element_type=F32)
            s = jnp.where(kill, NEG_INF, s + bias_ref[hq])
            sink = sink_ref[layer, hq]
            m = jnp.maximum(jnp.max(s, axis=1, keepdims=True), sink)
            p = jnp.exp(s - m)
            denom = jnp.sum(p, axis=1, keepdims=True) + jnp.exp(sink - m)
            o = jnp.dot(p.astype(BF16), vb, preferred_element_type=F32)
            outs.append(o / denom)
        o_ref[:, LANES * b:LANES * (b + 1)] = jnp.where(low, outs[0], outs[1]).astype(o_ref.dtype)


def swa_attention(proj, sinks_perm, bias_tab, layer):
    s = proj.shape[0]
    blk = SWA_BLOCK
    qi = OFF_AQ // SWA_WIDTH
    ki = OFF_AK // SWA_KV_WIDTH
    vi = OFF_AV // SWA_KV_WIDTH
    prev = lambda i: jnp.maximum(i - 1, 0)
    return pl.pallas_call(
        functools.partial(_swa_kernel, layer=layer),
        grid=(s // blk,),
        in_specs=[pl.BlockSpec(memory_space=pltpu.SMEM),
                  pl.BlockSpec((blk, SWA_WIDTH), lambda i: (i, qi)),
                  pl.BlockSpec((blk, SWA_KV_WIDTH), lambda i: (prev(i), ki)),
                  pl.BlockSpec((blk, SWA_KV_WIDTH), lambda i: (i, ki)),
                  pl.BlockSpec((blk, SWA_KV_WIDTH), lambda i: (prev(i), vi)),
                  pl.BlockSpec((blk, SWA_KV_WIDTH), lambda i: (i, vi)),
                  pl.BlockSpec((SWA_HEADS, blk, 2 * blk), lambda i: (0, 0, 0))],
        out_specs=pl.BlockSpec((blk, SWA_WIDTH), lambda i: (i, 0)),
        out_shape=jax.ShapeDtypeStruct((s, SWA_WIDTH), BF16),
        compiler_params=_cparams(("parallel",)),
        name="swa",
    )(sinks_perm, proj, proj, proj, proj, proj, bias_tab)


def _t5_causal_bucket(dist):
    max_exact = REL_BUCKETS // 2
    is_small = dist < max_exact
    ratio = (jnp.log(jnp.maximum(dist, 1).astype(F32) / max_exact)
             / math.log(REL_MAX_DIST / max_exact))
    large = max_exact + (ratio * (REL_BUCKETS - max_exact)).astype(jnp.int32)
    large = jnp.minimum(large, REL_BUCKETS - 1)
    return jnp.where(is_small, dist, large)


def swa_bias_table(rel_bias):
    blk = SWA_BLOCK
    qpos = jnp.arange(blk) + blk
    kpos = jnp.arange(2 * blk)
    dist = qpos[:, None] - kpos[None, :]
    in_window = (dist >= 0) & (dist < WINDOW)
    bucket = _t5_causal_bucket(jnp.maximum(dist, 0))
    table = rel_bias.astype(F32)
    table = jnp.stack([table[:, h] for h in SWA_Q_PERM])
    bias = jnp.full((SWA_HEADS, blk, 2 * blk), NEG_INF, F32)
    for bkt in range(REL_BUCKETS):
        hit = (in_window & (bucket == bkt))[None]
        bias = jnp.where(hit, table[:, bkt][:, None, None], bias)
    return bias


def _gla_kernel(q_ref, k_ref, v_ref, r_ref, lr_ref, gw_ref, gb_ref, gn_ref, lm_ref, o_ref,
                st_ref, *, rows, hps):
    t = pl.program_id(1)
    ch = GLA_CHUNK
    nch = rows // ch
    heads = range(hps)

    @pl.when(t == 0)
    def _():
        st_ref[...] = jnp.zeros_like(st_ref)

    nt = (((1,), (1,)), ((), ()))
    tn = (((0,), (0,)), ((), ()))
    rowi = lax.broadcasted_iota(jnp.int32, (rows, rows), 0)
    coli = lax.broadcasted_iota(jnp.int32, (rows, rows), 1)
    same_chunk_causal = jnp.logical_and(coli <= rowi, (rowi // ch) == (coli // ch))
    lm = lm_ref[...]
    lr = lr_ref[...]
    gn = gn_ref[...]

    gp = [jnp.dot(lr, gw_ref[h], preferred_element_type=F32) + gb_ref[h] for h in heads]
    lg = []
    for h in heads:
        log_sig = jnp.minimum(gp[h], 0.0) - jnp.log(1.0 + jnp.exp(-jnp.abs(gp[h])))
        lg.append(jnp.maximum(log_sig * (1.0 / GATE_NORMALIZER), GATE_LOG_MIN))
    bb = []
    for h in heads:
        hi, lo = _split_bf16(lg[h])
        bb.append(jnp.dot(lm, hi, preferred_element_type=F32)
                  + jnp.dot(lm, lo, preferred_element_type=F32))
    q_dec, k_inv, k_end, dec = [], [], [], []
    for h in heads:
        b, b_last = bb[h][:rows], bb[h][rows:]
        q = q_ref[:, GLA_DK_PAD * h:GLA_DK_PAD * (h + 1)].astype(F32) * (GLA_DK ** -0.5)
        kk = k_ref[:, GLA_DK_PAD * h:GLA_DK_PAD * (h + 1)].astype(F32)
        q_dec.append((q * jnp.exp(b)).astype(BF16))
        k_inv.append((kk * jnp.exp(-b)).astype(BF16))
        k_end.append((kk * jnp.exp(b_last - b)).astype(BF16))
        dec.append(jnp.exp(b_last))
    v = [v_ref[:, GLA_DV_PAD * h:GLA_DV_PAD * (h + 1)] for h in heads]
    sc = []
    for h in heads:
        s_h = lax.dot_general(q_dec[h], k_inv[h], nt, preferred_element_type=F32)
        sc.append(jnp.where(same_chunk_causal, s_h, 0.0).astype(BF16))
    o_intra = [jnp.dot(sc[h], v[h], preferred_element_type=F32) for h in heads]
    d_st = [[lax.dot_general(v[h][c * ch:(c + 1) * ch], k_end[h][c * ch:(c + 1) * ch], tn,
                             preferred_element_type=F32) for c in range(nch)] for h in heads]
    st_in = []
    for h in heads:
        st = st_ref[h]
        states = []
        for c in range(nch):
            states.append(st.astype(BF16))
            st = st * dec[h][c * ch:c * ch + 1, :] + d_st[h][c]
        st_ref[h] = st
        st_in.append(states)
    for h in heads:
        o_inter = [lax.dot_general(q_dec[h][c * ch:(c + 1) * ch], st_in[h][c], nt,
                                   preferred_element_type=F32) for c in range(nch)]
        o = o_intra[h] + jnp.concatenate(o_inter, axis=0)
        ms = jnp.sum(o * o, axis=1, keepdims=True) * (1.0 / GLA_DV)
        y = o * lax.rsqrt(ms + RMS_EPS) * gn
        rr = r_ref[:, GLA_DV_PAD * h:GLA_DV_PAD * (h + 1)].astype(F32)
        o_ref[:, GLA_DV_PAD * h:GLA_DV_PAD * (h + 1)] = (
            y * (rr / (1.0 + jnp.exp(-rr)))).astype(o_ref.dtype)


def gla_attention(proj, gw_pad, gb_pad, gn_pad, layer, rows=256, hps=4):
    s = proj.shape[0]
    rows = min(rows, s)
    ch = GLA_CHUNK
    dkw, dvw = hps * GLA_DK_PAD, hps * GLA_DV_PAD
    qi, ki = OFF_BQ // dkw, OFF_BK // dkw
    vi, ri = OFF_BV // dvw, OFF_BR // dvw
    li = OFF_BLR // GLA_LR_PAD
    idx = np.arange(rows)
    same = (idx[:, None] // ch) == (idx[None, :] // ch)
    prefix = same & (idx[None, :] <= idx[:, None])
    lm = jnp.asarray(np.concatenate([prefix, same], axis=0).astype(np.float32), dtype=BF16)
    return pl.pallas_call(
        functools.partial(_gla_kernel, rows=rows, hps=hps),
        grid=(GLA_HEADS // hps, s // rows),
        in_specs=[pl.BlockSpec((rows, dkw), lambda g, t: (t, qi + g)),
                  pl.BlockSpec((rows, dkw), lambda g, t: (t, ki + g)),
                  pl.BlockSpec((rows, dvw), lambda g, t: (t, vi + g)),
                  pl.BlockSpec((rows, dvw), lambda g, t: (t, ri + g)),
                  pl.BlockSpec((rows, GLA_LR_PAD), lambda g, t: (t, li)),
                  pl.BlockSpec((None, hps, GLA_LR_PAD, GLA_DK_PAD),
                               lambda g, t: (layer, g, 0, 0)),
                  pl.BlockSpec((None, hps, 1, GLA_DK_PAD), lambda g, t: (layer, g, 0, 0)),
                  pl.BlockSpec((None, 1, GLA_DV_PAD), lambda g, t: (layer, 0, 0)),
                  pl.BlockSpec((2 * rows, rows), lambda g, t: (0, 0))],
        out_specs=pl.BlockSpec((rows, dvw), lambda g, t: (t, g)),
        out_shape=jax.ShapeDtypeStruct((s, GLA_HEADS * GLA_DV_PAD), BF16),
        scratch_shapes=[pltpu.VMEM((hps, GLA_DV_PAD, GLA_DK_PAD), F32)],
        compiler_params=_cparams(("parallel", "arbitrary")),
        name="gla",
    )(proj, proj, proj, proj, proj, gw_pad, gb_pad, gn_pad, lm)


def _sb_kernel(q_ref, k_ref, v_ref, u_ref, o_ref, acc_ref, car_ref, qm_ref, vm_ref,
               z0_ref, z1_ref, z2_ref, z3_ref, nl0_ref, nl1_ref, w0_ref, w1_ref,
               *, bq, nblk, unroll):
    lane = lax.broadcasted_iota(jnp.int32, (bq, LANES), 1)
    low = lane < HEAD_DIM
    nt = (((1,), (1,)), ((), ()))
    z_ref = (z0_ref, z1_ref, z2_ref, z3_ref)
    nl_ref, w_ref = (nl0_ref, nl1_ref), (w0_ref, w1_ref)

    def rows(blk):
        if isinstance(blk, int):
            return pl.ds(blk * bq, bq)
        return pl.ds(pl.multiple_of(blk * bq, bq), bq)

    def strict_mask():
        rowi = lax.broadcasted_iota(jnp.int32, (2 * bq, bq), 0)
        coli = lax.broadcasted_iota(jnp.int32, (2 * bq, bq), 1)
        return coli < (rowi & (bq - 1))

    def s1a(item, tick4):
        i, c = item
        z_ref[tick4][...] = lax.dot_general(qm_ref[i], k_ref[rows(c), :], nt,
                                            preferred_element_type=F32)

    def s1b(tick4, masked):
        z = z_ref[tick4][...]
        nl = jnp.maximum(z, 0.0) + jnp.log(1.0 + jnp.exp2(jnp.abs(z) * (-LOG2E)))
        if masked:
            nl = jnp.where(strict_mask(), nl, 0.0)
        nl_ref[tick4 % 2][...] = nl.astype(BF16)

    def s2(item, tick4, masked, first):
        i, _ = item
        cs = jnp.dot(nl_ref[tick4 % 2][...], u_ref[...], preferred_element_type=F32)
        tot = jnp.broadcast_to(cs[:, 0:1], (2 * bq, LANES))
        if first:
            arg = z_ref[tick4][...] - cs
            car_ref[i] = tot
        else:
            car = car_ref[i]
            arg = z_ref[tick4][...] - cs - jnp.concatenate([car, car], axis=1)
            car_ref[i] = car + tot
        w = jnp.exp(arg)
        if masked:
            w = jnp.where(strict_mask(), w, 0.0)
        w = w.astype(BF16)
        w_ref[tick4 % 2][:, :bq] = w[:bq]
        w_ref[tick4 % 2][:, bq:] = w[bq:]

    def s3(item, tick4, first):
        i, c = item
        pv = jnp.dot(w_ref[tick4 % 2][...], vm_ref[c], preferred_element_type=F32)
        if first:
            acc_ref[rows(i), :] = pv
        else:
            acc_ref[rows(i), :] += pv

    def run_pipeline(items, step, masked, first):
        n = len(items)

        def tick(t4, its, do):
            if do[1]:
                s1b((t4 - 1) % 4, masked)
            if do[0]:
                s1a(its[0], t4)
            if do[2]:
                s2(its[2], (t4 - 2) % 4, masked, first)
            if do[3]:
                s3(its[3], (t4 - 3) % 4, first)

        def static_tick(t):
            do = [0 <= t - j < n for j in range(4)]
            its = [items[t - j] if do[j] else None for j in range(4)]
            tick(t % 4, its, do)

        lo_t, hi_t = 3, n - 1
        groups = max(hi_t - lo_t + 1, 0) // unroll
        if groups < 2:
            groups = 0
        for t in range(min(lo_t, n + 3)):
            static_tick(t)
        if groups:
            def body(_, carry):
                its = [(carry[2 * j], carry[2 * j + 1]) for j in range(4)]
                for k in range(unroll):
                    tick((lo_t + k) % 4, its, [True] * 4)
                    its = [step(*its[0])] + its[:3]
                return tuple(x for it in its for x in it)

            init = tuple(jnp.int32(x) for j in range(4) for x in items[lo_t - j])
            lax.fori_loop(0, groups, body, init)
        for t in range(lo_t + unroll * groups, n + 3):
            static_tick(t)

    for blk in range(nblk):
        r = rows(blk)
        qb, vb = q_ref[r, :], v_ref[r, :]
        zero = jnp.zeros_like(qb)
        qm_ref[blk, :bq] = jnp.where(low, qb, zero)
        qm_ref[blk, bq:] = jnp.where(low, zero, qb)
        vm_ref[blk, :bq] = jnp.where(low, vb, zero)
        vm_ref[blk, bq:] = jnp.where(low, zero, vb)

    diag = [(i, i) for i in range(nblk)]
    run_pipeline(diag, lambda i, c: (i + 1, c + 1), masked=True, first=True)

    off = [(i, c) for i in range(1, nblk) for c in range(i - 1, -1, -1)]

    def off_step(i, c):
        wrap = c == 0
        i2 = jnp.where(wrap, i + 1, i)
        return i2, jnp.where(wrap, i2 - 1, c - 1)

    if off:
        run_pipeline(off, off_step, masked=False, first=False)

    o_ref[...] = acc_ref[...].astype(o_ref.dtype)


def sb_attention(proj, bq=256, unroll=8):
    s = proj.shape[0]
    bq = min(bq, s)
    nblk = s // bq
    assert bq & (bq - 1) == 0 and unroll % 4 == 0
    qi = OFF_CQ // LANES
    ki = OFF_CK // LANES
    vi = OFF_CV // LANES
    u = (np.arange(bq)[:, None] >= np.arange(bq)[None, :]).astype(np.float32)
    u = jnp.asarray(u, dtype=BF16)
    slot = lambda shape, dtype: pltpu.VMEM(shape, dtype)
    return pl.pallas_call(
        functools.partial(_sb_kernel, bq=bq, nblk=nblk, unroll=unroll),
        grid=(SB_HEADS // 2,),
        in_specs=[pl.BlockSpec((s, LANES), lambda p: (0, qi + p)),
                  pl.BlockSpec((s, LANES), lambda p: (0, ki + p)),
                  pl.BlockSpec((s, LANES), lambda p: (0, vi + p)),
                  pl.BlockSpec((bq, bq), lambda p: (0, 0))],
        out_specs=pl.BlockSpec((s, LANES), lambda p: (0, p)),
        out_shape=jax.ShapeDtypeStruct((s, SB_WIDTH), BF16),
        scratch_shapes=[pltpu.VMEM((s, LANES), F32),
                        pltpu.VMEM((nblk, 2 * bq, LANES), F32),
                        pltpu.VMEM((nblk, 2 * bq, LANES), BF16),
                        pltpu.VMEM((nblk, 2 * bq, LANES), BF16)]
                       + [slot((2 * bq, bq), F32)] * 4
                       + [slot((2 * bq, bq), BF16)] * 2
                       + [slot((bq, 2 * bq), BF16)] * 2,
        compiler_params=_cparams(("parallel",)),
        name="stickbreak",
    )(proj, proj, proj, u)


def _oproj_kernel(a_ref, b_ref, c_ref, wa_ref, wb_ref, wc_ref, ga_ref, gc_ref,
                  x_ref, gn_ref, xo_ref, ho_ref):
    an = _rms(a_ref[...].astype(F32), ga_ref[...]).astype(BF16)
    cn = _rms(c_ref[...].astype(F32), gc_ref[...]).astype(BF16)
    y = (jnp.dot(an, wa_ref[...], preferred_element_type=F32)
         + jnp.dot(b_ref[...], wb_ref[...], preferred_element_type=F32)
         + jnp.dot(cn, wc_ref[...], preferred_element_type=F32))
    xn = x_ref[...] + y
    xo_ref[...] = xn
    ho_ref[...] = _rms(xn, gn_ref[...]).astype(ho_ref.dtype)


def out_projection(a, b, c, wa, wb, wc, ga, gc, x, gn, layer, tm=512):
    s, d = x.shape
    tm = min(tm, s)
    row = lambda i: (i, 0)
    fixed = lambda i: (layer, 0, 0)
    per_layer = lambda arr: pl.BlockSpec((None,) + arr.shape[1:], fixed)
    return pl.pallas_call(
        _oproj_kernel,
        grid=(s // tm,),
        in_specs=[pl.BlockSpec((tm, a.shape[1]), row),
                  pl.BlockSpec((tm, b.shape[1]), row),
                  pl.BlockSpec((tm, c.shape[1]), row),
                  per_layer(wa), per_layer(wb), per_layer(wc), per_layer(ga), per_layer(gc),
                  pl.BlockSpec((tm, d), row),
                  per_layer(gn)],
        out_specs=[pl.BlockSpec((tm, d), row), pl.BlockSpec((tm, d), row)],
        out_shape=[jax.ShapeDtypeStruct((s, d), F32), jax.ShapeDtypeStruct((s, d), BF16)],
        compiler_params=_cparams(("parallel",)),
        name="out_proj",
    )(a, b, c, wa, wb, wc, ga, gc, x, gn)


def _mlp_kernel(h_ref, x_ref, w1_ref, w2_ref, gn_ref, *out_refs, final):
    acc_ref = out_refs[0]
    j = pl.program_id(1)

    @pl.when(j == 0)
    def _():
        acc_ref[...] = x_ref[...]

    u = jnp.dot(h_ref[...], w1_ref[...].astype(BF16), preferred_element_type=F32)
    act = jnp.square(jnp.maximum(u, 0.0)).astype(BF16)
    acc_ref[...] += jnp.dot(act, w2_ref[...].astype(BF16), preferred_element_type=F32)

    @pl.when(j == pl.num_programs(1) - 1)
    def _():
        y = _rms(acc_ref[...], gn_ref[...])
        if final:
            acc_ref[...] = y
        else:
            out_refs[1][...] = y.astype(out_refs[1].dtype)


def mlp_block(h, x, w1_all, w2_all, layer, gn, final, tm=1024, tf=512):
    s, d = x.shape
    ff = w1_all.shape[2]
    tm = min(tm, s)
    row = lambda i, j: (i, 0)
    once = pl.Buffered(1)
    out_specs = [pl.BlockSpec((tm, d), row)]
    out_shape = [jax.ShapeDtypeStruct((s, d), F32)]
    if not final:
        out_specs.append(pl.BlockSpec((tm, d), row))
        out_shape.append(jax.ShapeDtypeStruct((s, d), BF16))
    return pl.pallas_call(
        functools.partial(_mlp_kernel, final=final),
        grid=(s // tm, ff // tf),
        in_specs=[pl.BlockSpec((tm, d), row, pipeline_mode=once),
                  pl.BlockSpec((tm, d), row, pipeline_mode=once),
                  pl.BlockSpec((None, d, tf), lambda i, j: (layer, 0, j)),
                  pl.BlockSpec((None, tf, d), lambda i, j: (layer, j, 0)),
                  pl.BlockSpec((None, 1, d), lambda i, j: (layer, 0, 0))],
        out_specs=out_specs,
        out_shape=out_shape,
        compiler_params=_cparams(("parallel", "arbitrary")),
        name="mlp",
    )(h, x, w1_all, w2_all, gn)


def _zero_cols(w, width):
    return jnp.zeros(w.shape[:-1] + (width,), w.dtype)


def _head_cols(w, off, heads, width, padded):
    pieces = []
    for h in range(heads):
        pieces.append(w[..., off + h * width:off + (h + 1) * width])
        if padded > width:
            pieces.append(_zero_cols(w, padded - width))
    return pieces


def _head_rows(w, off, heads, width, padded):
    pieces = []
    for h in range(heads):
        pieces.append(w[..., off + h * width:off + (h + 1) * width, :])
        if padded > width:
            pieces.append(jnp.zeros(w.shape[:-2] + (padded - width, w.shape[-1]), w.dtype))
    return pieces


def _prep_w_in(w_in):
    seg = lambda off, width: w_in[..., off:off + width]
    scale = HEAD_DIM ** -0.5
    pieces = (
        _head_cols(w_in, _O_BV, GLA_HEADS, GLA_DV, GLA_DV_PAD)
        + _head_cols(w_in, _O_BR, GLA_HEADS, GLA_DV, GLA_DV_PAD)
        + _head_cols(w_in, _O_BQ, GLA_HEADS, GLA_DK, GLA_DK_PAD)
        + _head_cols(w_in, _O_BK, GLA_HEADS, GLA_DK, GLA_DK_PAD)
        + [seg(_O_CQ, SB_WIDTH) * scale, seg(_O_CK, SB_WIDTH), seg(_O_CV, SB_WIDTH)]
        + [seg(_O_AQ + HEAD_DIM * h, HEAD_DIM) * scale for h in SWA_Q_PERM]
        + [seg(_O_AK, SWA_KV_WIDTH), seg(_O_AV, SWA_KV_WIDTH), seg(_O_BLR, GLA_LOWRANK),
           _zero_cols(w_in, PROJ_PAD_WIDTH - OFF_BLR - GLA_LOWRANK)])
    return jnp.concatenate(pieces, axis=-1).astype(BF16)


def _swa_perm_cols(w):
    return jnp.concatenate([w[..., HEAD_DIM * h:HEAD_DIM * (h + 1)] for h in SWA_Q_PERM], axis=-1)


def kernel(x, norm_mix, w_in, swa_sinks, rel_bias, gla_gate_w, gla_gate_b, gla_norm,
           swa_out_norm, sb_out_norm, w_out, norm_mlp, w_mlp_in, w_mlp_out, norm_final):
    depth = w_in.shape[0]
    xs = x[0]
    gla_w = GLA_HEADS * GLA_DV
    w_in_p = prep_w_in(jnp.swapaxes(w_in, 1, 2))
    wa = jnp.concatenate([w_out[:, HEAD_DIM * h:HEAD_DIM * (h + 1)] for h in SWA_Q_PERM],
                         axis=1).astype(BF16)
    wb = jnp.concatenate(_head_rows(w_out, SWA_WIDTH, GLA_HEADS, GLA_DV, GLA_DV_PAD),
                         axis=1).astype(BF16)
    wc = w_out[:, SWA_WIDTH + gla_w:].astype(BF16)
    ga = _swa_perm_cols(swa_out_norm)[:, None, :]
    gc = sb_out_norm[:, None, :]
    g_mlp = norm_mlp[:, None, :]
    g_next = jnp.concatenate([norm_mix[1:], norm_final[None]], axis=0)[:, None, :]
    sinks_p = jnp.concatenate([swa_sinks[:, h:h + 1] for h in SWA_Q_PERM], axis=1)
    bias_tab = swa_bias_table(rel_bias)
    gw = jnp.stack([jnp.pad(gla_gate_w[:, :, GLA_DK * h:GLA_DK * (h + 1)],
                            ((0, 0), (0, GLA_LR_PAD - GLA_LOWRANK), (0, GLA_DK_PAD - GLA_DK)))
                    for h in range(GLA_HEADS)], axis=1).astype(BF16)
    gb = jnp.stack([jnp.pad(gla_gate_b[:, GLA_DK * h:GLA_DK * (h + 1)],
                            ((0, 0), (0, GLA_DK_PAD - GLA_DK)))
                    for h in range(GLA_HEADS)], axis=1)[:, :, None, :]
    gn = jnp.pad(gla_norm, ((0, 0), (0, GLA_DV_PAD - GLA_DV)))[:, None, :]

    h = None
    for l in range(depth):
        if l == 0:
            proj = in_projection(xs, w_in_p, l, gain=norm_mix[0])
        else:
            proj = in_projection(h, w_in_p, l)
        a = swa_attention(proj, sinks_p, bias_tab, l)
        b = gla_attention(proj, gw, gb, gn, l)
        c = sb_attention(proj)
        xs, hm = out_projection(a, b, c, wa, wb, wc, ga, gc, xs, g_mlp, l)
        if l + 1 < depth:
            xs, h = mlp_block(hm, xs, w_mlp_in, w_mlp_out, l, g_next, final=False)
        else:
            (out,) = mlp_block(hm, xs, w_mlp_in, w_mlp_out, l, g_next, final=True)
    return out[None]
```

```python
import functools
import math

import numpy as np
import jax
import jax.numpy as jnp
from jax import lax
from jax.experimental import pallas as pl
from jax.experimental.pallas import tpu as pltpu

F32 = jnp.float32
BF16 = jnp.bfloat16

D_MODEL = 2048
HEAD_DIM = 64
SWA_HEADS = 12
SWA_KV_HEADS = 4
SWA_WIDTH = SWA_HEADS * HEAD_DIM
SWA_KV_WIDTH = SWA_KV_HEADS * HEAD_DIM
SWA_BLOCK = 128
WINDOW = 128
REL_BUCKETS = 32
REL_MAX_DIST = 128
GLA_HEADS = 4
GLA_DV = 192
GLA_DK = 96
GLA_DK_PAD = 128
GLA_DV_PAD = 256
GLA_LOWRANK = 16
GLA_LR_PAD = 128
GLA_CHUNK = 64
GATE_NORMALIZER = 16.0
GATE_LOG_MIN = -1.0
SB_HEADS = 8
SB_WIDTH = SB_HEADS * HEAD_DIM
D_FF = 4 * D_MODEL
RMS_EPS = 1e-6
LOG2E = 1.4426950408889634
NEG_INF = -1e30

LANES = 128
VMEM_LIMIT = 56 * 1024 * 1024

OFF_BV = 0
OFF_BR = 1024
OFF_BQ = 2048
OFF_BK = 2560
OFF_CQ = 3072
OFF_CK = 3584
OFF_CV = 4096
OFF_AQ = 4608
OFF_AK = 5376
OFF_AV = 5632
OFF_BLR = 5888
PROJ_PAD_WIDTH = 6144

SWA_Q_PERM = (0, 3, 1, 4, 2, 5, 6, 9, 7, 10, 8, 11)

_O_AQ, _O_AK, _O_AV = 0, 768, 1024
_O_BQ, _O_BK, _O_BV, _O_BR, _O_BLR = 1280, 1664, 2048, 2816, 3584
_O_CQ, _O_CK, _O_CV = 3600, 4112, 4624


def _cparams(sem):
    return pltpu.CompilerParams(dimension_semantics=sem, vmem_limit_bytes=VMEM_LIMIT)


def _rms(x, gain):
    ms = jnp.mean(x * x, axis=-1, keepdims=True)
    return x * lax.rsqrt(ms + RMS_EPS) * gain


def _split_bf16(x):
    hi = x.astype(BF16)
    lo = (x - hi.astype(F32)).astype(BF16)
    return hi, lo


_NT_DIMS = (((1,), (1,)), ((), ()))


def _matmul_kernel(h_ref, w_ref, o_ref):
    o_ref[...] = lax.dot_general(h_ref[...], w_ref[...], _NT_DIMS,
                                 preferred_element_type=F32).astype(o_ref.dtype)


def _norm_matmul_kernel(x_ref, g_ref, w_ref, o_ref, h_ref):
    @pl.when(pl.program_id(1) == 0)
    def _():
        h_ref[...] = _rms(x_ref[...], g_ref[...]).astype(h_ref.dtype)

    o_ref[...] = lax.dot_general(h_ref[...], w_ref[...], _NT_DIMS,
                                 preferred_element_type=F32).astype(o_ref.dtype)


def in_projection(h, w_all, layer, gain=None, tm=1024, tn=1024):
    s, d = h.shape
    n = w_all.shape[1]
    tm = min(tm, s)
    w_spec = pl.BlockSpec((None, tn, d), lambda i, j: (layer, j, 0))
    h_spec = pl.BlockSpec((tm, d), lambda i, j: (i, 0))
    common = dict(
        grid=(s // tm, n // tn),
        out_specs=pl.BlockSpec((tm, tn), lambda i, j: (i, j)),
        out_shape=jax.ShapeDtypeStruct((s, n), BF16),
        compiler_params=_cparams(("parallel", "arbitrary")),
        name="in_proj",
    )
    if gain is None:
        return pl.pallas_call(_matmul_kernel, in_specs=[h_spec, w_spec], **common)(h, w_all)
    return pl.pallas_call(
        _norm_matmul_kernel,
        in_specs=[h_spec, pl.BlockSpec((1, d), lambda i, j: (0, 0)), w_spec],
        scratch_shapes=[pltpu.VMEM((tm, d), BF16)],
        **common,
    )(h, gain.reshape(1, d), w_all)


def _w_in_moves():
    scale = HEAD_DIM ** -0.5
    moves = []
    for h in range(GLA_HEADS):
        moves.append((_O_BV + GLA_DV * h, OFF_BV + GLA_DV_PAD * h, GLA_DV, 1.0))
        moves.append((_O_BR + GLA_DV * h, OFF_BR + GLA_DV_PAD * h, GLA_DV, 1.0))
        moves.append((_O_BQ + GLA_DK * h, OFF_BQ + GLA_DK_PAD * h, GLA_DK, 1.0))
        moves.append((_O_BK + GLA_DK * h, OFF_BK + GLA_DK_PAD * h, GLA_DK, 1.0))
    moves += [(_O_CQ, OFF_CQ, SB_WIDTH, scale), (_O_CK, OFF_CK, SB_WIDTH, 1.0),
              (_O_CV, OFF_CV, SB_WIDTH, 1.0)]
    for j, h in enumerate(SWA_Q_PERM):
        moves.append((_O_AQ + HEAD_DIM * h, OFF_AQ + HEAD_DIM * j, HEAD_DIM, scale))
    moves += [(_O_AK, OFF_AK, SWA_KV_WIDTH, 1.0), (_O_AV, OFF_AV, SWA_KV_WIDTH, 1.0),
              (_O_BLR, OFF_BLR, GLA_LOWRANK, 1.0)]
    return moves


def _w_in_prep_kernel(w_ref, o_ref):
    o_ref[...] = jnp.zeros_like(o_ref)
    for src, dst, width, scale in _w_in_moves():
        piece = w_ref[src:src + width, :]
        if scale != 1.0:
            piece = piece * scale
        o_ref[dst:dst + width, :] = piece.astype(o_ref.dtype)


def prep_w_in(w_in_t, tc=256):
    layers, n, d = w_in_t.shape
    return pl.pallas_call(
        _w_in_prep_kernel,
        grid=(layers, d // tc),
        in_specs=[pl.BlockSpec((None, n, tc), lambda l, i: (l, 0, i))],
        out_specs=pl.BlockSpec((None, PROJ_PAD_WIDTH, tc), lambda l, i: (l, 0, i)),
        out_shape=jax.ShapeDtypeStruct((layers, PROJ_PAD_WIDTH, d), BF16),
        compiler_params=_cparams(("parallel", "parallel")),
        name="w_in_prep",
    )(w_in_t)


def _swa_kernel(sink_ref, q_ref, kp_ref, kc_ref, vp_ref, vc_ref, bias_ref, o_ref, *, layer):
    blk = SWA_BLOCK
    nb = SWA_HEADS // 2
    k = jnp.concatenate([kp_ref[...], kc_ref[...]], axis=0)
    v = jnp.concatenate([vp_ref[...], vc_ref[...]], axis=0)
    lane = lax.broadcasted_iota(jnp.int32, (blk, LANES), 1)
    low = lane < HEAD_DIM
    top = lax.broadcasted_iota(jnp.int32, (2 * blk, 1), 0) < blk
    ones = jnp.ones((2 * blk, LANES), BF16)
    nt = (((1,), (1,)), ((), ()))

    scores = []
    for b in range(nb):
        qb = q_ref[:, LANES * b:LANES * (b + 1)]
        zero = jnp.zeros_like(qb)
        lhs = jnp.concatenate([jnp.where(low, qb, zero), jnp.where(low, zero, qb)], axis=0)
        kb = k[:, LANES * (b // 3):LANES * (b // 3 + 1)]
        scores.append(lax.dot_general(lhs, kb, nt, preferred_element_type=F32) + bias_ref[b])
    probs, sink_term = [], []
    for b in range(nb):
        sink = jnp.where(top, sink_ref[layer, 2 * b], sink_ref[layer, 2 * b + 1])
        m = jnp.maximum(jnp.max(scores[b], axis=1, keepdims=True), sink)
        probs.append(jnp.exp(scores[b] - m).astype(BF16))
        sink_term.append(jnp.exp(sink - m))
    for b in range(nb):
        vb = v[:, LANES * (b // 3):LANES * (b // 3 + 1)]
        pv = jnp.dot(probs[b], jnp.concatenate([vb, ones], axis=1),
                     preferred_element_type=F32)
        o = pv[:, :LANES] / (pv[:, LANES:] + sink_term[b])
        o_ref[:, LANES * b:LANES * (b + 1)] = jnp.where(low, o[:blk], o[blk:]).astype(o_ref.dtype)


def swa_attention(proj, sinks_perm, bias_tab, layer):
    s = proj.shape[0]
    blk = SWA_BLOCK
    qi = OFF_AQ // SWA_WIDTH
    ki = OFF_AK // SWA_KV_WIDTH
    vi = OFF_AV // SWA_KV_WIDTH
    prev = lambda i: jnp.maximum(i - 1, 0)
    return pl.pallas_call(
        functools.partial(_swa_kernel, layer=layer),
        grid=(s // blk,),
        in_specs=[pl.BlockSpec(memory_space=pltpu.SMEM),
                  pl.BlockSpec((blk, SWA_WIDTH), lambda i: (i, qi)),
                  pl.BlockSpec((blk, SWA_KV_WIDTH), lambda i: (prev(i), ki)),
                  pl.BlockSpec((blk, SWA_KV_WIDTH), lambda i: (i, ki)),
                  pl.BlockSpec((blk, SWA_KV_WIDTH), lambda i: (prev(i), vi)),
                  pl.BlockSpec((blk, SWA_KV_WIDTH), lambda i: (i, vi)),
                  pl.BlockSpec((None, SWA_HEADS // 2, 2 * blk, 2 * blk),
                               lambda i: (jnp.minimum(i, 1), 0, 0, 0))],
        out_specs=pl.BlockSpec((blk, SWA_WIDTH), lambda i: (i, 0)),
        out_shape=jax.ShapeDtypeStruct((s, SWA_WIDTH), BF16),
        compiler_params=_cparams(("parallel",)),
        name="swa",
    )(sinks_perm, proj, proj, proj, proj, proj, bias_tab)


def _t5_causal_bucket(dist):
    max_exact = REL_BUCKETS // 2
    is_small = dist < max_exact
    ratio = (jnp.log(jnp.maximum(dist, 1).astype(F32) / max_exact)
             / math.log(REL_MAX_DIST / max_exact))
    large = max_exact + (ratio * (REL_BUCKETS - max_exact)).astype(jnp.int32)
    large = jnp.minimum(large, REL_BUCKETS - 1)
    return jnp.where(is_small, dist, large)


def swa_bias_table(rel_bias):
    blk = SWA_BLOCK
    qpos = jnp.arange(blk) + blk
    kpos = jnp.arange(2 * blk)
    dist = qpos[:, None] - kpos[None, :]
    in_window = (dist >= 0) & (dist < WINDOW)
    bucket = _t5_causal_bucket(jnp.maximum(dist, 0))
    table = rel_bias.astype(F32)
    table = jnp.stack([table[:, h] for h in SWA_Q_PERM])
    bias = jnp.full((SWA_HEADS, blk, 2 * blk), NEG_INF, F32)
    for bkt in range(REL_BUCKETS):
        hit = (in_window & (bucket == bkt))[None]
        bias = jnp.where(hit, table[:, bkt][:, None, None], bias)
    bias = bias.reshape(SWA_HEADS // 2, 2 * blk, 2 * blk)
    first = jnp.where((kpos < blk)[None, None, :], NEG_INF, bias)
    return jnp.stack([first, bias])


def _gla_kernel(q_ref, k_ref, v_ref, r_ref, lr_ref, gw_ref, gb_ref, gn_ref, lm_ref, o_ref,
                st_ref, *, rows, hps):
    t = pl.program_id(1)
    ch = GLA_CHUNK
    nch = rows // ch
    heads = range(hps)

    @pl.when(t == 0)
    def _():
        st_ref[...] = jnp.zeros_like(st_ref)

    nt = (((1,), (1,)), ((), ()))
    tn = (((0,), (0,)), ((), ()))
    rowi = lax.broadcasted_iota(jnp.int32, (rows, rows), 0)
    coli = lax.broadcasted_iota(jnp.int32, (rows, rows), 1)
    same_chunk_causal = jnp.logical_and(coli <= rowi, (rowi // ch) == (coli // ch))
    lm = lm_ref[...]
    lr = lr_ref[...]
    gn = gn_ref[...]

    gp = [jnp.dot(lr, gw_ref[h], preferred_element_type=F32) + gb_ref[h] for h in heads]
    lg = []
    for h in heads:
        log_sig = jnp.minimum(gp[h], 0.0) - jnp.log(1.0 + jnp.exp(-jnp.abs(gp[h])))
        lg.append(jnp.maximum(log_sig * (1.0 / GATE_NORMALIZER), GATE_LOG_MIN))
    bb = []
    for h in heads:
        hi, lo = _split_bf16(lg[h])
        bb.append(jnp.dot(lm, hi, preferred_element_type=F32)
                  + jnp.dot(lm, lo, preferred_element_type=F32))
    q_dec, k_inv, k_end, dec = [], [], [], []
    for h in heads:
        b, b_last = bb[h][:rows], bb[h][rows:]
        q = q_ref[:, GLA_DK_PAD * h:GLA_DK_PAD * (h + 1)].astype(F32) * (GLA_DK ** -0.5)
        kk = k_ref[:, GLA_DK_PAD * h:GLA_DK_PAD * (h + 1)].astype(F32)
        q_dec.append((q * jnp.exp(b)).astype(BF16))
        k_inv.append((kk * jnp.exp(-b)).astype(BF16))
        k_end.append((kk * jnp.exp(b_last - b)).astype(BF16))
        dec.append(jnp.exp(b_last))
    v = [v_ref[:, GLA_DV_PAD * h:GLA_DV_PAD * (h + 1)] for h in heads]
    sc = []
    for h in heads:
        s_h = lax.dot_general(q_dec[h], k_inv[h], nt, preferred_element_type=F32)
        sc.append(jnp.where(same_chunk_causal, s_h, 0.0).astype(BF16))
    o_intra = [jnp.dot(sc[h], v[h], preferred_element_type=F32) for h in heads]
    d_st = [[lax.dot_general(v[h][c * ch:(c + 1) * ch], k_end[h][c * ch:(c + 1) * ch], tn,
                             preferred_element_type=F32) for c in range(nch)] for h in heads]
    st_in = []
    for h in heads:
        st = st_ref[h]
        states = []
        for c in range(nch):
            states.append(st.astype(BF16))
            st = st * dec[h][c * ch:c * ch + 1, :] + d_st[h][c]
        st_ref[h] = st
        st_in.append(states)
    for h in heads:
        o_inter = [lax.dot_general(q_dec[h][c * ch:(c + 1) * ch], st_in[h][c], nt,
                                   preferred_element_type=F32) for c in range(nch)]
        o = o_intra[h] + jnp.concatenate(o_inter, axis=0)
        ms = jnp.sum(o * o, axis=1, keepdims=True) * (1.0 / GLA_DV)
        y = o * lax.rsqrt(ms + RMS_EPS) * gn
        rr = r_ref[:, GLA_DV_PAD * h:GLA_DV_PAD * (h + 1)].astype(F32)
        o_ref[:, GLA_DV_PAD * h:GLA_DV_PAD * (h + 1)] = (
            y * (rr / (1.0 + jnp.exp(-rr)))).astype(o_ref.dtype)


def gla_attention(proj, gw_pad, gb_pad, gn_pad, layer, rows=256, hps=4):
    s = proj.shape[0]
    rows = min(rows, s)
    ch = GLA_CHUNK
    dkw, dvw = hps * GLA_DK_PAD, hps * GLA_DV_PAD
    qi, ki = OFF_BQ // dkw, OFF_BK // dkw
    vi, ri = OFF_BV // dvw, OFF_BR // dvw
    li = OFF_BLR // GLA_LR_PAD
    idx = np.arange(rows)
    same = (idx[:, None] // ch) == (idx[None, :] // ch)
    prefix = same & (idx[None, :] <= idx[:, None])
    lm = jnp.asarray(np.concatenate([prefix, same], axis=0).astype(np.float32), dtype=BF16)
    return pl.pallas_call(
        functools.partial(_gla_kernel, rows=rows, hps=hps),
        grid=(GLA_HEADS // hps, s // rows),
        in_specs=[pl.BlockSpec((rows, dkw), lambda g, t: (t, qi + g)),
                  pl.BlockSpec((rows, dkw), lambda g, t: (t, ki + g)),
                  pl.BlockSpec((rows, dvw), lambda g, t: (t, vi + g)),
                  pl.BlockSpec((rows, dvw), lambda g, t: (t, ri + g)),
                  pl.BlockSpec((rows, GLA_LR_PAD), lambda g, t: (t, li)),
                  pl.BlockSpec((None, hps, GLA_LR_PAD, GLA_DK_PAD),
                               lambda g, t: (layer, g, 0, 0)),
                  pl.BlockSpec((None, hps, 1, GLA_DK_PAD), lambda g, t: (layer, g, 0, 0)),
                  pl.BlockSpec((None, 1, GLA_DV_PAD), lambda g, t: (layer, 0, 0)),
                  pl.BlockSpec((2 * rows, rows), lambda g, t: (0, 0))],
        out_specs=pl.BlockSpec((rows, dvw), lambda g, t: (t, g)),
        out_shape=jax.ShapeDtypeStruct((s, GLA_HEADS * GLA_DV_PAD), BF16),
        scratch_shapes=[pltpu.VMEM((hps, GLA_DV_PAD, GLA_DK_PAD), F32)],
        compiler_params=_cparams(("parallel", "arbitrary")),
        name="gla",
    )(proj, proj, proj, proj, proj, gw_pad, gb_pad, gn_pad, lm)


def _sb_kernel(q_ref, k_ref, v_ref, u_ref, o_ref, acc_ref, car_ref, qm_ref, vm_ref,
               z0_ref, z1_ref, z2_ref, z3_ref, nl0_ref, nl1_ref, w0_ref, w1_ref,
               *, bq, nblk, unroll):
    lane = lax.broadcasted_iota(jnp.int32, (bq, LANES), 1)
    low = lane < HEAD_DIM
    nt = (((1,), (1,)), ((), ()))
    z_ref = (z0_ref, z1_ref, z2_ref, z3_ref)
    nl_ref, w_ref = (nl0_ref, nl1_ref), (w0_ref, w1_ref)

    def rows(blk):
        if isinstance(blk, int):
            return pl.ds(blk * bq, bq)
        return pl.ds(pl.multiple_of(blk * bq, bq), bq)

    def strict_mask():
        rowi = lax.broadcasted_iota(jnp.int32, (2 * bq, bq), 0)
        coli = lax.broadcasted_iota(jnp.int32, (2 * bq, bq), 1)
        return coli < (rowi & (bq - 1))

    def s1a(item, tick4):
        i, c = item
        z_ref[tick4][...] = lax.dot_general(qm_ref[i], k_ref[rows(c), :], nt,
                                            preferred_element_type=F32)

    def s1b(tick4, masked):
        z = z_ref[tick4][...]
        nl = jnp.maximum(z, 0.0) + jnp.log(1.0 + jnp.exp2(jnp.abs(z) * (-LOG2E)))
        if masked:
            nl = jnp.where(strict_mask(), nl, 0.0)
        nl_ref[tick4 % 2][...] = nl.astype(BF16)

    def s2(item, tick4, masked, first):
        i, _ = item
        cs = jnp.dot(nl_ref[tick4 % 2][...], u_ref[...], preferred_element_type=F32)
        tot = jnp.broadcast_to(cs[:, 0:1], (2 * bq, LANES))
        if first:
            arg = z_ref[tick4][...] - cs
            car_ref[i] = tot
        else:
            car = car_ref[i]
            arg = z_ref[tick4][...] - cs - jnp.concatenate([car, car], axis=1)
            car_ref[i] = car + tot
        w = jnp.exp(arg)
        if masked:
            w = jnp.where(strict_mask(), w, 0.0)
        w = w.astype(BF16)
        w_ref[tick4 % 2][:, :bq] = w[:bq]
        w_ref[tick4 % 2][:, bq:] = w[bq:]

    def s3(item, tick4, first):
        i, c = item
        pv = jnp.dot(w_ref[tick4 % 2][...], vm_ref[c], preferred_element_type=F32)
        if first:
            acc_ref[rows(i), :] = pv
        else:
            acc_ref[rows(i), :] += pv

    def run_pipeline(items, step, masked, first):
        n = len(items)

        def tick(t4, its, do):
            if do[1]:
                s1b((t4 - 1) % 4, masked)
            if do[0]:
                s1a(its[0], t4)
            if do[2]:
                s2(its[2], (t4 - 2) % 4, masked, first)
            if do[3]:
                s3(its[3], (t4 - 3) % 4, first)

        def static_tick(t):
            do = [0 <= t - j < n for j in range(4)]
            its = [items[t - j] if do[j] else None for j in range(4)]
            tick(t % 4, its, do)

        lo_t, hi_t = 3, n - 1
        groups = max(hi_t - lo_t + 1, 0) // unroll
        if groups < 2:
            groups = 0
        for t in range(min(lo_t, n + 3)):
            static_tick(t)
        if groups:
            def body(_, carry):
                its = [(carry[2 * j], carry[2 * j + 1]) for j in range(4)]
                for k in range(unroll):
                    tick((lo_t + k) % 4, its, [True] * 4)
                    its = [step(*its[0])] + its[:3]
                return tuple(x for it in its for x in it)

            init = tuple(jnp.int32(x) for j in range(4) for x in items[lo_t - j])
            lax.fori_loop(0, groups, body, init)
        for t in range(lo_t + unroll * groups, n + 3):
            static_tick(t)

    for blk in range(nblk):
        r = rows(blk)
        qb, vb = q_ref[r, :], v_ref[r, :]
        zero = jnp.zeros_like(qb)
        qm_ref[blk, :bq] = jnp.where(low, qb, zero)
        qm_ref[blk, bq:] = jnp.where(low, zero, qb)
        vm_ref[blk, :bq] = jnp.where(low, vb, zero)
        vm_ref[blk, bq:] = jnp.where(low, zero, vb)

    diag = [(i, i) for i in range(nblk)]
    run_pipeline(diag, lambda i, c: (i + 1, c + 1), masked=True, first=True)

    off = [(i, c) for i in range(1, nblk) for c in range(i - 1, -1, -1)]

    def off_step(i, c):
        wrap = c == 0
        i2 = jnp.where(wrap, i + 1, i)
        return i2, jnp.where(wrap, i2 - 1, c - 1)

    if off:
        run_pipeline(off, off_step, masked=False, first=False)

    o_ref[...] = acc_ref[...].astype(o_ref.dtype)


def sb_attention(proj, bq=256, unroll=8):
    s = proj.shape[0]
    bq = min(bq, s)
    nblk = s // bq
    assert bq & (bq - 1) == 0 and unroll % 4 == 0
    qi = OFF_CQ // LANES
    ki = OFF_CK // LANES
    vi = OFF_CV // LANES
    u = (np.arange(bq)[:, None] >= np.arange(bq)[None, :]).astype(np.float32)
    u = jnp.asarray(u, dtype=BF16)
    slot = lambda shape, dtype: pltpu.VMEM(shape, dtype)
    return pl.pallas_call(
        functools.partial(_sb_kernel, bq=bq, nblk=nblk, unroll=unroll),
        grid=(SB_HEADS // 2,),
        in_specs=[pl.BlockSpec((s, LANES), lambda p: (0, qi + p)),
                  pl.BlockSpec((s, LANES), lambda p: (0, ki + p)),
                  pl.BlockSpec((s, LANES), lambda p: (0, vi + p)),
                  pl.BlockSpec((bq, bq), lambda p: (0, 0))],
        out_specs=pl.BlockSpec((s, LANES), lambda p: (0, p)),
        out_shape=jax.ShapeDtypeStruct((s, SB_WIDTH), BF16),
        scratch_shapes=[pltpu.VMEM((s, LANES), F32),
                        pltpu.VMEM((nblk, 2 * bq, LANES), F32),
                        pltpu.VMEM((nblk, 2 * bq, LANES), BF16),
                        pltpu.VMEM((nblk, 2 * bq, LANES), BF16)]
                       + [slot((2 * bq, bq), F32)] * 4
                       + [slot((2 * bq, bq), BF16)] * 2
                       + [slot((bq, 2 * bq), BF16)] * 2,
        compiler_params=_cparams(("parallel",)),
        name="stickbreak",
    )(proj, proj, proj, u)


def _oproj_kernel(a_ref, b_ref, c_ref, wa_ref, wb_ref, wc_ref, ga_ref, gc_ref,
                  x_ref, gn_ref, xo_ref, ho_ref):
    an = _rms(a_ref[...].astype(F32), ga_ref[...]).astype(BF16)
    cn = _rms(c_ref[...].astype(F32), gc_ref[...]).astype(BF16)
    y = (jnp.dot(an, wa_ref[...], preferred_element_type=F32)
         + jnp.dot(b_ref[...], wb_ref[...], preferred_element_type=F32)
         + jnp.dot(cn, wc_ref[...], preferred_element_type=F32))
    xn = x_ref[...] + y
    xo_ref[...] = xn
    ho_ref[...] = _rms(xn, gn_ref[...]).astype(ho_ref.dtype)


def out_projection(a, b, c, wa, wb, wc, ga, gc, x, gn, layer, tm=512):
    s, d = x.shape
    tm = min(tm, s)
    row = lambda i: (i, 0)
    fixed = lambda i: (layer, 0, 0)
    per_layer = lambda arr: pl.BlockSpec((None,) + arr.shape[1:], fixed)
    return pl.pallas_call(
        _oproj_kernel,
        grid=(s // tm,),
        in_specs=[pl.BlockSpec((tm, a.shape[1]), row),
                  pl.BlockSpec((tm, b.shape[1]), row),
                  pl.BlockSpec((tm, c.shape[1]), row),
                  per_layer(wa), per_layer(wb), per_layer(wc), per_layer(ga), per_layer(gc),
                  pl.BlockSpec((tm, d), row),
                  per_layer(gn)],
        out_specs=[pl.BlockSpec((tm, d), row), pl.BlockSpec((tm, d), row)],
        out_shape=[jax.ShapeDtypeStruct((s, d), F32), jax.ShapeDtypeStruct((s, d), BF16)],
        compiler_params=_cparams(("parallel",)),
        name="out_proj",
    )(a, b, c, wa, wb, wc, ga, gc, x, gn)


def _mlp_kernel(h_ref, x_ref, w1_ref, w2_ref, gn_ref, *out_refs, final):
    acc_ref = out_refs[0]
    j = pl.program_id(1)

    @pl.when(j == 0)
    def _():
        acc_ref[...] = x_ref[...]

    u = jnp.dot(h_ref[...], w1_ref[...].astype(BF16), preferred_element_type=F32)
    act = jnp.square(jnp.maximum(u, 0.0)).astype(BF16)
    acc_ref[...] += jnp.dot(act, w2_ref[...].astype(BF16), preferred_element_type=F32)

    @pl.when(j == pl.num_programs(1) - 1)
    def _():
        y = _rms(acc_ref[...], gn_ref[...])
        if final:
            acc_ref[...] = y
        else:
            out_refs[1][...] = y.astype(out_refs[1].dtype)


def mlp_block(h, x, w1_all, w2_all, layer, gn, final, tm=1024, tf=512):
    s, d = x.shape
    ff = w1_all.shape[2]
    tm = min(tm, s)
    row = lambda i, j: (i, 0)
    once = pl.Buffered(1)
    out_specs = [pl.BlockSpec((tm, d), row)]
    out_shape = [jax.ShapeDtypeStruct((s, d), F32)]
    if not final:
        out_specs.append(pl.BlockSpec((tm, d), row))
        out_shape.append(jax.ShapeDtypeStruct((s, d), BF16))
    return pl.pallas_call(
        functools.partial(_mlp_kernel, final=final),
        grid=(s // tm, ff // tf),
        in_specs=[pl.BlockSpec((tm, d), row, pipeline_mode=once),
                  pl.BlockSpec((tm, d), row, pipeline_mode=once),
                  pl.BlockSpec((None, d, tf), lambda i, j: (layer, 0, j)),
                  pl.BlockSpec((None, tf, d), lambda i, j: (layer, j, 0)),
                  pl.BlockSpec((None, 1, d), lambda i, j: (layer, 0, 0))],
        out_specs=out_specs,
        out_shape=out_shape,
        compiler_params=_cparams(("parallel", "arbitrary")),
        name="mlp",
    )(h, x, w1_all, w2_all, gn)


def _head_rows(w, off, heads, width, padded):
    pieces = []
    for h in range(heads):
        pieces.append(w[..., off + h * width:off + (h + 1) * width, :])
        if padded > width:
            pieces.append(jnp.zeros(w.shape[:-2] + (padded - width, w.shape[-1]), w.dtype))
    return pieces


def _swa_perm_cols(w):
    return jnp.concatenate([w[..., HEAD_DIM * h:HEAD_DIM * (h + 1)] for h in SWA_Q_PERM], axis=-1)


def kernel(x, norm_mix, w_in, swa_sinks, rel_bias, gla_gate_w, gla_gate_b, gla_norm,
           swa_out_norm, sb_out_norm, w_out, norm_mlp, w_mlp_in, w_mlp_out, norm_final):
    depth = w_in.shape[0]
    xs = x[0]
    gla_w = GLA_HEADS * GLA_DV
    w_in_p = prep_w_in(jnp.swapaxes(w_in, 1, 2))
    wa = jnp.concatenate([w_out[:, HEAD_DIM * h:HEAD_DIM * (h + 1)] for h in SWA_Q_PERM],
                         axis=1).astype(BF16)
    wb = jnp.concatenate(_head_rows(w_out, SWA_WIDTH, GLA_HEADS, GLA_DV, GLA_DV_PAD),
                         axis=1).astype(BF16)
    wc = w_out[:, SWA_WIDTH + gla_w:].astype(BF16)
    ga = _swa_perm_cols(swa_out_norm)[:, None, :]
    gc = sb_out_norm[:, None, :]
    g_mlp = norm_mlp[:, None, :]
    g_next = jnp.concatenate([norm_mix[1:], norm_final[None]], axis=0)[:, None, :]
    sinks_p = jnp.concatenate([swa_sinks[:, h:h + 1] for h in SWA_Q_PERM], axis=1)
    bias_tab = swa_bias_table(rel_bias)
    gw = jnp.stack([jnp.pad(gla_gate_w[:, :, GLA_DK * h:GLA_DK * (h + 1)],
                            ((0, 0), (0, GLA_LR_PAD - GLA_LOWRANK), (0, GLA_DK_PAD - GLA_DK)))
                    for h in range(GLA_HEADS)], axis=1).astype(BF16)
    gb = jnp.stack([jnp.pad(gla_gate_b[:, GLA_DK * h:GLA_DK * (h + 1)],
                            ((0, 0), (0, GLA_DK_PAD - GLA_DK)))
                    for h in range(GLA_HEADS)], axis=1)[:, :, None, :]
    gn = jnp.pad(gla_norm, ((0, 0), (0, GLA_DV_PAD - GLA_DV)))[:, None, :]

    h = None
    for l in range(depth):
        if l == 0:
            proj = in_projection(xs, w_in_p, l, gain=norm_mix[0])
        else:
            proj = in_projection(h, w_in_p, l)
        a = swa_attention(proj, sinks_p, bias_tab, l)
        b = gla_attention(proj, gw, gb, gn, l)
        c = sb_attention(proj)
        xs, hm = out_projection(a, b, c, wa, wb, wc, ga, gc, xs, g_mlp, l)
        if l + 1 < depth:
            xs, h = mlp_block(hm, xs, w_mlp_in, w_mlp_out, l, g_next, final=False)
        else:
            (out,) = mlp_block(hm, xs, w_mlp_in, w_mlp_out, l, g_next, final=True)
    return out[None]
```

```python
import functools
import math

import numpy as np
import jax
import jax.numpy as jnp
from jax import lax
from jax.experimental import pallas as pl
from jax.experimental.pallas import tpu as pltpu

F32 = jnp.float32
BF16 = jnp.bfloat16

HEAD_DIM = 64
SWA_HEADS = 12
SWA_KV_HEADS = 4
SWA_WIDTH = SWA_HEADS * HEAD_DIM
SWA_KV_WIDTH = SWA_KV_HEADS * HEAD_DIM
SWA_BLOCK = 128
WINDOW = 128
REL_BUCKETS = 32
REL_MAX_DIST = 128
GLA_HEADS = 4
GLA_DV = 192
GLA_DK = 96
GLA_DK_PAD = 128
GLA_DV_PAD = 256
GLA_LOWRANK = 16
GLA_LR_PAD = 128
GLA_CHUNK = 64
GATE_NORMALIZER = 16.0
GATE_LOG_MIN = -1.0
SB_HEADS = 8
SB_WIDTH = SB_HEADS * HEAD_DIM
RMS_EPS = 1e-6
LOG2E = 1.4426950408889634
NEG_INF = -1e30

LANES = 128
VMEM_LIMIT = 56 * 1024 * 1024

OFF_BV = 0
OFF_BR = 1024
OFF_BQ = 2048
OFF_BK = 2560
OFF_CQ = 3072
OFF_CK = 3584
OFF_CV = 4096
OFF_AQ = 4608
OFF_AK = 5376
OFF_AV = 5632
OFF_BLR = 5888
PROJ_PAD_WIDTH = 6144

SWA_Q_PERM = (0, 3, 1, 4, 2, 5, 6, 9, 7, 10, 8, 11)

_O_AQ, _O_AK, _O_AV = 0, 768, 1024
_O_BQ, _O_BK, _O_BV, _O_BR, _O_BLR = 1280, 1664, 2048, 2816, 3584
_O_CQ, _O_CK, _O_CV = 3600, 4112, 4624


def _cparams(sem):
    return pltpu.CompilerParams(dimension_semantics=sem, vmem_limit_bytes=VMEM_LIMIT)


def _rms(x, gain):
    ms = jnp.mean(x * x, axis=-1, keepdims=True)
    return x * lax.rsqrt(ms + RMS_EPS) * gain


def _split_bf16(x):
    hi = x.astype(BF16)
    lo = (x - hi.astype(F32)).astype(BF16)
    return hi, lo


_NT_DIMS = (((1,), (1,)), ((), ()))


def _matmul_kernel(h_ref, w_ref, o_ref):
    o_ref[...] = lax.dot_general(h_ref[...], w_ref[...], _NT_DIMS,
                                 preferred_element_type=F32).astype(o_ref.dtype)


def _norm_matmul_kernel(x_ref, g_ref, w_ref, o_ref):
    h = _rms(x_ref[...], g_ref[...]).astype(BF16)
    o_ref[...] = lax.dot_general(h, w_ref[...], _NT_DIMS,
                                 preferred_element_type=F32).astype(o_ref.dtype)


def in_projection(h, w_all, layer, gain=None, tm=1024, tn=1024):
    s, d = h.shape
    n = w_all.shape[1]
    tm = min(tm, s)
    w_spec = pl.BlockSpec((None, tn, d), lambda i, j: (layer, j, 0))
    h_spec = pl.BlockSpec((tm, d), lambda i, j: (i, 0))
    common = dict(
        grid=(s // tm, n // tn),
        out_specs=pl.BlockSpec((tm, tn), lambda i, j: (i, j)),
        out_shape=jax.ShapeDtypeStruct((s, n), BF16),
        compiler_params=_cparams(("parallel", "arbitrary")),
        name="in_proj",
    )
    if gain is None:
        return pl.pallas_call(_matmul_kernel, in_specs=[h_spec, w_spec], **common)(h, w_all)
    return pl.pallas_call(
        _norm_matmul_kernel,
        in_specs=[h_spec, pl.BlockSpec((1, d), lambda i, j: (0, 0)), w_spec],
        **common,
    )(h, gain.reshape(1, d), w_all)


def _w_in_moves():
    scale = HEAD_DIM ** -0.5
    moves = []
    for h in range(GLA_HEADS):
        moves.append((_O_BV + GLA_DV * h, OFF_BV + GLA_DV_PAD * h, GLA_DV, 1.0))
        moves.append((_O_BR + GLA_DV * h, OFF_BR + GLA_DV_PAD * h, GLA_DV, 1.0))
        moves.append((_O_BQ + GLA_DK * h, OFF_BQ + GLA_DK_PAD * h, GLA_DK, 1.0))
        moves.append((_O_BK + GLA_DK * h, OFF_BK + GLA_DK_PAD * h, GLA_DK, 1.0))
    moves += [(_O_CQ, OFF_CQ, SB_WIDTH, scale), (_O_CK, OFF_CK, SB_WIDTH, 1.0),
              (_O_CV, OFF_CV, SB_WIDTH, 1.0)]
    for j, h in enumerate(SWA_Q_PERM):
        moves.append((_O_AQ + HEAD_DIM * h, OFF_AQ + HEAD_DIM * j, HEAD_DIM, scale))
    moves += [(_O_AK, OFF_AK, SWA_KV_WIDTH, 1.0), (_O_AV, OFF_AV, SWA_KV_WIDTH, 1.0),
              (_O_BLR, OFF_BLR, GLA_LOWRANK, 1.0)]
    return moves


def _w_in_prep_kernel(w_ref, o_ref):
    o_ref[...] = jnp.zeros_like(o_ref)
    for src, dst, width, scale in _w_in_moves():
        piece = w_ref[src:src + width, :]
        if scale != 1.0:
            piece = piece * scale
        o_ref[dst:dst + width, :] = piece.astype(o_ref.dtype)


def prep_w_in(w_in_t, tc=256):
    layers, n, d = w_in_t.shape
    return pl.pallas_call(
        _w_in_prep_kernel,
        grid=(layers, d // tc),
        in_specs=[pl.BlockSpec((None, n, tc), lambda l, i: (l, 0, i))],
        out_specs=pl.BlockSpec((None, PROJ_PAD_WIDTH, tc), lambda l, i: (l, 0, i)),
        out_shape=jax.ShapeDtypeStruct((layers, PROJ_PAD_WIDTH, d), BF16),
        compiler_params=_cparams(("parallel", "parallel")),
        name="w_in_prep",
    )(w_in_t)


def _swa_kernel(sink_ref, q_ref, kp_ref, kc_ref, vp_ref, vc_ref, bias_ref, o_ref, *, layer):
    blk = SWA_BLOCK
    nb = SWA_HEADS // 2
    k = jnp.concatenate([kp_ref[...], kc_ref[...]], axis=0)
    v = jnp.concatenate([vp_ref[...], vc_ref[...]], axis=0)
    lane = lax.broadcasted_iota(jnp.int32, (blk, LANES), 1)
    low = lane < HEAD_DIM
    top = lax.broadcasted_iota(jnp.int32, (2 * blk, 1), 0) < blk
    ones = jnp.ones((2 * blk, LANES), BF16)
    nt = (((1,), (1,)), ((), ()))

    scores = []
    for b in range(nb):
        qb = q_ref[:, LANES * b:LANES * (b + 1)]
        zero = jnp.zeros_like(qb)
        lhs = jnp.concatenate([jnp.where(low, qb, zero), jnp.where(low, zero, qb)], axis=0)
        kb = k[:, LANES * (b // 3):LANES * (b // 3 + 1)]
        scores.append(lax.dot_general(lhs, kb, nt, preferred_element_type=F32) + bias_ref[b])
    probs, sink_term = [], []
    for b in range(nb):
        sink = jnp.where(top, sink_ref[layer, 2 * b], sink_ref[layer, 2 * b + 1])
        m = jnp.maximum(jnp.max(scores[b], axis=1, keepdims=True), sink)
        probs.append(jnp.exp(scores[b] - m).astype(BF16))
        sink_term.append(jnp.exp(sink - m))
    for b in range(nb):
        vb = v[:, LANES * (b // 3):LANES * (b // 3 + 1)]
        pv = jnp.dot(probs[b], jnp.concatenate([vb, ones], axis=1),
                     preferred_element_type=F32)
        o = pv[:, :LANES] / (pv[:, LANES:] + sink_term[b])
        o_ref[:, LANES * b:LANES * (b + 1)] = jnp.where(low, o[:blk], o[blk:]).astype(o_ref.dtype)


def swa_attention(proj, sinks_perm, bias_tab, layer):
    s = proj.shape[0]
    blk = SWA_BLOCK
    qi = OFF_AQ // SWA_WIDTH
    ki = OFF_AK // SWA_KV_WIDTH
    vi = OFF_AV // SWA_KV_WIDTH
    prev = lambda i: jnp.maximum(i - 1, 0)
    return pl.pallas_call(
        functools.partial(_swa_kernel, layer=layer),
        grid=(s // blk,),
        in_specs=[pl.BlockSpec(memory_space=pltpu.SMEM),
                  pl.BlockSpec((blk, SWA_WIDTH), lambda i: (i, qi)),
                  pl.BlockSpec((blk, SWA_KV_WIDTH), lambda i: (prev(i), ki)),
                  pl.BlockSpec((blk, SWA_KV_WIDTH), lambda i: (i, ki)),
                  pl.BlockSpec((blk, SWA_KV_WIDTH), lambda i: (prev(i), vi)),
                  pl.BlockSpec((blk, SWA_KV_WIDTH), lambda i: (i, vi)),
                  pl.BlockSpec((None, SWA_HEADS // 2, 2 * blk, 2 * blk),
                               lambda i: (jnp.minimum(i, 1), 0, 0, 0))],
        out_specs=pl.BlockSpec((blk, SWA_WIDTH), lambda i: (i, 0)),
        out_shape=jax.ShapeDtypeStruct((s, SWA_WIDTH), BF16),
        compiler_params=_cparams(("parallel",)),
        name="swa",
    )(sinks_perm, proj, proj, proj, proj, proj, bias_tab)


def _t5_causal_bucket(dist):
    max_exact = REL_BUCKETS // 2
    is_small = dist < max_exact
    ratio = (jnp.log(jnp.maximum(dist, 1).astype(F32) / max_exact)
             / math.log(REL_MAX_DIST / max_exact))
    large = max_exact + (ratio * (REL_BUCKETS - max_exact)).astype(jnp.int32)
    large = jnp.minimum(large, REL_BUCKETS - 1)
    return jnp.where(is_small, dist, large)


def swa_bias_table(rel_bias):
    blk = SWA_BLOCK
    qpos = jnp.arange(blk) + blk
    kpos = jnp.arange(2 * blk)
    dist = qpos[:, None] - kpos[None, :]
    in_window = (dist >= 0) & (dist < WINDOW)
    bucket = _t5_causal_bucket(jnp.maximum(dist, 0))
    table = rel_bias.astype(F32)
    table = jnp.stack([table[:, h] for h in SWA_Q_PERM])
    bias = jnp.full((SWA_HEADS, blk, 2 * blk), NEG_INF, F32)
    for bkt in range(REL_BUCKETS):
        hit = (in_window & (bucket == bkt))[None]
        bias = jnp.where(hit, table[:, bkt][:, None, None], bias)
    bias = bias.reshape(SWA_HEADS // 2, 2 * blk, 2 * blk)
    first = jnp.where((kpos < blk)[None, None, :], NEG_INF, bias)
    return jnp.stack([first, bias])


def _gla_kernel(q_ref, k_ref, v_ref, r_ref, lr_ref, gw_ref, gb_ref, gn_ref, lm_ref, o_ref,
                st_ref, *, rows, hps):
    t = pl.program_id(1)
    ch = GLA_CHUNK
    nch = rows // ch
    heads = range(hps)

    @pl.when(t == 0)
    def _():
        st_ref[...] = jnp.zeros_like(st_ref)

    nt = (((1,), (1,)), ((), ()))
    tn = (((0,), (0,)), ((), ()))
    rowi = lax.broadcasted_iota(jnp.int32, (rows, rows), 0)
    coli = lax.broadcasted_iota(jnp.int32, (rows, rows), 1)
    same_chunk_causal = jnp.logical_and(coli <= rowi, (rowi // ch) == (coli // ch))
    lm = lm_ref[...]
    lr = lr_ref[...]
    gn = gn_ref[...]

    gp = [jnp.dot(lr, gw_ref[h], preferred_element_type=F32) + gb_ref[h] for h in heads]
    lg = []
    for h in heads:
        log_sig = jnp.minimum(gp[h], 0.0) - jnp.log(1.0 + jnp.exp(-jnp.abs(gp[h])))
        lg.append(jnp.maximum(log_sig * (1.0 / GATE_NORMALIZER), GATE_LOG_MIN))
    bb = []
    for h in heads:
        hi, lo = _split_bf16(lg[h])
        bb.append(jnp.dot(lm, hi, preferred_element_type=F32)
                  + jnp.dot(lm, lo, preferred_element_type=F32))
    q_dec, k_inv, k_end, dec = [], [], [], []
    for h in heads:
        b, b_last = bb[h][:rows], bb[h][rows:]
        q = q_ref[:, GLA_DK_PAD * h:GLA_DK_PAD * (h + 1)].astype(F32) * (GLA_DK ** -0.5)
        kk = k_ref[:, GLA_DK_PAD * h:GLA_DK_PAD * (h + 1)].astype(F32)
        q_dec.append((q * jnp.exp(b)).astype(BF16))
        k_inv.append((kk * jnp.exp(-b)).astype(BF16))
        k_end.append((kk * jnp.exp(b_last - b)).astype(BF16))
        dec.append(jnp.exp(b_last))
    v = [v_ref[:, GLA_DV_PAD * h:GLA_DV_PAD * (h + 1)] for h in heads]
    sc = []
    for h in heads:
        s_h = lax.dot_general(q_dec[h], k_inv[h], nt, preferred_element_type=F32)
        sc.append(jnp.where(same_chunk_causal, s_h, 0.0).astype(BF16))
    o_intra = [jnp.dot(sc[h], v[h], preferred_element_type=F32) for h in heads]
    d_st = [[lax.dot_general(v[h][c * ch:(c + 1) * ch], k_end[h][c * ch:(c + 1) * ch], tn,
                             preferred_element_type=F32) for c in range(nch)] for h in heads]
    st_in = []
    for h in heads:
        st = st_ref[h]
        states = []
        for c in range(nch):
            states.append(st.astype(BF16))
            st = st * dec[h][c * ch:c * ch + 1, :] + d_st[h][c]
        st_ref[h] = st
        st_in.append(states)
    for h in heads:
        o_inter = [lax.dot_general(q_dec[h][c * ch:(c + 1) * ch], st_in[h][c], nt,
                                   preferred_element_type=F32) for c in range(nch)]
        o = o_intra[h] + jnp.concatenate(o_inter, axis=0)
        ms = jnp.sum(o * o, axis=1, keepdims=True) * (1.0 / GLA_DV)
        y = o * lax.rsqrt(ms + RMS_EPS) * gn
        rr = r_ref[:, GLA_DV_PAD * h:GLA_DV_PAD * (h + 1)].astype(F32)
        o_ref[:, GLA_DV_PAD * h:GLA_DV_PAD * (h + 1)] = (
            y * (rr / (1.0 + jnp.exp(-rr)))).astype(o_ref.dtype)


def gla_attention(proj, gw_pad, gb_pad, gn_pad, layer, rows=256, hps=4):
    s = proj.shape[0]
    rows = min(rows, s)
    ch = GLA_CHUNK
    dkw, dvw = hps * GLA_DK_PAD, hps * GLA_DV_PAD
    qi, ki = OFF_BQ // dkw, OFF_BK // dkw
    vi, ri = OFF_BV // dvw, OFF_BR // dvw
    li = OFF_BLR // GLA_LR_PAD
    idx = np.arange(rows)
    same = (idx[:, None] // ch) == (idx[None, :] // ch)
    prefix = same & (idx[None, :] <= idx[:, None])
    lm = jnp.asarray(np.concatenate([prefix, same], axis=0).astype(np.float32), dtype=BF16)
    return pl.pallas_call(
        functools.partial(_gla_kernel, rows=rows, hps=hps),
        grid=(GLA_HEADS // hps, s // rows),
        in_specs=[pl.BlockSpec((rows, dkw), lambda g, t: (t, qi + g)),
                  pl.BlockSpec((rows, dkw), lambda g, t: (t, ki + g)),
                  pl.BlockSpec((rows, dvw), lambda g, t: (t, vi + g)),
                  pl.BlockSpec((rows, dvw), lambda g, t: (t, ri + g)),
                  pl.BlockSpec((rows, GLA_LR_PAD), lambda g, t: (t, li)),
                  pl.BlockSpec((None, hps, GLA_LR_PAD, GLA_DK_PAD),
                               lambda g, t: (layer, g, 0, 0)),
                  pl.BlockSpec((None, hps, 1, GLA_DK_PAD), lambda g, t: (layer, g, 0, 0)),
                  pl.BlockSpec((None, 1, GLA_DV_PAD), lambda g, t: (layer, 0, 0)),
                  pl.BlockSpec((2 * rows, rows), lambda g, t: (0, 0))],
        out_specs=pl.BlockSpec((rows, dvw), lambda g, t: (t, g)),
        out_shape=jax.ShapeDtypeStruct((s, GLA_HEADS * GLA_DV_PAD), BF16),
        scratch_shapes=[pltpu.VMEM((hps, GLA_DV_PAD, GLA_DK_PAD), F32)],
        compiler_params=_cparams(("parallel", "arbitrary")),
        name="gla",
    )(proj, proj, proj, proj, proj, gw_pad, gb_pad, gn_pad, lm)


def _sb_kernel(q_ref, k_ref, v_ref, u_ref, o_ref, acc_ref, car_ref, qm_ref, vm_ref,
               z0_ref, z1_ref, z2_ref, z3_ref, nl0_ref, nl1_ref, w0_ref, w1_ref,
               *, bq, nblk, unroll):
    lane = lax.broadcasted_iota(jnp.int32, (bq, LANES), 1)
    low = lane < HEAD_DIM
    nt = (((1,), (1,)), ((), ()))
    z_ref = (z0_ref, z1_ref, z2_ref, z3_ref)
    nl_ref, w_ref = (nl0_ref, nl1_ref), (w0_ref, w1_ref)

    def rows(blk):
        if isinstance(blk, int):
            return pl.ds(blk * bq, bq)
        return pl.ds(pl.multiple_of(blk * bq, bq), bq)

    def strict_mask():
        rowi = lax.broadcasted_iota(jnp.int32, (2 * bq, bq), 0)
        coli = lax.broadcasted_iota(jnp.int32, (2 * bq, bq), 1)
        return coli < (rowi & (bq - 1))

    def s1a(item, tick4):
        i, c = item
        z_ref[tick4][...] = lax.dot_general(qm_ref[i], k_ref[rows(c), :], nt,
                                            preferred_element_type=F32)

    def s1b(tick4, masked):
        z = z_ref[tick4][...]
        nl = jnp.maximum(z, 0.0) + jnp.log(1.0 + jnp.exp2(jnp.abs(z) * (-LOG2E)))
        if masked:
            nl = jnp.where(strict_mask(), nl, 0.0)
        nl_ref[tick4 % 2][...] = nl.astype(BF16)

    def s2(item, tick4, masked, first):
        i, _ = item
        cs = jnp.dot(nl_ref[tick4 % 2][...], u_ref[...], preferred_element_type=F32)
        tot = jnp.broadcast_to(cs[:, 0:1], (2 * bq, LANES))
        if first:
            arg = z_ref[tick4][...] - cs
            car_ref[i] = tot
        else:
            car = car_ref[i]
            arg = z_ref[tick4][...] - cs - jnp.concatenate([car, car], axis=1)
            car_ref[i] = car + tot
        w = jnp.exp(arg)
        if masked:
            w = jnp.where(strict_mask(), w, 0.0)
        w = w.astype(BF16)
        w_ref[tick4 % 2][:, :bq] = w[:bq]
        w_ref[tick4 % 2][:, bq:] = w[bq:]

    def s3(item, tick4, first):
        i, c = item
        pv = jnp.dot(w_ref[tick4 % 2][...], vm_ref[c], preferred_element_type=F32)
        if first:
            acc_ref[rows(i), :] = pv
        else:
            acc_ref[rows(i), :] += pv

    def run_pipeline(items, step, masked, first):
        n = len(items)

        def tick(t4, its, do):
            if do[1]:
                s1b((t4 - 1) % 4, masked)
            if do[0]:
                s1a(its[0], t4)
            if do[2]:
                s2(its[2], (t4 - 2) % 4, masked, first)
            if do[3]:
                s3(its[3], (t4 - 3) % 4, first)

        def static_tick(t):
            do = [0 <= t - j < n for j in range(4)]
            its = [items[t - j] if do[j] else None for j in range(4)]
            tick(t % 4, its, do)

        lo_t, hi_t = 3, n - 1
        groups = max(hi_t - lo_t + 1, 0) // unroll
        if groups < 2:
            groups = 0
        for t in range(min(lo_t, n + 3)):
            static_tick(t)
        if groups:
            def body(_, carry):
                its = [(carry[2 * j], carry[2 * j + 1]) for j in range(4)]
                for k in range(unroll):
                    tick((lo_t + k) % 4, its, [True] * 4)
                    its = [step(*its[0])] + its[:3]
                return tuple(x for it in its for x in it)

            init = tuple(jnp.int32(x) for j in range(4) for x in items[lo_t - j])
            lax.fori_loop(0, groups, body, init)
        for t in range(lo_t + unroll * groups, n + 3):
            static_tick(t)

    for blk in range(nblk):
        r = rows(blk)
        qb, vb = q_ref[r, :], v_ref[r, :]
        zero = jnp.zeros_like(qb)
        qm_ref[blk, :bq] = jnp.where(low, qb, zero)
        qm_ref[blk, bq:] = jnp.where(low, zero, qb)
        vm_ref[blk, :bq] = jnp.where(low, vb, zero)
        vm_ref[blk, bq:] = jnp.where(low, zero, vb)

    diag = [(i, i) for i in range(nblk)]
    run_pipeline(diag, lambda i, c: (i + 1, c + 1), masked=True, first=True)

    off = [(i, c) for i in range(1, nblk) for c in range(i - 1, -1, -1)]

    def off_step(i, c):
        wrap = c == 0
        i2 = jnp.where(wrap, i + 1, i)
        return i2, jnp.where(wrap, i2 - 1, c - 1)

    if off:
        run_pipeline(off, off_step, masked=False, first=False)

    o_ref[...] = acc_ref[...].astype(o_ref.dtype)


def sb_attention(proj, bq=256, unroll=8):
    s = proj.shape[0]
    bq = min(bq, s)
    nblk = s // bq
    assert bq & (bq - 1) == 0 and unroll % 4 == 0
    qi = OFF_CQ // LANES
    ki = OFF_CK // LANES
    vi = OFF_CV // LANES
    u = (np.arange(bq)[:, None] >= np.arange(bq)[None, :]).astype(np.float32)
    u = jnp.asarray(u, dtype=BF16)
    slot = lambda shape, dtype: pltpu.VMEM(shape, dtype)
    return pl.pallas_call(
        functools.partial(_sb_kernel, bq=bq, nblk=nblk, unroll=unroll),
        grid=(SB_HEADS // 2,),
        in_specs=[pl.BlockSpec((s, LANES), lambda p: (0, qi + p)),
                  pl.BlockSpec((s, LANES), lambda p: (0, ki + p)),
                  pl.BlockSpec((s, LANES), lambda p: (0, vi + p)),
                  pl.BlockSpec((bq, bq), lambda p: (0, 0))],
        out_specs=pl.BlockSpec((s, LANES), lambda p: (0, p)),
        out_shape=jax.ShapeDtypeStruct((s, SB_WIDTH), BF16),
        scratch_shapes=[pltpu.VMEM((s, LANES), F32),
                        pltpu.VMEM((nblk, 2 * bq, LANES), F32),
                        pltpu.VMEM((nblk, 2 * bq, LANES), BF16),
                        pltpu.VMEM((nblk, 2 * bq, LANES), BF16)]
                       + [slot((2 * bq, bq), F32)] * 4
                       + [slot((2 * bq, bq), BF16)] * 2
                       + [slot((bq, 2 * bq), BF16)] * 2,
        compiler_params=_cparams(("parallel",)),
        name="stickbreak",
    )(proj, proj, proj, u)


def _oproj_kernel(a_ref, b_ref, c_ref, wa_ref, wb_ref, wc_ref, ga_ref, gc_ref,
                  x_ref, gn_ref, xo_ref, ho_ref):
    an = _rms(a_ref[...].astype(F32), ga_ref[...]).astype(BF16)
    cn = _rms(c_ref[...].astype(F32), gc_ref[...]).astype(BF16)
    y = (jnp.dot(an, wa_ref[...], preferred_element_type=F32)
         + jnp.dot(b_ref[...], wb_ref[...], preferred_element_type=F32)
         + jnp.dot(cn, wc_ref[...], preferred_element_type=F32))
    xn = x_ref[...] + y
    xo_ref[...] = xn
    ho_ref[...] = _rms(xn, gn_ref[...]).astype(ho_ref.dtype)


def out_projection(a, b, c, wa, wb, wc, ga, gc, x, gn, layer, tm=512):
    s, d = x.shape
    tm = min(tm, s)
    row = lambda i: (i, 0)
    fixed = lambda i: (layer, 0, 0)
    per_layer = lambda arr: pl.BlockSpec((None,) + arr.shape[1:], fixed)
    return pl.pallas_call(
        _oproj_kernel,
        grid=(s // tm,),
        in_specs=[pl.BlockSpec((tm, a.shape[1]), row),
                  pl.BlockSpec((tm, b.shape[1]), row),
                  pl.BlockSpec((tm, c.shape[1]), row),
                  per_layer(wa), per_layer(wb), per_layer(wc), per_layer(ga), per_layer(gc),
                  pl.BlockSpec((tm, d), row),
                  per_layer(gn)],
        out_specs=[pl.BlockSpec((tm, d), row), pl.BlockSpec((tm, d), row)],
        out_shape=[jax.ShapeDtypeStruct((s, d), F32), jax.ShapeDtypeStruct((s, d), BF16)],
        compiler_params=_cparams(("parallel",)),
        name="out_proj",
    )(a, b, c, wa, wb, wc, ga, gc, x, gn)


def _mlp_kernel(h_ref, x_ref, w1_ref, w2_ref, gn_ref, *out_refs, final):
    acc_ref = out_refs[0]
    j = pl.program_id(1)

    @pl.when(j == 0)
    def _():
        acc_ref[...] = x_ref[...]

    u = jnp.dot(h_ref[...], w1_ref[...].astype(BF16), preferred_element_type=F32)
    act = jnp.square(jnp.maximum(u, 0.0)).astype(BF16)
    acc_ref[...] += jnp.dot(act, w2_ref[...].astype(BF16), preferred_element_type=F32)

    @pl.when(j == pl.num_programs(1) - 1)
    def _():
        y = _rms(acc_ref[...], gn_ref[...])
        if final:
            acc_ref[...] = y
        else:
            out_refs[1][...] = y.astype(out_refs[1].dtype)


def mlp_block(h, x, w1_all, w2_all, layer, gn, final, tm=1024, tf=512):
    s, d = x.shape
    ff = w1_all.shape[2]
    tm = min(tm, s)
    row = lambda i, j: (i, 0)
    once = pl.Buffered(1)
    out_specs = [pl.BlockSpec((tm, d), row)]
    out_shape = [jax.ShapeDtypeStruct((s, d), F32)]
    if not final:
        out_specs.append(pl.BlockSpec((tm, d), row))
        out_shape.append(jax.ShapeDtypeStruct((s, d), BF16))
    return pl.pallas_call(
        functools.partial(_mlp_kernel, final=final),
        grid=(s // tm, ff // tf),
        in_specs=[pl.BlockSpec((tm, d), row, pipeline_mode=once),
                  pl.BlockSpec((tm, d), row, pipeline_mode=once),
                  pl.BlockSpec((None, d, tf), lambda i, j: (layer, 0, j)),
                  pl.BlockSpec((None, tf, d), lambda i, j: (layer, j, 0)),
                  pl.BlockSpec((None, 1, d), lambda i, j: (layer, 0, 0))],
        out_specs=out_specs,
        out_shape=out_shape,
        compiler_params=_cparams(("parallel", "arbitrary")),
        name="mlp",
    )(h, x, w1_all, w2_all, gn)


def _head_rows(w, off, heads, width, padded):
    pieces = []
    for h in range(heads):
        pieces.append(w[..., off + h * width:off + (h + 1) * width, :])
        if padded > width:
            pieces.append(jnp.zeros(w.shape[:-2] + (padded - width, w.shape[-1]), w.dtype))
    return pieces


def _swa_perm_cols(w):
    return jnp.concatenate([w[..., HEAD_DIM * h:HEAD_DIM * (h + 1)] for h in SWA_Q_PERM], axis=-1)


def kernel(x, norm_mix, w_in, swa_sinks, rel_bias, gla_gate_w, gla_gate_b, gla_norm,
           swa_out_norm, sb_out_norm, w_out, norm_mlp, w_mlp_in, w_mlp_out, norm_final):
    depth = w_in.shape[0]
    xs = x[0]
    gla_w = GLA_HEADS * GLA_DV
    w_in_p = prep_w_in(jnp.swapaxes(w_in, 1, 2))
    wa = jnp.concatenate([w_out[:, HEAD_DIM * h:HEAD_DIM * (h + 1)] for h in SWA_Q_PERM],
                         axis=1).astype(BF16)
    wb = jnp.concatenate(_head_rows(w_out, SWA_WIDTH, GLA_HEADS, GLA_DV, GLA_DV_PAD),
                         axis=1).astype(BF16)
    wc = w_out[:, SWA_WIDTH + gla_w:].astype(BF16)
    ga = _swa_perm_cols(swa_out_norm)[:, None, :]
    gc = sb_out_norm[:, None, :]
    g_mlp = norm_mlp[:, None, :]
    g_next = jnp.concatenate([norm_mix[1:], norm_final[None]], axis=0)[:, None, :]
    sinks_p = jnp.concatenate([swa_sinks[:, h:h + 1] for h in SWA_Q_PERM], axis=1)
    bias_tab = swa_bias_table(rel_bias)
    gw = jnp.stack([jnp.pad(gla_gate_w[:, :, GLA_DK * h:GLA_DK * (h + 1)],
                            ((0, 0), (0, GLA_LR_PAD - GLA_LOWRANK), (0, GLA_DK_PAD - GLA_DK)))
                    for h in range(GLA_HEADS)], axis=1).astype(BF16)
    gb = jnp.stack([jnp.pad(gla_gate_b[:, GLA_DK * h:GLA_DK * (h + 1)],
                            ((0, 0), (0, GLA_DK_PAD - GLA_DK)))
                    for h in range(GLA_HEADS)], axis=1)[:, :, None, :]
    gn = jnp.pad(gla_norm, ((0, 0), (0, GLA_DV_PAD - GLA_DV)))[:, None, :]

    h = None
    for l in range(depth):
        if l == 0:
            proj = in_projection(xs, w_in_p, l, gain=norm_mix[0])
        else:
            proj = in_projection(h, w_in_p, l)
        a = swa_attention(proj, sinks_p, bias_tab, l)
        b = gla_attention(proj, gw, gb, gn, l)
        c = sb_attention(proj)
        xs, hm = out_projection(a, b, c, wa, wb, wc, ga, gc, xs, g_mlp, l)
        if l + 1 < depth:
            xs, h = mlp_block(hm, xs, w_mlp_in, w_mlp_out, l, g_next, final=False)
        else:
            (out,) = mlp_block(hm, xs, w_mlp_in, w_mlp_out, l, g_next, final=True)
    return out[None]
```

```python
import functools
import math

import numpy as np
import jax
import jax.numpy as jnp
from jax import lax
from jax.experimental import pallas as pl
from jax.experimental.pallas import tpu as pltpu

F32 = jnp.float32
BF16 = jnp.bfloat16

HEAD_DIM = 64
SWA_HEADS = 12
SWA_KV_HEADS = 4
SWA_WIDTH = SWA_HEADS * HEAD_DIM
SWA_KV_WIDTH = SWA_KV_HEADS * HEAD_DIM
SWA_BLOCK = 128
WINDOW = 128
REL_BUCKETS = 32
REL_MAX_DIST = 128
GLA_HEADS = 4
GLA_DV = 192
GLA_DK = 96
GLA_DK_PAD = 128
GLA_DV_PAD = 256
GLA_LOWRANK = 16
GLA_LR_PAD = 128
GLA_CHUNK = 64
GATE_NORMALIZER = 16.0
GATE_LOG_MIN = -1.0
SB_HEADS = 8
SB_WIDTH = SB_HEADS * HEAD_DIM
RMS_EPS = 1e-6
LOG2E = 1.4426950408889634
NEG_INF = -1e30

LANES = 128
VMEM_LIMIT = 56 * 1024 * 1024

OFF_BV = 0
OFF_BR = 1024
OFF_BQ = 2048
OFF_BK = 2560
OFF_CQ = 3072
OFF_CK = 3584
OFF_CV = 4096
OFF_AQ = 4608
OFF_AK = 5376
OFF_AV = 5632
OFF_BLR = 5888
PROJ_PAD_WIDTH = 6144

SWA_Q_PERM = (0, 3, 1, 4, 2, 5, 6, 9, 7, 10, 8, 11)

_O_AQ, _O_AK, _O_AV = 0, 768, 1024
_O_BQ, _O_BK, _O_BV, _O_BR, _O_BLR = 1280, 1664, 2048, 2816, 3584
_O_CQ, _O_CK, _O_CV = 3600, 4112, 4624


def _cparams(sem):
    return pltpu.CompilerParams(dimension_semantics=sem, vmem_limit_bytes=VMEM_LIMIT)


def _rms(x, gain):
    ms = jnp.mean(x * x, axis=-1, keepdims=True)
    return x * lax.rsqrt(ms + RMS_EPS) * gain


def _split_bf16(x):
    hi = x.astype(BF16)
    lo = (x - hi.astype(F32)).astype(BF16)
    return hi, lo


_NT_DIMS = (((1,), (1,)), ((), ()))


def _matmul_kernel(h_ref, w_ref, o_ref):
    o_ref[...] = lax.dot_general(h_ref[...], w_ref[...], _NT_DIMS,
                                 preferred_element_type=F32).astype(o_ref.dtype)


def _norm_matmul_kernel(x_ref, g_ref, w_ref, o_ref):
    h = _rms(x_ref[...], g_ref[...]).astype(BF16)
    o_ref[...] = lax.dot_general(h, w_ref[...], _NT_DIMS,
                                 preferred_element_type=F32).astype(o_ref.dtype)


def in_projection(h, w_all, layer, gain=None, tm=1024, tn=1024):
    s, d = h.shape
    n = w_all.shape[1]
    tm = min(tm, s)
    w_spec = pl.BlockSpec((None, tn, d), lambda i, j: (layer, j, 0))
    h_spec = pl.BlockSpec((tm, d), lambda i, j: (i, 0))
    common = dict(
        grid=(s // tm, n // tn),
        out_specs=pl.BlockSpec((tm, tn), lambda i, j: (i, j)),
        out_shape=jax.ShapeDtypeStruct((s, n), BF16),
        compiler_params=_cparams(("parallel", "arbitrary")),
        name="in_proj",
    )
    if gain is None:
        return pl.pallas_call(_matmul_kernel, in_specs=[h_spec, w_spec], **common)(h, w_all)
    return pl.pallas_call(
        _norm_matmul_kernel,
        in_specs=[h_spec, pl.BlockSpec((1, d), lambda i, j: (0, 0)), w_spec],
        **common,
    )(h, gain.reshape(1, d), w_all)


def _w_in_moves():
    scale = HEAD_DIM ** -0.5
    moves = []
    for h in range(GLA_HEADS):
        moves.append((_O_BV + GLA_DV * h, OFF_BV + GLA_DV_PAD * h, GLA_DV, 1.0))
        moves.append((_O_BR + GLA_DV * h, OFF_BR + GLA_DV_PAD * h, GLA_DV, 1.0))
        moves.append((_O_BQ + GLA_DK * h, OFF_BQ + GLA_DK_PAD * h, GLA_DK, 1.0))
        moves.append((_O_BK + GLA_DK * h, OFF_BK + GLA_DK_PAD * h, GLA_DK, 1.0))
    moves += [(_O_CQ, OFF_CQ, SB_WIDTH, scale), (_O_CK, OFF_CK, SB_WIDTH, 1.0),
              (_O_CV, OFF_CV, SB_WIDTH, 1.0)]
    for j, h in enumerate(SWA_Q_PERM):
        moves.append((_O_AQ + HEAD_DIM * h, OFF_AQ + HEAD_DIM * j, HEAD_DIM, scale))
    moves += [(_O_AK, OFF_AK, SWA_KV_WIDTH, 1.0), (_O_AV, OFF_AV, SWA_KV_WIDTH, 1.0),
              (_O_BLR, OFF_BLR, GLA_LOWRANK, 1.0)]
    return moves


def _w_in_prep_kernel(w_ref, o_ref):
    o_ref[...] = jnp.zeros_like(o_ref)
    for src, dst, width, scale in _w_in_moves():
        piece = w_ref[src:src + width, :]
        if scale != 1.0:
            piece = piece * scale
        o_ref[dst:dst + width, :] = piece.astype(o_ref.dtype)


def prep_w_in(w_in_t, tc=256):
    layers, n, d = w_in_t.shape
    return pl.pallas_call(
        _w_in_prep_kernel,
        grid=(layers, d // tc),
        in_specs=[pl.BlockSpec((None, n, tc), lambda l, i: (l, 0, i))],
        out_specs=pl.BlockSpec((None, PROJ_PAD_WIDTH, tc), lambda l, i: (l, 0, i)),
        out_shape=jax.ShapeDtypeStruct((layers, PROJ_PAD_WIDTH, d), BF16),
        compiler_params=_cparams(("parallel", "parallel")),
        name="w_in_prep",
    )(w_in_t)


def _swa_kernel(sink_ref, q_ref, kp_ref, kc_ref, vp_ref, vc_ref, bias_ref, o_ref, *, layer):
    blk = SWA_BLOCK
    nb = SWA_HEADS // 2
    k = jnp.concatenate([kp_ref[...], kc_ref[...]], axis=0)
    v = jnp.concatenate([vp_ref[...], vc_ref[...]], axis=0)
    lane = lax.broadcasted_iota(jnp.int32, (blk, LANES), 1)
    low = lane < HEAD_DIM
    top = lax.broadcasted_iota(jnp.int32, (2 * blk, 1), 0) < blk
    ones = jnp.ones((2 * blk, LANES), BF16)
    nt = (((1,), (1,)), ((), ()))

    scores = []
    for b in range(nb):
        qb = q_ref[:, LANES * b:LANES * (b + 1)]
        zero = jnp.zeros_like(qb)
        lhs = jnp.concatenate([jnp.where(low, qb, zero), jnp.where(low, zero, qb)], axis=0)
        kb = k[:, LANES * (b // 3):LANES * (b // 3 + 1)]
        scores.append(lax.dot_general(lhs, kb, nt, preferred_element_type=F32) + bias_ref[b])
    probs, sink_term = [], []
    for b in range(nb):
        sink = jnp.where(top, sink_ref[layer, 2 * b], sink_ref[layer, 2 * b + 1])
        m = jnp.maximum(jnp.max(scores[b], axis=1, keepdims=True), sink)
        probs.append(jnp.exp(scores[b] - m).astype(BF16))
        sink_term.append(jnp.exp(sink - m))
    for b in range(nb):
        vb = v[:, LANES * (b // 3):LANES * (b // 3 + 1)]
        pv = jnp.dot(probs[b], jnp.concatenate([vb, ones], axis=1),
                     preferred_element_type=F32)
        o = pv[:, :LANES] / (pv[:, LANES:] + sink_term[b])
        o_ref[:, LANES * b:LANES * (b + 1)] = jnp.where(low, o[:blk], o[blk:]).astype(o_ref.dtype)


def swa_attention(proj, sinks_perm, bias_tab, layer):
    s = proj.shape[0]
    blk = SWA_BLOCK
    qi = OFF_AQ // SWA_WIDTH
    ki = OFF_AK // SWA_KV_WIDTH
    vi = OFF_AV // SWA_KV_WIDTH
    prev = lambda i: jnp.maximum(i - 1, 0)
    return pl.pallas_call(
        functools.partial(_swa_kernel, layer=layer),
        grid=(s // blk,),
        in_specs=[pl.BlockSpec(memory_space=pltpu.SMEM),
                  pl.BlockSpec((blk, SWA_WIDTH), lambda i: (i, qi)),
                  pl.BlockSpec((blk, SWA_KV_WIDTH), lambda i: (prev(i), ki)),
                  pl.BlockSpec((blk, SWA_KV_WIDTH), lambda i: (i, ki)),
                  pl.BlockSpec((blk, SWA_KV_WIDTH), lambda i: (prev(i), vi)),
                  pl.BlockSpec((blk, SWA_KV_WIDTH), lambda i: (i, vi)),
                  pl.BlockSpec((None, SWA_HEADS // 2, 2 * blk, 2 * blk),
                               lambda i: (jnp.minimum(i, 1), 0, 0, 0))],
        out_specs=pl.BlockSpec((blk, SWA_WIDTH), lambda i: (i, 0)),
        out_shape=jax.ShapeDtypeStruct((s, SWA_WIDTH), BF16),
        compiler_params=_cparams(("parallel",)),
        name="swa",
    )(sinks_perm, proj, proj, proj, proj, proj, bias_tab)


def _t5_causal_bucket(dist):
    max_exact = REL_BUCKETS // 2
    is_small = dist < max_exact
    ratio = (jnp.log(jnp.maximum(dist, 1).astype(F32) / max_exact)
             / math.log(REL_MAX_DIST / max_exact))
    large = max_exact + (ratio * (REL_BUCKETS - max_exact)).astype(jnp.int32)
    large = jnp.minimum(large, REL_BUCKETS - 1)
    return jnp.where(is_small, dist, large)


def swa_bias_table(rel_bias):
    blk = SWA_BLOCK
    qpos = jnp.arange(blk) + blk
    kpos = jnp.arange(2 * blk)
    dist = qpos[:, None] - kpos[None, :]
    in_window = (dist >= 0) & (dist < WINDOW)
    bucket = _t5_causal_bucket(jnp.maximum(dist, 0))
    table = rel_bias.astype(F32)
    table = jnp.stack([table[:, h] for h in SWA_Q_PERM])
    bias = jnp.full((SWA_HEADS, blk, 2 * blk), NEG_INF, F32)
    for bkt in range(REL_BUCKETS):
        hit = (in_window & (bucket == bkt))[None]
        bias = jnp.where(hit, table[:, bkt][:, None, None], bias)
    bias = bias.reshape(SWA_HEADS // 2, 2 * blk, 2 * blk)
    first = jnp.where((kpos < blk)[None, None, :], NEG_INF, bias)
    return jnp.stack([first, bias])


def _gla_kernel(q_ref, k_ref, v_ref, r_ref, lr_ref, gw_ref, gb_ref, gn_ref, lm_ref, o_ref,
                st_ref, *, rows, hps):
    t = pl.program_id(1)
    ch = GLA_CHUNK
    nch = rows // ch
    heads = range(hps)

    @pl.when(t == 0)
    def _():
        st_ref[...] = jnp.zeros_like(st_ref)

    nt = (((1,), (1,)), ((), ()))
    tn = (((0,), (0,)), ((), ()))
    rowi = lax.broadcasted_iota(jnp.int32, (rows, rows), 0)
    coli = lax.broadcasted_iota(jnp.int32, (rows, rows), 1)
    same_chunk_causal = jnp.logical_and(coli <= rowi, (rowi // ch) == (coli // ch))
    lm = lm_ref[...]
    lr = lr_ref[...]
    gn = gn_ref[...]

    gp = [jnp.dot(lr, gw_ref[h], preferred_element_type=F32) + gb_ref[h] for h in heads]
    lg = []
    for h in heads:
        log_sig = jnp.minimum(gp[h], 0.0) - jnp.log(1.0 + jnp.exp(-jnp.abs(gp[h])))
        lg.append(jnp.maximum(log_sig * (1.0 / GATE_NORMALIZER), GATE_LOG_MIN))
    bb = []
    for h in heads:
        hi, lo = _split_bf16(lg[h])
        bb.append(jnp.dot(lm, hi, preferred_element_type=F32)
                  + jnp.dot(lm, lo, preferred_element_type=F32))
    q_dec, k_inv, k_end, dec = [], [], [], []
    for h in heads:
        b, b_last = bb[h][:rows], bb[h][rows:]
        q = q_ref[:, GLA_DK_PAD * h:GLA_DK_PAD * (h + 1)].astype(F32) * (GLA_DK ** -0.5)
        kk = k_ref[:, GLA_DK_PAD * h:GLA_DK_PAD * (h + 1)].astype(F32)
        q_dec.append((q * jnp.exp(b)).astype(BF16))
        k_inv.append((kk * jnp.exp(-b)).astype(BF16))
        k_end.append((kk * jnp.exp(b_last - b)).astype(BF16))
        dec.append(jnp.exp(b_last))
    v = [v_ref[:, GLA_DV_PAD * h:GLA_DV_PAD * (h + 1)] for h in heads]
    sc = []
    for h in heads:
        s_h = lax.dot_general(q_dec[h], k_inv[h], nt, preferred_element_type=F32)
        sc.append(jnp.where(same_chunk_causal, s_h, 0.0).astype(BF16))
    o_intra = [jnp.dot(sc[h], v[h], preferred_element_type=F32) for h in heads]
    d_st = [[lax.dot_general(v[h][c * ch:(c + 1) * ch], k_end[h][c * ch:(c + 1) * ch], tn,
                             preferred_element_type=F32) for c in range(nch)] for h in heads]
    st_in = []
    for h in heads:
        st = st_ref[h]
        states = []
        for c in range(nch):
            states.append(st.astype(BF16))
            st = st * dec[h][c * ch:c * ch + 1, :] + d_st[h][c]
        st_ref[h] = st
        st_in.append(states)
    for h in heads:
        o_inter = [lax.dot_general(q_dec[h][c * ch:(c + 1) * ch], st_in[h][c], nt,
                                   preferred_element_type=F32) for c in range(nch)]
        o = o_intra[h] + jnp.concatenate(o_inter, axis=0)
        ms = jnp.sum(o * o, axis=1, keepdims=True) * (1.0 / GLA_DV)
        y = o * lax.rsqrt(ms + RMS_EPS) * gn
        rr = r_ref[:, GLA_DV_PAD * h:GLA_DV_PAD * (h + 1)].astype(F32)
        o_ref[:, GLA_DV_PAD * h:GLA_DV_PAD * (h + 1)] = (
            y * (rr / (1.0 + jnp.exp(-rr)))).astype(o_ref.dtype)


def gla_attention(proj, gw_pad, gb_pad, gn_pad, layer, rows=256, hps=4):
    s = proj.shape[0]
    rows = min(rows, s)
    ch = GLA_CHUNK
    dkw, dvw = hps * GLA_DK_PAD, hps * GLA_DV_PAD
    qi, ki = OFF_BQ // dkw, OFF_BK // dkw
    vi, ri = OFF_BV // dvw, OFF_BR // dvw
    li = OFF_BLR // GLA_LR_PAD
    idx = np.arange(rows)
    same = (idx[:, None] // ch) == (idx[None, :] // ch)
    prefix = same & (idx[None, :] <= idx[:, None])
    lm = jnp.asarray(np.concatenate([prefix, same], axis=0).astype(np.float32), dtype=BF16)
    return pl.pallas_call(
        functools.partial(_gla_kernel, rows=rows, hps=hps),
        grid=(GLA_HEADS // hps, s // rows),
        in_specs=[pl.BlockSpec((rows, dkw), lambda g, t: (t, qi + g)),
                  pl.BlockSpec((rows, dkw), lambda g, t: (t, ki + g)),
                  pl.BlockSpec((rows, dvw), lambda g, t: (t, vi + g)),
                  pl.BlockSpec((rows, dvw), lambda g, t: (t, ri + g)),
                  pl.BlockSpec((rows, GLA_LR_PAD), lambda g, t: (t, li)),
                  pl.BlockSpec((None, hps, GLA_LR_PAD, GLA_DK_PAD),
                               lambda g, t: (layer, g, 0, 0)),
                  pl.BlockSpec((None, hps, 1, GLA_DK_PAD), lambda g, t: (layer, g, 0, 0)),
                  pl.BlockSpec((None, 1, GLA_DV_PAD), lambda g, t: (layer, 0, 0)),
                  pl.BlockSpec((2 * rows, rows), lambda g, t: (0, 0))],
        out_specs=pl.BlockSpec((rows, dvw), lambda g, t: (t, g)),
        out_shape=jax.ShapeDtypeStruct((s, GLA_HEADS * GLA_DV_PAD), BF16),
        scratch_shapes=[pltpu.VMEM((hps, GLA_DV_PAD, GLA_DK_PAD), F32)],
        compiler_params=_cparams(("parallel", "arbitrary")),
        name="gla",
    )(proj, proj, proj, proj, proj, gw_pad, gb_pad, gn_pad, lm)


def _sb_kernel(q_ref, k_ref, v_ref, u_ref, o_ref, acc_ref, car_ref, qm_ref, vm_ref,
               z0_ref, z1_ref, z2_ref, z3_ref, nl0_ref, nl1_ref, w0_ref, w1_ref,
               *, bq, nblk, unroll):
    lane = lax.broadcasted_iota(jnp.int32, (bq, LANES), 1)
    low = lane < HEAD_DIM
    nt = (((1,), (1,)), ((), ()))
    z_ref = (z0_ref, z1_ref, z2_ref, z3_ref)
    nl_ref, w_ref = (nl0_ref, nl1_ref), (w0_ref, w1_ref)

    def rows(blk):
        if isinstance(blk, int):
            return pl.ds(blk * bq, bq)
        return pl.ds(pl.multiple_of(blk * bq, bq), bq)

    def strict_mask():
        rowi = lax.broadcasted_iota(jnp.int32, (2 * bq, bq), 0)
        coli = lax.broadcasted_iota(jnp.int32, (2 * bq, bq), 1)
        return coli < (rowi & (bq - 1))

    def s1a(item, tick4):
        i, c = item
        z_ref[tick4][...] = lax.dot_general(qm_ref[i], k_ref[rows(c), :], nt,
                                            preferred_element_type=F32)

    def s1b(tick4, masked):
        z = z_ref[tick4][...].astype(BF16)
        nl = jnp.maximum(z, 0.0) + jnp.log(1.0 + jnp.exp2(jnp.abs(z) * (-LOG2E)))
        if masked:
            nl = jnp.where(strict_mask(), nl, jnp.zeros_like(nl))
        nl_ref[tick4 % 2][...] = nl

    def s2(item, tick4, masked, first):
        i, _ = item
        cs = jnp.dot(nl_ref[tick4 % 2][...], u_ref[...], preferred_element_type=F32)
        tot = jnp.broadcast_to(cs[:, 0:1], (2 * bq, LANES))
        if first:
            arg = z_ref[tick4][...] - cs
            car_ref[i] = tot
        else:
            car = car_ref[i]
            arg = z_ref[tick4][...] - cs - jnp.concatenate([car, car], axis=1)
            car_ref[i] = car + tot
        w = jnp.exp(arg)
        if masked:
            w = jnp.where(strict_mask(), w, 0.0)
        w = w.astype(BF16)
        w_ref[tick4 % 2][:, :bq] = w[:bq]
        w_ref[tick4 % 2][:, bq:] = w[bq:]

    def s3(item, tick4, first):
        i, c = item
        pv = jnp.dot(w_ref[tick4 % 2][...], vm_ref[c], preferred_element_type=F32)
        if first:
            acc_ref[rows(i), :] = pv
        else:
            acc_ref[rows(i), :] += pv

    def run_pipeline(items, step, masked, first):
        n = len(items)

        def tick(t4, its, do):
            if do[1]:
                s1b((t4 - 1) % 4, masked)
            if do[0]:
                s1a(its[0], t4)
            if do[2]:
                s2(its[2], (t4 - 2) % 4, masked, first)
            if do[3]:
                s3(its[3], (t4 - 3) % 4, first)

        def static_tick(t):
            do = [0 <= t - j < n for j in range(4)]
            its = [items[t - j] if do[j] else None for j in range(4)]
            tick(t % 4, its, do)

        lo_t, hi_t = 3, n - 1
        groups = max(hi_t - lo_t + 1, 0) // unroll
        if groups < 2:
            groups = 0
        for t in range(min(lo_t, n + 3)):
            static_tick(t)
        if groups:
            def body(_, carry):
                its = [(carry[2 * j], carry[2 * j + 1]) for j in range(4)]
                for k in range(unroll):
                    tick((lo_t + k) % 4, its, [True] * 4)
                    its = [step(*its[0])] + its[:3]
                return tuple(x for it in its for x in it)

            init = tuple(jnp.int32(x) for j in range(4) for x in items[lo_t - j])
            lax.fori_loop(0, groups, body, init)
        for t in range(lo_t + unroll * groups, n + 3):
            static_tick(t)

    for blk in range(nblk):
        r = rows(blk)
        qb, vb = q_ref[r, :], v_ref[r, :]
        zero = jnp.zeros_like(qb)
        qm_ref[blk, :bq] = jnp.where(low, qb, zero)
        qm_ref[blk, bq:] = jnp.where(low, zero, qb)
        vm_ref[blk, :bq] = jnp.where(low, vb, zero)
        vm_ref[blk, bq:] = jnp.where(low, zero, vb)

    diag = [(i, i) for i in range(nblk)]
    run_pipeline(diag, lambda i, c: (i + 1, c + 1), masked=True, first=True)

    off = [(i, c) for i in range(1, nblk) for c in range(i - 1, -1, -1)]

    def off_step(i, c):
        wrap = c == 0
        i2 = jnp.where(wrap, i + 1, i)
        return i2, jnp.where(wrap, i2 - 1, c - 1)

    if off:
        run_pipeline(off, off_step, masked=False, first=False)

    o_ref[...] = acc_ref[...].astype(o_ref.dtype)


def sb_attention(proj, bq=256, unroll=8):
    s = proj.shape[0]
    bq = min(bq, s)
    nblk = s // bq
    assert bq & (bq - 1) == 0 and unroll % 4 == 0
    qi = OFF_CQ // LANES
    ki = OFF_CK // LANES
    vi = OFF_CV // LANES
    u = (np.arange(bq)[:, None] >= np.arange(bq)[None, :]).astype(np.float32)
    u = jnp.asarray(u, dtype=BF16)
    slot = lambda shape, dtype: pltpu.VMEM(shape, dtype)
    return pl.pallas_call(
        functools.partial(_sb_kernel, bq=bq, nblk=nblk, unroll=unroll),
        grid=(SB_HEADS // 2,),
        in_specs=[pl.BlockSpec((s, LANES), lambda p: (0, qi + p)),
                  pl.BlockSpec((s, LANES), lambda p: (0, ki + p)),
                  pl.BlockSpec((s, LANES), lambda p: (0, vi + p)),
                  pl.BlockSpec((bq, bq), lambda p: (0, 0))],
        out_specs=pl.BlockSpec((s, LANES), lambda p: (0, p)),
        out_shape=jax.ShapeDtypeStruct((s, SB_WIDTH), BF16),
        scratch_shapes=[pltpu.VMEM((s, LANES), F32),
                        pltpu.VMEM((nblk, 2 * bq, LANES), F32),
                        pltpu.VMEM((nblk, 2 * bq, LANES), BF16),
                        pltpu.VMEM((nblk, 2 * bq, LANES), BF16)]
                       + [slot((2 * bq, bq), F32)] * 4
                       + [slot((2 * bq, bq), BF16)] * 2
                       + [slot((bq, 2 * bq), BF16)] * 2,
        compiler_params=_cparams(("parallel",)),
        name="stickbreak",
    )(proj, proj, proj, u)


def _oproj_kernel(a_ref, b_ref, c_ref, wa_ref, wb_ref, wc_ref, ga_ref, gc_ref,
                  x_ref, gn_ref, xo_ref, ho_ref):
    an = _rms(a_ref[...].astype(F32), ga_ref[...]).astype(BF16)
    cn = _rms(c_ref[...].astype(F32), gc_ref[...]).astype(BF16)
    y = (jnp.dot(an, wa_ref[...], preferred_element_type=F32)
         + jnp.dot(b_ref[...], wb_ref[...], preferred_element_type=F32)
         + jnp.dot(cn, wc_ref[...], preferred_element_type=F32))
    xn = x_ref[...] + y
    xo_ref[...] = xn
    ho_ref[...] = _rms(xn, gn_ref[...]).astype(ho_ref.dtype)


def out_projection(a, b, c, wa, wb, wc, ga, gc, x, gn, layer, tm=512):
    s, d = x.shape
    tm = min(tm, s)
    row = lambda i: (i, 0)
    fixed = lambda i: (layer, 0, 0)
    per_layer = lambda arr: pl.BlockSpec((None,) + arr.shape[1:], fixed)
    return pl.pallas_call(
        _oproj_kernel,
        grid=(s // tm,),
        in_specs=[pl.BlockSpec((tm, a.shape[1]), row),
                  pl.BlockSpec((tm, b.shape[1]), row),
                  pl.BlockSpec((tm, c.shape[1]), row),
                  per_layer(wa), per_layer(wb), per_layer(wc), per_layer(ga), per_layer(gc),
                  pl.BlockSpec((tm, d), row),
                  per_layer(gn)],
        out_specs=[pl.BlockSpec((tm, d), row), pl.BlockSpec((tm, d), row)],
        out_shape=[jax.ShapeDtypeStruct((s, d), F32), jax.ShapeDtypeStruct((s, d), BF16)],
        compiler_params=_cparams(("parallel",)),
        name="out_proj",
    )(a, b, c, wa, wb, wc, ga, gc, x, gn)


def _mlp_kernel(h_ref, x_ref, w1_ref, w2_ref, gn_ref, *out_refs, final):
    acc_ref = out_refs[0]
    j = pl.program_id(1)

    @pl.when(j == 0)
    def _():
        acc_ref[...] = x_ref[...]

    u = jnp.dot(h_ref[...], w1_ref[...].astype(BF16), preferred_element_type=F32)
    act = jnp.square(jnp.maximum(u, 0.0)).astype(BF16)
    acc_ref[...] += jnp.dot(act, w2_ref[...].astype(BF16), preferred_element_type=F32)

    @pl.when(j == pl.num_programs(1) - 1)
    def _():
        y = _rms(acc_ref[...], gn_ref[...])
        if final:
            acc_ref[...] = y
        else:
            out_refs[1][...] = y.astype(out_refs[1].dtype)


def mlp_block(h, x, w1_all, w2_all, layer, gn, final, tm=1024, tf=512):
    s, d = x.shape
    ff = w1_all.shape[2]
    tm = min(tm, s)
    row = lambda i, j: (i, 0)
    once = pl.Buffered(1)
    out_specs = [pl.BlockSpec((tm, d), row)]
    out_shape = [jax.ShapeDtypeStruct((s, d), F32)]
    if not final:
        out_specs.append(pl.BlockSpec((tm, d), row))
        out_shape.append(jax.ShapeDtypeStruct((s, d), BF16))
    return pl.pallas_call(
        functools.partial(_mlp_kernel, final=final),
        grid=(s // tm, ff // tf),
        in_specs=[pl.BlockSpec((tm, d), row, pipeline_mode=once),
                  pl.BlockSpec((tm, d), row, pipeline_mode=once),
                  pl.BlockSpec((None, d, tf), lambda i, j: (layer, 0, j)),
                  pl.BlockSpec((None, tf, d), lambda i, j: (layer, j, 0)),
                  pl.BlockSpec((None, 1, d), lambda i, j: (layer, 0, 0))],
        out_specs=out_specs,
        out_shape=out_shape,
        compiler_params=_cparams(("parallel", "arbitrary")),
        name="mlp",
    )(h, x, w1_all, w2_all, gn)


def _head_rows(w, off, heads, width, padded):
    pieces = []
    for h in range(heads):
        pieces.append(w[..., off + h * width:off + (h + 1) * width, :])
        if padded > width:
            pieces.append(jnp.zeros(w.shape[:-2] + (padded - width, w.shape[-1]), w.dtype))
    return pieces


def _swa_perm_cols(w):
    return jnp.concatenate([w[..., HEAD_DIM * h:HEAD_DIM * (h + 1)] for h in SWA_Q_PERM], axis=-1)


def kernel(x, norm_mix, w_in, swa_sinks, rel_bias, gla_gate_w, gla_gate_b, gla_norm,
           swa_out_norm, sb_out_norm, w_out, norm_mlp, w_mlp_in, w_mlp_out, norm_final):
    depth = w_in.shape[0]
    xs = x[0]
    gla_w = GLA_HEADS * GLA_DV
    w_in_p = prep_w_in(jnp.swapaxes(w_in, 1, 2))
    wa = jnp.concatenate([w_out[:, HEAD_DIM * h:HEAD_DIM * (h + 1)] for h in SWA_Q_PERM],
                         axis=1).astype(BF16)
    wb = jnp.concatenate(_head_rows(w_out, SWA_WIDTH, GLA_HEADS, GLA_DV, GLA_DV_PAD),
                         axis=1).astype(BF16)
    wc = w_out[:, SWA_WIDTH + gla_w:].astype(BF16)
    ga = _swa_perm_cols(swa_out_norm)[:, None, :]
    gc = sb_out_norm[:, None, :]
    g_mlp = norm_mlp[:, None, :]
    g_next = jnp.concatenate([norm_mix[1:], norm_final[None]], axis=0)[:, None, :]
    sinks_p = jnp.concatenate([swa_sinks[:, h:h + 1] for h in SWA_Q_PERM], axis=1)
    bias_tab = swa_bias_table(rel_bias)
    gw = jnp.stack([jnp.pad(gla_gate_w[:, :, GLA_DK * h:GLA_DK * (h + 1)],
                            ((0, 0), (0, GLA_LR_PAD - GLA_LOWRANK), (0, GLA_DK_PAD - GLA_DK)))
                    for h in range(GLA_HEADS)], axis=1).astype(BF16)
    gb = jnp.stack([jnp.pad(gla_gate_b[:, GLA_DK * h:GLA_DK * (h + 1)],
                            ((0, 0), (0, GLA_DK_PAD - GLA_DK)))
                    for h in range(GLA_HEADS)], axis=1)[:, :, None, :]
    gn = jnp.pad(gla_norm, ((0, 0), (0, GLA_DV_PAD - GLA_DV)))[:, None, :]

    h = None
    for l in range(depth):
        if l == 0:
            proj = in_projection(xs, w_in_p, l, gain=norm_mix[0])
        else:
            proj = in_projection(h, w_in_p, l)
        a = swa_attention(proj, sinks_p, bias_tab, l)
        b = gla_attention(proj, gw, gb, gn, l)
        c = sb_attention(proj)
        xs, hm = out_projection(a, b, c, wa, wb, wc, ga, gc, xs, g_mlp, l)
        if l + 1 < depth:
            xs, h = mlp_block(hm, xs, w_mlp_in, w_mlp_out, l, g_next, final=False)
        else:
            (out,) = mlp_block(hm, xs, w_mlp_in, w_mlp_out, l, g_next, final=True)
    return out[None]
```

```python
import functools
import math

import numpy as np
import jax
import jax.numpy as jnp
from jax import lax
from jax.experimental import pallas as pl
from jax.experimental.pallas import tpu as pltpu

F32 = jnp.float32
BF16 = jnp.bfloat16

HEAD_DIM = 64
SWA_HEADS = 12
SWA_KV_HEADS = 4
SWA_WIDTH = SWA_HEADS * HEAD_DIM
SWA_KV_WIDTH = SWA_KV_HEADS * HEAD_DIM
SWA_BLOCK = 128
WINDOW = 128
REL_BUCKETS = 32
REL_MAX_DIST = 128
GLA_HEADS = 4
GLA_DV = 192
GLA_DK = 96
GLA_DK_PAD = 128
GLA_DV_PAD = 256
GLA_LOWRANK = 16
GLA_LR_PAD = 128
GLA_CHUNK = 64
GATE_NORMALIZER = 16.0
GATE_LOG_MIN = -1.0
SB_HEADS = 8
SB_WIDTH = SB_HEADS * HEAD_DIM
RMS_EPS = 1e-6
LOG2E = 1.4426950408889634
NEG_INF = -1e30

LANES = 128
VMEM_LIMIT = 56 * 1024 * 1024

OFF_BV = 0
OFF_BR = 1024
OFF_BQ = 2048
OFF_BK = 2560
OFF_CQ = 3072
OFF_CK = 3584
OFF_CV = 4096
OFF_AQ = 4608
OFF_AK = 5376
OFF_AV = 5632
OFF_BLR = 5888
PROJ_PAD_WIDTH = 6144

SWA_Q_PERM = (0, 3, 1, 4, 2, 5, 6, 9, 7, 10, 8, 11)

_O_AQ, _O_AK, _O_AV = 0, 768, 1024
_O_BQ, _O_BK, _O_BV, _O_BR, _O_BLR = 1280, 1664, 2048, 2816, 3584
_O_CQ, _O_CK, _O_CV = 3600, 4112, 4624


def _cparams(sem):
    return pltpu.CompilerParams(dimension_semantics=sem, vmem_limit_bytes=VMEM_LIMIT)


def _rms(x, gain):
    ms = jnp.mean(x * x, axis=-1, keepdims=True)
    return x * lax.rsqrt(ms + RMS_EPS) * gain


def _split_bf16(x):
    hi = x.astype(BF16)
    lo = (x - hi.astype(F32)).astype(BF16)
    return hi, lo


_NT_DIMS = (((1,), (1,)), ((), ()))


def _matmul_kernel(h_ref, w_ref, o_ref):
    o_ref[...] = lax.dot_general(h_ref[...], w_ref[...], _NT_DIMS,
                                 preferred_element_type=F32).astype(o_ref.dtype)


def _norm_matmul_kernel(x_ref, g_ref, w_ref, o_ref):
    h = _rms(x_ref[...], g_ref[...]).astype(BF16)
    o_ref[...] = lax.dot_general(h, w_ref[...], _NT_DIMS,
                                 preferred_element_type=F32).astype(o_ref.dtype)


def in_projection(h, w_all, layer, gain=None, tm=1024, tn=1024):
    s, d = h.shape
    n = w_all.shape[1]
    tm = min(tm, s)
    w_spec = pl.BlockSpec((None, tn, d), lambda i, j: (layer, j, 0))
    h_spec = pl.BlockSpec((tm, d), lambda i, j: (i, 0))
    common = dict(
        grid=(s // tm, n // tn),
        out_specs=pl.BlockSpec((tm, tn), lambda i, j: (i, j)),
        out_shape=jax.ShapeDtypeStruct((s, n), BF16),
        compiler_params=_cparams(("parallel", "arbitrary")),
        name="in_proj",
    )
    if gain is None:
        return pl.pallas_call(_matmul_kernel, in_specs=[h_spec, w_spec], **common)(h, w_all)
    return pl.pallas_call(
        _norm_matmul_kernel,
        in_specs=[h_spec, pl.BlockSpec((1, d), lambda i, j: (0, 0)), w_spec],
        **common,
    )(h, gain.reshape(1, d), w_all)


def _w_in_moves():
    scale = HEAD_DIM ** -0.5
    moves = []
    for h in range(GLA_HEADS):
        moves.append((_O_BV + GLA_DV * h, OFF_BV + GLA_DV_PAD * h, GLA_DV, 1.0))
        moves.append((_O_BR + GLA_DV * h, OFF_BR + GLA_DV_PAD * h, GLA_DV, 1.0))
        moves.append((_O_BQ + GLA_DK * h, OFF_BQ + GLA_DK_PAD * h, GLA_DK, 1.0))
        moves.append((_O_BK + GLA_DK * h, OFF_BK + GLA_DK_PAD * h, GLA_DK, 1.0))
    moves += [(_O_CQ, OFF_CQ, SB_WIDTH, scale), (_O_CK, OFF_CK, SB_WIDTH, 1.0),
              (_O_CV, OFF_CV, SB_WIDTH, 1.0)]
    for j, h in enumerate(SWA_Q_PERM):
        moves.append((_O_AQ + HEAD_DIM * h, OFF_AQ + HEAD_DIM * j, HEAD_DIM, scale))
    moves += [(_O_AK, OFF_AK, SWA_KV_WIDTH, 1.0), (_O_AV, OFF_AV, SWA_KV_WIDTH, 1.0),
              (_O_BLR, OFF_BLR, GLA_LOWRANK, 1.0)]
    return moves


def _w_in_prep_kernel(w_ref, o_ref):
    o_ref[...] = jnp.zeros_like(o_ref)
    for src, dst, width, scale in _w_in_moves():
        piece = w_ref[src:src + width, :]
        if scale != 1.0:
            piece = piece * scale
        o_ref[dst:dst + width, :] = piece.astype(o_ref.dtype)


def prep_w_in(w_in_t, tc=256):
    layers, n, d = w_in_t.shape
    return pl.pallas_call(
        _w_in_prep_kernel,
        grid=(layers, d // tc),
        in_specs=[pl.BlockSpec((None, n, tc), lambda l, i: (l, 0, i))],
        out_specs=pl.BlockSpec((None, PROJ_PAD_WIDTH, tc), lambda l, i: (l, 0, i)),
        out_shape=jax.ShapeDtypeStruct((layers, PROJ_PAD_WIDTH, d), BF16),
        compiler_params=_cparams(("parallel", "parallel")),
        name="w_in_prep",
    )(w_in_t)


def _swa_kernel(sink_ref, q_ref, kp_ref, kc_ref, vp_ref, vc_ref, bias_ref, o_ref, *, layer):
    blk = SWA_BLOCK
    nb = SWA_HEADS // 2
    k = jnp.concatenate([kp_ref[...], kc_ref[...]], axis=0)
    v = jnp.concatenate([vp_ref[...], vc_ref[...]], axis=0)
    lane = lax.broadcasted_iota(jnp.int32, (blk, LANES), 1)
    low = lane < HEAD_DIM
    top = lax.broadcasted_iota(jnp.int32, (2 * blk, 1), 0) < blk
    ones = jnp.ones((2 * blk, LANES), BF16)
    nt = (((1,), (1,)), ((), ()))

    scores = []
    for b in range(nb):
        qb = q_ref[:, LANES * b:LANES * (b + 1)]
        zero = jnp.zeros_like(qb)
        lhs = jnp.concatenate([jnp.where(low, qb, zero), jnp.where(low, zero, qb)], axis=0)
        kb = k[:, LANES * (b // 3):LANES * (b // 3 + 1)]
        scores.append(lax.dot_general(lhs, kb, nt, preferred_element_type=F32) + bias_ref[b])
    probs, sink_term = [], []
    for b in range(nb):
        sink = jnp.where(top, sink_ref[layer, 2 * b], sink_ref[layer, 2 * b + 1])
        m = jnp.maximum(jnp.max(scores[b], axis=1, keepdims=True), sink)
        probs.append(jnp.exp(scores[b] - m).astype(BF16))
        sink_term.append(jnp.exp(sink - m))
    for b in range(nb):
        vb = v[:, LANES * (b // 3):LANES * (b // 3 + 1)]
        pv = jnp.dot(probs[b], jnp.concatenate([vb, ones], axis=1),
                     preferred_element_type=F32)
        o = pv[:, :LANES] / (pv[:, LANES:] + sink_term[b])
        o_ref[:, LANES * b:LANES * (b + 1)] = jnp.where(low, o[:blk], o[blk:]).astype(o_ref.dtype)


def swa_attention(proj, sinks_perm, bias_tab, layer):
    s = proj.shape[0]
    blk = SWA_BLOCK
    qi = OFF_AQ // SWA_WIDTH
    ki = OFF_AK // SWA_KV_WIDTH
    vi = OFF_AV // SWA_KV_WIDTH
    prev = lambda i: jnp.maximum(i - 1, 0)
    return pl.pallas_call(
        functools.partial(_swa_kernel, layer=layer),
        grid=(s // blk,),
        in_specs=[pl.BlockSpec(memory_space=pltpu.SMEM),
                  pl.BlockSpec((blk, SWA_WIDTH), lambda i: (i, qi)),
                  pl.BlockSpec((blk, SWA_KV_WIDTH), lambda i: (prev(i), ki)),
                  pl.BlockSpec((blk, SWA_KV_WIDTH), lambda i: (i, ki)),
                  pl.BlockSpec((blk, SWA_KV_WIDTH), lambda i: (prev(i), vi)),
                  pl.BlockSpec((blk, SWA_KV_WIDTH), lambda i: (i, vi)),
                  pl.BlockSpec((None, SWA_HEADS // 2, 2 * blk, 2 * blk),
                               lambda i: (jnp.minimum(i, 1), 0, 0, 0))],
        out_specs=pl.BlockSpec((blk, SWA_WIDTH), lambda i: (i, 0)),
        out_shape=jax.ShapeDtypeStruct((s, SWA_WIDTH), BF16),
        compiler_params=_cparams(("parallel",)),
        name="swa",
    )(sinks_perm, proj, proj, proj, proj, proj, bias_tab)


def _t5_causal_bucket(dist):
    max_exact = REL_BUCKETS // 2
    is_small = dist < max_exact
    ratio = (jnp.log(jnp.maximum(dist, 1).astype(F32) / max_exact)
             / math.log(REL_MAX_DIST / max_exact))
    large = max_exact + (ratio * (REL_BUCKETS - max_exact)).astype(jnp.int32)
    large = jnp.minimum(large, REL_BUCKETS - 1)
    return jnp.where(is_small, dist, large)


def swa_bias_table(rel_bias):
    blk = SWA_BLOCK
    qpos = jnp.arange(blk) + blk
    kpos = jnp.arange(2 * blk)
    dist = qpos[:, None] - kpos[None, :]
    in_window = (dist >= 0) & (dist < WINDOW)
    bucket = _t5_causal_bucket(jnp.maximum(dist, 0))
    table = rel_bias.astype(F32)
    table = jnp.stack([table[:, h] for h in SWA_Q_PERM])
    bias = jnp.full((SWA_HEADS, blk, 2 * blk), NEG_INF, F32)
    for bkt in range(REL_BUCKETS):
        hit = (in_window & (bucket == bkt))[None]
        bias = jnp.where(hit, table[:, bkt][:, None, None], bias)
    bias = bias.reshape(SWA_HEADS // 2, 2 * blk, 2 * blk)
    first = jnp.where((kpos < blk)[None, None, :], NEG_INF, bias)
    return jnp.stack([first, bias])


def _gla_kernel(q_ref, k_ref, v_ref, r_ref, lr_ref, gw_ref, gb_ref, gn_ref, lm_ref, o_ref,
                st_ref, *, rows, hps):
    t = pl.program_id(1)
    ch = GLA_CHUNK
    nch = rows // ch
    heads = range(hps)

    @pl.when(t == 0)
    def _():
        st_ref[...] = jnp.zeros_like(st_ref)

    nt = (((1,), (1,)), ((), ()))
    tn = (((0,), (0,)), ((), ()))
    rowi = lax.broadcasted_iota(jnp.int32, (rows, rows), 0)
    coli = lax.broadcasted_iota(jnp.int32, (rows, rows), 1)
    same_chunk_causal = jnp.logical_and(coli <= rowi, (rowi // ch) == (coli // ch))
    lm = lm_ref[...]
    lr = lr_ref[...]
    gn = gn_ref[...]

    gp = [jnp.dot(lr, gw_ref[h], preferred_element_type=F32) + gb_ref[h] for h in heads]
    lg = []
    for h in heads:
        log_sig = jnp.minimum(gp[h], 0.0) - jnp.log(1.0 + jnp.exp(-jnp.abs(gp[h])))
        lg.append(jnp.maximum(log_sig * (1.0 / GATE_NORMALIZER), GATE_LOG_MIN))
    bb = []
    for h in heads:
        hi, lo = _split_bf16(lg[h])
        bb.append(jnp.dot(lm, hi, preferred_element_type=F32)
                  + jnp.dot(lm, lo, preferred_element_type=F32))
    q_dec, k_inv, k_end, dec = [], [], [], []
    for h in heads:
        b, b_last = bb[h][:rows], bb[h][rows:]
        q = q_ref[:, GLA_DK_PAD * h:GLA_DK_PAD * (h + 1)].astype(F32) * (GLA_DK ** -0.5)
        kk = k_ref[:, GLA_DK_PAD * h:GLA_DK_PAD * (h + 1)].astype(F32)
        q_dec.append((q * jnp.exp(b)).astype(BF16))
        k_inv.append((kk * jnp.exp(-b)).astype(BF16))
        k_end.append((kk * jnp.exp(b_last - b)).astype(BF16))
        dec.append(jnp.exp(b_last))
    v = [v_ref[:, GLA_DV_PAD * h:GLA_DV_PAD * (h + 1)] for h in heads]
    sc = []
    for h in heads:
        s_h = lax.dot_general(q_dec[h], k_inv[h], nt, preferred_element_type=F32)
        sc.append(jnp.where(same_chunk_causal, s_h, 0.0).astype(BF16))
    o_intra = [jnp.dot(sc[h], v[h], preferred_element_type=F32) for h in heads]
    d_st = [[lax.dot_general(v[h][c * ch:(c + 1) * ch], k_end[h][c * ch:(c + 1) * ch], tn,
                             preferred_element_type=F32) for c in range(nch)] for h in heads]
    st_in = []
    for h in heads:
        st = st_ref[h]
        states = []
        for c in range(nch):
            states.append(st.astype(BF16))
            st = st * dec[h][c * ch:c * ch + 1, :] + d_st[h][c]
        st_ref[h] = st
        st_in.append(states)
    for h in heads:
        o_inter = [lax.dot_general(q_dec[h][c * ch:(c + 1) * ch], st_in[h][c], nt,
                                   preferred_element_type=F32) for c in range(nch)]
        o = o_intra[h] + jnp.concatenate(o_inter, axis=0)
        ms = jnp.sum(o * o, axis=1, keepdims=True) * (1.0 / GLA_DV)
        y = o * lax.rsqrt(ms + RMS_EPS) * gn
        rr = r_ref[:, GLA_DV_PAD * h:GLA_DV_PAD * (h + 1)].astype(F32)
        o_ref[:, GLA_DV_PAD * h:GLA_DV_PAD * (h + 1)] = (
            y * (rr / (1.0 + jnp.exp(-rr)))).astype(o_ref.dtype)


def gla_attention(proj, gw_pad, gb_pad, gn_pad, layer, rows=256, hps=4):
    s = proj.shape[0]
    rows = min(rows, s)
    ch = GLA_CHUNK
    dkw, dvw = hps * GLA_DK_PAD, hps * GLA_DV_PAD
    qi, ki = OFF_BQ // dkw, OFF_BK // dkw
    vi, ri = OFF_BV // dvw, OFF_BR // dvw
    li = OFF_BLR // GLA_LR_PAD
    idx = np.arange(rows)
    same = (idx[:, None] // ch) == (idx[None, :] // ch)
    prefix = same & (idx[None, :] <= idx[:, None])
    lm = jnp.asarray(np.concatenate([prefix, same], axis=0).astype(np.float32), dtype=BF16)
    return pl.pallas_call(
        functools.partial(_gla_kernel, rows=rows, hps=hps),
        grid=(GLA_HEADS // hps, s // rows),
        in_specs=[pl.BlockSpec((rows, dkw), lambda g, t: (t, qi + g)),
                  pl.BlockSpec((rows, dkw), lambda g, t: (t, ki + g)),
                  pl.BlockSpec((rows, dvw), lambda g, t: (t, vi + g)),
                  pl.BlockSpec((rows, dvw), lambda g, t: (t, ri + g)),
                  pl.BlockSpec((rows, GLA_LR_PAD), lambda g, t: (t, li)),
                  pl.BlockSpec((None, hps, GLA_LR_PAD, GLA_DK_PAD),
                               lambda g, t: (layer, g, 0, 0)),
                  pl.BlockSpec((None, hps, 1, GLA_DK_PAD), lambda g, t: (layer, g, 0, 0)),
                  pl.BlockSpec((None, 1, GLA_DV_PAD), lambda g, t: (layer, 0, 0)),
                  pl.BlockSpec((2 * rows, rows), lambda g, t: (0, 0))],
        out_specs=pl.BlockSpec((rows, dvw), lambda g, t: (t, g)),
        out_shape=jax.ShapeDtypeStruct((s, GLA_HEADS * GLA_DV_PAD), BF16),
        scratch_shapes=[pltpu.VMEM((hps, GLA_DV_PAD, GLA_DK_PAD), F32)],
        compiler_params=_cparams(("parallel", "arbitrary")),
        name="gla",
    )(proj, proj, proj, proj, proj, gw_pad, gb_pad, gn_pad, lm)


def _sb_kernel(q_ref, k_ref, v_ref, u_ref, o_ref, acc_ref, car_ref, qm_ref, vm_ref,
               z0_ref, z1_ref, z2_ref, z3_ref, nl0_ref, nl1_ref, w0_ref, w1_ref,
               *, bq, nblk, unroll):
    lane = lax.broadcasted_iota(jnp.int32, (bq, LANES), 1)
    low = lane < HEAD_DIM
    nt = (((1,), (1,)), ((), ()))
    z_ref = (z0_ref, z1_ref, z2_ref, z3_ref)
    nl_ref, w_ref = (nl0_ref, nl1_ref), (w0_ref, w1_ref)

    def rows(blk):
        if isinstance(blk, int):
            return pl.ds(blk * bq, bq)
        return pl.ds(pl.multiple_of(blk * bq, bq), bq)

    def strict_mask():
        rowi = lax.broadcasted_iota(jnp.int32, (2 * bq, bq), 0)
        coli = lax.broadcasted_iota(jnp.int32, (2 * bq, bq), 1)
        return coli < (rowi & (bq - 1))

    def s1a(item, tick4):
        i, c = item
        z_ref[tick4][...] = lax.dot_general(qm_ref[i], k_ref[rows(c), :], nt,
                                            preferred_element_type=F32)

    def s1b(tick4, masked):
        z = z_ref[tick4][...].astype(BF16)
        nl = jnp.maximum(z, 0.0) + jnp.log(1.0 + jnp.exp2(jnp.abs(z) * (-LOG2E)))
        if masked:
            nl = jnp.where(strict_mask(), nl, jnp.zeros_like(nl))
        nl_ref[tick4 % 2][...] = nl

    def s2(item, tick4, masked, first):
        i, _ = item
        cs = jnp.dot(nl_ref[tick4 % 2][...], u_ref[...], preferred_element_type=F32)
        tot = jnp.broadcast_to(cs[:, 0:1], (2 * bq, LANES))
        if first:
            arg = z_ref[tick4][...] - cs
            car_ref[i] = tot
        else:
            car = car_ref[i]
            arg = z_ref[tick4][...] - cs - jnp.concatenate([car, car], axis=1)
            car_ref[i] = car + tot
        w = jnp.exp(arg)
        if masked:
            w = jnp.where(strict_mask(), w, 0.0)
        w = w.astype(BF16)
        w_ref[tick4 % 2][:, :bq] = w[:bq]
        w_ref[tick4 % 2][:, bq:] = w[bq:]

    def s3(item, tick4, first):
        i, c = item
        pv = jnp.dot(w_ref[tick4 % 2][...], vm_ref[c], preferred_element_type=F32)
        if first:
            acc_ref[rows(i), :] = pv
        else:
            acc_ref[rows(i), :] += pv

    def run_pipeline(items, step, masked, first):
        n = len(items)

        def tick(t4, its, do):
            if do[1]:
                s1b((t4 - 1) % 4, masked)
            if do[0]:
                s1a(its[0], t4)
            if do[2]:
                s2(its[2], (t4 - 2) % 4, masked, first)
            if do[3]:
                s3(its[3], (t4 - 3) % 4, first)

        def static_tick(t):
            do = [0 <= t - j < n for j in range(4)]
            its = [items[t - j] if do[j] else None for j in range(4)]
            tick(t % 4, its, do)

        lo_t, hi_t = 3, n - 1
        groups = max(hi_t - lo_t + 1, 0) // unroll
        if groups < 2:
            groups = 0
        for t in range(min(lo_t, n + 3)):
            static_tick(t)
        if groups:
            def body(_, carry):
                its = [(carry[2 * j], carry[2 * j + 1]) for j in range(4)]
                for k in range(unroll):
                    tick((lo_t + k) % 4, its, [True] * 4)
                    its = [step(*its[0])] + its[:3]
                return tuple(x for it in its for x in it)

            init = tuple(jnp.int32(x) for j in range(4) for x in items[lo_t - j])
            lax.fori_loop(0, groups, body, init)
        for t in range(lo_t + unroll * groups, n + 3):
            static_tick(t)

    for blk in range(nblk):
        r = rows(blk)
        qb, vb = q_ref[r, :], v_ref[r, :]
        zero = jnp.zeros_like(qb)
        qm_ref[blk, :bq] = jnp.where(low, qb, zero)
        qm_ref[blk, bq:] = jnp.where(low, zero, qb)
        vm_ref[blk, :bq] = jnp.where(low, vb, zero)
        vm_ref[blk, bq:] = jnp.where(low, zero, vb)

    diag = [(i, i) for i in range(nblk)]
    run_pipeline(diag, lambda i, c: (i + 1, c + 1), masked=True, first=True)

    off = [(i, c) for i in range(1, nblk) for c in range(i - 1, -1, -1)]

    def off_step(i, c):
        wrap = c == 0
        i2 = jnp.where(wrap, i + 1, i)
        return i2, jnp.where(wrap, i2 - 1, c - 1)

    if off:
        run_pipeline(off, off_step, masked=False, first=False)

    o_ref[...] = acc_ref[...].astype(o_ref.dtype)


def sb_attention(proj, bq=256, unroll=12):
    s = proj.shape[0]
    bq = min(bq, s)
    nblk = s // bq
    assert bq & (bq - 1) == 0 and unroll % 4 == 0
    qi = OFF_CQ // LANES
    ki = OFF_CK // LANES
    vi = OFF_CV // LANES
    u = (np.arange(bq)[:, None] >= np.arange(bq)[None, :]).astype(np.float32)
    u = jnp.asarray(u, dtype=BF16)
    slot = lambda shape, dtype: pltpu.VMEM(shape, dtype)
    return pl.pallas_call(
        functools.partial(_sb_kernel, bq=bq, nblk=nblk, unroll=unroll),
        grid=(SB_HEADS // 2,),
        in_specs=[pl.BlockSpec((s, LANES), lambda p: (0, qi + p)),
                  pl.BlockSpec((s, LANES), lambda p: (0, ki + p)),
                  pl.BlockSpec((s, LANES), lambda p: (0, vi + p)),
                  pl.BlockSpec((bq, bq), lambda p: (0, 0))],
        out_specs=pl.BlockSpec((s, LANES), lambda p: (0, p)),
        out_shape=jax.ShapeDtypeStruct((s, SB_WIDTH), BF16),
        scratch_shapes=[pltpu.VMEM((s, LANES), F32),
                        pltpu.VMEM((nblk, 2 * bq, LANES), F32),
                        pltpu.VMEM((nblk, 2 * bq, LANES), BF16),
                        pltpu.VMEM((nblk, 2 * bq, LANES), BF16)]
                       + [slot((2 * bq, bq), F32)] * 4
                       + [slot((2 * bq, bq), BF16)] * 2
                       + [slot((bq, 2 * bq), BF16)] * 2,
        compiler_params=_cparams(("parallel",)),
        name="stickbreak",
    )(proj, proj, proj, u)


def _oproj_kernel(a_ref, b_ref, c_ref, wa_ref, wb_ref, wc_ref, ga_ref, gc_ref,
                  x_ref, gn_ref, xo_ref, ho_ref):
    an = _rms(a_ref[...].astype(F32), ga_ref[...]).astype(BF16)
    cn = _rms(c_ref[...].astype(F32), gc_ref[...]).astype(BF16)
    y = (jnp.dot(an, wa_ref[...], preferred_element_type=F32)
         + jnp.dot(b_ref[...], wb_ref[...], preferred_element_type=F32)
         + jnp.dot(cn, wc_ref[...], preferred_element_type=F32))
    xn = x_ref[...] + y
    xo_ref[...] = xn
    ho_ref[...] = _rms(xn, gn_ref[...]).astype(ho_ref.dtype)


def out_projection(a, b, c, wa, wb, wc, ga, gc, x, gn, layer, tm=512):
    s, d = x.shape
    tm = min(tm, s)
    row = lambda i: (i, 0)
    fixed = lambda i: (layer, 0, 0)
    per_layer = lambda arr: pl.BlockSpec((None,) + arr.shape[1:], fixed)
    return pl.pallas_call(
        _oproj_kernel,
        grid=(s // tm,),
        in_specs=[pl.BlockSpec((tm, a.shape[1]), row),
                  pl.BlockSpec((tm, b.shape[1]), row),
                  pl.BlockSpec((tm, c.shape[1]), row),
                  per_layer(wa), per_layer(wb), per_layer(wc), per_layer(ga), per_layer(gc),
                  pl.BlockSpec((tm, d), row),
                  per_layer(gn)],
        out_specs=[pl.BlockSpec((tm, d), row), pl.BlockSpec((tm, d), row)],
        out_shape=[jax.ShapeDtypeStruct((s, d), F32), jax.ShapeDtypeStruct((s, d), BF16)],
        compiler_params=_cparams(("parallel",)),
        name="out_proj",
    )(a, b, c, wa, wb, wc, ga, gc, x, gn)


def _mlp_kernel(h_ref, x_ref, w1_ref, w2_ref, gn_ref, *out_refs, final):
    acc_ref = out_refs[0]
    j = pl.program_id(1)

    @pl.when(j == 0)
    def _():
        acc_ref[...] = x_ref[...]

    u = jnp.dot(h_ref[...], w1_ref[...].astype(BF16), preferred_element_type=F32)
    act = jnp.square(jnp.maximum(u, 0.0)).astype(BF16)
    acc_ref[...] += jnp.dot(act, w2_ref[...].astype(BF16), preferred_element_type=F32)

    @pl.when(j == pl.num_programs(1) - 1)
    def _():
        y = _rms(acc_ref[...], gn_ref[...])
        if final:
            acc_ref[...] = y
        else:
            out_refs[1][...] = y.astype(out_refs[1].dtype)


def mlp_block(h, x, w1_all, w2_all, layer, gn, final, tm=1024, tf=512):
    s, d = x.shape
    ff = w1_all.shape[2]
    tm = min(tm, s)
    row = lambda i, j: (i, 0)
    once = pl.Buffered(1)
    out_specs = [pl.BlockSpec((tm, d), row)]
    out_shape = [jax.ShapeDtypeStruct((s, d), F32)]
    if not final:
        out_specs.append(pl.BlockSpec((tm, d), row))
        out_shape.append(jax.ShapeDtypeStruct((s, d), BF16))
    return pl.pallas_call(
        functools.partial(_mlp_kernel, final=final),
        grid=(s // tm, ff // tf),
        in_specs=[pl.BlockSpec((tm, d), row, pipeline_mode=once),
                  pl.BlockSpec((tm, d), row, pipeline_mode=once),
                  pl.BlockSpec((None, d, tf), lambda i, j: (layer, 0, j)),
                  pl.BlockSpec((None, tf, d), lambda i, j: (layer, j, 0)),
                  pl.BlockSpec((None, 1, d), lambda i, j: (layer, 0, 0))],
        out_specs=out_specs,
        out_shape=out_shape,
        compiler_params=_cparams(("parallel", "arbitrary")),
        name="mlp",
    )(h, x, w1_all, w2_all, gn)


def _head_rows(w, off, heads, width, padded):
    pieces = []
    for h in range(heads):
        pieces.append(w[..., off + h * width:off + (h + 1) * width, :])
        if padded > width:
            pieces.append(jnp.zeros(w.shape[:-2] + (padded - width, w.shape[-1]), w.dtype))
    return pieces


def _swa_perm_cols(w):
    return jnp.concatenate([w[..., HEAD_DIM * h:HEAD_DIM * (h + 1)] for h in SWA_Q_PERM], axis=-1)


def kernel(x, norm_mix, w_in, swa_sinks, rel_bias, gla_gate_w, gla_gate_b, gla_norm,
           swa_out_norm, sb_out_norm, w_out, norm_mlp, w_mlp_in, w_mlp_out, norm_final):
    depth = w_in.shape[0]
    xs = x[0]
    gla_w = GLA_HEADS * GLA_DV
    w_in_p = prep_w_in(jnp.swapaxes(w_in, 1, 2))
    wa = jnp.concatenate([w_out[:, HEAD_DIM * h:HEAD_DIM * (h + 1)] for h in SWA_Q_PERM],
                         axis=1).astype(BF16)
    wb = jnp.concatenate(_head_rows(w_out, SWA_WIDTH, GLA_HEADS, GLA_DV, GLA_DV_PAD),
                         axis=1).astype(BF16)
    wc = w_out[:, SWA_WIDTH + gla_w:].astype(BF16)
    ga = _swa_perm_cols(swa_out_norm)[:, None, :]
    gc = sb_out_norm[:, None, :]
    g_mlp = norm_mlp[:, None, :]
    g_next = jnp.concatenate([norm_mix[1:], norm_final[None]], axis=0)[:, None, :]
    sinks_p = jnp.concatenate([swa_sinks[:, h:h + 1] for h in SWA_Q_PERM], axis=1)
    bias_tab = swa_bias_table(rel_bias)
    gw = jnp.stack([jnp.pad(gla_gate_w[:, :, GLA_DK * h:GLA_DK * (h + 1)],
                            ((0, 0), (0, GLA_LR_PAD - GLA_LOWRANK), (0, GLA_DK_PAD - GLA_DK)))
                    for h in range(GLA_HEADS)], axis=1).astype(BF16)
    gb = jnp.stack([jnp.pad(gla_gate_b[:, GLA_DK * h:GLA_DK * (h + 1)],
                            ((0, 0), (0, GLA_DK_PAD - GLA_DK)))
                    for h in range(GLA_HEADS)], axis=1)[:, :, None, :]
    gn = jnp.pad(gla_norm, ((0, 0), (0, GLA_DV_PAD - GLA_DV)))[:, None, :]

    h = None
    for l in range(depth):
        if l == 0:
            proj = in_projection(xs, w_in_p, l, gain=norm_mix[0])
        else:
            proj = in_projection(h, w_in_p, l)
        a = swa_attention(proj, sinks_p, bias_tab, l)
        b = gla_attention(proj, gw, gb, gn, l)
        c = sb_attention(proj)
        xs, hm = out_projection(a, b, c, wa, wb, wc, ga, gc, xs, g_mlp, l)
        if l + 1 < depth:
            xs, h = mlp_block(hm, xs, w_mlp_in, w_mlp_out, l, g_next, final=False)
        else:
            (out,) = mlp_block(hm, xs, w_mlp_in, w_mlp_out, l, g_next, final=True)
    return out[None]
```

```python
import functools
import math

import numpy as np
import jax
import jax.numpy as jnp
from jax import lax
from jax.experimental import pallas as pl
from jax.experimental.pallas import tpu as pltpu

F32 = jnp.float32
BF16 = jnp.bfloat16

HEAD_DIM = 64
SWA_HEADS = 12
SWA_KV_HEADS = 4
SWA_WIDTH = SWA_HEADS * HEAD_DIM
SWA_KV_WIDTH = SWA_KV_HEADS * HEAD_DIM
SWA_BLOCK = 128
WINDOW = 128
REL_BUCKETS = 32
REL_MAX_DIST = 128
GLA_HEADS = 4
GLA_DV = 192
GLA_DK = 96
GLA_DK_PAD = 128
GLA_DV_PAD = 256
GLA_LOWRANK = 16
GLA_LR_PAD = 128
GLA_CHUNK = 64
GATE_NORMALIZER = 16.0
GATE_LOG_MIN = -1.0
SB_HEADS = 8
SB_WIDTH = SB_HEADS * HEAD_DIM
RMS_EPS = 1e-6
LOG2E = 1.4426950408889634
NEG_INF = -1e30

LANES = 128
VMEM_LIMIT = 56 * 1024 * 1024

OFF_BV = 0
OFF_BR = 1024
OFF_BQ = 2048
OFF_BK = 2560
OFF_CQ = 3072
OFF_CK = 3584
OFF_CV = 4096
OFF_AQ = 4608
OFF_AK = 5376
OFF_AV = 5632
OFF_BLR = 5888
PROJ_PAD_WIDTH = 6144

SWA_Q_PERM = (0, 3, 1, 4, 2, 5, 6, 9, 7, 10, 8, 11)

_O_AQ, _O_AK, _O_AV = 0, 768, 1024
_O_BQ, _O_BK, _O_BV, _O_BR, _O_BLR = 1280, 1664, 2048, 2816, 3584
_O_CQ, _O_CK, _O_CV = 3600, 4112, 4624


def _cparams(sem):
    return pltpu.CompilerParams(dimension_semantics=sem, vmem_limit_bytes=VMEM_LIMIT)


def _rms(x, gain):
    ms = jnp.mean(x * x, axis=-1, keepdims=True)
    return x * lax.rsqrt(ms + RMS_EPS) * gain


def _split_bf16(x):
    hi = x.astype(BF16)
    lo = (x - hi.astype(F32)).astype(BF16)
    return hi, lo


_NT_DIMS = (((1,), (1,)), ((), ()))


def _matmul_kernel(h_ref, w_ref, o_ref):
    o_ref[...] = lax.dot_general(h_ref[...], w_ref[...], _NT_DIMS,
                                 preferred_element_type=F32).astype(o_ref.dtype)


def _norm_matmul_kernel(x_ref, g_ref, w_ref, o_ref):
    h = _rms(x_ref[...], g_ref[...]).astype(BF16)
    o_ref[...] = lax.dot_general(h, w_ref[...], _NT_DIMS,
                                 preferred_element_type=F32).astype(o_ref.dtype)


def in_projection(h, w_all, layer, gain=None, tm=1024, tn=1024):
    s, d = h.shape
    n = w_all.shape[1]
    tm = min(tm, s)
    w_spec = pl.BlockSpec((None, tn, d), lambda i, j: (layer, j, 0))
    h_spec = pl.BlockSpec((tm, d), lambda i, j: (i, 0))
    common = dict(
        grid=(s // tm, n // tn),
        out_specs=pl.BlockSpec((tm, tn), lambda i, j: (i, j)),
        out_shape=jax.ShapeDtypeStruct((s, n), BF16),
        compiler_params=_cparams(("parallel", "arbitrary")),
        name="in_proj",
    )
    if gain is None:
        return pl.pallas_call(_matmul_kernel, in_specs=[h_spec, w_spec], **common)(h, w_all)
    return pl.pallas_call(
        _norm_matmul_kernel,
        in_specs=[h_spec, pl.BlockSpec((1, d), lambda i, j: (0, 0)), w_spec],
        **common,
    )(h, gain.reshape(1, d), w_all)


def _w_in_moves():
    scale = HEAD_DIM ** -0.5
    moves = []
    for h in range(GLA_HEADS):
        moves.append((_O_BV + GLA_DV * h, OFF_BV + GLA_DV_PAD * h, GLA_DV, 1.0))
        moves.append((_O_BR + GLA_DV * h, OFF_BR + GLA_DV_PAD * h, GLA_DV, 1.0))
        moves.append((_O_BQ + GLA_DK * h, OFF_BQ + GLA_DK_PAD * h, GLA_DK, 1.0))
        moves.append((_O_BK + GLA_DK * h, OFF_BK + GLA_DK_PAD * h, GLA_DK, 1.0))
    moves += [(_O_CQ, OFF_CQ, SB_WIDTH, scale), (_O_CK, OFF_CK, SB_WIDTH, 1.0),
              (_O_CV, OFF_CV, SB_WIDTH, 1.0)]
    for j, h in enumerate(SWA_Q_PERM):
        moves.append((_O_AQ + HEAD_DIM * h, OFF_AQ + HEAD_DIM * j, HEAD_DIM, scale))
    moves += [(_O_AK, OFF_AK, SWA_KV_WIDTH, 1.0), (_O_AV, OFF_AV, SWA_KV_WIDTH, 1.0),
              (_O_BLR, OFF_BLR, GLA_LOWRANK, 1.0)]
    return moves


def _w_in_prep_kernel(w_ref, o_ref):
    o_ref[...] = jnp.zeros_like(o_ref)
    for src, dst, width, scale in _w_in_moves():
        piece = w_ref[src:src + width, :]
        if scale != 1.0:
            piece = piece * scale
        o_ref[dst:dst + width, :] = piece.astype(o_ref.dtype)


def prep_w_in(w_in_t, tc=256):
    layers, n, d = w_in_t.shape
    return pl.pallas_call(
        _w_in_prep_kernel,
        grid=(layers, d // tc),
        in_specs=[pl.BlockSpec((None, n, tc), lambda l, i: (l, 0, i))],
        out_specs=pl.BlockSpec((None, PROJ_PAD_WIDTH, tc), lambda l, i: (l, 0, i)),
        out_shape=jax.ShapeDtypeStruct((layers, PROJ_PAD_WIDTH, d), BF16),
        compiler_params=_cparams(("parallel", "parallel")),
        name="w_in_prep",
    )(w_in_t)


def _swa_kernel(sink_ref, q_ref, kp_ref, kc_ref, vp_ref, vc_ref, bias_ref, o_ref, *, layer):
    blk = SWA_BLOCK
    nb = SWA_HEADS // 2
    k = jnp.concatenate([kp_ref[...], kc_ref[...]], axis=0)
    v = jnp.concatenate([vp_ref[...], vc_ref[...]], axis=0)
    lane = lax.broadcasted_iota(jnp.int32, (blk, LANES), 1)
    low = lane < HEAD_DIM
    top = lax.broadcasted_iota(jnp.int32, (2 * blk, 1), 0) < blk
    ones = jnp.ones((2 * blk, LANES), BF16)
    nt = (((1,), (1,)), ((), ()))

    scores = []
    for b in range(nb):
        qb = q_ref[:, LANES * b:LANES * (b + 1)]
        zero = jnp.zeros_like(qb)
        lhs = jnp.concatenate([jnp.where(low, qb, zero), jnp.where(low, zero, qb)], axis=0)
        kb = k[:, LANES * (b // 3):LANES * (b // 3 + 1)]
        scores.append(lax.dot_general(lhs, kb, nt, preferred_element_type=F32) + bias_ref[b])
    probs, sink_term = [], []
    for b in range(nb):
        sink = jnp.where(top, sink_ref[layer, 2 * b], sink_ref[layer, 2 * b + 1])
        m = jnp.maximum(jnp.max(scores[b], axis=1, keepdims=True), sink)
        probs.append(jnp.exp(scores[b] - m).astype(BF16))
        sink_term.append(jnp.exp(sink - m))
    for b in range(nb):
        vb = v[:, LANES * (b // 3):LANES * (b // 3 + 1)]
        pv = jnp.dot(probs[b], jnp.concatenate([vb, ones], axis=1),
                     preferred_element_type=F32)
        o = pv[:, :LANES] / (pv[:, LANES:] + sink_term[b])
        o_ref[:, LANES * b:LANES * (b + 1)] = jnp.where(low, o[:blk], o[blk:]).astype(o_ref.dtype)


def swa_attention(proj, sinks_perm, bias_tab, layer):
    s = proj.shape[0]
    blk = SWA_BLOCK
    qi = OFF_AQ // SWA_WIDTH
    ki = OFF_AK // SWA_KV_WIDTH
    vi = OFF_AV // SWA_KV_WIDTH
    prev = lambda i: jnp.maximum(i - 1, 0)
    return pl.pallas_call(
        functools.partial(_swa_kernel, layer=layer),
        grid=(s // blk,),
        in_specs=[pl.BlockSpec(memory_space=pltpu.SMEM),
                  pl.BlockSpec((blk, SWA_WIDTH), lambda i: (i, qi)),
                  pl.BlockSpec((blk, SWA_KV_WIDTH), lambda i: (prev(i), ki)),
                  pl.BlockSpec((blk, SWA_KV_WIDTH), lambda i: (i, ki)),
                  pl.BlockSpec((blk, SWA_KV_WIDTH), lambda i: (prev(i), vi)),
                  pl.BlockSpec((blk, SWA_KV_WIDTH), lambda i: (i, vi)),
                  pl.BlockSpec((None, SWA_HEADS // 2, 2 * blk, 2 * blk),
                               lambda i: (jnp.minimum(i, 1), 0, 0, 0))],
        out_specs=pl.BlockSpec((blk, SWA_WIDTH), lambda i: (i, 0)),
        out_shape=jax.ShapeDtypeStruct((s, SWA_WIDTH), BF16),
        compiler_params=_cparams(("parallel",)),
        name="swa",
    )(sinks_perm, proj, proj, proj, proj, proj, bias_tab)


def _t5_causal_bucket(dist):
    max_exact = REL_BUCKETS // 2
    is_small = dist < max_exact
    ratio = (jnp.log(jnp.maximum(dist, 1).astype(F32) / max_exact)
             / math.log(REL_MAX_DIST / max_exact))
    large = max_exact + (ratio * (REL_BUCKETS - max_exact)).astype(jnp.int32)
    large = jnp.minimum(large, REL_BUCKETS - 1)
    return jnp.where(is_small, dist, large)


def swa_bias_table(rel_bias):
    blk = SWA_BLOCK
    qpos = jnp.arange(blk) + blk
    kpos = jnp.arange(2 * blk)
    dist = qpos[:, None] - kpos[None, :]
    in_window = (dist >= 0) & (dist < WINDOW)
    bucket = _t5_causal_bucket(jnp.maximum(dist, 0))
    table = rel_bias.astype(F32)
    table = jnp.stack([table[:, h] for h in SWA_Q_PERM])
    bias = jnp.full((SWA_HEADS, blk, 2 * blk), NEG_INF, F32)
    for bkt in range(REL_BUCKETS):
        hit = (in_window & (bucket == bkt))[None]
        bias = jnp.where(hit, table[:, bkt][:, None, None], bias)
    bias = bias.reshape(SWA_HEADS // 2, 2 * blk, 2 * blk)
    first = jnp.where((kpos < blk)[None, None, :], NEG_INF, bias)
    return jnp.stack([first, bias])


def _gla_kernel(q_ref, k_ref, v_ref, r_ref, lr_ref, gw_ref, gb_ref, gn_ref, lm_ref, o_ref,
                st_ref, *, rows, hps):
    t = pl.program_id(1)
    ch = GLA_CHUNK
    nch = rows // ch
    heads = range(hps)

    @pl.when(t == 0)
    def _():
        st_ref[...] = jnp.zeros_like(st_ref)

    nt = (((1,), (1,)), ((), ()))
    tn = (((0,), (0,)), ((), ()))
    rowi = lax.broadcasted_iota(jnp.int32, (rows, rows), 0)
    coli = lax.broadcasted_iota(jnp.int32, (rows, rows), 1)
    same_chunk_causal = jnp.logical_and(coli <= rowi, (rowi // ch) == (coli // ch))
    lm = lm_ref[...]
    lr = lr_ref[...]
    gn = gn_ref[...]

    gp = [jnp.dot(lr, gw_ref[h], preferred_element_type=F32) + gb_ref[h] for h in heads]
    lg = []
    for h in heads:
        log_sig = jnp.minimum(gp[h], 0.0) - jnp.log(1.0 + jnp.exp(-jnp.abs(gp[h])))
        lg.append(jnp.maximum(log_sig * (1.0 / GATE_NORMALIZER), GATE_LOG_MIN))
    bb = []
    for h in heads:
        hi, lo = _split_bf16(lg[h])
        bb.append(jnp.dot(lm, hi, preferred_element_type=F32)
                  + jnp.dot(lm, lo, preferred_element_type=F32))
    q_dec, k_inv, k_end, dec = [], [], [], []
    for h in heads:
        b, b_last = bb[h][:rows], bb[h][rows:]
        q = q_ref[:, GLA_DK_PAD * h:GLA_DK_PAD * (h + 1)].astype(F32) * (GLA_DK ** -0.5)
        kk = k_ref[:, GLA_DK_PAD * h:GLA_DK_PAD * (h + 1)].astype(F32)
        q_dec.append((q * jnp.exp(b)).astype(BF16))
        k_inv.append((kk * jnp.exp(-b)).astype(BF16))
        k_end.append((kk * jnp.exp(b_last - b)).astype(BF16))
        dec.append(jnp.exp(b_last))
    v = [v_ref[:, GLA_DV_PAD * h:GLA_DV_PAD * (h + 1)] for h in heads]
    sc = []
    for h in heads:
        s_h = lax.dot_general(q_dec[h], k_inv[h], nt, preferred_element_type=F32)
        sc.append(jnp.where(same_chunk_causal, s_h, 0.0).astype(BF16))
    o_intra = [jnp.dot(sc[h], v[h], preferred_element_type=F32) for h in heads]
    d_st = [[lax.dot_general(v[h][c * ch:(c + 1) * ch], k_end[h][c * ch:(c + 1) * ch], tn,
                             preferred_element_type=F32) for c in range(nch)] for h in heads]
    st_in = []
    for h in heads:
        st = st_ref[h]
        states = []
        for c in range(nch):
            states.append(st.astype(BF16))
            st = st * dec[h][c * ch:c * ch + 1, :] + d_st[h][c]
        st_ref[h] = st
        st_in.append(states)
    for h in heads:
        o_inter = [lax.dot_general(q_dec[h][c * ch:(c + 1) * ch], st_in[h][c], nt,
                                   preferred_element_type=F32) for c in range(nch)]
        o = o_intra[h] + jnp.concatenate(o_inter, axis=0)
        ms = jnp.sum(o * o, axis=1, keepdims=True) * (1.0 / GLA_DV)
        y = o * lax.rsqrt(ms + RMS_EPS) * gn
        rr = r_ref[:, GLA_DV_PAD * h:GLA_DV_PAD * (h + 1)].astype(F32)
        o_ref[:, GLA_DV_PAD * h:GLA_DV_PAD * (h + 1)] = (
            y * (rr / (1.0 + jnp.exp(-rr)))).astype(o_ref.dtype)


def gla_attention(proj, gw_pad, gb_pad, gn_pad, layer, rows=256, hps=4):
    s = proj.shape[0]
    rows = min(rows, s)
    ch = GLA_CHUNK
    dkw, dvw = hps * GLA_DK_PAD, hps * GLA_DV_PAD
    qi, ki = OFF_BQ // dkw, OFF_BK // dkw
    vi, ri = OFF_BV // dvw, OFF_BR // dvw
    li = OFF_BLR // GLA_LR_PAD
    idx = np.arange(rows)
    same = (idx[:, None] // ch) == (idx[None, :] // ch)
    prefix = same & (idx[None, :] <= idx[:, None])
    lm = jnp.asarray(np.concatenate([prefix, same], axis=0).astype(np.float32), dtype=BF16)
    return pl.pallas_call(
        functools.partial(_gla_kernel, rows=rows, hps=hps),
        grid=(GLA_HEADS // hps, s // rows),
        in_specs=[pl.BlockSpec((rows, dkw), lambda g, t: (t, qi + g)),
                  pl.BlockSpec((rows, dkw), lambda g, t: (t, ki + g)),
                  pl.BlockSpec((rows, dvw), lambda g, t: (t, vi + g)),
                  pl.BlockSpec((rows, dvw), lambda g, t: (t, ri + g)),
                  pl.BlockSpec((rows, GLA_LR_PAD), lambda g, t: (t, li)),
                  pl.BlockSpec((None, hps, GLA_LR_PAD, GLA_DK_PAD),
                               lambda g, t: (layer, g, 0, 0)),
                  pl.BlockSpec((None, hps, 1, GLA_DK_PAD), lambda g, t: (layer, g, 0, 0)),
                  pl.BlockSpec((None, 1, GLA_DV_PAD), lambda g, t: (layer, 0, 0)),
                  pl.BlockSpec((2 * rows, rows), lambda g, t: (0, 0))],
        out_specs=pl.BlockSpec((rows, dvw), lambda g, t: (t, g)),
        out_shape=jax.ShapeDtypeStruct((s, GLA_HEADS * GLA_DV_PAD), BF16),
        scratch_shapes=[pltpu.VMEM((hps, GLA_DV_PAD, GLA_DK_PAD), F32)],
        compiler_params=_cparams(("parallel", "arbitrary")),
        name="gla",
    )(proj, proj, proj, proj, proj, gw_pad, gb_pad, gn_pad, lm)


def _sb_kernel(q_ref, k_ref, v_ref, u_ref, o_ref, acc_ref, car_ref, qm_ref, vm_ref,
               z0_ref, z1_ref, z2_ref, z3_ref, nl0_ref, nl1_ref, w0_ref, w1_ref,
               *, bq, nblk, unroll):
    lane = lax.broadcasted_iota(jnp.int32, (bq, LANES), 1)
    low = lane < HEAD_DIM
    nt = (((1,), (1,)), ((), ()))
    z_ref = (z0_ref, z1_ref, z2_ref, z3_ref)
    nl_ref, w_ref = (nl0_ref, nl1_ref), (w0_ref, w1_ref)

    def rows(blk):
        if isinstance(blk, int):
            return pl.ds(blk * bq, bq)
        return pl.ds(pl.multiple_of(blk * bq, bq), bq)

    def strict_mask():
        rowi = lax.broadcasted_iota(jnp.int32, (2 * bq, bq), 0)
        coli = lax.broadcasted_iota(jnp.int32, (2 * bq, bq), 1)
        return coli < (rowi & (bq - 1))

    def s1a(item, tick4):
        i, c = item
        z_ref[tick4][...] = lax.dot_general(qm_ref[i], k_ref[rows(c), :], nt,
                                            preferred_element_type=F32)

    def s1b(tick4, masked):
        z = z_ref[tick4][...].astype(BF16)
        nl = jnp.maximum(z, 0.0) + jnp.log(1.0 + jnp.exp2(jnp.abs(z) * (-LOG2E)))
        if masked:
            nl = jnp.where(strict_mask(), nl, jnp.zeros_like(nl))
        nl_ref[tick4 % 2][...] = nl

    def s2(item, tick4, masked, first):
        i, _ = item
        cs = jnp.dot(nl_ref[tick4 % 2][...], u_ref[...], preferred_element_type=F32)
        tot = jnp.broadcast_to(cs[:, 0:1], (2 * bq, LANES))
        if first:
            arg = z_ref[tick4][...] - cs
            car_ref[i] = tot
        else:
            car = car_ref[i]
            arg = z_ref[tick4][...] - cs - jnp.concatenate([car, car], axis=1)
            car_ref[i] = car + tot
        w = jnp.exp(arg)
        if masked:
            w = jnp.where(strict_mask(), w, 0.0)
        w = w.astype(BF16)
        w_ref[tick4 % 2][:, :bq] = w[:bq]
        w_ref[tick4 % 2][:, bq:] = w[bq:]

    def s3(item, tick4, first):
        i, c = item
        pv = jnp.dot(w_ref[tick4 % 2][...], vm_ref[c], preferred_element_type=F32)
        if first:
            acc_ref[rows(i), :] = pv
        else:
            acc_ref[rows(i), :] += pv

    def run_pipeline(items, step, masked, first):
        n = len(items)

        def tick(t4, its, do):
            if do[1]:
                s1b((t4 - 1) % 4, masked)
            if do[0]:
                s1a(its[0], t4)
            if do[2]:
                s2(its[2], (t4 - 2) % 4, masked, first)
            if do[3]:
                s3(its[3], (t4 - 3) % 4, first)

        def static_tick(t):
            do = [0 <= t - j < n for j in range(4)]
            its = [items[t - j] if do[j] else None for j in range(4)]
            tick(t % 4, its, do)

        lo_t, hi_t = 3, n - 1
        groups = max(hi_t - lo_t + 1, 0) // unroll
        if groups < 2:
            groups = 0
        for t in range(min(lo_t, n + 3)):
            static_tick(t)
        if groups:
            def body(_, carry):
                its = [(carry[2 * j], carry[2 * j + 1]) for j in range(4)]
                for k in range(unroll):
                    tick((lo_t + k) % 4, its, [True] * 4)
                    its = [step(*its[0])] + its[:3]
                return tuple(x for it in its for x in it)

            init = tuple(jnp.int32(x) for j in range(4) for x in items[lo_t - j])
            lax.fori_loop(0, groups, body, init)
        for t in range(lo_t + unroll * groups, n + 3):
            static_tick(t)

    for blk in range(nblk):
        r = rows(blk)
        qb, vb = q_ref[r, :], v_ref[r, :]
        zero = jnp.zeros_like(qb)
        qm_ref[blk, :bq] = jnp.where(low, qb, zero)
        qm_ref[blk, bq:] = jnp.where(low, zero, qb)
        vm_ref[blk, :bq] = jnp.where(low, vb, zero)
        vm_ref[blk, bq:] = jnp.where(low, zero, vb)

    diag = [(i, i) for i in range(nblk)]
    run_pipeline(diag, lambda i, c: (i + 1, c + 1), masked=True, first=True)

    off = [(i, c) for i in range(1, nblk) for c in range(i - 1, -1, -1)]

    def off_step(i, c):
        wrap = c == 0
        i2 = jnp.where(wrap, i + 1, i)
        return i2, jnp.where(wrap, i2 - 1, c - 1)

    if off:
        run_pipeline(off, off_step, masked=False, first=False)

    o_ref[...] = acc_ref[...].astype(o_ref.dtype)


def sb_attention(proj, bq=256, unroll=16):
    s = proj.shape[0]
    bq = min(bq, s)
    nblk = s // bq
    assert bq & (bq - 1) == 0 and unroll % 4 == 0
    qi = OFF_CQ // LANES
    ki = OFF_CK // LANES
    vi = OFF_CV // LANES
    u = (np.arange(bq)[:, None] >= np.arange(bq)[None, :]).astype(np.float32)
    u = jnp.asarray(u, dtype=BF16)
    slot = lambda shape, dtype: pltpu.VMEM(shape, dtype)
    return pl.pallas_call(
        functools.partial(_sb_kernel, bq=bq, nblk=nblk, unroll=unroll),
        grid=(SB_HEADS // 2,),
        in_specs=[pl.BlockSpec((s, LANES), lambda p: (0, qi + p)),
                  pl.BlockSpec((s, LANES), lambda p: (0, ki + p)),
                  pl.BlockSpec((s, LANES), lambda p: (0, vi + p)),
                  pl.BlockSpec((bq, bq), lambda p: (0, 0))],
        out_specs=pl.BlockSpec((s, LANES), lambda p: (0, p)),
        out_shape=jax.ShapeDtypeStruct((s, SB_WIDTH), BF16),
        scratch_shapes=[pltpu.VMEM((s, LANES), F32),
                        pltpu.VMEM((nblk, 2 * bq, LANES), F32),
                        pltpu.VMEM((nblk, 2 * bq, LANES), BF16),
                        pltpu.VMEM((nblk, 2 * bq, LANES), BF16)]
                       + [slot((2 * bq, bq), F32)] * 4
                       + [slot((2 * bq, bq), BF16)] * 2
                       + [slot((bq, 2 * bq), BF16)] * 2,
        compiler_params=_cparams(("parallel",)),
        name="stickbreak",
    )(proj, proj, proj, u)


def _oproj_kernel(a_ref, b_ref, c_ref, wa_ref, wb_ref, wc_ref, ga_ref, gc_ref,
                  x_ref, gn_ref, xo_ref, ho_ref):
    an = _rms(a_ref[...].astype(F32), ga_ref[...]).astype(BF16)
    cn = _rms(c_ref[...].astype(F32), gc_ref[...]).astype(BF16)
    y = (jnp.dot(an, wa_ref[...], preferred_element_type=F32)
         + jnp.dot(b_ref[...], wb_ref[...], preferred_element_type=F32)
         + jnp.dot(cn, wc_ref[...], preferred_element_type=F32))
    xn = x_ref[...] + y
    xo_ref[...] = xn
    ho_ref[...] = _rms(xn, gn_ref[...]).astype(ho_ref.dtype)


def out_projection(a, b, c, wa, wb, wc, ga, gc, x, gn, layer, tm=512):
    s, d = x.shape
    tm = min(tm, s)
    row = lambda i: (i, 0)
    fixed = lambda i: (layer, 0, 0)
    per_layer = lambda arr: pl.BlockSpec((None,) + arr.shape[1:], fixed)
    return pl.pallas_call(
        _oproj_kernel,
        grid=(s // tm,),
        in_specs=[pl.BlockSpec((tm, a.shape[1]), row),
                  pl.BlockSpec((tm, b.shape[1]), row),
                  pl.BlockSpec((tm, c.shape[1]), row),
                  per_layer(wa), per_layer(wb), per_layer(wc), per_layer(ga), per_layer(gc),
                  pl.BlockSpec((tm, d), row),
                  per_layer(gn)],
        out_specs=[pl.BlockSpec((tm, d), row), pl.BlockSpec((tm, d), row)],
        out_shape=[jax.ShapeDtypeStruct((s, d), F32), jax.ShapeDtypeStruct((s, d), BF16)],
        compiler_params=_cparams(("parallel",)),
        name="out_proj",
    )(a, b, c, wa, wb, wc, ga, gc, x, gn)


def _mlp_kernel(h_ref, x_ref, w1_ref, w2_ref, gn_ref, *out_refs, final):
    acc_ref = out_refs[0]
    j = pl.program_id(1)

    @pl.when(j == 0)
    def _():
        acc_ref[...] = x_ref[...]

    u = jnp.dot(h_ref[...], w1_ref[...].astype(BF16), preferred_element_type=F32)
    act = jnp.square(jnp.maximum(u, 0.0)).astype(BF16)
    acc_ref[...] += jnp.dot(act, w2_ref[...].astype(BF16), preferred_element_type=F32)

    @pl.when(j == pl.num_programs(1) - 1)
    def _():
        y = _rms(acc_ref[...], gn_ref[...])
        if final:
            acc_ref[...] = y
        else:
            out_refs[1][...] = y.astype(out_refs[1].dtype)


def mlp_block(h, x, w1_all, w2_all, layer, gn, final, tm=1024, tf=512):
    s, d = x.shape
    ff = w1_all.shape[2]
    tm = min(tm, s)
    row = lambda i, j: (i, 0)
    once = pl.Buffered(1)
    out_specs = [pl.BlockSpec((tm, d), row)]
    out_shape = [jax.ShapeDtypeStruct((s, d), F32)]
    if not final:
        out_specs.append(pl.BlockSpec((tm, d), row))
        out_shape.append(jax.ShapeDtypeStruct((s, d), BF16))
    return pl.pallas_call(
        functools.partial(_mlp_kernel, final=final),
        grid=(s // tm, ff // tf),
        in_specs=[pl.BlockSpec((tm, d), row, pipeline_mode=once),
                  pl.BlockSpec((tm, d), row, pipeline_mode=once),
                  pl.BlockSpec((None, d, tf), lambda i, j: (layer, 0, j)),
                  pl.BlockSpec((None, tf, d), lambda i, j: (layer, j, 0)),
                  pl.BlockSpec((None, 1, d), lambda i, j: (layer, 0, 0))],
        out_specs=out_specs,
        out_shape=out_shape,
        compiler_params=_cparams(("parallel", "arbitrary")),
        name="mlp",
    )(h, x, w1_all, w2_all, gn)


def _head_rows(w, off, heads, width, padded):
    pieces = []
    for h in range(heads):
        pieces.append(w[..., off + h * width:off + (h + 1) * width, :])
        if padded > width:
            pieces.append(jnp.zeros(w.shape[:-2] + (padded - width, w.shape[-1]), w.dtype))
    return pieces


def _swa_perm_cols(w):
    return jnp.concatenate([w[..., HEAD_DIM * h:HEAD_DIM * (h + 1)] for h in SWA_Q_PERM], axis=-1)


def kernel(x, norm_mix, w_in, swa_sinks, rel_bias, gla_gate_w, gla_gate_b, gla_norm,
           swa_out_norm, sb_out_norm, w_out, norm_mlp, w_mlp_in, w_mlp_out, norm_final):
    depth = w_in.shape[0]
    xs = x[0]
    gla_w = GLA_HEADS * GLA_DV
    w_in_p = prep_w_in(jnp.swapaxes(w_in, 1, 2))
    wa = jnp.concatenate([w_out[:, HEAD_DIM * h:HEAD_DIM * (h + 1)] for h in SWA_Q_PERM],
                         axis=1).astype(BF16)
    wb = jnp.concatenate(_head_rows(w_out, SWA_WIDTH, GLA_HEADS, GLA_DV, GLA_DV_PAD),
                         axis=1).astype(BF16)
    wc = w_out[:, SWA_WIDTH + gla_w:].astype(BF16)
    ga = _swa_perm_cols(swa_out_norm)[:, None, :]
    gc = sb_out_norm[:, None, :]
    g_mlp = norm_mlp[:, None, :]
    g_next = jnp.concatenate([norm_mix[1:], norm_final[None]], axis=0)[:, None, :]
    sinks_p = jnp.concatenate([swa_sinks[:, h:h + 1] for h in SWA_Q_PERM], axis=1)
    bias_tab = swa_bias_table(rel_bias)
    gw = jnp.stack([jnp.pad(gla_gate_w[:, :, GLA_DK * h:GLA_DK * (h + 1)],
                            ((0, 0), (0, GLA_LR_PAD - GLA_LOWRANK), (0, GLA_DK_PAD - GLA_DK)))
                    for h in range(GLA_HEADS)], axis=1).astype(BF16)
    gb = jnp.stack([jnp.pad(gla_gate_b[:, GLA_DK * h:GLA_DK * (h + 1)],
                            ((0, 0), (0, GLA_DK_PAD - GLA_DK)))
                    for h in range(GLA_HEADS)], axis=1)[:, :, None, :]
    gn = jnp.pad(gla_norm, ((0, 0), (0, GLA_DV_PAD - GLA_DV)))[:, None, :]

    h = None
    for l in range(depth):
        if l == 0:
            proj = in_projection(xs, w_in_p, l, gain=norm_mix[0])
        else:
            proj = in_projection(h, w_in_p, l)
        a = swa_attention(proj, sinks_p, bias_tab, l)
        b = gla_attention(proj, gw, gb, gn, l)
        c = sb_attention(proj)
        xs, hm = out_projection(a, b, c, wa, wb, wc, ga, gc, xs, g_mlp, l)
        if l + 1 < depth:
            xs, h = mlp_block(hm, xs, w_mlp_in, w_mlp_out, l, g_next, final=False)
        else:
            (out,) = mlp_block(hm, xs, w_mlp_in, w_mlp_out, l, g_next, final=True)
    return out[None]
```

```python
import functools
import math

import numpy as np
import jax
import jax.numpy as jnp
from jax import lax
from jax.experimental import pallas as pl
from jax.experimental.pallas import tpu as pltpu

F32 = jnp.float32
BF16 = jnp.bfloat16

HEAD_DIM = 64
SWA_HEADS = 12
SWA_KV_HEADS = 4
SWA_WIDTH = SWA_HEADS * HEAD_DIM
SWA_KV_WIDTH = SWA_KV_HEADS * HEAD_DIM
SWA_BLOCK = 128
WINDOW = 128
REL_BUCKETS = 32
REL_MAX_DIST = 128
GLA_HEADS = 4
GLA_DV = 192
GLA_DK = 96
GLA_DK_PAD = 128
GLA_DV_PAD = 256
GLA_LOWRANK = 16
GLA_LR_PAD = 128
GLA_CHUNK = 64
GATE_NORMALIZER = 16.0
GATE_LOG_MIN = -1.0
SB_HEADS = 8
SB_WIDTH = SB_HEADS * HEAD_DIM
RMS_EPS = 1e-6
LOG2E = 1.4426950408889634
NEG_INF = -1e30

LANES = 128
VMEM_LIMIT = 56 * 1024 * 1024

OFF_BV = 0
OFF_BR = 1024
OFF_BQ = 2048
OFF_BK = 2560
OFF_CQ = 3072
OFF_CK = 3584
OFF_CV = 4096
OFF_AQ = 4608
OFF_AK = 5376
OFF_AV = 5632
OFF_BLR = 5888
PROJ_PAD_WIDTH = 6144

SWA_Q_PERM = (0, 3, 1, 4, 2, 5, 6, 9, 7, 10, 8, 11)

_O_AQ, _O_AK, _O_AV = 0, 768, 1024
_O_BQ, _O_BK, _O_BV, _O_BR, _O_BLR = 1280, 1664, 2048, 2816, 3584
_O_CQ, _O_CK, _O_CV = 3600, 4112, 4624


def _cparams(sem):
    return pltpu.CompilerParams(dimension_semantics=sem, vmem_limit_bytes=VMEM_LIMIT)


def _rms(x, gain):
    ms = jnp.mean(x * x, axis=-1, keepdims=True)
    return x * lax.rsqrt(ms + RMS_EPS) * gain


def _split_bf16(x):
    hi = x.astype(BF16)
    lo = (x - hi.astype(F32)).astype(BF16)
    return hi, lo


_NT_DIMS = (((1,), (1,)), ((), ()))


def _matmul_kernel(h_ref, w_ref, o_ref):
    o_ref[...] = lax.dot_general(h_ref[...], w_ref[...], _NT_DIMS,
                                 preferred_element_type=F32).astype(o_ref.dtype)


def _norm_matmul_kernel(x_ref, g_ref, w_ref, o_ref):
    h = _rms(x_ref[...], g_ref[...]).astype(BF16)
    o_ref[...] = lax.dot_general(h, w_ref[...], _NT_DIMS,
                                 preferred_element_type=F32).astype(o_ref.dtype)


def in_projection(h, w_all, layer, gain=None, tm=1024, tn=1024):
    s, d = h.shape
    n = w_all.shape[1]
    tm = min(tm, s)
    w_spec = pl.BlockSpec((None, tn, d), lambda i, j: (layer, j, 0))
    h_spec = pl.BlockSpec((tm, d), lambda i, j: (i, 0))
    common = dict(
        grid=(s // tm, n // tn),
        out_specs=pl.BlockSpec((tm, tn), lambda i, j: (i, j)),
        out_shape=jax.ShapeDtypeStruct((s, n), BF16),
        compiler_params=_cparams(("parallel", "arbitrary")),
        name="in_proj",
    )
    if gain is None:
        return pl.pallas_call(_matmul_kernel, in_specs=[h_spec, w_spec], **common)(h, w_all)
    return pl.pallas_call(
        _norm_matmul_kernel,
        in_specs=[h_spec, pl.BlockSpec((1, d), lambda i, j: (0, 0)), w_spec],
        **common,
    )(h, gain.reshape(1, d), w_all)


def _w_in_moves():
    scale = HEAD_DIM ** -0.5
    moves = []
    for h in range(GLA_HEADS):
        moves.append((_O_BV + GLA_DV * h, OFF_BV + GLA_DV_PAD * h, GLA_DV, 1.0))
        moves.append((_O_BR + GLA_DV * h, OFF_BR + GLA_DV_PAD * h, GLA_DV, 1.0))
        moves.append((_O_BQ + GLA_DK * h, OFF_BQ + GLA_DK_PAD * h, GLA_DK, 1.0))
        moves.append((_O_BK + GLA_DK * h, OFF_BK + GLA_DK_PAD * h, GLA_DK, 1.0))
    moves += [(_O_CQ, OFF_CQ, SB_WIDTH, scale), (_O_CK, OFF_CK, SB_WIDTH, 1.0),
              (_O_CV, OFF_CV, SB_WIDTH, 1.0)]
    for j, h in enumerate(SWA_Q_PERM):
        moves.append((_O_AQ + HEAD_DIM * h, OFF_AQ + HEAD_DIM * j, HEAD_DIM, scale))
    moves += [(_O_AK, OFF_AK, SWA_KV_WIDTH, 1.0), (_O_AV, OFF_AV, SWA_KV_WIDTH, 1.0),
              (_O_BLR, OFF_BLR, GLA_LOWRANK, 1.0)]
    return moves


def _w_in_prep_kernel(w_ref, o_ref):
    o_ref[...] = jnp.zeros_like(o_ref)
    for src, dst, width, scale in _w_in_moves():
        piece = w_ref[src:src + width, :]
        if scale != 1.0:
            piece = piece * scale
        o_ref[dst:dst + width, :] = piece.astype(o_ref.dtype)


def prep_w_in(w_in_t, tc=256):
    layers, n, d = w_in_t.shape
    return pl.pallas_call(
        _w_in_prep_kernel,
        grid=(layers, d // tc),
        in_specs=[pl.BlockSpec((None, n, tc), lambda l, i: (l, 0, i))],
        out_specs=pl.BlockSpec((None, PROJ_PAD_WIDTH, tc), lambda l, i: (l, 0, i)),
        out_shape=jax.ShapeDtypeStruct((layers, PROJ_PAD_WIDTH, d), BF16),
        compiler_params=_cparams(("parallel", "parallel")),
        name="w_in_prep",
    )(w_in_t)


def _swa_kernel(sink_ref, q_ref, kp_ref, kc_ref, vp_ref, vc_ref, bias_ref, o_ref, *, layer):
    blk = SWA_BLOCK
    nb = SWA_HEADS // 2
    k = jnp.concatenate([kp_ref[...], kc_ref[...]], axis=0)
    v = jnp.concatenate([vp_ref[...], vc_ref[...]], axis=0)
    lane = lax.broadcasted_iota(jnp.int32, (blk, LANES), 1)
    low = lane < HEAD_DIM
    top = lax.broadcasted_iota(jnp.int32, (2 * blk, 1), 0) < blk
    ones = jnp.ones((2 * blk, LANES), BF16)
    nt = (((1,), (1,)), ((), ()))

    scores = []
    for b in range(nb):
        qb = q_ref[:, LANES * b:LANES * (b + 1)]
        zero = jnp.zeros_like(qb)
        lhs = jnp.concatenate([jnp.where(low, qb, zero), jnp.where(low, zero, qb)], axis=0)
        kb = k[:, LANES * (b // 3):LANES * (b // 3 + 1)]
        scores.append(lax.dot_general(lhs, kb, nt, preferred_element_type=F32) + bias_ref[b])
    probs, sink_term = [], []
    for b in range(nb):
        sink = jnp.where(top, sink_ref[layer, 2 * b], sink_ref[layer, 2 * b + 1])
        m = jnp.maximum(jnp.max(scores[b], axis=1, keepdims=True), sink)
        probs.append(jnp.exp(scores[b] - m).astype(BF16))
        sink_term.append(jnp.exp(sink - m))
    for b in range(nb):
        vb = v[:, LANES * (b // 3):LANES * (b // 3 + 1)]
        pv = jnp.dot(probs[b], jnp.concatenate([vb, ones], axis=1),
                     preferred_element_type=F32)
        o = pv[:, :LANES] / (pv[:, LANES:] + sink_term[b])
        o_ref[:, LANES * b:LANES * (b + 1)] = jnp.where(low, o[:blk], o[blk:]).astype(o_ref.dtype)


def swa_attention(proj, sinks_perm, bias_tab, layer):
    s = proj.shape[0]
    blk = SWA_BLOCK
    qi = OFF_AQ // SWA_WIDTH
    ki = OFF_AK // SWA_KV_WIDTH
    vi = OFF_AV // SWA_KV_WIDTH
    prev = lambda i: jnp.maximum(i - 1, 0)
    return pl.pallas_call(
        functools.partial(_swa_kernel, layer=layer),
        grid=(s // blk,),
        in_specs=[pl.BlockSpec(memory_space=pltpu.SMEM),
                  pl.BlockSpec((blk, SWA_WIDTH), lambda i: (i, qi)),
                  pl.BlockSpec((blk, SWA_KV_WIDTH), lambda i: (prev(i), ki)),
                  pl.BlockSpec((blk, SWA_KV_WIDTH), lambda i: (i, ki)),
                  pl.BlockSpec((blk, SWA_KV_WIDTH), lambda i: (prev(i), vi)),
                  pl.BlockSpec((blk, SWA_KV_WIDTH), lambda i: (i, vi)),
                  pl.BlockSpec((None, SWA_HEADS // 2, 2 * blk, 2 * blk),
                               lambda i: (jnp.minimum(i, 1), 0, 0, 0))],
        out_specs=pl.BlockSpec((blk, SWA_WIDTH), lambda i: (i, 0)),
        out_shape=jax.ShapeDtypeStruct((s, SWA_WIDTH), BF16),
        compiler_params=_cparams(("parallel",)),
        name="swa",
    )(sinks_perm, proj, proj, proj, proj, proj, bias_tab)


def _t5_causal_bucket(dist):
    max_exact = REL_BUCKETS // 2
    is_small = dist < max_exact
    ratio = (jnp.log(jnp.maximum(dist, 1).astype(F32) / max_exact)
             / math.log(REL_MAX_DIST / max_exact))
    large = max_exact + (ratio * (REL_BUCKETS - max_exact)).astype(jnp.int32)
    large = jnp.minimum(large, REL_BUCKETS - 1)
    return jnp.where(is_small, dist, large)


def swa_bias_table(rel_bias):
    blk = SWA_BLOCK
    qpos = jnp.arange(blk) + blk
    kpos = jnp.arange(2 * blk)
    dist = qpos[:, None] - kpos[None, :]
    in_window = (dist >= 0) & (dist < WINDOW)
    bucket = _t5_causal_bucket(jnp.maximum(dist, 0))
    table = rel_bias.astype(F32)
    table = jnp.stack([table[:, h] for h in SWA_Q_PERM])
    bias = jnp.full((SWA_HEADS, blk, 2 * blk), NEG_INF, F32)
    for bkt in range(REL_BUCKETS):
        hit = (in_window & (bucket == bkt))[None]
        bias = jnp.where(hit, table[:, bkt][:, None, None], bias)
    bias = bias.reshape(SWA_HEADS // 2, 2 * blk, 2 * blk)
    first = jnp.where((kpos < blk)[None, None, :], NEG_INF, bias)
    return jnp.stack([first, bias])


def _gla_kernel(q_ref, k_ref, v_ref, r_ref, lr_ref, gw_ref, gb_ref, gn_ref, lm_ref, o_ref,
                st_ref, *, rows, hps):
    t = pl.program_id(1)
    ch = GLA_CHUNK
    nch = rows // ch
    heads = range(hps)

    @pl.when(t == 0)
    def _():
        st_ref[...] = jnp.zeros_like(st_ref)

    nt = (((1,), (1,)), ((), ()))
    tn = (((0,), (0,)), ((), ()))
    rowi = lax.broadcasted_iota(jnp.int32, (rows, rows), 0)
    coli = lax.broadcasted_iota(jnp.int32, (rows, rows), 1)
    same_chunk_causal = jnp.logical_and(coli <= rowi, (rowi // ch) == (coli // ch))
    lm = lm_ref[...]
    lr = lr_ref[...]
    gn = gn_ref[...]

    gp = [jnp.dot(lr, gw_ref[h], preferred_element_type=F32) + gb_ref[h] for h in heads]
    lg = []
    for h in heads:
        log_sig = jnp.minimum(gp[h], 0.0) - jnp.log(1.0 + jnp.exp(-jnp.abs(gp[h])))
        lg.append(jnp.maximum(log_sig * (1.0 / GATE_NORMALIZER), GATE_LOG_MIN))
    bb = []
    for h in heads:
        hi, lo = _split_bf16(lg[h])
        bb.append(jnp.dot(lm, hi, preferred_element_type=F32)
                  + jnp.dot(lm, lo, preferred_element_type=F32))
    q_dec, k_inv, k_end, dec = [], [], [], []
    for h in heads:
        b, b_last = bb[h][:rows], bb[h][rows:]
        q = q_ref[:, GLA_DK_PAD * h:GLA_DK_PAD * (h + 1)].astype(F32) * (GLA_DK ** -0.5)
        kk = k_ref[:, GLA_DK_PAD * h:GLA_DK_PAD * (h + 1)].astype(F32)
        q_dec.append((q * jnp.exp(b)).astype(BF16))
        k_inv.append((kk * jnp.exp(-b)).astype(BF16))
        k_end.append((kk * jnp.exp(b_last - b)).astype(BF16))
        dec.append(jnp.exp(b_last))
    v = [v_ref[:, GLA_DV_PAD * h:GLA_DV_PAD * (h + 1)] for h in heads]
    sc = []
    for h in heads:
        s_h = lax.dot_general(q_dec[h], k_inv[h], nt, preferred_element_type=F32)
        sc.append(jnp.where(same_chunk_causal, s_h, 0.0).astype(BF16))
    o_intra = [jnp.dot(sc[h], v[h], preferred_element_type=F32) for h in heads]
    d_st = [[lax.dot_general(v[h][c * ch:(c + 1) * ch], k_end[h][c * ch:(c + 1) * ch], tn,
                             preferred_element_type=F32) for c in range(nch)] for h in heads]
    st_in = []
    for h in heads:
        st = st_ref[h]
        states = []
        for c in range(nch):
            states.append(st.astype(BF16))
            st = st * dec[h][c * ch:c * ch + 1, :] + d_st[h][c]
        st_ref[h] = st
        st_in.append(states)
    for h in heads:
        o_inter = [lax.dot_general(q_dec[h][c * ch:(c + 1) * ch], st_in[h][c], nt,
                                   preferred_element_type=F32) for c in range(nch)]
        o = o_intra[h] + jnp.concatenate(o_inter, axis=0)
        ms = jnp.sum(o * o, axis=1, keepdims=True) * (1.0 / GLA_DV)
        y = o * lax.rsqrt(ms + RMS_EPS) * gn
        rr = r_ref[:, GLA_DV_PAD * h:GLA_DV_PAD * (h + 1)].astype(F32)
        o_ref[:, GLA_DV_PAD * h:GLA_DV_PAD * (h + 1)] = (
            y * (rr / (1.0 + jnp.exp(-rr)))).astype(o_ref.dtype)


def gla_attention(proj, gw_pad, gb_pad, gn_pad, layer, rows=256, hps=4):
    s = proj.shape[0]
    rows = min(rows, s)
    ch = GLA_CHUNK
    dkw, dvw = hps * GLA_DK_PAD, hps * GLA_DV_PAD
    qi, ki = OFF_BQ // dkw, OFF_BK // dkw
    vi, ri = OFF_BV // dvw, OFF_BR // dvw
    li = OFF_BLR // GLA_LR_PAD
    idx = np.arange(rows)
    same = (idx[:, None] // ch) == (idx[None, :] // ch)
    prefix = same & (idx[None, :] <= idx[:, None])
    lm = jnp.asarray(np.concatenate([prefix, same], axis=0).astype(np.float32), dtype=BF16)
    return pl.pallas_call(
        functools.partial(_gla_kernel, rows=rows, hps=hps),
        grid=(GLA_HEADS // hps, s // rows),
        in_specs=[pl.BlockSpec((rows, dkw), lambda g, t: (t, qi + g)),
                  pl.BlockSpec((rows, dkw), lambda g, t: (t, ki + g)),
                  pl.BlockSpec((rows, dvw), lambda g, t: (t, vi + g)),
                  pl.BlockSpec((rows, dvw), lambda g, t: (t, ri + g)),
                  pl.BlockSpec((rows, GLA_LR_PAD), lambda g, t: (t, li)),
                  pl.BlockSpec((None, hps, GLA_LR_PAD, GLA_DK_PAD),
                               lambda g, t: (layer, g, 0, 0)),
                  pl.BlockSpec((None, hps, 1, GLA_DK_PAD), lambda g, t: (layer, g, 0, 0)),
                  pl.BlockSpec((None, 1, GLA_DV_PAD), lambda g, t: (layer, 0, 0)),
                  pl.BlockSpec((2 * rows, rows), lambda g, t: (0, 0))],
        out_specs=pl.BlockSpec((rows, dvw), lambda g, t: (t, g)),
        out_shape=jax.ShapeDtypeStruct((s, GLA_HEADS * GLA_DV_PAD), BF16),
        scratch_shapes=[pltpu.VMEM((hps, GLA_DV_PAD, GLA_DK_PAD), F32)],
        compiler_params=_cparams(("parallel", "arbitrary")),
        name="gla",
    )(proj, proj, proj, proj, proj, gw_pad, gb_pad, gn_pad, lm)


def _sb_kernel(q_ref, k_ref, v_ref, u_ref, o_ref, acc_ref, car_ref, qm_ref, vm_ref,
               z0_ref, z1_ref, z2_ref, z3_ref, nl0_ref, nl1_ref, w0_ref, w1_ref,
               *, bq, nblk, unroll):
    lane = lax.broadcasted_iota(jnp.int32, (bq, LANES), 1)
    low = lane < HEAD_DIM
    nt = (((1,), (1,)), ((), ()))
    z_ref = (z0_ref, z1_ref, z2_ref, z3_ref)
    nl_ref, w_ref = (nl0_ref, nl1_ref), (w0_ref, w1_ref)

    def rows(blk):
        if isinstance(blk, int):
            return pl.ds(blk * bq, bq)
        return pl.ds(pl.multiple_of(blk * bq, bq), bq)

    def strict_mask():
        rowi = lax.broadcasted_iota(jnp.int32, (2 * bq, bq), 0)
        coli = lax.broadcasted_iota(jnp.int32, (2 * bq, bq), 1)
        return coli < (rowi & (bq - 1))

    def s1a(item, tick4):
        i, c = item
        z_ref[tick4][...] = lax.dot_general(qm_ref[i], k_ref[rows(c), :], nt,
                                            preferred_element_type=F32)

    def s1b(tick4, masked):
        z = z_ref[tick4][...].astype(BF16)
        nl = jnp.maximum(z, 0.0) + jnp.log(1.0 + jnp.exp2(jnp.abs(z) * (-LOG2E)))
        if masked:
            nl = jnp.where(strict_mask(), nl, jnp.zeros_like(nl))
        nl_ref[tick4 % 2][...] = nl

    def s2(item, tick4, masked, first):
        i, _ = item
        cs = jnp.dot(nl_ref[tick4 % 2][...], u_ref[...], preferred_element_type=F32)
        tot = jnp.broadcast_to(cs[:, 0:1], (2 * bq, LANES))
        if first:
            arg = z_ref[tick4][...] - cs
            car_ref[i] = tot
        else:
            car = car_ref[i]
            arg = z_ref[tick4][...] - cs - jnp.concatenate([car, car], axis=1)
            car_ref[i] = car + tot
        w = jnp.exp(arg)
        if masked:
            w = jnp.where(strict_mask(), w, 0.0)
        w = w.astype(BF16)
        w_ref[tick4 % 2][:, :bq] = w[:bq]
        w_ref[tick4 % 2][:, bq:] = w[bq:]

    def s3(item, tick4, first):
        i, c = item
        pv = jnp.dot(w_ref[tick4 % 2][...], vm_ref[c], preferred_element_type=F32)
        if first:
            acc_ref[rows(i), :] = pv
        else:
            acc_ref[rows(i), :] += pv

    def run_pipeline(items, step, masked, first):
        n = len(items)

        def tick(t4, its, do):
            if do[1]:
                s1b((t4 - 1) % 4, masked)
            if do[0]:
                s1a(its[0], t4)
            if do[2]:
                s2(its[2], (t4 - 2) % 4, masked, first)
            if do[3]:
                s3(its[3], (t4 - 3) % 4, first)

        def static_tick(t):
            do = [0 <= t - j < n for j in range(4)]
            its = [items[t - j] if do[j] else None for j in range(4)]
            tick(t % 4, its, do)

        lo_t, hi_t = 3, n - 1
        groups = max(hi_t - lo_t + 1, 0) // unroll
        if groups < 2:
            groups = 0
        for t in range(min(lo_t, n + 3)):
            static_tick(t)
        if groups:
            def body(_, carry):
                its = [(carry[2 * j], carry[2 * j + 1]) for j in range(4)]
                for k in range(unroll):
                    tick((lo_t + k) % 4, its, [True] * 4)
                    its = [step(*its[0])] + its[:3]
                return tuple(x for it in its for x in it)

            init = tuple(jnp.int32(x) for j in range(4) for x in items[lo_t - j])
            lax.fori_loop(0, groups, body, init)
        for t in range(lo_t + unroll * groups, n + 3):
            static_tick(t)

    for blk in range(nblk):
        r = rows(blk)
        qb, vb = q_ref[r, :], v_ref[r, :]
        zero = jnp.zeros_like(qb)
        qm_ref[blk, :bq] = jnp.where(low, qb, zero)
        qm_ref[blk, bq:] = jnp.where(low, zero, qb)
        vm_ref[blk, :bq] = jnp.where(low, vb, zero)
        vm_ref[blk, bq:] = jnp.where(low, zero, vb)

    diag = [(i, i) for i in range(nblk)]
    run_pipeline(diag, lambda i, c: (i + 1, c + 1), masked=True, first=True)

    off = [(i, c) for i in range(1, nblk) for c in range(i - 1, -1, -1)]

    def off_step(i, c):
        wrap = c == 0
        i2 = jnp.where(wrap, i + 1, i)
        return i2, jnp.where(wrap, i2 - 1, c - 1)

    if off:
        run_pipeline(off, off_step, masked=False, first=False)

    o_ref[...] = acc_ref[...].astype(o_ref.dtype)


def sb_attention(proj, bq=256, unroll=32):
    s = proj.shape[0]
    bq = min(bq, s)
    nblk = s // bq
    assert bq & (bq - 1) == 0 and unroll % 4 == 0
    qi = OFF_CQ // LANES
    ki = OFF_CK // LANES
    vi = OFF_CV // LANES
    u = (np.arange(bq)[:, None] >= np.arange(bq)[None, :]).astype(np.float32)
    u = jnp.asarray(u, dtype=BF16)
    slot = lambda shape, dtype: pltpu.VMEM(shape, dtype)
    return pl.pallas_call(
        functools.partial(_sb_kernel, bq=bq, nblk=nblk, unroll=unroll),
        grid=(SB_HEADS // 2,),
        in_specs=[pl.BlockSpec((s, LANES), lambda p: (0, qi + p)),
                  pl.BlockSpec((s, LANES), lambda p: (0, ki + p)),
                  pl.BlockSpec((s, LANES), lambda p: (0, vi + p)),
                  pl.BlockSpec((bq, bq), lambda p: (0, 0))],
        out_specs=pl.BlockSpec((s, LANES), lambda p: (0, p)),
        out_shape=jax.ShapeDtypeStruct((s, SB_WIDTH), BF16),
        scratch_shapes=[pltpu.VMEM((s, LANES), F32),
                        pltpu.VMEM((nblk, 2 * bq, LANES), F32),
                        pltpu.VMEM((nblk, 2 * bq, LANES), BF16),
                        pltpu.VMEM((nblk, 2 * bq, LANES), BF16)]
                       + [slot((2 * bq, bq), F32)] * 4
                       + [slot((2 * bq, bq), BF16)] * 2
                       + [slot((bq, 2 * bq), BF16)] * 2,
        compiler_params=_cparams(("parallel",)),
        name="stickbreak",
    )(proj, proj, proj, u)


def _oproj_kernel(a_ref, b_ref, c_ref, wa_ref, wb_ref, wc_ref, ga_ref, gc_ref,
                  x_ref, gn_ref, xo_ref, ho_ref):
    an = _rms(a_ref[...].astype(F32), ga_ref[...]).astype(BF16)
    cn = _rms(c_ref[...].astype(F32), gc_ref[...]).astype(BF16)
    y = (jnp.dot(an, wa_ref[...], preferred_element_type=F32)
         + jnp.dot(b_ref[...], wb_ref[...], preferred_element_type=F32)
         + jnp.dot(cn, wc_ref[...], preferred_element_type=F32))
    xn = x_ref[...] + y
    xo_ref[...] = xn
    ho_ref[...] = _rms(xn, gn_ref[...]).astype(ho_ref.dtype)


def out_projection(a, b, c, wa, wb, wc, ga, gc, x, gn, layer, tm=512):
    s, d = x.shape
    tm = min(tm, s)
    row = lambda i: (i, 0)
    fixed = lambda i: (layer, 0, 0)
    per_layer = lambda arr: pl.BlockSpec((None,) + arr.shape[1:], fixed)
    return pl.pallas_call(
        _oproj_kernel,
        grid=(s // tm,),
        in_specs=[pl.BlockSpec((tm, a.shape[1]), row),
                  pl.BlockSpec((tm, b.shape[1]), row),
                  pl.BlockSpec((tm, c.shape[1]), row),
                  per_layer(wa), per_layer(wb), per_layer(wc), per_layer(ga), per_layer(gc),
                  pl.BlockSpec((tm, d), row),
                  per_layer(gn)],
        out_specs=[pl.BlockSpec((tm, d), row), pl.BlockSpec((tm, d), row)],
        out_shape=[jax.ShapeDtypeStruct((s, d), F32), jax.ShapeDtypeStruct((s, d), BF16)],
        compiler_params=_cparams(("parallel",)),
        name="out_proj",
    )(a, b, c, wa, wb, wc, ga, gc, x, gn)


def _mlp_kernel(h_ref, x_ref, w1_ref, w2_ref, gn_ref, *out_refs, final):
    acc_ref = out_refs[0]
    j = pl.program_id(1)

    @pl.when(j == 0)
    def _():
        acc_ref[...] = x_ref[...]

    u = jnp.dot(h_ref[...], w1_ref[...].astype(BF16), preferred_element_type=F32)
    act = jnp.square(jnp.maximum(u, 0.0)).astype(BF16)
    acc_ref[...] += jnp.dot(act, w2_ref[...].astype(BF16), preferred_element_type=F32)

    @pl.when(j == pl.num_programs(1) - 1)
    def _():
        y = _rms(acc_ref[...], gn_ref[...])
        if final:
            acc_ref[...] = y
        else:
            out_refs[1][...] = y.astype(out_refs[1].dtype)


def mlp_block(h, x, w1_all, w2_all, layer, gn, final, tm=1024, tf=512):
    s, d = x.shape
    ff = w1_all.shape[2]
    tm = min(tm, s)
    row = lambda i, j: (i, 0)
    once = pl.Buffered(1)
    out_specs = [pl.BlockSpec((tm, d), row)]
    out_shape = [jax.ShapeDtypeStruct((s, d), F32)]
    if not final:
        out_specs.append(pl.BlockSpec((tm, d), row))
        out_shape.append(jax.ShapeDtypeStruct((s, d), BF16))
    return pl.pallas_call(
        functools.partial(_mlp_kernel, final=final),
        grid=(s // tm, ff // tf),
        in_specs=[pl.BlockSpec((tm, d), row, pipeline_mode=once),
                  pl.BlockSpec((tm, d), row, pipeline_mode=once),
                  pl.BlockSpec((None, d, tf), lambda i, j: (layer, 0, j)),
                  pl.BlockSpec((None, tf, d), lambda i, j: (layer, j, 0)),
                  pl.BlockSpec((None, 1, d), lambda i, j: (layer, 0, 0))],
        out_specs=out_specs,
        out_shape=out_shape,
        compiler_params=_cparams(("parallel", "arbitrary")),
        name="mlp",
    )(h, x, w1_all, w2_all, gn)


def _head_rows(w, off, heads, width, padded):
    pieces = []
    for h in range(heads):
        pieces.append(w[..., off + h * width:off + (h + 1) * width, :])
        if padded > width:
            pieces.append(jnp.zeros(w.shape[:-2] + (padded - width, w.shape[-1]), w.dtype))
    return pieces


def _swa_perm_cols(w):
    return jnp.concatenate([w[..., HEAD_DIM * h:HEAD_DIM * (h + 1)] for h in SWA_Q_PERM], axis=-1)


def kernel(x, norm_mix, w_in, swa_sinks, rel_bias, gla_gate_w, gla_gate_b, gla_norm,
           swa_out_norm, sb_out_norm, w_out, norm_mlp, w_mlp_in, w_mlp_out, norm_final):
    depth = w_in.shape[0]
    xs = x[0]
    gla_w = GLA_HEADS * GLA_DV
    w_in_p = prep_w_in(jnp.swapaxes(w_in, 1, 2))
    wa = jnp.concatenate([w_out[:, HEAD_DIM * h:HEAD_DIM * (h + 1)] for h in SWA_Q_PERM],
                         axis=1).astype(BF16)
    wb = jnp.concatenate(_head_rows(w_out, SWA_WIDTH, GLA_HEADS, GLA_DV, GLA_DV_PAD),
                         axis=1).astype(BF16)
    wc = w_out[:, SWA_WIDTH + gla_w:].astype(BF16)
    ga = _swa_perm_cols(swa_out_norm)[:, None, :]
    gc = sb_out_norm[:, None, :]
    g_mlp = norm_mlp[:, None, :]
    g_next = jnp.concatenate([norm_mix[1:], norm_final[None]], axis=0)[:, None, :]
    sinks_p = jnp.concatenate([swa_sinks[:, h:h + 1] for h in SWA_Q_PERM], axis=1)
    bias_tab = swa_bias_table(rel_bias)
    gw = jnp.stack([jnp.pad(gla_gate_w[:, :, GLA_DK * h:GLA_DK * (h + 1)],
                            ((0, 0), (0, GLA_LR_PAD - GLA_LOWRANK), (0, GLA_DK_PAD - GLA_DK)))
                    for h in range(GLA_HEADS)], axis=1).astype(BF16)
    gb = jnp.stack([jnp.pad(gla_gate_b[:, GLA_DK * h:GLA_DK * (h + 1)],
                            ((0, 0), (0, GLA_DK_PAD - GLA_DK)))
                    for h in range(GLA_HEADS)], axis=1)[:, :, None, :]
    gn = jnp.pad(gla_norm, ((0, 0), (0, GLA_DV_PAD - GLA_DV)))[:, None, :]

    h = None
    for l in range(depth):
        if l == 0:
            proj = in_projection(xs, w_in_p, l, gain=norm_mix[0])
        else:
            proj = in_projection(h, w_in_p, l)
        a = swa_attention(proj, sinks_p, bias_tab, l)
        b = gla_attention(proj, gw, gb, gn, l)
        c = sb_attention(proj)
        xs, hm = out_projection(a, b, c, wa, wb, wc, ga, gc, xs, g_mlp, l)
        if l + 1 < depth:
            xs, h = mlp_block(hm, xs, w_mlp_in, w_mlp_out, l, g_next, final=False)
        else:
            (out,) = mlp_block(hm, xs, w_mlp_in, w_mlp_out, l, g_next, final=True)
    return out[None]
```

```python
import functools
import math

import numpy as np
import jax
import jax.numpy as jnp
from jax import lax
from jax.experimental import pallas as pl
from jax.experimental.pallas import tpu as pltpu

F32 = jnp.float32
BF16 = jnp.bfloat16

HEAD_DIM = 64
SWA_HEADS = 12
SWA_KV_HEADS = 4
SWA_WIDTH = SWA_HEADS * HEAD_DIM
SWA_KV_WIDTH = SWA_KV_HEADS * HEAD_DIM
SWA_BLOCK = 128
WINDOW = 128
REL_BUCKETS = 32
REL_MAX_DIST = 128
GLA_HEADS = 4
GLA_DV = 192
GLA_DK = 96
GLA_DK_PAD = 128
GLA_DV_PAD = 256
GLA_LOWRANK = 16
GLA_LR_PAD = 128
GLA_CHUNK = 64
GATE_NORMALIZER = 16.0
GATE_LOG_MIN = -1.0
SB_HEADS = 8
SB_WIDTH = SB_HEADS * HEAD_DIM
RMS_EPS = 1e-6
LOG2E = 1.4426950408889634
NEG_INF = -1e30

LANES = 128
VMEM_LIMIT = 56 * 1024 * 1024

OFF_BV = 0
OFF_BR = 1024
OFF_BQ = 2048
OFF_BK = 2560
OFF_CQ = 3072
OFF_CK = 3584
OFF_CV = 4096
OFF_AQ = 4608
OFF_AK = 5376
OFF_AV = 5632
OFF_BLR = 5888
PROJ_PAD_WIDTH = 6144

SWA_Q_PERM = (0, 3, 1, 4, 2, 5, 6, 9, 7, 10, 8, 11)

_O_AQ, _O_AK, _O_AV = 0, 768, 1024
_O_BQ, _O_BK, _O_BV, _O_BR, _O_BLR = 1280, 1664, 2048, 2816, 3584
_O_CQ, _O_CK, _O_CV = 3600, 4112, 4624


def _cparams(sem):
    return pltpu.CompilerParams(dimension_semantics=sem, vmem_limit_bytes=VMEM_LIMIT)


def _rms(x, gain):
    ms = jnp.mean(x * x, axis=-1, keepdims=True)
    return x * lax.rsqrt(ms + RMS_EPS) * gain


def _split_bf16(x):
    hi = x.astype(BF16)
    lo = (x - hi.astype(F32)).astype(BF16)
    return hi, lo


_NT_DIMS = (((1,), (1,)), ((), ()))


def _matmul_kernel(h_ref, w_ref, o_ref):
    o_ref[...] = lax.dot_general(h_ref[...], w_ref[...], _NT_DIMS,
                                 preferred_element_type=F32).astype(o_ref.dtype)


def _norm_matmul_kernel(x_ref, g_ref, w_ref, o_ref):
    h = _rms(x_ref[...], g_ref[...]).astype(BF16)
    o_ref[...] = lax.dot_general(h, w_ref[...], _NT_DIMS,
                                 preferred_element_type=F32).astype(o_ref.dtype)


def in_projection(h, w_all, layer, gain=None, tm=1024, tn=1024):
    s, d = h.shape
    n = w_all.shape[1]
    tm = min(tm, s)
    w_spec = pl.BlockSpec((None, tn, d), lambda i, j: (layer, j, 0))
    h_spec = pl.BlockSpec((tm, d), lambda i, j: (i, 0))
    common = dict(
        grid=(s // tm, n // tn),
        out_specs=pl.BlockSpec((tm, tn), lambda i, j: (i, j)),
        out_shape=jax.ShapeDtypeStruct((s, n), BF16),
        compiler_params=_cparams(("parallel", "arbitrary")),
        name="in_proj",
    )
    if gain is None:
        return pl.pallas_call(_matmul_kernel, in_specs=[h_spec, w_spec], **common)(h, w_all)
    return pl.pallas_call(
        _norm_matmul_kernel,
        in_specs=[h_spec, pl.BlockSpec((1, d), lambda i, j: (0, 0)), w_spec],
        **common,
    )(h, gain.reshape(1, d), w_all)


def _w_in_moves():
    scale = HEAD_DIM ** -0.5
    moves = []
    for h in range(GLA_HEADS):
        moves.append((_O_BV + GLA_DV * h, OFF_BV + GLA_DV_PAD * h, GLA_DV, 1.0))
        moves.append((_O_BR + GLA_DV * h, OFF_BR + GLA_DV_PAD * h, GLA_DV, 1.0))
        moves.append((_O_BQ + GLA_DK * h, OFF_BQ + GLA_DK_PAD * h, GLA_DK, 1.0))
        moves.append((_O_BK + GLA_DK * h, OFF_BK + GLA_DK_PAD * h, GLA_DK, 1.0))
    moves += [(_O_CQ, OFF_CQ, SB_WIDTH, scale), (_O_CK, OFF_CK, SB_WIDTH, 1.0),
              (_O_CV, OFF_CV, SB_WIDTH, 1.0)]
    for j, h in enumerate(SWA_Q_PERM):
        moves.append((_O_AQ + HEAD_DIM * h, OFF_AQ + HEAD_DIM * j, HEAD_DIM, scale))
    moves += [(_O_AK, OFF_AK, SWA_KV_WIDTH, 1.0), (_O_AV, OFF_AV, SWA_KV_WIDTH, 1.0),
              (_O_BLR, OFF_BLR, GLA_LOWRANK, 1.0)]
    return moves


def _w_in_prep_kernel(w_ref, o_ref):
    o_ref[...] = jnp.zeros_like(o_ref)
    for src, dst, width, scale in _w_in_moves():
        piece = w_ref[src:src + width, :]
        if scale != 1.0:
            piece = piece * scale
        o_ref[dst:dst + width, :] = piece.astype(o_ref.dtype)


def prep_w_in(w_in_t, tc=256):
    layers, n, d = w_in_t.shape
    return pl.pallas_call(
        _w_in_prep_kernel,
        grid=(layers, d // tc),
        in_specs=[pl.BlockSpec((None, n, tc), lambda l, i: (l, 0, i))],
        out_specs=pl.BlockSpec((None, PROJ_PAD_WIDTH, tc), lambda l, i: (l, 0, i)),
        out_shape=jax.ShapeDtypeStruct((layers, PROJ_PAD_WIDTH, d), BF16),
        compiler_params=_cparams(("parallel", "parallel")),
        name="w_in_prep",
    )(w_in_t)


def _swa_kernel(sink_ref, q_ref, kp_ref, kc_ref, vp_ref, vc_ref, bias_ref, o_ref, *, layer):
    blk = SWA_BLOCK
    nb = SWA_HEADS // 2
    k = jnp.concatenate([kp_ref[...], kc_ref[...]], axis=0)
    v = jnp.concatenate([vp_ref[...], vc_ref[...]], axis=0)
    lane = lax.broadcasted_iota(jnp.int32, (blk, LANES), 1)
    low = lane < HEAD_DIM
    top = lax.broadcasted_iota(jnp.int32, (2 * blk, 1), 0) < blk
    ones = jnp.ones((2 * blk, LANES), BF16)
    nt = (((1,), (1,)), ((), ()))

    scores = []
    for b in range(nb):
        qb = q_ref[:, LANES * b:LANES * (b + 1)]
        zero = jnp.zeros_like(qb)
        lhs = jnp.concatenate([jnp.where(low, qb, zero), jnp.where(low, zero, qb)], axis=0)
        kb = k[:, LANES * (b // 3):LANES * (b // 3 + 1)]
        scores.append(lax.dot_general(lhs, kb, nt, preferred_element_type=F32) + bias_ref[b])
    probs, sink_term = [], []
    for b in range(nb):
        sink = jnp.where(top, sink_ref[layer, 2 * b], sink_ref[layer, 2 * b + 1])
        m = jnp.maximum(jnp.max(scores[b], axis=1, keepdims=True), sink)
        probs.append(jnp.exp(scores[b] - m).astype(BF16))
        sink_term.append(jnp.exp(sink - m))
    for b in range(nb):
        vb = v[:, LANES * (b // 3):LANES * (b // 3 + 1)]
        pv = jnp.dot(probs[b], jnp.concatenate([vb, ones], axis=1),
                     preferred_element_type=F32)
        o = pv[:, :LANES] / (pv[:, LANES:] + sink_term[b])
        o_ref[:, LANES * b:LANES * (b + 1)] = jnp.where(low, o[:blk], o[blk:]).astype(o_ref.dtype)


def swa_attention(proj, sinks_perm, bias_tab, layer):
    s = proj.shape[0]
    blk = SWA_BLOCK
    qi = OFF_AQ // SWA_WIDTH
    ki = OFF_AK // SWA_KV_WIDTH
    vi = OFF_AV // SWA_KV_WIDTH
    prev = lambda i: jnp.maximum(i - 1, 0)
    return pl.pallas_call(
        functools.partial(_swa_kernel, layer=layer),
        grid=(s // blk,),
        in_specs=[pl.BlockSpec(memory_space=pltpu.SMEM),
                  pl.BlockSpec((blk, SWA_WIDTH), lambda i: (i, qi)),
                  pl.BlockSpec((blk, SWA_KV_WIDTH), lambda i: (prev(i), ki)),
                  pl.BlockSpec((blk, SWA_KV_WIDTH), lambda i: (i, ki)),
                  pl.BlockSpec((blk, SWA_KV_WIDTH), lambda i: (prev(i), vi)),
                  pl.BlockSpec((blk, SWA_KV_WIDTH), lambda i: (i, vi)),
                  pl.BlockSpec((None, SWA_HEADS // 2, 2 * blk, 2 * blk),
                               lambda i: (jnp.minimum(i, 1), 0, 0, 0))],
        out_specs=pl.BlockSpec((blk, SWA_WIDTH), lambda i: (i, 0)),
        out_shape=jax.ShapeDtypeStruct((s, SWA_WIDTH), BF16),
        compiler_params=_cparams(("parallel",)),
        name="swa",
    )(sinks_perm, proj, proj, proj, proj, proj, bias_tab)


def _t5_causal_bucket(dist):
    max_exact = REL_BUCKETS // 2
    is_small = dist < max_exact
    ratio = (jnp.log(jnp.maximum(dist, 1).astype(F32) / max_exact)
             / math.log(REL_MAX_DIST / max_exact))
    large = max_exact + (ratio * (REL_BUCKETS - max_exact)).astype(jnp.int32)
    large = jnp.minimum(large, REL_BUCKETS - 1)
    return jnp.where(is_small, dist, large)


def swa_bias_table(rel_bias):
    blk = SWA_BLOCK
    qpos = jnp.arange(blk) + blk
    kpos = jnp.arange(2 * blk)
    dist = qpos[:, None] - kpos[None, :]
    in_window = (dist >= 0) & (dist < WINDOW)
    bucket = _t5_causal_bucket(jnp.maximum(dist, 0))
    table = rel_bias.astype(F32)
    table = jnp.stack([table[:, h] for h in SWA_Q_PERM])
    bias = jnp.full((SWA_HEADS, blk, 2 * blk), NEG_INF, F32)
    for bkt in range(REL_BUCKETS):
        hit = (in_window & (bucket == bkt))[None]
        bias = jnp.where(hit, table[:, bkt][:, None, None], bias)
    bias = bias.reshape(SWA_HEADS // 2, 2 * blk, 2 * blk)
    first = jnp.where((kpos < blk)[None, None, :], NEG_INF, bias)
    return jnp.stack([first, bias])


def _gla_kernel(q_ref, k_ref, v_ref, r_ref, lr_ref, gw_ref, gb_ref, gn_ref, lm_ref, o_ref,
                st_ref, *, rows, hps):
    t = pl.program_id(1)
    ch = GLA_CHUNK
    nch = rows // ch
    heads = range(hps)

    @pl.when(t == 0)
    def _():
        st_ref[...] = jnp.zeros_like(st_ref)

    nt = (((1,), (1,)), ((), ()))
    tn = (((0,), (0,)), ((), ()))
    rowi = lax.broadcasted_iota(jnp.int32, (rows, rows), 0)
    coli = lax.broadcasted_iota(jnp.int32, (rows, rows), 1)
    same_chunk_causal = jnp.logical_and(coli <= rowi, (rowi // ch) == (coli // ch))
    lm = lm_ref[...]
    lr = lr_ref[...]
    gn = gn_ref[...]

    gp = [jnp.dot(lr, gw_ref[h], preferred_element_type=F32) + gb_ref[h] for h in heads]
    lg = []
    for h in heads:
        log_sig = jnp.minimum(gp[h], 0.0) - jnp.log(1.0 + jnp.exp(-jnp.abs(gp[h])))
        lg.append(jnp.maximum(log_sig * (1.0 / GATE_NORMALIZER), GATE_LOG_MIN))
    bb = []
    for h in heads:
        hi, lo = _split_bf16(lg[h])
        bb.append(jnp.dot(lm, hi, preferred_element_type=F32)
                  + jnp.dot(lm, lo, preferred_element_type=F32))
    q_dec, k_inv, k_end, dec = [], [], [], []
    for h in heads:
        b, b_last = bb[h][:rows], bb[h][rows:]
        q = q_ref[:, GLA_DK_PAD * h:GLA_DK_PAD * (h + 1)].astype(F32) * (GLA_DK ** -0.5)
        kk = k_ref[:, GLA_DK_PAD * h:GLA_DK_PAD * (h + 1)].astype(F32)
        q_dec.append((q * jnp.exp(b)).astype(BF16))
        k_inv.append((kk * jnp.exp(-b)).astype(BF16))
        k_end.append((kk * jnp.exp(b_last - b)).astype(BF16))
        dec.append(jnp.exp(b_last))
    v = [v_ref[:, GLA_DV_PAD * h:GLA_DV_PAD * (h + 1)] for h in heads]
    sc = []
    for h in heads:
        s_h = lax.dot_general(q_dec[h], k_inv[h], nt, preferred_element_type=F32)
        sc.append(jnp.where(same_chunk_causal, s_h, 0.0).astype(BF16))
    o_intra = [jnp.dot(sc[h], v[h], preferred_element_type=F32) for h in heads]
    d_st = [[lax.dot_general(v[h][c * ch:(c + 1) * ch], k_end[h][c * ch:(c + 1) * ch], tn,
                             preferred_element_type=F32) for c in range(nch)] for h in heads]
    st_in = []
    for h in heads:
        st = st_ref[h]
        states = []
        for c in range(nch):
            states.append(st.astype(BF16))
            st = st * dec[h][c * ch:c * ch + 1, :] + d_st[h][c]
        st_ref[h] = st
        st_in.append(states)
    for h in heads:
        o_inter = [lax.dot_general(q_dec[h][c * ch:(c + 1) * ch], st_in[h][c], nt,
                                   preferred_element_type=F32) for c in range(nch)]
        o = o_intra[h] + jnp.concatenate(o_inter, axis=0)
        ms = jnp.sum(o * o, axis=1, keepdims=True) * (1.0 / GLA_DV)
        y = o * lax.rsqrt(ms + RMS_EPS) * gn
        rr = r_ref[:, GLA_DV_PAD * h:GLA_DV_PAD * (h + 1)].astype(F32)
        o_ref[:, GLA_DV_PAD * h:GLA_DV_PAD * (h + 1)] = (
            y * (rr / (1.0 + jnp.exp(-rr)))).astype(o_ref.dtype)


def gla_attention(proj, gw_pad, gb_pad, gn_pad, layer, rows=256, hps=4):
    s = proj.shape[0]
    rows = min(rows, s)
    ch = GLA_CHUNK
    dkw, dvw = hps * GLA_DK_PAD, hps * GLA_DV_PAD
    qi, ki = OFF_BQ // dkw, OFF_BK // dkw
    vi, ri = OFF_BV // dvw, OFF_BR // dvw
    li = OFF_BLR // GLA_LR_PAD
    idx = np.arange(rows)
    same = (idx[:, None] // ch) == (idx[None, :] // ch)
    prefix = same & (idx[None, :] <= idx[:, None])
    lm = jnp.asarray(np.concatenate([prefix, same], axis=0).astype(np.float32), dtype=BF16)
    return pl.pallas_call(
        functools.partial(_gla_kernel, rows=rows, hps=hps),
        grid=(GLA_HEADS // hps, s // rows),
        in_specs=[pl.BlockSpec((rows, dkw), lambda g, t: (t, qi + g)),
                  pl.BlockSpec((rows, dkw), lambda g, t: (t, ki + g)),
                  pl.BlockSpec((rows, dvw), lambda g, t: (t, vi + g)),
                  pl.BlockSpec((rows, dvw), lambda g, t: (t, ri + g)),
                  pl.BlockSpec((rows, GLA_LR_PAD), lambda g, t: (t, li)),
                  pl.BlockSpec((None, hps, GLA_LR_PAD, GLA_DK_PAD),
                               lambda g, t: (layer, g, 0, 0)),
                  pl.BlockSpec((None, hps, 1, GLA_DK_PAD), lambda g, t: (layer, g, 0, 0)),
                  pl.BlockSpec((None, 1, GLA_DV_PAD), lambda g, t: (layer, 0, 0)),
                  pl.BlockSpec((2 * rows, rows), lambda g, t: (0, 0))],
        out_specs=pl.BlockSpec((rows, dvw), lambda g, t: (t, g)),
        out_shape=jax.ShapeDtypeStruct((s, GLA_HEADS * GLA_DV_PAD), BF16),
        scratch_shapes=[pltpu.VMEM((hps, GLA_DV_PAD, GLA_DK_PAD), F32)],
        compiler_params=_cparams(("parallel", "arbitrary")),
        name="gla",
    )(proj, proj, proj, proj, proj, gw_pad, gb_pad, gn_pad, lm)


def _sb_kernel(q_ref, k_ref, v_ref, u_ref, o_ref, acc_ref, car_ref, qm_ref, vm_ref,
               z0_ref, z1_ref, z2_ref, z3_ref, nl0_ref, nl1_ref, w0_ref, w1_ref,
               *, bq, nblk, unroll):
    lane = lax.broadcasted_iota(jnp.int32, (bq, LANES), 1)
    low = lane < HEAD_DIM
    nt = (((1,), (1,)), ((), ()))
    z_ref = (z0_ref, z1_ref, z2_ref, z3_ref)
    nl_ref, w_ref = (nl0_ref, nl1_ref), (w0_ref, w1_ref)

    def rows(blk):
        if isinstance(blk, int):
            return pl.ds(blk * bq, bq)
        return pl.ds(pl.multiple_of(blk * bq, bq), bq)

    def strict_mask():
        rowi = lax.broadcasted_iota(jnp.int32, (2 * bq, bq), 0)
        coli = lax.broadcasted_iota(jnp.int32, (2 * bq, bq), 1)
        return coli < (rowi & (bq - 1))

    def s1a(item, tick4):
        i, c = item
        z_ref[tick4][...] = lax.dot_general(qm_ref[i], k_ref[rows(c), :], nt,
                                            preferred_element_type=F32)

    def s1b(tick4, masked):
        z = z_ref[tick4][...].astype(BF16)
        nl = jnp.maximum(z, 0.0) + jnp.log(1.0 + jnp.exp2(jnp.abs(z) * (-LOG2E)))
        if masked:
            nl = jnp.where(strict_mask(), nl, jnp.zeros_like(nl))
        nl_ref[tick4 % 2][...] = nl

    def s2(item, tick4, masked, first):
        i, _ = item
        cs = jnp.dot(nl_ref[tick4 % 2][...], u_ref[...], preferred_element_type=F32)
        tot = jnp.broadcast_to(cs[:, 0:1], (2 * bq, LANES))
        if first:
            arg = z_ref[tick4][...] - cs
            car_ref[i] = tot
        else:
            car = car_ref[i]
            arg = z_ref[tick4][...] - cs - jnp.concatenate([car, car], axis=1)
            car_ref[i] = car + tot
        w = jnp.exp(arg)
        if masked:
            w = jnp.where(strict_mask(), w, 0.0)
        w = w.astype(BF16)
        w_ref[tick4 % 2][:, :bq] = w[:bq]
        w_ref[tick4 % 2][:, bq:] = w[bq:]

    def s3(item, tick4, first):
        i, c = item
        pv = jnp.dot(w_ref[tick4 % 2][...], vm_ref[c], preferred_element_type=F32)
        if first:
            acc_ref[rows(i), :] = pv
        else:
            acc_ref[rows(i), :] += pv

    def run_pipeline(items, step, masked, first):
        n = len(items)

        def tick(t4, its, do):
            if do[1]:
                s1b((t4 - 1) % 4, masked)
            if do[0]:
                s1a(its[0], t4)
            if do[2]:
                s2(its[2], (t4 - 2) % 4, masked, first)
            if do[3]:
                s3(its[3], (t4 - 3) % 4, first)

        def static_tick(t):
            do = [0 <= t - j < n for j in range(4)]
            its = [items[t - j] if do[j] else None for j in range(4)]
            tick(t % 4, its, do)

        lo_t, hi_t = 3, n - 1
        groups = max(hi_t - lo_t + 1, 0) // unroll
        if groups < 2:
            groups = 0
        for t in range(min(lo_t, n + 3)):
            static_tick(t)
        if groups:
            def body(_, carry):
                its = [(carry[2 * j], carry[2 * j + 1]) for j in range(4)]
                for k in range(unroll):
                    tick((lo_t + k) % 4, its, [True] * 4)
                    its = [step(*its[0])] + its[:3]
                return tuple(x for it in its for x in it)

            init = tuple(jnp.int32(x) for j in range(4) for x in items[lo_t - j])
            lax.fori_loop(0, groups, body, init)
        for t in range(lo_t + unroll * groups, n + 3):
            static_tick(t)

    for blk in range(nblk):
        r = rows(blk)
        qb, vb = q_ref[r, :], v_ref[r, :]
        zero = jnp.zeros_like(qb)
        qm_ref[blk, :bq] = jnp.where(low, qb, zero)
        qm_ref[blk, bq:] = jnp.where(low, zero, qb)
        vm_ref[blk, :bq] = jnp.where(low, vb, zero)
        vm_ref[blk, bq:] = jnp.where(low, zero, vb)

    diag = [(i, i) for i in range(nblk)]
    run_pipeline(diag, lambda i, c: (i + 1, c + 1), masked=True, first=True)

    off = [(i, c) for i in range(1, nblk) for c in range(i - 1, -1, -1)]

    def off_step(i, c):
        wrap = c == 0
        i2 = jnp.where(wrap, i + 1, i)
        return i2, jnp.where(wrap, i2 - 1, c - 1)

    if off:
        run_pipeline(off, off_step, masked=False, first=False)

    o_ref[...] = acc_ref[...].astype(o_ref.dtype)


def sb_attention(proj, bq=256, unroll=64):
    s = proj.shape[0]
    bq = min(bq, s)
    nblk = s // bq
    assert bq & (bq - 1) == 0 and unroll % 4 == 0
    qi = OFF_CQ // LANES
    ki = OFF_CK // LANES
    vi = OFF_CV // LANES
    u = (np.arange(bq)[:, None] >= np.arange(bq)[None, :]).astype(np.float32)
    u = jnp.asarray(u, dtype=BF16)
    slot = lambda shape, dtype: pltpu.VMEM(shape, dtype)
    return pl.pallas_call(
        functools.partial(_sb_kernel, bq=bq, nblk=nblk, unroll=unroll),
        grid=(SB_HEADS // 2,),
        in_specs=[pl.BlockSpec((s, LANES), lambda p: (0, qi + p)),
                  pl.BlockSpec((s, LANES), lambda p: (0, ki + p)),
                  pl.BlockSpec((s, LANES), lambda p: (0, vi + p)),
                  pl.BlockSpec((bq, bq), lambda p: (0, 0))],
        out_specs=pl.BlockSpec((s, LANES), lambda p: (0, p)),
        out_shape=jax.ShapeDtypeStruct((s, SB_WIDTH), BF16),
        scratch_shapes=[pltpu.VMEM((s, LANES), F32),
                        pltpu.VMEM((nblk, 2 * bq, LANES), F32),
                        pltpu.VMEM((nblk, 2 * bq, LANES), BF16),
                        pltpu.VMEM((nblk, 2 * bq, LANES), BF16)]
                       + [slot((2 * bq, bq), F32)] * 4
                       + [slot((2 * bq, bq), BF16)] * 2
                       + [slot((bq, 2 * bq), BF16)] * 2,
        compiler_params=_cparams(("parallel",)),
        name="stickbreak",
    )(proj, proj, proj, u)


def _oproj_kernel(a_ref, b_ref, c_ref, wa_ref, wb_ref, wc_ref, ga_ref, gc_ref,
                  x_ref, gn_ref, xo_ref, ho_ref):
    an = _rms(a_ref[...].astype(F32), ga_ref[...]).astype(BF16)
    cn = _rms(c_ref[...].astype(F32), gc_ref[...]).astype(BF16)
    y = (jnp.dot(an, wa_ref[...], preferred_element_type=F32)
         + jnp.dot(b_ref[...], wb_ref[...], preferred_element_type=F32)
         + jnp.dot(cn, wc_ref[...], preferred_element_type=F32))
    xn = x_ref[...] + y
    xo_ref[...] = xn
    ho_ref[...] = _rms(xn, gn_ref[...]).astype(ho_ref.dtype)


def out_projection(a, b, c, wa, wb, wc, ga, gc, x, gn, layer, tm=512):
    s, d = x.shape
    tm = min(tm, s)
    row = lambda i: (i, 0)
    fixed = lambda i: (layer, 0, 0)
    per_layer = lambda arr: pl.BlockSpec((None,) + arr.shape[1:], fixed)
    return pl.pallas_call(
        _oproj_kernel,
        grid=(s // tm,),
        in_specs=[pl.BlockSpec((tm, a.shape[1]), row),
                  pl.BlockSpec((tm, b.shape[1]), row),
                  pl.BlockSpec((tm, c.shape[1]), row),
                  per_layer(wa), per_layer(wb), per_layer(wc), per_layer(ga), per_layer(gc),
                  pl.BlockSpec((tm, d), row),
                  per_layer(gn)],
        out_specs=[pl.BlockSpec((tm, d), row), pl.BlockSpec((tm, d), row)],
        out_shape=[jax.ShapeDtypeStruct((s, d), F32), jax.ShapeDtypeStruct((s, d), BF16)],
        compiler_params=_cparams(("parallel",)),
        name="out_proj",
    )(a, b, c, wa, wb, wc, ga, gc, x, gn)


def _mlp_kernel(h_ref, x_ref, w1_ref, w2_ref, gn_ref, *out_refs, final):
    acc_ref = out_refs[0]
    j = pl.program_id(1)

    @pl.when(j == 0)
    def _():
        acc_ref[...] = x_ref[...]

    u = jnp.dot(h_ref[...], w1_ref[...].astype(BF16), preferred_element_type=F32)
    act = jnp.square(jnp.maximum(u, 0.0)).astype(BF16)
    acc_ref[...] += jnp.dot(act, w2_ref[...].astype(BF16), preferred_element_type=F32)

    @pl.when(j == pl.num_programs(1) - 1)
    def _():
        y = _rms(acc_ref[...], gn_ref[...])
        if final:
            acc_ref[...] = y
        else:
            out_refs[1][...] = y.astype(out_refs[1].dtype)


def mlp_block(h, x, w1_all, w2_all, layer, gn, final, tm=1024, tf=512):
    s, d = x.shape
    ff = w1_all.shape[2]
    tm = min(tm, s)
    row = lambda i, j: (i, 0)
    once = pl.Buffered(1)
    out_specs = [pl.BlockSpec((tm, d), row)]
    out_shape = [jax.ShapeDtypeStruct((s, d), F32)]
    if not final:
        out_specs.append(pl.BlockSpec((tm, d), row))
        out_shape.append(jax.ShapeDtypeStruct((s, d), BF16))
    return pl.pallas_call(
        functools.partial(_mlp_kernel, final=final),
        grid=(s // tm, ff // tf),
        in_specs=[pl.BlockSpec((tm, d), row, pipeline_mode=once),
                  pl.BlockSpec((tm, d), row, pipeline_mode=once),
                  pl.BlockSpec((None, d, tf), lambda i, j: (layer, 0, j)),
                  pl.BlockSpec((None, tf, d), lambda i, j: (layer, j, 0)),
                  pl.BlockSpec((None, 1, d), lambda i, j: (layer, 0, 0))],
        out_specs=out_specs,
        out_shape=out_shape,
        compiler_params=_cparams(("parallel", "arbitrary")),
        name="mlp",
    )(h, x, w1_all, w2_all, gn)


def _head_rows(w, off, heads, width, padded):
    pieces = []
    for h in range(heads):
        pieces.append(w[..., off + h * width:off + (h + 1) * width, :])
        if padded > width:
            pieces.append(jnp.zeros(w.shape[:-2] + (padded - width, w.shape[-1]), w.dtype))
    return pieces


def _swa_perm_cols(w):
    return jnp.concatenate([w[..., HEAD_DIM * h:HEAD_DIM * (h + 1)] for h in SWA_Q_PERM], axis=-1)


def kernel(x, norm_mix, w_in, swa_sinks, rel_bias, gla_gate_w, gla_gate_b, gla_norm,
           swa_out_norm, sb_out_norm, w_out, norm_mlp, w_mlp_in, w_mlp_out, norm_final):
    depth = w_in.shape[0]
    xs = x[0]
    gla_w = GLA_HEADS * GLA_DV
    w_in_p = prep_w_in(jnp.swapaxes(w_in, 1, 2))
    wa = jnp.concatenate([w_out[:, HEAD_DIM * h:HEAD_DIM * (h + 1)] for h in SWA_Q_PERM],
                         axis=1).astype(BF16)
    wb = jnp.concatenate(_head_rows(w_out, SWA_WIDTH, GLA_HEADS, GLA_DV, GLA_DV_PAD),
                         axis=1).astype(BF16)
    wc = w_out[:, SWA_WIDTH + gla_w:].astype(BF16)
    ga = _swa_perm_cols(swa_out_norm)[:, None, :]
    gc = sb_out_norm[:, None, :]
    g_mlp = norm_mlp[:, None, :]
    g_next = jnp.concatenate([norm_mix[1:], norm_final[None]], axis=0)[:, None, :]
    sinks_p = jnp.concatenate([swa_sinks[:, h:h + 1] for h in SWA_Q_PERM], axis=1)
    bias_tab = swa_bias_table(rel_bias)
    gw = jnp.stack([jnp.pad(gla_gate_w[:, :, GLA_DK * h:GLA_DK * (h + 1)],
                            ((0, 0), (0, GLA_LR_PAD - GLA_LOWRANK), (0, GLA_DK_PAD - GLA_DK)))
                    for h in range(GLA_HEADS)], axis=1).astype(BF16)
    gb = jnp.stack([jnp.pad(gla_gate_b[:, GLA_DK * h:GLA_DK * (h + 1)],
                            ((0, 0), (0, GLA_DK_PAD - GLA_DK)))
                    for h in range(GLA_HEADS)], axis=1)[:, :, None, :]
    gn = jnp.pad(gla_norm, ((0, 0), (0, GLA_DV_PAD - GLA_DV)))[:, None, :]

    h = None
    for l in range(depth):
        if l == 0:
            proj = in_projection(xs, w_in_p, l, gain=norm_mix[0])
        else:
            proj = in_projection(h, w_in_p, l)
        a = swa_attention(proj, sinks_p, bias_tab, l)
        b = gla_attention(proj, gw, gb, gn, l)
        c = sb_attention(proj)
        xs, hm = out_projection(a, b, c, wa, wb, wc, ga, gc, xs, g_mlp, l)
        if l + 1 < depth:
            xs, h = mlp_block(hm, xs, w_mlp_in, w_mlp_out, l, g_next, final=False)
        else:
            (out,) = mlp_block(hm, xs, w_mlp_in, w_mlp_out, l, g_next, final=True)
    return out[None]
```

```python
import functools
import math

import numpy as np
import jax
import jax.numpy as jnp
from jax import lax
from jax.experimental import pallas as pl
from jax.experimental.pallas import tpu as pltpu

F32 = jnp.float32
BF16 = jnp.bfloat16

HEAD_DIM = 64
SWA_HEADS = 12
SWA_KV_HEADS = 4
SWA_WIDTH = SWA_HEADS * HEAD_DIM
SWA_KV_WIDTH = SWA_KV_HEADS * HEAD_DIM
SWA_BLOCK = 128
WINDOW = 128
REL_BUCKETS = 32
REL_MAX_DIST = 128
GLA_HEADS = 4
GLA_DV = 192
GLA_DK = 96
GLA_DK_PAD = 128
GLA_DV_PAD = 256
GLA_LOWRANK = 16
GLA_LR_PAD = 128
GLA_CHUNK = 64
GATE_NORMALIZER = 16.0
GATE_LOG_MIN = -1.0
SB_HEADS = 8
SB_WIDTH = SB_HEADS * HEAD_DIM
RMS_EPS = 1e-6
LOG2E = 1.4426950408889634
NEG_INF = -1e30

LANES = 128
VMEM_LIMIT = 56 * 1024 * 1024

OFF_BV = 0
OFF_BR = 1024
OFF_BQ = 2048
OFF_BK = 2560
OFF_CQ = 3072
OFF_CK = 3584
OFF_CV = 4096
OFF_AQ = 4608
OFF_AK = 5376
OFF_AV = 5632
OFF_BLR = 5888
PROJ_PAD_WIDTH = 6144

SWA_Q_PERM = (0, 3, 1, 4, 2, 5, 6, 9, 7, 10, 8, 11)

_O_AQ, _O_AK, _O_AV = 0, 768, 1024
_O_BQ, _O_BK, _O_BV, _O_BR, _O_BLR = 1280, 1664, 2048, 2816, 3584
_O_CQ, _O_CK, _O_CV = 3600, 4112, 4624


def _cparams(sem):
    return pltpu.CompilerParams(dimension_semantics=sem, vmem_limit_bytes=VMEM_LIMIT)


def _rms(x, gain):
    ms = jnp.mean(x * x, axis=-1, keepdims=True)
    return x * lax.rsqrt(ms + RMS_EPS) * gain


def _split_bf16(x):
    hi = x.astype(BF16)
    lo = (x - hi.astype(F32)).astype(BF16)
    return hi, lo


_NT_DIMS = (((1,), (1,)), ((), ()))


def _matmul_kernel(h_ref, w_ref, o_ref):
    o_ref[...] = lax.dot_general(h_ref[...], w_ref[...], _NT_DIMS,
                                 preferred_element_type=F32).astype(o_ref.dtype)


def _norm_matmul_kernel(x_ref, g_ref, w_ref, o_ref):
    h = _rms(x_ref[...], g_ref[...]).astype(BF16)
    o_ref[...] = lax.dot_general(h, w_ref[...], _NT_DIMS,
                                 preferred_element_type=F32).astype(o_ref.dtype)


def in_projection(h, w_all, layer, gain=None, tm=1024, tn=1024):
    s, d = h.shape
    n = w_all.shape[1]
    tm = min(tm, s)
    w_spec = pl.BlockSpec((None, tn, d), lambda i, j: (layer, j, 0))
    h_spec = pl.BlockSpec((tm, d), lambda i, j: (i, 0))
    common = dict(
        grid=(s // tm, n // tn),
        out_specs=pl.BlockSpec((tm, tn), lambda i, j: (i, j)),
        out_shape=jax.ShapeDtypeStruct((s, n), BF16),
        compiler_params=_cparams(("parallel", "arbitrary")),
        name="in_proj",
    )
    if gain is None:
        return pl.pallas_call(_matmul_kernel, in_specs=[h_spec, w_spec], **common)(h, w_all)
    return pl.pallas_call(
        _norm_matmul_kernel,
        in_specs=[h_spec, pl.BlockSpec((1, d), lambda i, j: (0, 0)), w_spec],
        **common,
    )(h, gain.reshape(1, d), w_all)


def _w_in_moves():
    scale = HEAD_DIM ** -0.5
    moves = []
    for h in range(GLA_HEADS):
        moves.append((_O_BV + GLA_DV * h, OFF_BV + GLA_DV_PAD * h, GLA_DV, 1.0))
        moves.append((_O_BR + GLA_DV * h, OFF_BR + GLA_DV_PAD * h, GLA_DV, 1.0))
        moves.append((_O_BQ + GLA_DK * h, OFF_BQ + GLA_DK_PAD * h, GLA_DK, 1.0))
        moves.append((_O_BK + GLA_DK * h, OFF_BK + GLA_DK_PAD * h, GLA_DK, 1.0))
    moves += [(_O_CQ, OFF_CQ, SB_WIDTH, scale), (_O_CK, OFF_CK, SB_WIDTH, 1.0),
              (_O_CV, OFF_CV, SB_WIDTH, 1.0)]
    for j, h in enumerate(SWA_Q_PERM):
        moves.append((_O_AQ + HEAD_DIM * h, OFF_AQ + HEAD_DIM * j, HEAD_DIM, scale))
    moves += [(_O_AK, OFF_AK, SWA_KV_WIDTH, 1.0), (_O_AV, OFF_AV, SWA_KV_WIDTH, 1.0),
              (_O_BLR, OFF_BLR, GLA_LOWRANK, 1.0)]
    return moves


def _w_in_prep_kernel(w_ref, o_ref):
    o_ref[...] = jnp.zeros_like(o_ref)
    for src, dst, width, scale in _w_in_moves():
        piece = w_ref[src:src + width, :]
        if scale != 1.0:
            piece = piece * scale
        o_ref[dst:dst + width, :] = piece.astype(o_ref.dtype)


def prep_w_in(w_in_t, tc=256):
    layers, n, d = w_in_t.shape
    return pl.pallas_call(
        _w_in_prep_kernel,
        grid=(layers, d // tc),
        in_specs=[pl.BlockSpec((None, n, tc), lambda l, i: (l, 0, i))],
        out_specs=pl.BlockSpec((None, PROJ_PAD_WIDTH, tc), lambda l, i: (l, 0, i)),
        out_shape=jax.ShapeDtypeStruct((layers, PROJ_PAD_WIDTH, d), BF16),
        compiler_params=_cparams(("parallel", "parallel")),
        name="w_in_prep",
    )(w_in_t)


def _swa_kernel(sink_ref, q_ref, kp_ref, kc_ref, vp_ref, vc_ref, bias_ref, o_ref, *, layer):
    blk = SWA_BLOCK
    nb = SWA_HEADS // 2
    k = jnp.concatenate([kp_ref[...], kc_ref[...]], axis=0)
    v = jnp.concatenate([vp_ref[...], vc_ref[...]], axis=0)
    lane = lax.broadcasted_iota(jnp.int32, (blk, LANES), 1)
    low = lane < HEAD_DIM
    top = lax.broadcasted_iota(jnp.int32, (2 * blk, 1), 0) < blk
    ones = jnp.ones((2 * blk, LANES), BF16)
    nt = (((1,), (1,)), ((), ()))

    scores = []
    for b in range(nb):
        qb = q_ref[:, LANES * b:LANES * (b + 1)]
        zero = jnp.zeros_like(qb)
        lhs = jnp.concatenate([jnp.where(low, qb, zero), jnp.where(low, zero, qb)], axis=0)
        kb = k[:, LANES * (b // 3):LANES * (b // 3 + 1)]
        scores.append(lax.dot_general(lhs, kb, nt, preferred_element_type=F32) + bias_ref[b])
    probs, sink_term = [], []
    for b in range(nb):
        sink = jnp.where(top, sink_ref[layer, 2 * b], sink_ref[layer, 2 * b + 1])
        m = jnp.maximum(jnp.max(scores[b], axis=1, keepdims=True), sink)
        probs.append(jnp.exp(scores[b] - m).astype(BF16))
        sink_term.append(jnp.exp(sink - m))
    for b in range(nb):
        vb = v[:, LANES * (b // 3):LANES * (b // 3 + 1)]
        pv = jnp.dot(probs[b], jnp.concatenate([vb, ones], axis=1),
                     preferred_element_type=F32)
        o = pv[:, :LANES] / (pv[:, LANES:] + sink_term[b])
        o_ref[:, LANES * b:LANES * (b + 1)] = jnp.where(low, o[:blk], o[blk:]).astype(o_ref.dtype)


def swa_attention(proj, sinks_perm, bias_tab, layer):
    s = proj.shape[0]
    blk = SWA_BLOCK
    qi = OFF_AQ // SWA_WIDTH
    ki = OFF_AK // SWA_KV_WIDTH
    vi = OFF_AV // SWA_KV_WIDTH
    prev = lambda i: jnp.maximum(i - 1, 0)
    return pl.pallas_call(
        functools.partial(_swa_kernel, layer=layer),
        grid=(s // blk,),
        in_specs=[pl.BlockSpec(memory_space=pltpu.SMEM),
                  pl.BlockSpec((blk, SWA_WIDTH), lambda i: (i, qi)),
                  pl.BlockSpec((blk, SWA_KV_WIDTH), lambda i: (prev(i), ki)),
                  pl.BlockSpec((blk, SWA_KV_WIDTH), lambda i: (i, ki)),
                  pl.BlockSpec((blk, SWA_KV_WIDTH), lambda i: (prev(i), vi)),
                  pl.BlockSpec((blk, SWA_KV_WIDTH), lambda i: (i, vi)),
                  pl.BlockSpec((None, SWA_HEADS // 2, 2 * blk, 2 * blk),
                               lambda i: (jnp.minimum(i, 1), 0, 0, 0))],
        out_specs=pl.BlockSpec((blk, SWA_WIDTH), lambda i: (i, 0)),
        out_shape=jax.ShapeDtypeStruct((s, SWA_WIDTH), BF16),
        compiler_params=_cparams(("parallel",)),
        name="swa",
    )(sinks_perm, proj, proj, proj, proj, proj, bias_tab)


def _t5_causal_bucket(dist):
    max_exact = REL_BUCKETS // 2
    is_small = dist < max_exact
    ratio = (jnp.log(jnp.maximum(dist, 1).astype(F32) / max_exact)
             / math.log(REL_MAX_DIST / max_exact))
    large = max_exact + (ratio * (REL_BUCKETS - max_exact)).astype(jnp.int32)
    large = jnp.minimum(large, REL_BUCKETS - 1)
    return jnp.where(is_small, dist, large)


def swa_bias_table(rel_bias):
    blk = SWA_BLOCK
    qpos = jnp.arange(blk) + blk
    kpos = jnp.arange(2 * blk)
    dist = qpos[:, None] - kpos[None, :]
    in_window = (dist >= 0) & (dist < WINDOW)
    bucket = _t5_causal_bucket(jnp.maximum(dist, 0))
    table = rel_bias.astype(F32)
    table = jnp.stack([table[:, h] for h in SWA_Q_PERM])
    bias = jnp.full((SWA_HEADS, blk, 2 * blk), NEG_INF, F32)
    for bkt in range(REL_BUCKETS):
        hit = (in_window & (bucket == bkt))[None]
        bias = jnp.where(hit, table[:, bkt][:, None, None], bias)
    bias = bias.reshape(SWA_HEADS // 2, 2 * blk, 2 * blk)
    first = jnp.where((kpos < blk)[None, None, :], NEG_INF, bias)
    return jnp.stack([first, bias])


def _gla_kernel(q_ref, k_ref, v_ref, r_ref, lr_ref, gw_ref, gb_ref, gn_ref, lm_ref, o_ref,
                st_ref, *, rows, hps):
    t = pl.program_id(1)
    ch = GLA_CHUNK
    nch = rows // ch
    heads = range(hps)

    @pl.when(t == 0)
    def _():
        st_ref[...] = jnp.zeros_like(st_ref)

    nt = (((1,), (1,)), ((), ()))
    tn = (((0,), (0,)), ((), ()))
    rowi = lax.broadcasted_iota(jnp.int32, (rows, rows), 0)
    coli = lax.broadcasted_iota(jnp.int32, (rows, rows), 1)
    same_chunk_causal = jnp.logical_and(coli <= rowi, (rowi // ch) == (coli // ch))
    lm = lm_ref[...]
    lr = lr_ref[...]
    gn = gn_ref[...]

    gp = [jnp.dot(lr, gw_ref[h], preferred_element_type=F32) + gb_ref[h] for h in heads]
    lg = []
    for h in heads:
        log_sig = jnp.minimum(gp[h], 0.0) - jnp.log(1.0 + jnp.exp(-jnp.abs(gp[h])))
        lg.append(jnp.maximum(log_sig * (1.0 / GATE_NORMALIZER), GATE_LOG_MIN))
    bb = []
    for h in heads:
        hi, lo = _split_bf16(lg[h])
        bb.append(jnp.dot(lm, hi, preferred_element_type=F32)
                  + jnp.dot(lm, lo, preferred_element_type=F32))
    q_dec, k_inv, k_end, dec = [], [], [], []
    for h in heads:
        b, b_last = bb[h][:rows], bb[h][rows:]
        q = q_ref[:, GLA_DK_PAD * h:GLA_DK_PAD * (h + 1)].astype(F32) * (GLA_DK ** -0.5)
        kk = k_ref[:, GLA_DK_PAD * h:GLA_DK_PAD * (h + 1)].astype(F32)
        q_dec.append((q * jnp.exp(b)).astype(BF16))
        k_inv.append((kk * jnp.exp(-b)).astype(BF16))
        k_end.append((kk * jnp.exp(b_last - b)).astype(BF16))
        dec.append(jnp.exp(b_last))
    v = [v_ref[:, GLA_DV_PAD * h:GLA_DV_PAD * (h + 1)] for h in heads]
    sc = []
    for h in heads:
        s_h = lax.dot_general(q_dec[h], k_inv[h], nt, preferred_element_type=F32)
        sc.append(jnp.where(same_chunk_causal, s_h, 0.0).astype(BF16))
    o_intra = [jnp.dot(sc[h], v[h], preferred_element_type=F32) for h in heads]
    d_st = [[lax.dot_general(v[h][c * ch:(c + 1) * ch], k_end[h][c * ch:(c + 1) * ch], tn,
                             preferred_element_type=F32) for c in range(nch)] for h in heads]
    st_in = []
    for h in heads:
        st = st_ref[h]
        states = []
        for c in range(nch):
            states.append(st.astype(BF16))
            st = st * dec[h][c * ch:c * ch + 1, :] + d_st[h][c]
        st_ref[h] = st
        st_in.append(states)
    for h in heads:
        o_inter = [lax.dot_general(q_dec[h][c * ch:(c + 1) * ch], st_in[h][c], nt,
                                   preferred_element_type=F32) for c in range(nch)]
        o = o_intra[h] + jnp.concatenate(o_inter, axis=0)
        ms = jnp.sum(o * o, axis=1, keepdims=True) * (1.0 / GLA_DV)
        y = o * lax.rsqrt(ms + RMS_EPS) * gn
        rr = r_ref[:, GLA_DV_PAD * h:GLA_DV_PAD * (h + 1)].astype(F32)
        o_ref[:, GLA_DV_PAD * h:GLA_DV_PAD * (h + 1)] = (
            y * (rr / (1.0 + jnp.exp(-rr)))).astype(o_ref.dtype)


def gla_attention(proj, gw_pad, gb_pad, gn_pad, layer, rows=256, hps=4):
    s = proj.shape[0]
    rows = min(rows, s)
    ch = GLA_CHUNK
    dkw, dvw = hps * GLA_DK_PAD, hps * GLA_DV_PAD
    qi, ki = OFF_BQ // dkw, OFF_BK // dkw
    vi, ri = OFF_BV // dvw, OFF_BR // dvw
    li = OFF_BLR // GLA_LR_PAD
    idx = np.arange(rows)
    same = (idx[:, None] // ch) == (idx[None, :] // ch)
    prefix = same & (idx[None, :] <= idx[:, None])
    lm = jnp.asarray(np.concatenate([prefix, same], axis=0).astype(np.float32), dtype=BF16)
    return pl.pallas_call(
        functools.partial(_gla_kernel, rows=rows, hps=hps),
        grid=(GLA_HEADS // hps, s // rows),
        in_specs=[pl.BlockSpec((rows, dkw), lambda g, t: (t, qi + g)),
                  pl.BlockSpec((rows, dkw), lambda g, t: (t, ki + g)),
                  pl.BlockSpec((rows, dvw), lambda g, t: (t, vi + g)),
                  pl.BlockSpec((rows, dvw), lambda g, t: (t, ri + g)),
                  pl.BlockSpec((rows, GLA_LR_PAD), lambda g, t: (t, li)),
                  pl.BlockSpec((None, hps, GLA_LR_PAD, GLA_DK_PAD),
                               lambda g, t: (layer, g, 0, 0)),
                  pl.BlockSpec((None, hps, 1, GLA_DK_PAD), lambda g, t: (layer, g, 0, 0)),
                  pl.BlockSpec((None, 1, GLA_DV_PAD), lambda g, t: (layer, 0, 0)),
                  pl.BlockSpec((2 * rows, rows), lambda g, t: (0, 0))],
        out_specs=pl.BlockSpec((rows, dvw), lambda g, t: (t, g)),
        out_shape=jax.ShapeDtypeStruct((s, GLA_HEADS * GLA_DV_PAD), BF16),
        scratch_shapes=[pltpu.VMEM((hps, GLA_DV_PAD, GLA_DK_PAD), F32)],
        compiler_params=_cparams(("parallel", "arbitrary")),
        name="gla",
    )(proj, proj, proj, proj, proj, gw_pad, gb_pad, gn_pad, lm)


def _swa_gla_kernel(sink_ref, aq_ref, akp_ref, akc_ref, avp_ref, avc_ref, bias_ref,
                    q_ref, k_ref, v_ref, r_ref, lr_ref, gw_ref, gb_ref, gn_ref, lm_ref,
                    a_ref, b_ref, st_ref, *, layer, rows):
    i = pl.program_id(0)
    blk = SWA_BLOCK
    nb = SWA_HEADS // 2
    ch = GLA_CHUNK
    nch = rows // ch
    heads = range(GLA_HEADS)
    nt = (((1,), (1,)), ((), ()))
    tn = (((0,), (0,)), ((), ()))

    @pl.when(i == 0)
    def _():
        st_ref[...] = jnp.zeros_like(st_ref)

    lane = lax.broadcasted_iota(jnp.int32, (blk, LANES), 1)
    low = lane < HEAD_DIM
    top = lax.broadcasted_iota(jnp.int32, (2 * blk, 1), 0) < blk
    ones = jnp.ones((2 * blk, LANES), BF16)
    kc, vc = akc_ref[...], avc_ref[...]
    swa_k = (jnp.concatenate([akp_ref[...], kc[:blk]], axis=0), kc)
    swa_v = (jnp.concatenate([avp_ref[...], vc[:blk]], axis=0), vc)
    swa = [dict(), dict()]

    def swa_scores(j):
        variant = jnp.minimum(2 * i + j, 1)
        out = []
        for b in range(nb):
            qb = aq_ref[blk * j:blk * (j + 1), LANES * b:LANES * (b + 1)]
            zero = jnp.zeros_like(qb)
            lhs = jnp.concatenate([jnp.where(low, qb, zero), jnp.where(low, zero, qb)], axis=0)
            kb = swa_k[j][:, LANES * (b // 3):LANES * (b // 3 + 1)]
            out.append(lax.dot_general(lhs, kb, nt, preferred_element_type=F32)
                       + bias_ref[variant, b])
        swa[j]["scores"] = out

    def swa_softmax(j):
        probs, sink_term = [], []
        for b in range(nb):
            sink = jnp.where(top, sink_ref[layer, 2 * b], sink_ref[layer, 2 * b + 1])
            sc_b = swa[j]["scores"][b]
            m = jnp.maximum(jnp.max(sc_b, axis=1, keepdims=True), sink)
            probs.append(jnp.exp(sc_b - m).astype(BF16))
            sink_term.append(jnp.exp(sink - m))
        swa[j]["probs"], swa[j]["sink"] = probs, sink_term

    def swa_output(j):
        for b in range(nb):
            vb = swa_v[j][:, LANES * (b // 3):LANES * (b // 3 + 1)]
            pv = jnp.dot(swa[j]["probs"][b], jnp.concatenate([vb, ones], axis=1),
                         preferred_element_type=F32)
            o = pv[:, :LANES] / (pv[:, LANES:] + swa[j]["sink"][b])
            a_ref[blk * j:blk * (j + 1), LANES * b:LANES * (b + 1)] = (
                jnp.where(low, o[:blk], o[blk:]).astype(a_ref.dtype))

    rowi = lax.broadcasted_iota(jnp.int32, (rows, rows), 0)
    coli = lax.broadcasted_iota(jnp.int32, (rows, rows), 1)
    same_chunk_causal = jnp.logical_and(coli <= rowi, (rowi // ch) == (coli // ch))
    gla = dict()

    def gla_gate():
        lr = lr_ref[...]
        lg = []
        for h in heads:
            gp = jnp.dot(lr, gw_ref[h], preferred_element_type=F32) + gb_ref[h]
            log_sig = jnp.minimum(gp, 0.0) - jnp.log(1.0 + jnp.exp(-jnp.abs(gp)))
            lg.append(jnp.maximum(log_sig * (1.0 / GATE_NORMALIZER), GATE_LOG_MIN))
        gla["lg"] = lg

    def gla_decay():
        lm = lm_ref[...]
        q_dec, k_inv, k_end, dec = [], [], [], []
        for h in heads:
            hi, lo = _split_bf16(gla["lg"][h])
            bb = (jnp.dot(lm, hi, preferred_element_type=F32)
                  + jnp.dot(lm, lo, preferred_element_type=F32))
            b, b_last = bb[:rows], bb[rows:]
            q = q_ref[:, GLA_DK_PAD * h:GLA_DK_PAD * (h + 1)].astype(F32) * (GLA_DK ** -0.5)
            kk = k_ref[:, GLA_DK_PAD * h:GLA_DK_PAD * (h + 1)].astype(F32)
            q_dec.append((q * jnp.exp(b)).astype(BF16))
            k_inv.append((kk * jnp.exp(-b)).astype(BF16))
            k_end.append((kk * jnp.exp(b_last - b)).astype(BF16))
            dec.append(jnp.exp(b_last))
        gla.update(q_dec=q_dec, k_inv=k_inv, k_end=k_end, dec=dec)

    def gla_intra():
        v = [v_ref[:, GLA_DV_PAD * h:GLA_DV_PAD * (h + 1)] for h in heads]
        sc = []
        for h in heads:
            s_h = lax.dot_general(gla["q_dec"][h], gla["k_inv"][h], nt,
                                  preferred_element_type=F32)
            sc.append(jnp.where(same_chunk_causal, s_h, 0.0).astype(BF16))
        gla["o_intra"] = [jnp.dot(sc[h], v[h], preferred_element_type=F32) for h in heads]
        gla["d_st"] = [[lax.dot_general(v[h][c * ch:(c + 1) * ch],
                                        gla["k_end"][h][c * ch:(c + 1) * ch], tn,
                                        preferred_element_type=F32)
                        for c in range(nch)] for h in heads]

    def gla_state():
        st_in = []
        for h in heads:
            st = st_ref[h]
            states = []
            for c in range(nch):
                states.append(st.astype(BF16))
                st = st * gla["dec"][h][c * ch:c * ch + 1, :] + gla["d_st"][h][c]
            st_ref[h] = st
            st_in.append(states)
        gla["st_in"] = st_in

    def gla_output():
        gn = gn_ref[...]
        for h in heads:
            o_inter = [lax.dot_general(gla["q_dec"][h][c * ch:(c + 1) * ch], gla["st_in"][h][c],
                                       nt, preferred_element_type=F32) for c in range(nch)]
            o = gla["o_intra"][h] + jnp.concatenate(o_inter, axis=0)
            ms = jnp.sum(o * o, axis=1, keepdims=True) * (1.0 / GLA_DV)
            y = o * lax.rsqrt(ms + RMS_EPS) * gn
            rr = r_ref[:, GLA_DV_PAD * h:GLA_DV_PAD * (h + 1)].astype(F32)
            b_ref[:, GLA_DV_PAD * h:GLA_DV_PAD * (h + 1)] = (
                y * (rr / (1.0 + jnp.exp(-rr)))).astype(b_ref.dtype)

    gla_gate()
    swa_scores(0)
    swa_scores(1)
    gla_decay()
    swa_softmax(0)
    gla_intra()
    swa_softmax(1)
    swa_output(0)
    gla_state()
    swa_output(1)
    gla_output()


def swa_gla_attention(proj, sinks_perm, bias_tab, gw_pad, gb_pad, gn_pad, layer, rows=256):
    s = proj.shape[0]
    rows = min(rows, s)
    blk, ch = SWA_BLOCK, GLA_CHUNK
    assert rows == 2 * blk
    aq_i = OFF_AQ // SWA_WIDTH
    kv_w = SWA_KV_WIDTH
    dkw, dvw = GLA_HEADS * GLA_DK_PAD, GLA_HEADS * GLA_DV_PAD
    idx = np.arange(rows)
    same = (idx[:, None] // ch) == (idx[None, :] // ch)
    prefix = same & (idx[None, :] <= idx[:, None])
    lm = jnp.asarray(np.concatenate([prefix, same], axis=0).astype(np.float32), dtype=BF16)
    prev = lambda i: jnp.maximum(2 * i - 1, 0)
    col = lambda off, width: (lambda i: (i, off // width))
    return pl.pallas_call(
        functools.partial(_swa_gla_kernel, layer=layer, rows=rows),
        grid=(s // rows,),
        in_specs=[pl.BlockSpec(memory_space=pltpu.SMEM),
                  pl.BlockSpec((rows, SWA_WIDTH), lambda i: (i, aq_i)),
                  pl.BlockSpec((blk, kv_w), lambda i: (prev(i), OFF_AK // kv_w)),
                  pl.BlockSpec((rows, kv_w), col(OFF_AK, kv_w)),
                  pl.BlockSpec((blk, kv_w), lambda i: (prev(i), OFF_AV // kv_w)),
                  pl.BlockSpec((rows, kv_w), col(OFF_AV, kv_w)),
                  pl.BlockSpec(bias_tab.shape, lambda i: (0, 0, 0, 0)),
                  pl.BlockSpec((rows, dkw), col(OFF_BQ, dkw)),
                  pl.BlockSpec((rows, dkw), col(OFF_BK, dkw)),
                  pl.BlockSpec((rows, dvw), col(OFF_BV, dvw)),
                  pl.BlockSpec((rows, dvw), col(OFF_BR, dvw)),
                  pl.BlockSpec((rows, GLA_LR_PAD), col(OFF_BLR, GLA_LR_PAD)),
                  pl.BlockSpec((None, GLA_HEADS, GLA_LR_PAD, GLA_DK_PAD),
                               lambda i: (layer, 0, 0, 0)),
                  pl.BlockSpec((None, GLA_HEADS, 1, GLA_DK_PAD), lambda i: (layer, 0, 0, 0)),
                  pl.BlockSpec((None, 1, GLA_DV_PAD), lambda i: (layer, 0, 0)),
                  pl.BlockSpec((2 * rows, rows), lambda i: (0, 0))],
        out_specs=[pl.BlockSpec((rows, SWA_WIDTH), lambda i: (i, 0)),
                   pl.BlockSpec((rows, dvw), lambda i: (i, 0))],
        out_shape=[jax.ShapeDtypeStruct((s, SWA_WIDTH), BF16),
                   jax.ShapeDtypeStruct((s, dvw), BF16)],
        scratch_shapes=[pltpu.VMEM((GLA_HEADS, GLA_DV_PAD, GLA_DK_PAD), F32)],
        compiler_params=_cparams(("arbitrary",)),
        name="swa_gla",
    )(sinks_perm, proj, proj, proj, proj, proj, bias_tab,
      proj, proj, proj, proj, proj, gw_pad, gb_pad, gn_pad, lm)


def _sb_kernel(q_ref, k_ref, v_ref, u_ref, o_ref, acc_ref, car_ref, qm_ref, vm_ref,
               z0_ref, z1_ref, z2_ref, z3_ref, nl0_ref, nl1_ref, w0_ref, w1_ref,
               *, bq, nblk, unroll):
    lane = lax.broadcasted_iota(jnp.int32, (bq, LANES), 1)
    low = lane < HEAD_DIM
    nt = (((1,), (1,)), ((), ()))
    z_ref = (z0_ref, z1_ref, z2_ref, z3_ref)
    nl_ref, w_ref = (nl0_ref, nl1_ref), (w0_ref, w1_ref)

    def rows(blk):
        if isinstance(blk, int):
            return pl.ds(blk * bq, bq)
        return pl.ds(pl.multiple_of(blk * bq, bq), bq)

    def strict_mask():
        rowi = lax.broadcasted_iota(jnp.int32, (2 * bq, bq), 0)
        coli = lax.broadcasted_iota(jnp.int32, (2 * bq, bq), 1)
        return coli < (rowi & (bq - 1))

    def s1a(item, tick4):
        i, c = item
        z_ref[tick4][...] = lax.dot_general(qm_ref[i], k_ref[rows(c), :], nt,
                                            preferred_element_type=F32)

    def s1b(tick4, masked):
        z = z_ref[tick4][...].astype(BF16)
        nl = jnp.maximum(z, 0.0) + jnp.log(1.0 + jnp.exp2(jnp.abs(z) * (-LOG2E)))
        if masked:
            nl = jnp.where(strict_mask(), nl, jnp.zeros_like(nl))
        nl_ref[tick4 % 2][...] = nl

    def s2(item, tick4, masked, first):
        i, _ = item
        cs = jnp.dot(nl_ref[tick4 % 2][...], u_ref[...], preferred_element_type=F32)
        tot = jnp.broadcast_to(cs[:, 0:1], (2 * bq, LANES))
        if first:
            arg = z_ref[tick4][...] - cs
            car_ref[i] = tot
        else:
            car = car_ref[i]
            arg = z_ref[tick4][...] - cs - jnp.concatenate([car, car], axis=1)
            car_ref[i] = car + tot
        w = jnp.exp(arg)
        if masked:
            w = jnp.where(strict_mask(), w, 0.0)
        w = w.astype(BF16)
        w_ref[tick4 % 2][:, :bq] = w[:bq]
        w_ref[tick4 % 2][:, bq:] = w[bq:]

    def s3(item, tick4, first):
        i, c = item
        pv = jnp.dot(w_ref[tick4 % 2][...], vm_ref[c], preferred_element_type=F32)
        if first:
            acc_ref[rows(i), :] = pv
        else:
            acc_ref[rows(i), :] += pv

    def run_pipeline(items, step, masked, first):
        n = len(items)

        def tick(t4, its, do):
            if do[1]:
                s1b((t4 - 1) % 4, masked)
            if do[0]:
                s1a(its[0], t4)
            if do[2]:
                s2(its[2], (t4 - 2) % 4, masked, first)
            if do[3]:
                s3(its[3], (t4 - 3) % 4, first)

        def static_tick(t):
            do = [0 <= t - j < n for j in range(4)]
            its = [items[t - j] if do[j] else None for j in range(4)]
            tick(t % 4, its, do)

        lo_t, hi_t = 3, n - 1
        groups = max(hi_t - lo_t + 1, 0) // unroll
        if groups < 2:
            groups = 0
        for t in range(min(lo_t, n + 3)):
            static_tick(t)
        if groups:
            def body(_, carry):
                its = [(carry[2 * j], carry[2 * j + 1]) for j in range(4)]
                for k in range(unroll):
                    tick((lo_t + k) % 4, its, [True] * 4)
                    its = [step(*its[0])] + its[:3]
                return tuple(x for it in its for x in it)

            init = tuple(jnp.int32(x) for j in range(4) for x in items[lo_t - j])
            lax.fori_loop(0, groups, body, init)
        for t in range(lo_t + unroll * groups, n + 3):
            static_tick(t)

    for blk in range(nblk):
        r = rows(blk)
        qb, vb = q_ref[r, :], v_ref[r, :]
        zero = jnp.zeros_like(qb)
        qm_ref[blk, :bq] = jnp.where(low, qb, zero)
        qm_ref[blk, bq:] = jnp.where(low, zero, qb)
        vm_ref[blk, :bq] = jnp.where(low, vb, zero)
        vm_ref[blk, bq:] = jnp.where(low, zero, vb)

    diag = [(i, i) for i in range(nblk)]
    run_pipeline(diag, lambda i, c: (i + 1, c + 1), masked=True, first=True)

    off = [(i, c) for i in range(1, nblk) for c in range(i - 1, -1, -1)]

    def off_step(i, c):
        wrap = c == 0
        i2 = jnp.where(wrap, i + 1, i)
        return i2, jnp.where(wrap, i2 - 1, c - 1)

    if off:
        run_pipeline(off, off_step, masked=False, first=False)

    o_ref[...] = acc_ref[...].astype(o_ref.dtype)


def sb_attention(proj, bq=256, unroll=32):
    s = proj.shape[0]
    bq = min(bq, s)
    nblk = s // bq
    assert bq & (bq - 1) == 0 and unroll % 4 == 0
    qi = OFF_CQ // LANES
    ki = OFF_CK // LANES
    vi = OFF_CV // LANES
    u = (np.arange(bq)[:, None] >= np.arange(bq)[None, :]).astype(np.float32)
    u = jnp.asarray(u, dtype=BF16)
    slot = lambda shape, dtype: pltpu.VMEM(shape, dtype)
    return pl.pallas_call(
        functools.partial(_sb_kernel, bq=bq, nblk=nblk, unroll=unroll),
        grid=(SB_HEADS // 2,),
        in_specs=[pl.BlockSpec((s, LANES), lambda p: (0, qi + p)),
                  pl.BlockSpec((s, LANES), lambda p: (0, ki + p)),
                  pl.BlockSpec((s, LANES), lambda p: (0, vi + p)),
                  pl.BlockSpec((bq, bq), lambda p: (0, 0))],
        out_specs=pl.BlockSpec((s, LANES), lambda p: (0, p)),
        out_shape=jax.ShapeDtypeStruct((s, SB_WIDTH), BF16),
        scratch_shapes=[pltpu.VMEM((s, LANES), F32),
                        pltpu.VMEM((nblk, 2 * bq, LANES), F32),
                        pltpu.VMEM((nblk, 2 * bq, LANES), BF16),
                        pltpu.VMEM((nblk, 2 * bq, LANES), BF16)]
                       + [slot((2 * bq, bq), F32)] * 4
                       + [slot((2 * bq, bq), BF16)] * 2
                       + [slot((bq, 2 * bq), BF16)] * 2,
        compiler_params=_cparams(("parallel",)),
        name="stickbreak",
    )(proj, proj, proj, u)


def _oproj_kernel(a_ref, b_ref, c_ref, wa_ref, wb_ref, wc_ref, ga_ref, gc_ref,
                  x_ref, gn_ref, xo_ref, ho_ref):
    an = _rms(a_ref[...].astype(F32), ga_ref[...]).astype(BF16)
    cn = _rms(c_ref[...].astype(F32), gc_ref[...]).astype(BF16)
    y = (jnp.dot(an, wa_ref[...], preferred_element_type=F32)
         + jnp.dot(b_ref[...], wb_ref[...], preferred_element_type=F32)
         + jnp.dot(cn, wc_ref[...], preferred_element_type=F32))
    xn = x_ref[...] + y
    xo_ref[...] = xn
    ho_ref[...] = _rms(xn, gn_ref[...]).astype(ho_ref.dtype)


def out_projection(a, b, c, wa, wb, wc, ga, gc, x, gn, layer, tm=512):
    s, d = x.shape
    tm = min(tm, s)
    row = lambda i: (i, 0)
    fixed = lambda i: (layer, 0, 0)
    per_layer = lambda arr: pl.BlockSpec((None,) + arr.shape[1:], fixed)
    return pl.pallas_call(
        _oproj_kernel,
        grid=(s // tm,),
        in_specs=[pl.BlockSpec((tm, a.shape[1]), row),
                  pl.BlockSpec((tm, b.shape[1]), row),
                  pl.BlockSpec((tm, c.shape[1]), row),
                  per_layer(wa), per_layer(wb), per_layer(wc), per_layer(ga), per_layer(gc),
                  pl.BlockSpec((tm, d), row),
                  per_layer(gn)],
        out_specs=[pl.BlockSpec((tm, d), row), pl.BlockSpec((tm, d), row)],
        out_shape=[jax.ShapeDtypeStruct((s, d), F32), jax.ShapeDtypeStruct((s, d), BF16)],
        compiler_params=_cparams(("parallel",)),
        name="out_proj",
    )(a, b, c, wa, wb, wc, ga, gc, x, gn)


def _mlp_kernel(h_ref, x_ref, w1_ref, w2_ref, gn_ref, *out_refs, final):
    acc_ref = out_refs[0]
    j = pl.program_id(1)

    @pl.when(j == 0)
    def _():
        acc_ref[...] = x_ref[...]

    u = jnp.dot(h_ref[...], w1_ref[...].astype(BF16), preferred_element_type=F32)
    act = jnp.square(jnp.maximum(u, 0.0)).astype(BF16)
    acc_ref[...] += jnp.dot(act, w2_ref[...].astype(BF16), preferred_element_type=F32)

    @pl.when(j == pl.num_programs(1) - 1)
    def _():
        y = _rms(acc_ref[...], gn_ref[...])
        if final:
            acc_ref[...] = y
        else:
            out_refs[1][...] = y.astype(out_refs[1].dtype)


def mlp_block(h, x, w1_all, w2_all, layer, gn, final, tm=1024, tf=512):
    s, d = x.shape
    ff = w1_all.shape[2]
    tm = min(tm, s)
    row = lambda i, j: (i, 0)
    once = pl.Buffered(1)
    out_specs = [pl.BlockSpec((tm, d), row)]
    out_shape = [jax.ShapeDtypeStruct((s, d), F32)]
    if not final:
        out_specs.append(pl.BlockSpec((tm, d), row))
        out_shape.append(jax.ShapeDtypeStruct((s, d), BF16))
    return pl.pallas_call(
        functools.partial(_mlp_kernel, final=final),
        grid=(s // tm, ff // tf),
        in_specs=[pl.BlockSpec((tm, d), row, pipeline_mode=once),
                  pl.BlockSpec((tm, d), row, pipeline_mode=once),
                  pl.BlockSpec((None, d, tf), lambda i, j: (layer, 0, j)),
                  pl.BlockSpec((None, tf, d), lambda i, j: (layer, j, 0)),
                  pl.BlockSpec((None, 1, d), lambda i, j: (layer, 0, 0))],
        out_specs=out_specs,
        out_shape=out_shape,
        compiler_params=_cparams(("parallel", "arbitrary")),
        name="mlp",
    )(h, x, w1_all, w2_all, gn)


def _head_rows(w, off, heads, width, padded):
    pieces = []
    for h in range(heads):
        pieces.append(w[..., off + h * width:off + (h + 1) * width, :])
        if padded > width:
            pieces.append(jnp.zeros(w.shape[:-2] + (padded - width, w.shape[-1]), w.dtype))
    return pieces


def _swa_perm_cols(w):
    return jnp.concatenate([w[..., HEAD_DIM * h:HEAD_DIM * (h + 1)] for h in SWA_Q_PERM], axis=-1)


def kernel(x, norm_mix, w_in, swa_sinks, rel_bias, gla_gate_w, gla_gate_b, gla_norm,
           swa_out_norm, sb_out_norm, w_out, norm_mlp, w_mlp_in, w_mlp_out, norm_final):
    depth = w_in.shape[0]
    xs = x[0]
    gla_w = GLA_HEADS * GLA_DV
    w_in_p = prep_w_in(jnp.swapaxes(w_in, 1, 2))
    wa = jnp.concatenate([w_out[:, HEAD_DIM * h:HEAD_DIM * (h + 1)] for h in SWA_Q_PERM],
                         axis=1).astype(BF16)
    wb = jnp.concatenate(_head_rows(w_out, SWA_WIDTH, GLA_HEADS, GLA_DV, GLA_DV_PAD),
                         axis=1).astype(BF16)
    wc = w_out[:, SWA_WIDTH + gla_w:].astype(BF16)
    ga = _swa_perm_cols(swa_out_norm)[:, None, :]
    gc = sb_out_norm[:, None, :]
    g_mlp = norm_mlp[:, None, :]
    g_next = jnp.concatenate([norm_mix[1:], norm_final[None]], axis=0)[:, None, :]
    sinks_p = jnp.concatenate([swa_sinks[:, h:h + 1] for h in SWA_Q_PERM], axis=1)
    bias_tab = swa_bias_table(rel_bias)
    gw = jnp.stack([jnp.pad(gla_gate_w[:, :, GLA_DK * h:GLA_DK * (h + 1)],
                            ((0, 0), (0, GLA_LR_PAD - GLA_LOWRANK), (0, GLA_DK_PAD - GLA_DK)))
                    for h in range(GLA_HEADS)], axis=1).astype(BF16)
    gb = jnp.stack([jnp.pad(gla_gate_b[:, GLA_DK * h:GLA_DK * (h + 1)],
                            ((0, 0), (0, GLA_DK_PAD - GLA_DK)))
                    for h in range(GLA_HEADS)], axis=1)[:, :, None, :]
    gn = jnp.pad(gla_norm, ((0, 0), (0, GLA_DV_PAD - GLA_DV)))[:, None, :]

    h = None
    for l in range(depth):
        if l == 0:
            proj = in_projection(xs, w_in_p, l, gain=norm_mix[0])
        else:
            proj = in_projection(h, w_in_p, l)
        a, b = swa_gla_attention(proj, sinks_p, bias_tab, gw, gb, gn, l)
        c = sb_attention(proj)
        xs, hm = out_projection(a, b, c, wa, wb, wc, ga, gc, xs, g_mlp, l)
        if l + 1 < depth:
            xs, h = mlp_block(hm, xs, w_mlp_in, w_mlp_out, l, g_next, final=False)
        else:
            (out,) = mlp_block(hm, xs, w_mlp_in, w_mlp_out, l, g_next, final=True)
    return out[None]
```

```python
import functools
import math

import numpy as np
import jax
import jax.numpy as jnp
from jax import lax
from jax.experimental import pallas as pl
from jax.experimental.pallas import tpu as pltpu

F32 = jnp.float32
BF16 = jnp.bfloat16

HEAD_DIM = 64
SWA_HEADS = 12
SWA_KV_HEADS = 4
SWA_WIDTH = SWA_HEADS * HEAD_DIM
SWA_KV_WIDTH = SWA_KV_HEADS * HEAD_DIM
SWA_BLOCK = 128
WINDOW = 128
REL_BUCKETS = 32
REL_MAX_DIST = 128
GLA_HEADS = 4
GLA_DV = 192
GLA_DK = 96
GLA_DK_PAD = 128
GLA_DV_PAD = 256
GLA_LOWRANK = 16
GLA_LR_PAD = 128
GLA_CHUNK = 64
GATE_NORMALIZER = 16.0
GATE_LOG_MIN = -1.0
SB_HEADS = 8
SB_WIDTH = SB_HEADS * HEAD_DIM
RMS_EPS = 1e-6
LOG2E = 1.4426950408889634
NEG_INF = -1e30

LANES = 128
VMEM_LIMIT = 56 * 1024 * 1024

OFF_BV = 0
OFF_BR = 1024
OFF_BQ = 2048
OFF_BK = 2560
OFF_CQ = 3072
OFF_CK = 3584
OFF_CV = 4096
OFF_AQ = 4608
OFF_AK = 5376
OFF_AV = 5632
OFF_BLR = 5888
PROJ_PAD_WIDTH = 6144

SWA_Q_PERM = (0, 3, 1, 4, 2, 5, 6, 9, 7, 10, 8, 11)

_O_AQ, _O_AK, _O_AV = 0, 768, 1024
_O_BQ, _O_BK, _O_BV, _O_BR, _O_BLR = 1280, 1664, 2048, 2816, 3584
_O_CQ, _O_CK, _O_CV = 3600, 4112, 4624


def _cparams(sem):
    return pltpu.CompilerParams(dimension_semantics=sem, vmem_limit_bytes=VMEM_LIMIT)


def _rms(x, gain):
    ms = jnp.mean(x * x, axis=-1, keepdims=True)
    return x * lax.rsqrt(ms + RMS_EPS) * gain


def _split_bf16(x):
    hi = x.astype(BF16)
    lo = (x - hi.astype(F32)).astype(BF16)
    return hi, lo


_NT_DIMS = (((1,), (1,)), ((), ()))


def _matmul_kernel(h_ref, w_ref, o_ref):
    o_ref[...] = lax.dot_general(h_ref[...], w_ref[...], _NT_DIMS,
                                 preferred_element_type=F32).astype(o_ref.dtype)


def _norm_matmul_kernel(x_ref, g_ref, w_ref, o_ref):
    h = _rms(x_ref[...], g_ref[...]).astype(BF16)
    o_ref[...] = lax.dot_general(h, w_ref[...], _NT_DIMS,
                                 preferred_element_type=F32).astype(o_ref.dtype)


def in_projection(h, w_all, layer, gain=None, tm=1024, tn=1024):
    s, d = h.shape
    n = w_all.shape[1]
    tm = min(tm, s)
    w_spec = pl.BlockSpec((None, tn, d), lambda i, j: (layer, j, 0))
    h_spec = pl.BlockSpec((tm, d), lambda i, j: (i, 0))
    common = dict(
        grid=(s // tm, n // tn),
        out_specs=pl.BlockSpec((tm, tn), lambda i, j: (i, j)),
        out_shape=jax.ShapeDtypeStruct((s, n), BF16),
        compiler_params=_cparams(("parallel", "arbitrary")),
        name="in_proj",
    )
    if gain is None:
        return pl.pallas_call(_matmul_kernel, in_specs=[h_spec, w_spec], **common)(h, w_all)
    return pl.pallas_call(
        _norm_matmul_kernel,
        in_specs=[h_spec, pl.BlockSpec((1, d), lambda i, j: (0, 0)), w_spec],
        **common,
    )(h, gain.reshape(1, d), w_all)


def _w_in_moves():
    scale = HEAD_DIM ** -0.5
    moves = []
    for h in range(GLA_HEADS):
        moves.append((_O_BV + GLA_DV * h, OFF_BV + GLA_DV_PAD * h, GLA_DV, 1.0))
        moves.append((_O_BR + GLA_DV * h, OFF_BR + GLA_DV_PAD * h, GLA_DV, 1.0))
        moves.append((_O_BQ + GLA_DK * h, OFF_BQ + GLA_DK_PAD * h, GLA_DK, 1.0))
        moves.append((_O_BK + GLA_DK * h, OFF_BK + GLA_DK_PAD * h, GLA_DK, 1.0))
    moves += [(_O_CQ, OFF_CQ, SB_WIDTH, scale), (_O_CK, OFF_CK, SB_WIDTH, 1.0),
              (_O_CV, OFF_CV, SB_WIDTH, 1.0)]
    for j, h in enumerate(SWA_Q_PERM):
        moves.append((_O_AQ + HEAD_DIM * h, OFF_AQ + HEAD_DIM * j, HEAD_DIM, scale))
    moves += [(_O_AK, OFF_AK, SWA_KV_WIDTH, 1.0), (_O_AV, OFF_AV, SWA_KV_WIDTH, 1.0),
              (_O_BLR, OFF_BLR, GLA_LOWRANK, 1.0)]
    return moves


def _w_in_prep_kernel(w_ref, o_ref):
    o_ref[...] = jnp.zeros_like(o_ref)
    for src, dst, width, scale in _w_in_moves():
        piece = w_ref[src:src + width, :]
        if scale != 1.0:
            piece = piece * scale
        o_ref[dst:dst + width, :] = piece.astype(o_ref.dtype)


def prep_w_in(w_in_t, tc=256):
    layers, n, d = w_in_t.shape
    return pl.pallas_call(
        _w_in_prep_kernel,
        grid=(layers, d // tc),
        in_specs=[pl.BlockSpec((None, n, tc), lambda l, i: (l, 0, i))],
        out_specs=pl.BlockSpec((None, PROJ_PAD_WIDTH, tc), lambda l, i: (l, 0, i)),
        out_shape=jax.ShapeDtypeStruct((layers, PROJ_PAD_WIDTH, d), BF16),
        compiler_params=_cparams(("parallel", "parallel")),
        name="w_in_prep",
    )(w_in_t)


def _t5_causal_bucket(dist):
    max_exact = REL_BUCKETS // 2
    is_small = dist < max_exact
    ratio = (jnp.log(jnp.maximum(dist, 1).astype(F32) / max_exact)
             / math.log(REL_MAX_DIST / max_exact))
    large = max_exact + (ratio * (REL_BUCKETS - max_exact)).astype(jnp.int32)
    large = jnp.minimum(large, REL_BUCKETS - 1)
    return jnp.where(is_small, dist, large)


def swa_bias_table(rel_bias):
    blk = SWA_BLOCK
    qpos = jnp.arange(blk) + blk
    kpos = jnp.arange(2 * blk)
    dist = qpos[:, None] - kpos[None, :]
    in_window = (dist >= 0) & (dist < WINDOW)
    bucket = _t5_causal_bucket(jnp.maximum(dist, 0))
    table = rel_bias.astype(F32)
    table = jnp.stack([table[:, h] for h in SWA_Q_PERM])
    bias = jnp.full((SWA_HEADS, blk, 2 * blk), NEG_INF, F32)
    for bkt in range(REL_BUCKETS):
        hit = (in_window & (bucket == bkt))[None]
        bias = jnp.where(hit, table[:, bkt][:, None, None], bias)
    bias = bias.reshape(SWA_HEADS // 2, 2 * blk, 2 * blk)
    first = jnp.where((kpos < blk)[None, None, :], NEG_INF, bias)
    return jnp.stack([first, bias])


def _swa_gla_kernel(sink_ref, aq_ref, akp_ref, akc_ref, avp_ref, avc_ref, bias_ref,
                    q_ref, k_ref, v_ref, r_ref, lr_ref, gw_ref, gb_ref, gn_ref, lm_ref,
                    a_ref, b_ref, st_ref, *, layer, rows):
    i = pl.program_id(0)
    blk = SWA_BLOCK
    nb = SWA_HEADS // 2
    ch = GLA_CHUNK
    nch = rows // ch
    heads = range(GLA_HEADS)
    nt = (((1,), (1,)), ((), ()))
    tn = (((0,), (0,)), ((), ()))

    @pl.when(i == 0)
    def _():
        st_ref[...] = jnp.zeros_like(st_ref)

    lane = lax.broadcasted_iota(jnp.int32, (blk, LANES), 1)
    low = lane < HEAD_DIM
    top = lax.broadcasted_iota(jnp.int32, (2 * blk, 1), 0) < blk
    ones = jnp.ones((2 * blk, LANES), BF16)
    kc, vc = akc_ref[...], avc_ref[...]
    swa_k = (jnp.concatenate([akp_ref[...], kc[:blk]], axis=0), kc)
    swa_v = (jnp.concatenate([avp_ref[...], vc[:blk]], axis=0), vc)
    swa = [dict(), dict()]

    def swa_scores(j):
        variant = jnp.minimum(2 * i + j, 1)
        out = []
        for b in range(nb):
            qb = aq_ref[blk * j:blk * (j + 1), LANES * b:LANES * (b + 1)]
            zero = jnp.zeros_like(qb)
            lhs = jnp.concatenate([jnp.where(low, qb, zero), jnp.where(low, zero, qb)], axis=0)
            kb = swa_k[j][:, LANES * (b // 3):LANES * (b // 3 + 1)]
            out.append(lax.dot_general(lhs, kb, nt, preferred_element_type=F32)
                       + bias_ref[variant, b])
        swa[j]["scores"] = out

    def swa_softmax(j):
        probs, sink_term = [], []
        for b in range(nb):
            sink = jnp.where(top, sink_ref[layer, 2 * b], sink_ref[layer, 2 * b + 1])
            sc_b = swa[j]["scores"][b]
            m = jnp.maximum(jnp.max(sc_b, axis=1, keepdims=True), sink)
            probs.append(jnp.exp(sc_b - m).astype(BF16))
            sink_term.append(jnp.exp(sink - m))
        swa[j]["probs"], swa[j]["sink"] = probs, sink_term

    def swa_output(j):
        for b in range(nb):
            vb = swa_v[j][:, LANES * (b // 3):LANES * (b // 3 + 1)]
            pv = jnp.dot(swa[j]["probs"][b], jnp.concatenate([vb, ones], axis=1),
                         preferred_element_type=F32)
            o = pv[:, :LANES] / (pv[:, LANES:] + swa[j]["sink"][b])
            a_ref[blk * j:blk * (j + 1), LANES * b:LANES * (b + 1)] = (
                jnp.where(low, o[:blk], o[blk:]).astype(a_ref.dtype))

    rowi = lax.broadcasted_iota(jnp.int32, (rows, rows), 0)
    coli = lax.broadcasted_iota(jnp.int32, (rows, rows), 1)
    same_chunk_causal = jnp.logical_and(coli <= rowi, (rowi // ch) == (coli // ch))
    gla = dict()

    def gla_gate():
        lr = lr_ref[...]
        lg = []
        for h in heads:
            gp = jnp.dot(lr, gw_ref[h], preferred_element_type=F32) + gb_ref[h]
            log_sig = jnp.minimum(gp, 0.0) - jnp.log(1.0 + jnp.exp(-jnp.abs(gp)))
            lg.append(jnp.maximum(log_sig * (1.0 / GATE_NORMALIZER), GATE_LOG_MIN))
        gla["lg"] = lg

    def gla_decay():
        lm = lm_ref[...]
        q_dec, k_inv, k_end, dec = [], [], [], []
        for h in heads:
            hi, lo = _split_bf16(gla["lg"][h])
            bb = (jnp.dot(lm, hi, preferred_element_type=F32)
                  + jnp.dot(lm, lo, preferred_element_type=F32))
            b, b_last = bb[:rows], bb[rows:]
            q = q_ref[:, GLA_DK_PAD * h:GLA_DK_PAD * (h + 1)].astype(F32) * (GLA_DK ** -0.5)
            kk = k_ref[:, GLA_DK_PAD * h:GLA_DK_PAD * (h + 1)].astype(F32)
            q_dec.append((q * jnp.exp(b)).astype(BF16))
            k_inv.append((kk * jnp.exp(-b)).astype(BF16))
            k_end.append((kk * jnp.exp(b_last - b)).astype(BF16))
            dec.append(jnp.exp(b_last))
        gla.update(q_dec=q_dec, k_inv=k_inv, k_end=k_end, dec=dec)

    def gla_intra():
        v = [v_ref[:, GLA_DV_PAD * h:GLA_DV_PAD * (h + 1)] for h in heads]
        sc = []
        for h in heads:
            s_h = lax.dot_general(gla["q_dec"][h], gla["k_inv"][h], nt,
                                  preferred_element_type=F32)
            sc.append(jnp.where(same_chunk_causal, s_h, 0.0).astype(BF16))
        gla["o_intra"] = [jnp.dot(sc[h], v[h], preferred_element_type=F32) for h in heads]
        gla["d_st"] = [[lax.dot_general(v[h][c * ch:(c + 1) * ch],
                                        gla["k_end"][h][c * ch:(c + 1) * ch], tn,
                                        preferred_element_type=F32)
                        for c in range(nch)] for h in heads]

    def gla_state():
        st_in = []
        for h in heads:
            st = st_ref[h]
            states = []
            for c in range(nch):
                states.append(st.astype(BF16))
                st = st * gla["dec"][h][c * ch:c * ch + 1, :] + gla["d_st"][h][c]
            st_ref[h] = st
            st_in.append(states)
        gla["st_in"] = st_in

    def gla_output():
        gn = gn_ref[...]
        for h in heads:
            o_inter = [lax.dot_general(gla["q_dec"][h][c * ch:(c + 1) * ch], gla["st_in"][h][c],
                                       nt, preferred_element_type=F32) for c in range(nch)]
            o = gla["o_intra"][h] + jnp.concatenate(o_inter, axis=0)
            ms = jnp.sum(o * o, axis=1, keepdims=True) * (1.0 / GLA_DV)
            y = o * lax.rsqrt(ms + RMS_EPS) * gn
            rr = r_ref[:, GLA_DV_PAD * h:GLA_DV_PAD * (h + 1)].astype(F32)
            b_ref[:, GLA_DV_PAD * h:GLA_DV_PAD * (h + 1)] = (
                y * (rr / (1.0 + jnp.exp(-rr)))).astype(b_ref.dtype)

    gla_gate()
    swa_scores(0)
    swa_scores(1)
    gla_decay()
    swa_softmax(0)
    gla_intra()
    swa_softmax(1)
    swa_output(0)
    gla_state()
    swa_output(1)
    gla_output()


def swa_gla_attention(proj, sinks_perm, bias_tab, gw_pad, gb_pad, gn_pad, layer, rows=256):
    s = proj.shape[0]
    rows = min(rows, s)
    blk, ch = SWA_BLOCK, GLA_CHUNK
    assert rows == 2 * blk
    aq_i = OFF_AQ // SWA_WIDTH
    kv_w = SWA_KV_WIDTH
    dkw, dvw = GLA_HEADS * GLA_DK_PAD, GLA_HEADS * GLA_DV_PAD
    idx = np.arange(rows)
    same = (idx[:, None] // ch) == (idx[None, :] // ch)
    prefix = same & (idx[None, :] <= idx[:, None])
    lm = jnp.asarray(np.concatenate([prefix, same], axis=0).astype(np.float32), dtype=BF16)
    prev = lambda i: jnp.maximum(2 * i - 1, 0)
    col = lambda off, width: (lambda i: (i, off // width))
    return pl.pallas_call(
        functools.partial(_swa_gla_kernel, layer=layer, rows=rows),
        grid=(s // rows,),
        in_specs=[pl.BlockSpec(memory_space=pltpu.SMEM),
                  pl.BlockSpec((rows, SWA_WIDTH), lambda i: (i, aq_i)),
                  pl.BlockSpec((blk, kv_w), lambda i: (prev(i), OFF_AK // kv_w)),
                  pl.BlockSpec((rows, kv_w), col(OFF_AK, kv_w)),
                  pl.BlockSpec((blk, kv_w), lambda i: (prev(i), OFF_AV // kv_w)),
                  pl.BlockSpec((rows, kv_w), col(OFF_AV, kv_w)),
                  pl.BlockSpec(bias_tab.shape, lambda i: (0, 0, 0, 0)),
                  pl.BlockSpec((rows, dkw), col(OFF_BQ, dkw)),
                  pl.BlockSpec((rows, dkw), col(OFF_BK, dkw)),
                  pl.BlockSpec((rows, dvw), col(OFF_BV, dvw)),
                  pl.BlockSpec((rows, dvw), col(OFF_BR, dvw)),
                  pl.BlockSpec((rows, GLA_LR_PAD), col(OFF_BLR, GLA_LR_PAD)),
                  pl.BlockSpec((None, GLA_HEADS, GLA_LR_PAD, GLA_DK_PAD),
                               lambda i: (layer, 0, 0, 0)),
                  pl.BlockSpec((None, GLA_HEADS, 1, GLA_DK_PAD), lambda i: (layer, 0, 0, 0)),
                  pl.BlockSpec((None, 1, GLA_DV_PAD), lambda i: (layer, 0, 0)),
                  pl.BlockSpec((2 * rows, rows), lambda i: (0, 0))],
        out_specs=[pl.BlockSpec((rows, SWA_WIDTH), lambda i: (i, 0)),
                   pl.BlockSpec((rows, dvw), lambda i: (i, 0))],
        out_shape=[jax.ShapeDtypeStruct((s, SWA_WIDTH), BF16),
                   jax.ShapeDtypeStruct((s, dvw), BF16)],
        scratch_shapes=[pltpu.VMEM((GLA_HEADS, GLA_DV_PAD, GLA_DK_PAD), F32)],
        compiler_params=_cparams(("arbitrary",)),
        name="swa_gla",
    )(sinks_perm, proj, proj, proj, proj, proj, bias_tab,
      proj, proj, proj, proj, proj, gw_pad, gb_pad, gn_pad, lm)


def _sb_kernel(q_ref, k_ref, v_ref, u_ref, o_ref, acc_ref, car_ref, qm_ref, vm_ref,
               z0_ref, z1_ref, z2_ref, z3_ref, nl0_ref, nl1_ref, w0_ref, w1_ref,
               *, bq, nblk, unroll):
    lane = lax.broadcasted_iota(jnp.int32, (bq, LANES), 1)
    low = lane < HEAD_DIM
    nt = (((1,), (1,)), ((), ()))
    z_ref = (z0_ref, z1_ref, z2_ref, z3_ref)
    nl_ref, w_ref = (nl0_ref, nl1_ref), (w0_ref, w1_ref)

    def rows(blk):
        if isinstance(blk, int):
            return pl.ds(blk * bq, bq)
        return pl.ds(pl.multiple_of(blk * bq, bq), bq)

    def strict_mask():
        rowi = lax.broadcasted_iota(jnp.int32, (2 * bq, bq), 0)
        coli = lax.broadcasted_iota(jnp.int32, (2 * bq, bq), 1)
        return coli < (rowi & (bq - 1))

    def s1a(item, tick4):
        i, c = item
        z_ref[tick4][...] = lax.dot_general(qm_ref[i], k_ref[rows(c), :], nt,
                                            preferred_element_type=F32)

    def s1b(tick4, masked):
        z = z_ref[tick4][...].astype(BF16)
        nl = jnp.maximum(z, 0.0) + jnp.log(1.0 + jnp.exp2(jnp.abs(z) * (-LOG2E)))
        if masked:
            nl = jnp.where(strict_mask(), nl, jnp.zeros_like(nl))
        nl_ref[tick4 % 2][...] = nl

    def s2(item, tick4, masked, first):
        i, _ = item
        cs = jnp.dot(nl_ref[tick4 % 2][...], u_ref[...], preferred_element_type=F32)
        tot = jnp.broadcast_to(cs[:, 0:1], (2 * bq, LANES))
        if first:
            arg = z_ref[tick4][...] - cs
            car_ref[i] = tot
        else:
            car = car_ref[i]
            arg = z_ref[tick4][...] - cs - jnp.concatenate([car, car], axis=1)
            car_ref[i] = car + tot
        w = jnp.exp(arg)
        if masked:
            w = jnp.where(strict_mask(), w, 0.0)
        w = w.astype(BF16)
        w_ref[tick4 % 2][:, :bq] = w[:bq]
        w_ref[tick4 % 2][:, bq:] = w[bq:]

    def s3(item, tick4, first):
        i, c = item
        pv = jnp.dot(w_ref[tick4 % 2][...], vm_ref[c], preferred_element_type=F32)
        if first:
            acc_ref[rows(i), :] = pv
        else:
            acc_ref[rows(i), :] += pv

    def run_pipeline(items, step, masked, first):
        n = len(items)

        def tick(t4, its, do):
            if do[1]:
                s1b((t4 - 1) % 4, masked)
            if do[0]:
                s1a(its[0], t4)
            if do[2]:
                s2(its[2], (t4 - 2) % 4, masked, first)
            if do[3]:
                s3(its[3], (t4 - 3) % 4, first)

        def static_tick(t):
            do = [0 <= t - j < n for j in range(4)]
            its = [items[t - j] if do[j] else None for j in range(4)]
            tick(t % 4, its, do)

        lo_t, hi_t = 3, n - 1
        groups = max(hi_t - lo_t + 1, 0) // unroll
        if groups < 2:
            groups = 0
        for t in range(min(lo_t, n + 3)):
            static_tick(t)
        if groups:
            def body(_, carry):
                its = [(carry[2 * j], carry[2 * j + 1]) for j in range(4)]
                for k in range(unroll):
                    tick((lo_t + k) % 4, its, [True] * 4)
                    its = [step(*its[0])] + its[:3]
                return tuple(x for it in its for x in it)

            init = tuple(jnp.int32(x) for j in range(4) for x in items[lo_t - j])
            lax.fori_loop(0, groups, body, init)
        for t in range(lo_t + unroll * groups, n + 3):
            static_tick(t)

    for blk in range(nblk):
        r = rows(blk)
        qb, vb = q_ref[r, :], v_ref[r, :]
        zero = jnp.zeros_like(qb)
        qm_ref[blk, :bq] = jnp.where(low, qb, zero)
        qm_ref[blk, bq:] = jnp.where(low, zero, qb)
        vm_ref[blk, :bq] = jnp.where(low, vb, zero)
        vm_ref[blk, bq:] = jnp.where(low, zero, vb)

    diag = [(i, i) for i in range(nblk)]
    run_pipeline(diag, lambda i, c: (i + 1, c + 1), masked=True, first=True)

    off = [(i, c) for i in range(1, nblk) for c in range(i - 1, -1, -1)]

    def off_step(i, c):
        wrap = c == 0
        i2 = jnp.where(wrap, i + 1, i)
        return i2, jnp.where(wrap, i2 - 1, c - 1)

    if off:
        run_pipeline(off, off_step, masked=False, first=False)

    o_ref[...] = acc_ref[...].astype(o_ref.dtype)


def sb_attention(proj, bq=256, unroll=32):
    s = proj.shape[0]
    bq = min(bq, s)
    nblk = s // bq
    assert bq & (bq - 1) == 0 and unroll % 4 == 0
    qi = OFF_CQ // LANES
    ki = OFF_CK // LANES
    vi = OFF_CV // LANES
    u = (np.arange(bq)[:, None] >= np.arange(bq)[None, :]).astype(np.float32)
    u = jnp.asarray(u, dtype=BF16)
    slot = lambda shape, dtype: pltpu.VMEM(shape, dtype)
    return pl.pallas_call(
        functools.partial(_sb_kernel, bq=bq, nblk=nblk, unroll=unroll),
        grid=(SB_HEADS // 2,),
        in_specs=[pl.BlockSpec((s, LANES), lambda p: (0, qi + p)),
                  pl.BlockSpec((s, LANES), lambda p: (0, ki + p)),
                  pl.BlockSpec((s, LANES), lambda p: (0, vi + p)),
                  pl.BlockSpec((bq, bq), lambda p: (0, 0))],
        out_specs=pl.BlockSpec((s, LANES), lambda p: (0, p)),
        out_shape=jax.ShapeDtypeStruct((s, SB_WIDTH), BF16),
        scratch_shapes=[pltpu.VMEM((s, LANES), F32),
                        pltpu.VMEM((nblk, 2 * bq, LANES), F32),
                        pltpu.VMEM((nblk, 2 * bq, LANES), BF16),
                        pltpu.VMEM((nblk, 2 * bq, LANES), BF16)]
                       + [slot((2 * bq, bq), F32)] * 4
                       + [slot((2 * bq, bq), BF16)] * 2
                       + [slot((bq, 2 * bq), BF16)] * 2,
        compiler_params=_cparams(("parallel",)),
        name="stickbreak",
    )(proj, proj, proj, u)


def _oproj_kernel(a_ref, b_ref, c_ref, wa_ref, wb_ref, wc_ref, ga_ref, gc_ref,
                  x_ref, gn_ref, xo_ref, ho_ref):
    an = _rms(a_ref[...].astype(F32), ga_ref[...]).astype(BF16)
    cn = _rms(c_ref[...].astype(F32), gc_ref[...]).astype(BF16)
    y = (jnp.dot(an, wa_ref[...], preferred_element_type=F32)
         + jnp.dot(b_ref[...], wb_ref[...], preferred_element_type=F32)
         + jnp.dot(cn, wc_ref[...], preferred_element_type=F32))
    xn = x_ref[...] + y
    xo_ref[...] = xn
    ho_ref[...] = _rms(xn, gn_ref[...]).astype(ho_ref.dtype)


def out_projection(a, b, c, wa, wb, wc, ga, gc, x, gn, layer, tm=512):
    s, d = x.shape
    tm = min(tm, s)
    row = lambda i: (i, 0)
    fixed = lambda i: (layer, 0, 0)
    per_layer = lambda arr: pl.BlockSpec((None,) + arr.shape[1:], fixed)
    return pl.pallas_call(
        _oproj_kernel,
        grid=(s // tm,),
        in_specs=[pl.BlockSpec((tm, a.shape[1]), row),
                  pl.BlockSpec((tm, b.shape[1]), row),
                  pl.BlockSpec((tm, c.shape[1]), row),
                  per_layer(wa), per_layer(wb), per_layer(wc), per_layer(ga), per_layer(gc),
                  pl.BlockSpec((tm, d), row),
                  per_layer(gn)],
        out_specs=[pl.BlockSpec((tm, d), row), pl.BlockSpec((tm, d), row)],
        out_shape=[jax.ShapeDtypeStruct((s, d), F32), jax.ShapeDtypeStruct((s, d), BF16)],
        compiler_params=_cparams(("parallel",)),
        name="out_proj",
    )(a, b, c, wa, wb, wc, ga, gc, x, gn)


def _mlp_kernel(h_ref, x_ref, w1_ref, w2_ref, gn_ref, *out_refs, final):
    acc_ref = out_refs[0]
    j = pl.program_id(1)

    @pl.when(j == 0)
    def _():
        acc_ref[...] = x_ref[...]

    u = jnp.dot(h_ref[...], w1_ref[...].astype(BF16), preferred_element_type=F32)
    act = jnp.square(jnp.maximum(u, 0.0)).astype(BF16)
    acc_ref[...] += jnp.dot(act, w2_ref[...].astype(BF16), preferred_element_type=F32)

    @pl.when(j == pl.num_programs(1) - 1)
    def _():
        y = _rms(acc_ref[...], gn_ref[...])
        if final:
            acc_ref[...] = y
        else:
            out_refs[1][...] = y.astype(out_refs[1].dtype)


def mlp_block(h, x, w1_all, w2_all, layer, gn, final, tm=1024, tf=512):
    s, d = x.shape
    ff = w1_all.shape[2]
    tm = min(tm, s)
    row = lambda i, j: (i, 0)
    once = pl.Buffered(1)
    out_specs = [pl.BlockSpec((tm, d), row)]
    out_shape = [jax.ShapeDtypeStruct((s, d), F32)]
    if not final:
        out_specs.append(pl.BlockSpec((tm, d), row))
        out_shape.append(jax.ShapeDtypeStruct((s, d), BF16))
    return pl.pallas_call(
        functools.partial(_mlp_kernel, final=final),
        grid=(s // tm, ff // tf),
        in_specs=[pl.BlockSpec((tm, d), row, pipeline_mode=once),
                  pl.BlockSpec((tm, d), row, pipeline_mode=once),
                  pl.BlockSpec((None, d, tf), lambda i, j: (layer, 0, j)),
                  pl.BlockSpec((None, tf, d), lambda i, j: (layer, j, 0)),
                  pl.BlockSpec((None, 1, d), lambda i, j: (layer, 0, 0))],
        out_specs=out_specs,
        out_shape=out_shape,
        compiler_params=_cparams(("parallel", "arbitrary")),
        name="mlp",
    )(h, x, w1_all, w2_all, gn)


def _head_rows(w, off, heads, width, padded):
    pieces = []
    for h in range(heads):
        pieces.append(w[..., off + h * width:off + (h + 1) * width, :])
        if padded > width:
            pieces.append(jnp.zeros(w.shape[:-2] + (padded - width, w.shape[-1]), w.dtype))
    return pieces


def _swa_perm_cols(w):
    return jnp.concatenate([w[..., HEAD_DIM * h:HEAD_DIM * (h + 1)] for h in SWA_Q_PERM], axis=-1)


def kernel(x, norm_mix, w_in, swa_sinks, rel_bias, gla_gate_w, gla_gate_b, gla_norm,
           swa_out_norm, sb_out_norm, w_out, norm_mlp, w_mlp_in, w_mlp_out, norm_final):
    depth = w_in.shape[0]
    xs = x[0]
    gla_w = GLA_HEADS * GLA_DV
    w_in_p = prep_w_in(jnp.swapaxes(w_in, 1, 2))
    wa = jnp.concatenate([w_out[:, HEAD_DIM * h:HEAD_DIM * (h + 1)] for h in SWA_Q_PERM],
                         axis=1).astype(BF16)
    wb = jnp.concatenate(_head_rows(w_out, SWA_WIDTH, GLA_HEADS, GLA_DV, GLA_DV_PAD),
                         axis=1).astype(BF16)
    wc = w_out[:, SWA_WIDTH + gla_w:].astype(BF16)
    ga = _swa_perm_cols(swa_out_norm)[:, None, :]
    gc = sb_out_norm[:, None, :]
    g_mlp = norm_mlp[:, None, :]
    g_next = jnp.concatenate([norm_mix[1:], norm_final[None]], axis=0)[:, None, :]
    sinks_p = jnp.concatenate([swa_sinks[:, h:h + 1] for h in SWA_Q_PERM], axis=1)
    bias_tab = swa_bias_table(rel_bias)
    gw = jnp.stack([jnp.pad(gla_gate_w[:, :, GLA_DK * h:GLA_DK * (h + 1)],
                            ((0, 0), (0, GLA_LR_PAD - GLA_LOWRANK), (0, GLA_DK_PAD - GLA_DK)))
                    for h in range(GLA_HEADS)], axis=1).astype(BF16)
    gb = jnp.stack([jnp.pad(gla_gate_b[:, GLA_DK * h:GLA_DK * (h + 1)],
                            ((0, 0), (0, GLA_DK_PAD - GLA_DK)))
                    for h in range(GLA_HEADS)], axis=1)[:, :, None, :]
    gn = jnp.pad(gla_norm, ((0, 0), (0, GLA_DV_PAD - GLA_DV)))[:, None, :]

    h = None
    for l in range(depth):
        if l == 0:
            proj = in_projection(xs, w_in_p, l, gain=norm_mix[0])
        else:
            proj = in_projection(h, w_in_p, l)
        a, b = swa_gla_attention(proj, sinks_p, bias_tab, gw, gb, gn, l)
        c = sb_attention(proj)
        xs, hm = out_projection(a, b, c, wa, wb, wc, ga, gc, xs, g_mlp, l)
        if l + 1 < depth:
            xs, h = mlp_block(hm, xs, w_mlp_in, w_mlp_out, l, g_next, final=False)
        else:
            (out,) = mlp_block(hm, xs, w_mlp_in, w_mlp_out, l, g_next, final=True)
    return out[None]
```

```python
import functools
import math

import numpy as np
import jax
import jax.numpy as jnp
from jax import lax
from jax.experimental import pallas as pl
from jax.experimental.pallas import tpu as pltpu

F32 = jnp.float32
BF16 = jnp.bfloat16

HEAD_DIM = 64
SWA_HEADS = 12
SWA_KV_HEADS = 4
SWA_WIDTH = SWA_HEADS * HEAD_DIM
SWA_KV_WIDTH = SWA_KV_HEADS * HEAD_DIM
SWA_BLOCK = 128
WINDOW = 128
REL_BUCKETS = 32
REL_MAX_DIST = 128
GLA_HEADS = 4
GLA_DV = 192
GLA_DK = 96
GLA_DK_PAD = 128
GLA_DV_PAD = 256
GLA_LOWRANK = 16
GLA_LR_PAD = 128
GLA_CHUNK = 64
GATE_NORMALIZER = 16.0
GATE_LOG_MIN = -1.0
SB_HEADS = 8
SB_WIDTH = SB_HEADS * HEAD_DIM
RMS_EPS = 1e-6
LOG2E = 1.4426950408889634
NEG_INF = -1e30

LANES = 128
VMEM_LIMIT = 56 * 1024 * 1024

GLA_VR = 2 * GLA_DV
OFF_BVR = 0
OFF_BQ = 1536
OFF_BK = 2048
OFF_CQ = 2560
OFF_CK = 3072
OFF_CV = 3584
OFF_AK = 4096
OFF_AV = 4352
OFF_AQ = 4608
OFF_BLR = 5376
PROJ_PAD_WIDTH = 5632

SWA_Q_PERM = (0, 3, 1, 4, 2, 5, 6, 9, 7, 10, 8, 11)

_O_AQ, _O_AK, _O_AV = 0, 768, 1024
_O_BQ, _O_BK, _O_BV, _O_BR, _O_BLR = 1280, 1664, 2048, 2816, 3584
_O_CQ, _O_CK, _O_CV = 3600, 4112, 4624


def _cparams(sem):
    return pltpu.CompilerParams(dimension_semantics=sem, vmem_limit_bytes=VMEM_LIMIT)


def _rms(x, gain):
    ms = jnp.mean(x * x, axis=-1, keepdims=True)
    return x * lax.rsqrt(ms + RMS_EPS) * gain


def _split_bf16(x):
    hi = x.astype(BF16)
    lo = (x - hi.astype(F32)).astype(BF16)
    return hi, lo


_NT_DIMS = (((1,), (1,)), ((), ()))


def _matmul_kernel(h_ref, w_ref, o_ref):
    o_ref[...] = lax.dot_general(h_ref[...], w_ref[...], _NT_DIMS,
                                 preferred_element_type=F32).astype(o_ref.dtype)


def _norm_matmul_kernel(x_ref, g_ref, w_ref, o_ref):
    h = _rms(x_ref[...], g_ref[...]).astype(BF16)
    o_ref[...] = lax.dot_general(h, w_ref[...], _NT_DIMS,
                                 preferred_element_type=F32).astype(o_ref.dtype)


def in_projection(h, w_all, layer, gain=None, tm=1024, tn=PROJ_PAD_WIDTH // 2):
    s, d = h.shape
    n = w_all.shape[1]
    tm = min(tm, s)
    w_spec = pl.BlockSpec((None, tn, d), lambda i, j: (layer, j, 0))
    h_spec = pl.BlockSpec((tm, d), lambda i, j: (i, 0))
    common = dict(
        grid=(s // tm, n // tn),
        out_specs=pl.BlockSpec((tm, tn), lambda i, j: (i, j)),
        out_shape=jax.ShapeDtypeStruct((s, n), BF16),
        compiler_params=_cparams(("parallel", "arbitrary")),
        name="in_proj",
    )
    if gain is None:
        return pl.pallas_call(_matmul_kernel, in_specs=[h_spec, w_spec], **common)(h, w_all)
    return pl.pallas_call(
        _norm_matmul_kernel,
        in_specs=[h_spec, pl.BlockSpec((1, d), lambda i, j: (0, 0)), w_spec],
        **common,
    )(h, gain.reshape(1, d), w_all)


def _w_in_moves():
    scale = HEAD_DIM ** -0.5
    moves = []
    for h in range(GLA_HEADS):
        moves.append((_O_BV + GLA_DV * h, OFF_BVR + GLA_VR * h, GLA_DV, 1.0))
        moves.append((_O_BR + GLA_DV * h, OFF_BVR + GLA_VR * h + GLA_DV, GLA_DV, 1.0))
        moves.append((_O_BQ + GLA_DK * h, OFF_BQ + GLA_DK_PAD * h, GLA_DK, 1.0))
        moves.append((_O_BK + GLA_DK * h, OFF_BK + GLA_DK_PAD * h, GLA_DK, 1.0))
    moves += [(_O_CQ, OFF_CQ, SB_WIDTH, scale), (_O_CK, OFF_CK, SB_WIDTH, 1.0),
              (_O_CV, OFF_CV, SB_WIDTH, 1.0)]
    for j, h in enumerate(SWA_Q_PERM):
        moves.append((_O_AQ + HEAD_DIM * h, OFF_AQ + HEAD_DIM * j, HEAD_DIM, scale))
    moves += [(_O_AK, OFF_AK, SWA_KV_WIDTH, 1.0), (_O_AV, OFF_AV, SWA_KV_WIDTH, 1.0),
              (_O_BLR, OFF_BLR, GLA_LOWRANK, 1.0)]
    return moves


def _w_in_prep_kernel(w_ref, o_ref):
    o_ref[...] = jnp.zeros_like(o_ref)
    for src, dst, width, scale in _w_in_moves():
        piece = w_ref[src:src + width, :]
        if scale != 1.0:
            piece = piece * scale
        o_ref[dst:dst + width, :] = piece.astype(o_ref.dtype)


def prep_w_in(w_in_t, tc=256):
    layers, n, d = w_in_t.shape
    return pl.pallas_call(
        _w_in_prep_kernel,
        grid=(layers, d // tc),
        in_specs=[pl.BlockSpec((None, n, tc), lambda l, i: (l, 0, i))],
        out_specs=pl.BlockSpec((None, PROJ_PAD_WIDTH, tc), lambda l, i: (l, 0, i)),
        out_shape=jax.ShapeDtypeStruct((layers, PROJ_PAD_WIDTH, d), BF16),
        compiler_params=_cparams(("parallel", "parallel")),
        name="w_in_prep",
    )(w_in_t)


def _t5_causal_bucket(dist):
    max_exact = REL_BUCKETS // 2
    is_small = dist < max_exact
    ratio = (jnp.log(jnp.maximum(dist, 1).astype(F32) / max_exact)
             / math.log(REL_MAX_DIST / max_exact))
    large = max_exact + (ratio * (REL_BUCKETS - max_exact)).astype(jnp.int32)
    large = jnp.minimum(large, REL_BUCKETS - 1)
    return jnp.where(is_small, dist, large)


def swa_bias_table(rel_bias):
    blk = SWA_BLOCK
    qpos = jnp.arange(blk) + blk
    kpos = jnp.arange(2 * blk)
    dist = qpos[:, None] - kpos[None, :]
    in_window = (dist >= 0) & (dist < WINDOW)
    bucket = _t5_causal_bucket(jnp.maximum(dist, 0))
    table = rel_bias.astype(F32)
    table = jnp.stack([table[:, h] for h in SWA_Q_PERM])
    bias = jnp.full((SWA_HEADS, blk, 2 * blk), NEG_INF, F32)
    for bkt in range(REL_BUCKETS):
        hit = (in_window & (bucket == bkt))[None]
        bias = jnp.where(hit, table[:, bkt][:, None, None], bias)
    bias = bias.reshape(SWA_HEADS // 2, 2 * blk, 2 * blk)
    first = jnp.where((kpos < blk)[None, None, :], NEG_INF, bias)
    return jnp.stack([first, bias])


def _swa_gla_kernel(sink_ref, aq_ref, akp_ref, akc_ref, avp_ref, avc_ref, bias_ref,
                    q_ref, k_ref, vr_ref, lr_ref, gw_ref, gb_ref, gn_ref, lm_ref,
                    a_ref, b_ref, st_ref, *, layer, rows):
    i = pl.program_id(0)
    blk = SWA_BLOCK
    nb = SWA_HEADS // 2
    ch = GLA_CHUNK
    nch = rows // ch
    heads = range(GLA_HEADS)
    nt = (((1,), (1,)), ((), ()))
    tn = (((0,), (0,)), ((), ()))

    @pl.when(i == 0)
    def _():
        st_ref[...] = jnp.zeros_like(st_ref)

    lane = lax.broadcasted_iota(jnp.int32, (blk, LANES), 1)
    low = lane < HEAD_DIM
    top = lax.broadcasted_iota(jnp.int32, (2 * blk, 1), 0) < blk
    ones = jnp.ones((2 * blk, LANES), BF16)
    kc, vc = akc_ref[...], avc_ref[...]
    swa_k = (jnp.concatenate([akp_ref[...], kc[:blk]], axis=0), kc)
    swa_v = (jnp.concatenate([avp_ref[...], vc[:blk]], axis=0), vc)
    swa = [dict(), dict()]

    def swa_scores(j):
        variant = jnp.minimum(2 * i + j, 1)
        out = []
        for b in range(nb):
            qb = aq_ref[blk * j:blk * (j + 1), LANES * b:LANES * (b + 1)]
            zero = jnp.zeros_like(qb)
            lhs = jnp.concatenate([jnp.where(low, qb, zero), jnp.where(low, zero, qb)], axis=0)
            kb = swa_k[j][:, LANES * (b // 3):LANES * (b // 3 + 1)]
            out.append(lax.dot_general(lhs, kb, nt, preferred_element_type=F32)
                       + bias_ref[variant, b])
        swa[j]["scores"] = out

    def swa_softmax(j):
        probs, sink_term = [], []
        for b in range(nb):
            sink = jnp.where(top, sink_ref[layer, 2 * b], sink_ref[layer, 2 * b + 1])
            sc_b = swa[j]["scores"][b]
            m = jnp.maximum(jnp.max(sc_b, axis=1, keepdims=True), sink)
            probs.append(jnp.exp(sc_b - m).astype(BF16))
            sink_term.append(jnp.exp(sink - m))
        swa[j]["probs"], swa[j]["sink"] = probs, sink_term

    def swa_output(j):
        for b in range(nb):
            vb = swa_v[j][:, LANES * (b // 3):LANES * (b // 3 + 1)]
            pv = jnp.dot(swa[j]["probs"][b], jnp.concatenate([vb, ones], axis=1),
                         preferred_element_type=F32)
            o = pv[:, :LANES] / (pv[:, LANES:] + swa[j]["sink"][b])
            a_ref[blk * j:blk * (j + 1), LANES * b:LANES * (b + 1)] = (
                jnp.where(low, o[:blk], o[blk:]).astype(a_ref.dtype))

    rowi = lax.broadcasted_iota(jnp.int32, (rows, rows), 0)
    coli = lax.broadcasted_iota(jnp.int32, (rows, rows), 1)
    same_chunk_causal = jnp.logical_and(coli <= rowi, (rowi // ch) == (coli // ch))
    gla = dict()

    def gla_gate():
        lr = lr_ref[...]
        lg = []
        for h in heads:
            gp = jnp.dot(lr, gw_ref[h], preferred_element_type=F32) + gb_ref[h]
            log_sig = jnp.minimum(gp, 0.0) - jnp.log(1.0 + jnp.exp(-jnp.abs(gp)))
            lg.append(jnp.maximum(log_sig * (1.0 / GATE_NORMALIZER), GATE_LOG_MIN))
        gla["lg"] = lg

    def gla_decay():
        lm = lm_ref[...]
        q_dec, k_inv, k_end, dec = [], [], [], []
        for h in heads:
            hi, lo = _split_bf16(gla["lg"][h])
            bb = (jnp.dot(lm, hi, preferred_element_type=F32)
                  + jnp.dot(lm, lo, preferred_element_type=F32))
            b, b_last = bb[:rows], bb[rows:]
            q = q_ref[:, GLA_DK_PAD * h:GLA_DK_PAD * (h + 1)].astype(F32) * (GLA_DK ** -0.5)
            kk = k_ref[:, GLA_DK_PAD * h:GLA_DK_PAD * (h + 1)].astype(F32)
            q_dec.append((q * jnp.exp(b)).astype(BF16))
            k_inv.append((kk * jnp.exp(-b)).astype(BF16))
            k_end.append((kk * jnp.exp(b_last - b)).astype(BF16))
            dec.append(jnp.exp(b_last))
        gla.update(q_dec=q_dec, k_inv=k_inv, k_end=k_end, dec=dec)

    def gla_intra():
        vlane = lax.broadcasted_iota(jnp.int32, (rows, GLA_DV_PAD), 1) < GLA_DV
        v = []
        for h in heads:
            win = vr_ref[:, GLA_VR * h:GLA_VR * h + GLA_DV_PAD]
            v.append(jnp.where(vlane, win, jnp.zeros_like(win)))
        sc = []
        for h in heads:
            s_h = lax.dot_general(gla["q_dec"][h], gla["k_inv"][h], nt,
                                  preferred_element_type=F32)
            sc.append(jnp.where(same_chunk_causal, s_h, 0.0).astype(BF16))
        gla["o_intra"] = [jnp.dot(sc[h], v[h], preferred_element_type=F32) for h in heads]
        gla["d_st"] = [[lax.dot_general(v[h][c * ch:(c + 1) * ch],
                                        gla["k_end"][h][c * ch:(c + 1) * ch], tn,
                                        preferred_element_type=F32)
                        for c in range(nch)] for h in heads]

    def gla_state():
        st_in = []
        for h in heads:
            st = st_ref[h]
            states = []
            for c in range(nch):
                states.append(st.astype(BF16))
                st = st * gla["dec"][h][c * ch:c * ch + 1, :] + gla["d_st"][h][c]
            st_ref[h] = st
            st_in.append(states)
        gla["st_in"] = st_in

    def gla_output():
        gn = gn_ref[...]
        for h in heads:
            o_inter = [lax.dot_general(gla["q_dec"][h][c * ch:(c + 1) * ch], gla["st_in"][h][c],
                                       nt, preferred_element_type=F32) for c in range(nch)]
            o = gla["o_intra"][h] + jnp.concatenate(o_inter, axis=0)
            ms = jnp.sum(o * o, axis=1, keepdims=True) * (1.0 / GLA_DV)
            y = o * lax.rsqrt(ms + RMS_EPS) * gn
            win = vr_ref[:, GLA_VR * h + LANES:GLA_VR * (h + 1)].astype(F32)
            rr = win[:, GLA_DV_PAD - GLA_DV:]
            out = y[:, :GLA_DV] * (rr / (1.0 + jnp.exp(-rr)))
            b_ref[:, GLA_DV_PAD * h:GLA_DV_PAD * h + GLA_DV] = out.astype(b_ref.dtype)
            b_ref[:, GLA_DV_PAD * h + GLA_DV:GLA_DV_PAD * (h + 1)] = jnp.zeros(
                (rows, GLA_DV_PAD - GLA_DV), b_ref.dtype)

    gla_gate()
    swa_scores(0)
    swa_scores(1)
    gla_decay()
    swa_softmax(0)
    gla_intra()
    swa_softmax(1)
    swa_output(0)
    gla_state()
    swa_output(1)
    gla_output()


def swa_gla_attention(proj, sinks_perm, bias_tab, gw_pad, gb_pad, gn_pad, layer, rows=256):
    s = proj.shape[0]
    rows = min(rows, s)
    blk, ch = SWA_BLOCK, GLA_CHUNK
    assert rows == 2 * blk
    aq_i = OFF_AQ // SWA_WIDTH
    kv_w = SWA_KV_WIDTH
    dkw, dvw = GLA_HEADS * GLA_DK_PAD, GLA_HEADS * GLA_DV_PAD
    idx = np.arange(rows)
    same = (idx[:, None] // ch) == (idx[None, :] // ch)
    prefix = same & (idx[None, :] <= idx[:, None])
    lm = jnp.asarray(np.concatenate([prefix, same], axis=0).astype(np.float32), dtype=BF16)
    prev = lambda i: jnp.maximum(2 * i - 1, 0)
    col = lambda off, width: (lambda i: (i, off // width))
    return pl.pallas_call(
        functools.partial(_swa_gla_kernel, layer=layer, rows=rows),
        grid=(s // rows,),
        in_specs=[pl.BlockSpec(memory_space=pltpu.SMEM),
                  pl.BlockSpec((rows, SWA_WIDTH), lambda i: (i, aq_i)),
                  pl.BlockSpec((blk, kv_w), lambda i: (prev(i), OFF_AK // kv_w)),
                  pl.BlockSpec((rows, kv_w), col(OFF_AK, kv_w)),
                  pl.BlockSpec((blk, kv_w), lambda i: (prev(i), OFF_AV // kv_w)),
                  pl.BlockSpec((rows, kv_w), col(OFF_AV, kv_w)),
                  pl.BlockSpec(bias_tab.shape, lambda i: (0, 0, 0, 0)),
                  pl.BlockSpec((rows, dkw), col(OFF_BQ, dkw)),
                  pl.BlockSpec((rows, dkw), col(OFF_BK, dkw)),
                  pl.BlockSpec((rows, GLA_HEADS * GLA_VR), col(OFF_BVR, GLA_HEADS * GLA_VR)),
                  pl.BlockSpec((rows, GLA_LR_PAD), col(OFF_BLR, GLA_LR_PAD)),
                  pl.BlockSpec((None, GLA_HEADS, GLA_LR_PAD, GLA_DK_PAD),
                               lambda i: (layer, 0, 0, 0)),
                  pl.BlockSpec((None, GLA_HEADS, 1, GLA_DK_PAD), lambda i: (layer, 0, 0, 0)),
                  pl.BlockSpec((None, 1, GLA_DV_PAD), lambda i: (layer, 0, 0)),
                  pl.BlockSpec((2 * rows, rows), lambda i: (0, 0))],
        out_specs=[pl.BlockSpec((rows, SWA_WIDTH), lambda i: (i, 0)),
                   pl.BlockSpec((rows, dvw), lambda i: (i, 0))],
        out_shape=[jax.ShapeDtypeStruct((s, SWA_WIDTH), BF16),
                   jax.ShapeDtypeStruct((s, dvw), BF16)],
        scratch_shapes=[pltpu.VMEM((GLA_HEADS, GLA_DV_PAD, GLA_DK_PAD), F32)],
        compiler_params=_cparams(("arbitrary",)),
        name="swa_gla",
    )(sinks_perm, proj, proj, proj, proj, proj, bias_tab,
      proj, proj, proj, proj, gw_pad, gb_pad, gn_pad, lm)


def _sb_kernel(q_ref, k_ref, v_ref, u_ref, o_ref, acc_ref, car_ref, qm_ref, vm_ref,
               z0_ref, z1_ref, z2_ref, z3_ref, nl0_ref, nl1_ref, w0_ref, w1_ref,
               *, bq, nblk, unroll):
    lane = lax.broadcasted_iota(jnp.int32, (bq, LANES), 1)
    low = lane < HEAD_DIM
    nt = (((1,), (1,)), ((), ()))
    z_ref = (z0_ref, z1_ref, z2_ref, z3_ref)
    nl_ref, w_ref = (nl0_ref, nl1_ref), (w0_ref, w1_ref)

    def rows(blk):
        if isinstance(blk, int):
            return pl.ds(blk * bq, bq)
        return pl.ds(pl.multiple_of(blk * bq, bq), bq)

    def strict_mask():
        rowi = lax.broadcasted_iota(jnp.int32, (2 * bq, bq), 0)
        coli = lax.broadcasted_iota(jnp.int32, (2 * bq, bq), 1)
        return coli < (rowi & (bq - 1))

    def s1a(item, tick4):
        i, c = item
        z_ref[tick4][...] = lax.dot_general(qm_ref[i], k_ref[rows(c), :], nt,
                                            preferred_element_type=F32)

    def s1b(tick4, masked):
        z = z_ref[tick4][...].astype(BF16)
        nl = jnp.maximum(z, 0.0) + jnp.log(1.0 + jnp.exp2(jnp.abs(z) * (-LOG2E)))
        if masked:
            nl = jnp.where(strict_mask(), nl, jnp.zeros_like(nl))
        nl_ref[tick4 % 2][...] = nl

    def s2(item, tick4, masked, first):
        i, _ = item
        cs = jnp.dot(nl_ref[tick4 % 2][...], u_ref[...], preferred_element_type=F32)
        tot = jnp.broadcast_to(cs[:, 0:1], (2 * bq, LANES))
        if first:
            arg = z_ref[tick4][...] - cs
            car_ref[i] = tot
        else:
            car = car_ref[i]
            arg = z_ref[tick4][...] - cs - jnp.concatenate([car, car], axis=1)
            car_ref[i] = car + tot
        w = jnp.exp(arg)
        if masked:
            w = jnp.where(strict_mask(), w, 0.0)
        w = w.astype(BF16)
        w_ref[tick4 % 2][:, :bq] = w[:bq]
        w_ref[tick4 % 2][:, bq:] = w[bq:]

    def s3(item, tick4, first):
        i, c = item
        pv = jnp.dot(w_ref[tick4 % 2][...], vm_ref[c], preferred_element_type=F32)
        if first:
            acc_ref[rows(i), :] = pv
        else:
            acc_ref[rows(i), :] += pv

    def run_pipeline(items, step, masked, first):
        n = len(items)

        def tick(t4, its, do):
            if do[1]:
                s1b((t4 - 1) % 4, masked)
            if do[0]:
                s1a(its[0], t4)
            if do[2]:
                s2(its[2], (t4 - 2) % 4, masked, first)
            if do[3]:
                s3(its[3], (t4 - 3) % 4, first)

        def static_tick(t):
            do = [0 <= t - j < n for j in range(4)]
            its = [items[t - j] if do[j] else None for j in range(4)]
            tick(t % 4, its, do)

        lo_t, hi_t = 3, n - 1
        groups = max(hi_t - lo_t + 1, 0) // unroll
        if groups < 2:
            groups = 0
        for t in range(min(lo_t, n + 3)):
            static_tick(t)
        if groups:
            def body(_, carry):
                its = [(carry[2 * j], carry[2 * j + 1]) for j in range(4)]
                for k in range(unroll):
                    tick((lo_t + k) % 4, its, [True] * 4)
                    its = [step(*its[0])] + its[:3]
                return tuple(x for it in its for x in it)

            init = tuple(jnp.int32(x) for j in range(4) for x in items[lo_t - j])
            lax.fori_loop(0, groups, body, init)
        for t in range(lo_t + unroll * groups, n + 3):
            static_tick(t)

    for blk in range(nblk):
        r = rows(blk)
        qb, vb = q_ref[r, :], v_ref[r, :]
        zero = jnp.zeros_like(qb)
        qm_ref[blk, :bq] = jnp.where(low, qb, zero)
        qm_ref[blk, bq:] = jnp.where(low, zero, qb)
        vm_ref[blk, :bq] = jnp.where(low, vb, zero)
        vm_ref[blk, bq:] = jnp.where(low, zero, vb)

    diag = [(i, i) for i in range(nblk)]
    run_pipeline(diag, lambda i, c: (i + 1, c + 1), masked=True, first=True)

    off = [(i, c) for i in range(1, nblk) for c in range(i - 1, -1, -1)]

    def off_step(i, c):
        wrap = c == 0
        i2 = jnp.where(wrap, i + 1, i)
        return i2, jnp.where(wrap, i2 - 1, c - 1)

    if off:
        run_pipeline(off, off_step, masked=False, first=False)

    o_ref[...] = acc_ref[...].astype(o_ref.dtype)


def sb_attention(proj, bq=256, unroll=32):
    s = proj.shape[0]
    bq = min(bq, s)
    nblk = s // bq
    assert bq & (bq - 1) == 0 and unroll % 4 == 0
    qi = OFF_CQ // LANES
    ki = OFF_CK // LANES
    vi = OFF_CV // LANES
    u = (np.arange(bq)[:, None] >= np.arange(bq)[None, :]).astype(np.float32)
    u = jnp.asarray(u, dtype=BF16)
    slot = lambda shape, dtype: pltpu.VMEM(shape, dtype)
    return pl.pallas_call(
        functools.partial(_sb_kernel, bq=bq, nblk=nblk, unroll=unroll),
        grid=(SB_HEADS // 2,),
        in_specs=[pl.BlockSpec((s, LANES), lambda p: (0, qi + p)),
                  pl.BlockSpec((s, LANES), lambda p: (0, ki + p)),
                  pl.BlockSpec((s, LANES), lambda p: (0, vi + p)),
                  pl.BlockSpec((bq, bq), lambda p: (0, 0))],
        out_specs=pl.BlockSpec((s, LANES), lambda p: (0, p)),
        out_shape=jax.ShapeDtypeStruct((s, SB_WIDTH), BF16),
        scratch_shapes=[pltpu.VMEM((s, LANES), F32),
                        pltpu.VMEM((nblk, 2 * bq, LANES), F32),
                        pltpu.VMEM((nblk, 2 * bq, LANES), BF16),
                        pltpu.VMEM((nblk, 2 * bq, LANES), BF16)]
                       + [slot((2 * bq, bq), F32)] * 4
                       + [slot((2 * bq, bq), BF16)] * 2
                       + [slot((bq, 2 * bq), BF16)] * 2,
        compiler_params=_cparams(("parallel",)),
        name="stickbreak",
    )(proj, proj, proj, u)


def _oproj_kernel(a_ref, b_ref, c_ref, wa_ref, wb_ref, wc_ref, ga_ref, gc_ref,
                  x_ref, gn_ref, xo_ref, ho_ref):
    an = _rms(a_ref[...].astype(F32), ga_ref[...]).astype(BF16)
    cn = _rms(c_ref[...].astype(F32), gc_ref[...]).astype(BF16)
    y = (jnp.dot(an, wa_ref[...], preferred_element_type=F32)
         + jnp.dot(b_ref[...], wb_ref[...], preferred_element_type=F32)
         + jnp.dot(cn, wc_ref[...], preferred_element_type=F32))
    xn = x_ref[...] + y
    xo_ref[...] = xn
    ho_ref[...] = _rms(xn, gn_ref[...]).astype(ho_ref.dtype)


def out_projection(a, b, c, wa, wb, wc, ga, gc, x, gn, layer, tm=512):
    s, d = x.shape
    tm = min(tm, s)
    row = lambda i: (i, 0)
    fixed = lambda i: (layer, 0, 0)
    per_layer = lambda arr: pl.BlockSpec((None,) + arr.shape[1:], fixed)
    return pl.pallas_call(
        _oproj_kernel,
        grid=(s // tm,),
        in_specs=[pl.BlockSpec((tm, a.shape[1]), row),
                  pl.BlockSpec((tm, b.shape[1]), row),
                  pl.BlockSpec((tm, c.shape[1]), row),
                  per_layer(wa), per_layer(wb), per_layer(wc), per_layer(ga), per_layer(gc),
                  pl.BlockSpec((tm, d), row),
                  per_layer(gn)],
        out_specs=[pl.BlockSpec((tm, d), row), pl.BlockSpec((tm, d), row)],
        out_shape=[jax.ShapeDtypeStruct((s, d), F32), jax.ShapeDtypeStruct((s, d), BF16)],
        compiler_params=_cparams(("parallel",)),
        name="out_proj",
    )(a, b, c, wa, wb, wc, ga, gc, x, gn)


def _mlp_kernel(h_ref, x_ref, w1_ref, w2_ref, gn_ref, *out_refs, final):
    acc_ref = out_refs[0]
    j = pl.program_id(1)

    @pl.when(j == 0)
    def _():
        acc_ref[...] = x_ref[...]

    u = jnp.dot(h_ref[...], w1_ref[...].astype(BF16), preferred_element_type=F32)
    act = jnp.square(jnp.maximum(u, 0.0)).astype(BF16)
    acc_ref[...] += jnp.dot(act, w2_ref[...].astype(BF16), preferred_element_type=F32)

    @pl.when(j == pl.num_programs(1) - 1)
    def _():
        y = _rms(acc_ref[...], gn_ref[...])
        if final:
            acc_ref[...] = y
        else:
            out_refs[1][...] = y.astype(out_refs[1].dtype)


def mlp_block(h, x, w1_all, w2_all, layer, gn, final, tm=1024, tf=512):
    s, d = x.shape
    ff = w1_all.shape[2]
    tm = min(tm, s)
    row = lambda i, j: (i, 0)
    once = pl.Buffered(1)
    out_specs = [pl.BlockSpec((tm, d), row)]
    out_shape = [jax.ShapeDtypeStruct((s, d), F32)]
    if not final:
        out_specs.append(pl.BlockSpec((tm, d), row))
        out_shape.append(jax.ShapeDtypeStruct((s, d), BF16))
    return pl.pallas_call(
        functools.partial(_mlp_kernel, final=final),
        grid=(s // tm, ff // tf),
        in_specs=[pl.BlockSpec((tm, d), row, pipeline_mode=once),
                  pl.BlockSpec((tm, d), row, pipeline_mode=once),
                  pl.BlockSpec((None, d, tf), lambda i, j: (layer, 0, j)),
                  pl.BlockSpec((None, tf, d), lambda i, j: (layer, j, 0)),
                  pl.BlockSpec((None, 1, d), lambda i, j: (layer, 0, 0))],
        out_specs=out_specs,
        out_shape=out_shape,
        compiler_params=_cparams(("parallel", "arbitrary")),
        name="mlp",
    )(h, x, w1_all, w2_all, gn)


def _head_rows(w, off, heads, width, padded):
    pieces = []
    for h in range(heads):
        pieces.append(w[..., off + h * width:off + (h + 1) * width, :])
        if padded > width:
            pieces.append(jnp.zeros(w.shape[:-2] + (padded - width, w.shape[-1]), w.dtype))
    return pieces


def _swa_perm_cols(w):
    return jnp.concatenate([w[..., HEAD_DIM * h:HEAD_DIM * (h + 1)] for h in SWA_Q_PERM], axis=-1)


def kernel(x, norm_mix, w_in, swa_sinks, rel_bias, gla_gate_w, gla_gate_b, gla_norm,
           swa_out_norm, sb_out_norm, w_out, norm_mlp, w_mlp_in, w_mlp_out, norm_final):
    depth = w_in.shape[0]
    xs = x[0]
    gla_w = GLA_HEADS * GLA_DV
    w_in_p = prep_w_in(jnp.swapaxes(w_in, 1, 2))
    wa = jnp.concatenate([w_out[:, HEAD_DIM * h:HEAD_DIM * (h + 1)] for h in SWA_Q_PERM],
                         axis=1).astype(BF16)
    wb = jnp.concatenate(_head_rows(w_out, SWA_WIDTH, GLA_HEADS, GLA_DV, GLA_DV_PAD),
                         axis=1).astype(BF16)
    wc = w_out[:, SWA_WIDTH + gla_w:].astype(BF16)
    ga = _swa_perm_cols(swa_out_norm)[:, None, :]
    gc = sb_out_norm[:, None, :]
    g_mlp = norm_mlp[:, None, :]
    g_next = jnp.concatenate([norm_mix[1:], norm_final[None]], axis=0)[:, None, :]
    sinks_p = jnp.concatenate([swa_sinks[:, h:h + 1] for h in SWA_Q_PERM], axis=1)
    bias_tab = swa_bias_table(rel_bias)
    gw = jnp.stack([jnp.pad(gla_gate_w[:, :, GLA_DK * h:GLA_DK * (h + 1)],
                            ((0, 0), (0, GLA_LR_PAD - GLA_LOWRANK), (0, GLA_DK_PAD - GLA_DK)))
                    for h in range(GLA_HEADS)], axis=1).astype(BF16)
    gb = jnp.stack([jnp.pad(gla_gate_b[:, GLA_DK * h:GLA_DK * (h + 1)],
                            ((0, 0), (0, GLA_DK_PAD - GLA_DK)))
                    for h in range(GLA_HEADS)], axis=1)[:, :, None, :]
    gn = jnp.pad(gla_norm, ((0, 0), (0, GLA_DV_PAD - GLA_DV)))[:, None, :]

    h = None
    for l in range(depth):
        if l == 0:
            proj = in_projection(xs, w_in_p, l, gain=norm_mix[0])
        else:
            proj = in_projection(h, w_in_p, l)
        a, b = swa_gla_attention(proj, sinks_p, bias_tab, gw, gb, gn, l)
        c = sb_attention(proj)
        xs, hm = out_projection(a, b, c, wa, wb, wc, ga, gc, xs, g_mlp, l)
        if l + 1 < depth:
            xs, h = mlp_block(hm, xs, w_mlp_in, w_mlp_out, l, g_next, final=False)
        else:
            (out,) = mlp_block(hm, xs, w_mlp_in, w_mlp_out, l, g_next, final=True)
    return out[None]
```

```python
import functools
import math

import numpy as np
import jax
import jax.numpy as jnp
from jax import lax
from jax.experimental import pallas as pl
from jax.experimental.pallas import tpu as pltpu

F32 = jnp.float32
BF16 = jnp.bfloat16

HEAD_DIM = 64
SWA_HEADS = 12
SWA_KV_HEADS = 4
SWA_WIDTH = SWA_HEADS * HEAD_DIM
SWA_KV_WIDTH = SWA_KV_HEADS * HEAD_DIM
SWA_BLOCK = 128
WINDOW = 128
REL_BUCKETS = 32
REL_MAX_DIST = 128
GLA_HEADS = 4
GLA_DV = 192
GLA_DK = 96
GLA_DK_PAD = 128
GLA_DV_PAD = 256
GLA_LOWRANK = 16
GLA_LR_PAD = 128
GLA_CHUNK = 64
GATE_NORMALIZER = 16.0
GATE_LOG_MIN = -1.0
SB_HEADS = 8
SB_WIDTH = SB_HEADS * HEAD_DIM
RMS_EPS = 1e-6
LOG2E = 1.4426950408889634
NEG_INF = -1e30

LANES = 128
VMEM_LIMIT = 56 * 1024 * 1024

GLA_VR = 2 * GLA_DV
OFF_BVR = 0
OFF_BQ = 1536
OFF_BK = 2048
OFF_CQ = 2560
OFF_CK = 3072
OFF_CV = 3584
OFF_AK = 4096
OFF_AV = 4352
OFF_AQ = 4608
OFF_BLR = 5376
PROJ_PAD_WIDTH = 5632

SWA_Q_PERM = (0, 3, 1, 4, 2, 5, 6, 9, 7, 10, 8, 11)

_O_AQ, _O_AK, _O_AV = 0, 768, 1024
_O_BQ, _O_BK, _O_BV, _O_BR, _O_BLR = 1280, 1664, 2048, 2816, 3584
_O_CQ, _O_CK, _O_CV = 3600, 4112, 4624


def _cparams(sem):
    return pltpu.CompilerParams(dimension_semantics=sem, vmem_limit_bytes=VMEM_LIMIT)


def _rms(x, gain):
    ms = jnp.mean(x * x, axis=-1, keepdims=True)
    return x * lax.rsqrt(ms + RMS_EPS) * gain


def _split_bf16(x):
    hi = x.astype(BF16)
    lo = (x - hi.astype(F32)).astype(BF16)
    return hi, lo


_NT_DIMS = (((1,), (1,)), ((), ()))


def _matmul_kernel(h_ref, w_ref, o_ref):
    o_ref[...] = lax.dot_general(h_ref[...], w_ref[...], _NT_DIMS,
                                 preferred_element_type=F32).astype(o_ref.dtype)


def _norm_matmul_kernel(x_ref, g_ref, w_ref, o_ref):
    h = _rms(x_ref[...], g_ref[...]).astype(BF16)
    o_ref[...] = lax.dot_general(h, w_ref[...], _NT_DIMS,
                                 preferred_element_type=F32).astype(o_ref.dtype)


def in_projection(h, w_all, layer, gain=None, tm=1024, tn=PROJ_PAD_WIDTH // 2):
    s, d = h.shape
    n = w_all.shape[1]
    tm = min(tm, s)
    w_spec = pl.BlockSpec((None, tn, d), lambda i, j: (layer, j, 0))
    h_spec = pl.BlockSpec((tm, d), lambda i, j: (i, 0))
    common = dict(
        grid=(s // tm, n // tn),
        out_specs=pl.BlockSpec((tm, tn), lambda i, j: (i, j)),
        out_shape=jax.ShapeDtypeStruct((s, n), BF16),
        compiler_params=_cparams(("parallel", "arbitrary")),
        name="in_proj",
    )
    if gain is None:
        return pl.pallas_call(_matmul_kernel, in_specs=[h_spec, w_spec], **common)(h, w_all)
    return pl.pallas_call(
        _norm_matmul_kernel,
        in_specs=[h_spec, pl.BlockSpec((1, d), lambda i, j: (0, 0)), w_spec],
        **common,
    )(h, gain.reshape(1, d), w_all)


def _w_in_moves():
    scale = HEAD_DIM ** -0.5
    moves = []
    for h in range(GLA_HEADS):
        moves.append((_O_BV + GLA_DV * h, OFF_BVR + GLA_VR * h, GLA_DV, 1.0))
        moves.append((_O_BR + GLA_DV * h, OFF_BVR + GLA_VR * h + GLA_DV, GLA_DV, 1.0))
        moves.append((_O_BQ + GLA_DK * h, OFF_BQ + GLA_DK_PAD * h, GLA_DK, 1.0))
        moves.append((_O_BK + GLA_DK * h, OFF_BK + GLA_DK_PAD * h, GLA_DK, 1.0))
    moves += [(_O_CQ, OFF_CQ, SB_WIDTH, scale), (_O_CK, OFF_CK, SB_WIDTH, 1.0),
              (_O_CV, OFF_CV, SB_WIDTH, 1.0)]
    for j, h in enumerate(SWA_Q_PERM):
        moves.append((_O_AQ + HEAD_DIM * h, OFF_AQ + HEAD_DIM * j, HEAD_DIM, scale))
    moves += [(_O_AK, OFF_AK, SWA_KV_WIDTH, 1.0), (_O_AV, OFF_AV, SWA_KV_WIDTH, 1.0),
              (_O_BLR, OFF_BLR, GLA_LOWRANK, 1.0)]
    return moves


def _w_in_prep_kernel(w_ref, o_ref):
    o_ref[...] = jnp.zeros_like(o_ref)
    for src, dst, width, scale in _w_in_moves():
        piece = w_ref[src:src + width, :]
        if scale != 1.0:
            piece = piece * scale
        o_ref[dst:dst + width, :] = piece.astype(o_ref.dtype)


def prep_w_in(w_in_t, tc=256):
    layers, n, d = w_in_t.shape
    return pl.pallas_call(
        _w_in_prep_kernel,
        grid=(layers, d // tc),
        in_specs=[pl.BlockSpec((None, n, tc), lambda l, i: (l, 0, i))],
        out_specs=pl.BlockSpec((None, PROJ_PAD_WIDTH, tc), lambda l, i: (l, 0, i)),
        out_shape=jax.ShapeDtypeStruct((layers, PROJ_PAD_WIDTH, d), BF16),
        compiler_params=_cparams(("parallel", "parallel")),
        name="w_in_prep",
    )(w_in_t)


def _t5_causal_bucket(dist):
    max_exact = REL_BUCKETS // 2
    is_small = dist < max_exact
    ratio = (jnp.log(jnp.maximum(dist, 1).astype(F32) / max_exact)
             / math.log(REL_MAX_DIST / max_exact))
    large = max_exact + (ratio * (REL_BUCKETS - max_exact)).astype(jnp.int32)
    large = jnp.minimum(large, REL_BUCKETS - 1)
    return jnp.where(is_small, dist, large)


def swa_bias_table(rel_bias):
    blk = SWA_BLOCK
    qpos = jnp.arange(blk) + blk
    kpos = jnp.arange(2 * blk)
    dist = qpos[:, None] - kpos[None, :]
    in_window = (dist >= 0) & (dist < WINDOW)
    bucket = _t5_causal_bucket(jnp.maximum(dist, 0))
    table = rel_bias.astype(F32)
    table = jnp.stack([table[:, h] for h in SWA_Q_PERM])
    bias = jnp.full((SWA_HEADS, blk, 2 * blk), NEG_INF, F32)
    for bkt in range(REL_BUCKETS):
        hit = (in_window & (bucket == bkt))[None]
        bias = jnp.where(hit, table[:, bkt][:, None, None], bias)
    bias = bias.reshape(SWA_HEADS // 2, 2 * blk, 2 * blk)
    first = jnp.where((kpos < blk)[None, None, :], NEG_INF, bias)
    return jnp.stack([first, bias])


def _swa_gla_kernel(sink_ref, aq_ref, akp_ref, akc_ref, avp_ref, avc_ref, bias_ref,
                    q_ref, k_ref, vr_ref, lr_ref, gw_ref, gb_ref, gn_ref, lm_ref,
                    a_ref, b_ref, st_ref, *, layer, rows):
    i = pl.program_id(0)
    blk = SWA_BLOCK
    nb = SWA_HEADS // 2
    ch = GLA_CHUNK
    nch = rows // ch
    heads = range(GLA_HEADS)
    nt = (((1,), (1,)), ((), ()))
    tn = (((0,), (0,)), ((), ()))

    @pl.when(i == 0)
    def _():
        st_ref[...] = jnp.zeros_like(st_ref)

    lane = lax.broadcasted_iota(jnp.int32, (blk, LANES), 1)
    low = lane < HEAD_DIM
    top = lax.broadcasted_iota(jnp.int32, (2 * blk, 1), 0) < blk
    ones = jnp.ones((2 * blk, LANES), BF16)
    kc, vc = akc_ref[...], avc_ref[...]
    swa_k = (jnp.concatenate([akp_ref[...], kc[:blk]], axis=0), kc)
    swa_v = (jnp.concatenate([avp_ref[...], vc[:blk]], axis=0), vc)
    swa = [dict(), dict()]

    def swa_scores(j):
        variant = jnp.minimum(2 * i + j, 1)
        out = []
        for b in range(nb):
            qb = aq_ref[blk * j:blk * (j + 1), LANES * b:LANES * (b + 1)]
            zero = jnp.zeros_like(qb)
            lhs = jnp.concatenate([jnp.where(low, qb, zero), jnp.where(low, zero, qb)], axis=0)
            kb = swa_k[j][:, LANES * (b // 3):LANES * (b // 3 + 1)]
            out.append(lax.dot_general(lhs, kb, nt, preferred_element_type=F32)
                       + bias_ref[variant, b])
        swa[j]["scores"] = out

    def swa_softmax(j):
        probs, sink_term = [], []
        for b in range(nb):
            sink = jnp.where(top, sink_ref[layer, 2 * b], sink_ref[layer, 2 * b + 1])
            sc_b = swa[j]["scores"][b]
            m = jnp.maximum(jnp.max(sc_b, axis=1, keepdims=True), sink)
            probs.append(jnp.exp(sc_b - m).astype(BF16))
            sink_term.append(jnp.exp(sink - m))
        swa[j]["probs"], swa[j]["sink"] = probs, sink_term

    def swa_output(j):
        for b in range(nb):
            vb = swa_v[j][:, LANES * (b // 3):LANES * (b // 3 + 1)]
            pv = jnp.dot(swa[j]["probs"][b], jnp.concatenate([vb, ones], axis=1),
                         preferred_element_type=F32)
            o = pv[:, :LANES] / (pv[:, LANES:] + swa[j]["sink"][b])
            a_ref[blk * j:blk * (j + 1), LANES * b:LANES * (b + 1)] = (
                jnp.where(low, o[:blk], o[blk:]).astype(a_ref.dtype))

    rowi = lax.broadcasted_iota(jnp.int32, (rows, rows), 0)
    coli = lax.broadcasted_iota(jnp.int32, (rows, rows), 1)
    same_chunk_causal = jnp.logical_and(coli <= rowi, (rowi // ch) == (coli // ch))
    gla = dict()

    def gla_gate():
        lr = lr_ref[...]
        lg = []
        for h in heads:
            gp = jnp.dot(lr, gw_ref[h], preferred_element_type=F32) + gb_ref[h]
            log_sig = jnp.minimum(gp, 0.0) - jnp.log(1.0 + jnp.exp(-jnp.abs(gp)))
            lg.append(jnp.maximum(log_sig * (1.0 / GATE_NORMALIZER), GATE_LOG_MIN))
        gla["lg"] = lg

    def gla_decay():
        lm = lm_ref[...]
        q_dec, k_inv, k_end, dec = [], [], [], []
        for h in heads:
            hi, lo = _split_bf16(gla["lg"][h])
            bb = (jnp.dot(lm, hi, preferred_element_type=F32)
                  + jnp.dot(lm, lo, preferred_element_type=F32))
            b, b_last = bb[:rows], bb[rows:]
            q = q_ref[:, GLA_DK_PAD * h:GLA_DK_PAD * (h + 1)].astype(F32) * (GLA_DK ** -0.5)
            kk = k_ref[:, GLA_DK_PAD * h:GLA_DK_PAD * (h + 1)].astype(F32)
            q_dec.append((q * jnp.exp(b)).astype(BF16))
            k_inv.append((kk * jnp.exp(-b)).astype(BF16))
            k_end.append((kk * jnp.exp(b_last - b)).astype(BF16))
            dec.append(jnp.exp(b_last))
        gla.update(q_dec=q_dec, k_inv=k_inv, k_end=k_end, dec=dec)

    def gla_intra():
        vlane = lax.broadcasted_iota(jnp.int32, (rows, GLA_DV_PAD), 1) < GLA_DV
        v = []
        for h in heads:
            win = vr_ref[:, GLA_VR * h:GLA_VR * h + GLA_DV_PAD]
            v.append(jnp.where(vlane, win, jnp.zeros_like(win)))
        sc = []
        for h in heads:
            s_h = lax.dot_general(gla["q_dec"][h], gla["k_inv"][h], nt,
                                  preferred_element_type=F32)
            sc.append(jnp.where(same_chunk_causal, s_h, 0.0).astype(BF16))
        gla["o_intra"] = [jnp.dot(sc[h], v[h], preferred_element_type=F32) for h in heads]
        gla["d_st"] = [[lax.dot_general(v[h][c * ch:(c + 1) * ch],
                                        gla["k_end"][h][c * ch:(c + 1) * ch], tn,
                                        preferred_element_type=F32)
                        for c in range(nch)] for h in heads]

    def gla_state():
        st_in = []
        for h in heads:
            st = st_ref[h]
            states = []
            for c in range(nch):
                states.append(st.astype(BF16))
                st = st * gla["dec"][h][c * ch:c * ch + 1, :] + gla["d_st"][h][c]
            st_ref[h] = st
            st_in.append(states)
        gla["st_in"] = st_in

    def gla_output():
        gn = gn_ref[...]
        for h in heads:
            o_inter = [lax.dot_general(gla["q_dec"][h][c * ch:(c + 1) * ch], gla["st_in"][h][c],
                                       nt, preferred_element_type=F32) for c in range(nch)]
            o = gla["o_intra"][h] + jnp.concatenate(o_inter, axis=0)
            ms = jnp.sum(o * o, axis=1, keepdims=True) * (1.0 / GLA_DV)
            y = o * lax.rsqrt(ms + RMS_EPS) * gn
            win = vr_ref[:, GLA_VR * h + LANES:GLA_VR * (h + 1)].astype(F32)
            rr = win[:, GLA_DV_PAD - GLA_DV:]
            out = y[:, :GLA_DV] * (rr / (1.0 + jnp.exp(-rr)))
            b_ref[:, GLA_DV * h:GLA_DV * (h + 1)] = out.astype(b_ref.dtype)

    gla_gate()
    swa_scores(0)
    swa_scores(1)
    gla_decay()
    swa_softmax(0)
    gla_intra()
    swa_softmax(1)
    swa_output(0)
    gla_state()
    swa_output(1)
    gla_output()


def swa_gla_attention(proj, sinks_perm, bias_tab, gw_pad, gb_pad, gn_pad, layer, rows=256):
    s = proj.shape[0]
    rows = min(rows, s)
    blk, ch = SWA_BLOCK, GLA_CHUNK
    assert rows == 2 * blk
    aq_i = OFF_AQ // SWA_WIDTH
    kv_w = SWA_KV_WIDTH
    dkw, dvw = GLA_HEADS * GLA_DK_PAD, GLA_HEADS * GLA_DV
    idx = np.arange(rows)
    same = (idx[:, None] // ch) == (idx[None, :] // ch)
    prefix = same & (idx[None, :] <= idx[:, None])
    lm = jnp.asarray(np.concatenate([prefix, same], axis=0).astype(np.float32), dtype=BF16)
    prev = lambda i: jnp.maximum(2 * i - 1, 0)
    col = lambda off, width: (lambda i: (i, off // width))
    return pl.pallas_call(
        functools.partial(_swa_gla_kernel, layer=layer, rows=rows),
        grid=(s // rows,),
        in_specs=[pl.BlockSpec(memory_space=pltpu.SMEM),
                  pl.BlockSpec((rows, SWA_WIDTH), lambda i: (i, aq_i)),
                  pl.BlockSpec((blk, kv_w), lambda i: (prev(i), OFF_AK // kv_w)),
                  pl.BlockSpec((rows, kv_w), col(OFF_AK, kv_w)),
                  pl.BlockSpec((blk, kv_w), lambda i: (prev(i), OFF_AV // kv_w)),
                  pl.BlockSpec((rows, kv_w), col(OFF_AV, kv_w)),
                  pl.BlockSpec(bias_tab.shape, lambda i: (0, 0, 0, 0)),
                  pl.BlockSpec((rows, dkw), col(OFF_BQ, dkw)),
                  pl.BlockSpec((rows, dkw), col(OFF_BK, dkw)),
                  pl.BlockSpec((rows, GLA_HEADS * GLA_VR), col(OFF_BVR, GLA_HEADS * GLA_VR)),
                  pl.BlockSpec((rows, GLA_LR_PAD), col(OFF_BLR, GLA_LR_PAD)),
                  pl.BlockSpec((None, GLA_HEADS, GLA_LR_PAD, GLA_DK_PAD),
                               lambda i: (layer, 0, 0, 0)),
                  pl.BlockSpec((None, GLA_HEADS, 1, GLA_DK_PAD), lambda i: (layer, 0, 0, 0)),
                  pl.BlockSpec((None, 1, GLA_DV_PAD), lambda i: (layer, 0, 0)),
                  pl.BlockSpec((2 * rows, rows), lambda i: (0, 0))],
        out_specs=[pl.BlockSpec((rows, SWA_WIDTH), lambda i: (i, 0)),
                   pl.BlockSpec((rows, dvw), lambda i: (i, 0))],
        out_shape=[jax.ShapeDtypeStruct((s, SWA_WIDTH), BF16),
                   jax.ShapeDtypeStruct((s, dvw), BF16)],
        scratch_shapes=[pltpu.VMEM((GLA_HEADS, GLA_DV_PAD, GLA_DK_PAD), F32)],
        compiler_params=_cparams(("arbitrary",)),
        name="swa_gla",
    )(sinks_perm, proj, proj, proj, proj, proj, bias_tab,
      proj, proj, proj, proj, gw_pad, gb_pad, gn_pad, lm)


def _sb_kernel(q_ref, k_ref, v_ref, u_ref, o_ref, acc_ref, car_ref, qm_ref, vm_ref,
               z0_ref, z1_ref, z2_ref, z3_ref, nl0_ref, nl1_ref, w0_ref, w1_ref,
               *, bq, nblk, unroll):
    lane = lax.broadcasted_iota(jnp.int32, (bq, LANES), 1)
    low = lane < HEAD_DIM
    nt = (((1,), (1,)), ((), ()))
    z_ref = (z0_ref, z1_ref, z2_ref, z3_ref)
    nl_ref, w_ref = (nl0_ref, nl1_ref), (w0_ref, w1_ref)

    def rows(blk):
        if isinstance(blk, int):
            return pl.ds(blk * bq, bq)
        return pl.ds(pl.multiple_of(blk * bq, bq), bq)

    def strict_mask():
        rowi = lax.broadcasted_iota(jnp.int32, (2 * bq, bq), 0)
        coli = lax.broadcasted_iota(jnp.int32, (2 * bq, bq), 1)
        return coli < (rowi & (bq - 1))

    def s1a(item, tick4):
        i, c = item
        z_ref[tick4][...] = lax.dot_general(qm_ref[i], k_ref[rows(c), :], nt,
                                            preferred_element_type=F32)

    def s1b(tick4, masked):
        z = z_ref[tick4][...].astype(BF16)
        nl = jnp.maximum(z, 0.0) + jnp.log(1.0 + jnp.exp2(jnp.abs(z) * (-LOG2E)))
        if masked:
            nl = jnp.where(strict_mask(), nl, jnp.zeros_like(nl))
        nl_ref[tick4 % 2][...] = nl

    def s2(item, tick4, masked, first):
        i, _ = item
        cs = jnp.dot(nl_ref[tick4 % 2][...], u_ref[...], preferred_element_type=F32)
        tot = jnp.broadcast_to(cs[:, 0:1], (2 * bq, LANES))
        if first:
            arg = z_ref[tick4][...] - cs
            car_ref[i] = tot
        else:
            car = car_ref[i]
            arg = z_ref[tick4][...] - cs - jnp.concatenate([car, car], axis=1)
            car_ref[i] = car + tot
        w = jnp.exp(arg)
        if masked:
            w = jnp.where(strict_mask(), w, 0.0)
        w = w.astype(BF16)
        w_ref[tick4 % 2][:, :bq] = w[:bq]
        w_ref[tick4 % 2][:, bq:] = w[bq:]

    def s3(item, tick4, first):
        i, c = item
        pv = jnp.dot(w_ref[tick4 % 2][...], vm_ref[c], preferred_element_type=F32)
        if first:
            acc_ref[rows(i), :] = pv
        else:
            acc_ref[rows(i), :] += pv

    def run_pipeline(items, step, masked, first):
        n = len(items)

        def tick(t4, its, do):
            if do[1]:
                s1b((t4 - 1) % 4, masked)
            if do[0]:
                s1a(its[0], t4)
            if do[2]:
                s2(its[2], (t4 - 2) % 4, masked, first)
            if do[3]:
                s3(its[3], (t4 - 3) % 4, first)

        def static_tick(t):
            do = [0 <= t - j < n for j in range(4)]
            its = [items[t - j] if do[j] else None for j in range(4)]
            tick(t % 4, its, do)

        lo_t, hi_t = 3, n - 1
        groups = max(hi_t - lo_t + 1, 0) // unroll
        if groups < 2:
            groups = 0
        for t in range(min(lo_t, n + 3)):
            static_tick(t)
        if groups:
            def body(_, carry):
                its = [(carry[2 * j], carry[2 * j + 1]) for j in range(4)]
                for k in range(unroll):
                    tick((lo_t + k) % 4, its, [True] * 4)
                    its = [step(*its[0])] + its[:3]
                return tuple(x for it in its for x in it)

            init = tuple(jnp.int32(x) for j in range(4) for x in items[lo_t - j])
            lax.fori_loop(0, groups, body, init)
        for t in range(lo_t + unroll * groups, n + 3):
            static_tick(t)

    for blk in range(nblk):
        r = rows(blk)
        qb, vb = q_ref[r, :], v_ref[r, :]
        zero = jnp.zeros_like(qb)
        qm_ref[blk, :bq] = jnp.where(low, qb, zero)
        qm_ref[blk, bq:] = jnp.where(low, zero, qb)
        vm_ref[blk, :bq] = jnp.where(low, vb, zero)
        vm_ref[blk, bq:] = jnp.where(low, zero, vb)

    diag = [(i, i) for i in range(nblk)]
    run_pipeline(diag, lambda i, c: (i + 1, c + 1), masked=True, first=True)

    off = [(i, c) for i in range(1, nblk) for c in range(i - 1, -1, -1)]

    def off_step(i, c):
        wrap = c == 0
        i2 = jnp.where(wrap, i + 1, i)
        return i2, jnp.where(wrap, i2 - 1, c - 1)

    if off:
        run_pipeline(off, off_step, masked=False, first=False)

    o_ref[...] = acc_ref[...].astype(o_ref.dtype)


def sb_attention(proj, bq=256, unroll=32):
    s = proj.shape[0]
    bq = min(bq, s)
    nblk = s // bq
    assert bq & (bq - 1) == 0 and unroll % 4 == 0
    qi = OFF_CQ // LANES
    ki = OFF_CK // LANES
    vi = OFF_CV // LANES
    u = (np.arange(bq)[:, None] >= np.arange(bq)[None, :]).astype(np.float32)
    u = jnp.asarray(u, dtype=BF16)
    slot = lambda shape, dtype: pltpu.VMEM(shape, dtype)
    return pl.pallas_call(
        functools.partial(_sb_kernel, bq=bq, nblk=nblk, unroll=unroll),
        grid=(SB_HEADS // 2,),
        in_specs=[pl.BlockSpec((s, LANES), lambda p: (0, qi + p)),
                  pl.BlockSpec((s, LANES), lambda p: (0, ki + p)),
                  pl.BlockSpec((s, LANES), lambda p: (0, vi + p)),
                  pl.BlockSpec((bq, bq), lambda p: (0, 0))],
        out_specs=pl.BlockSpec((s, LANES), lambda p: (0, p)),
        out_shape=jax.ShapeDtypeStruct((s, SB_WIDTH), BF16),
        scratch_shapes=[pltpu.VMEM((s, LANES), F32),
                        pltpu.VMEM((nblk, 2 * bq, LANES), F32),
                        pltpu.VMEM((nblk, 2 * bq, LANES), BF16),
                        pltpu.VMEM((nblk, 2 * bq, LANES), BF16)]
                       + [slot((2 * bq, bq), F32)] * 4
                       + [slot((2 * bq, bq), BF16)] * 2
                       + [slot((bq, 2 * bq), BF16)] * 2,
        compiler_params=_cparams(("parallel",)),
        name="stickbreak",
    )(proj, proj, proj, u)


def _oproj_kernel(a_ref, b_ref, c_ref, wa_ref, wb_ref, wc_ref, ga_ref, gc_ref,
                  x_ref, gn_ref, xo_ref, ho_ref):
    an = _rms(a_ref[...].astype(F32), ga_ref[...]).astype(BF16)
    cn = _rms(c_ref[...].astype(F32), gc_ref[...]).astype(BF16)
    y = (jnp.dot(an, wa_ref[...], preferred_element_type=F32)
         + jnp.dot(b_ref[...], wb_ref[...], preferred_element_type=F32)
         + jnp.dot(cn, wc_ref[...], preferred_element_type=F32))
    xn = x_ref[...] + y
    xo_ref[...] = xn
    ho_ref[...] = _rms(xn, gn_ref[...]).astype(ho_ref.dtype)


def out_projection(a, b, c, wa, wb, wc, ga, gc, x, gn, layer, tm=512):
    s, d = x.shape
    tm = min(tm, s)
    row = lambda i: (i, 0)
    fixed = lambda i: (layer, 0, 0)
    per_layer = lambda arr: pl.BlockSpec((None,) + arr.shape[1:], fixed)
    return pl.pallas_call(
        _oproj_kernel,
        grid=(s // tm,),
        in_specs=[pl.BlockSpec((tm, a.shape[1]), row),
                  pl.BlockSpec((tm, b.shape[1]), row),
                  pl.BlockSpec((tm, c.shape[1]), row),
                  per_layer(wa), per_layer(wb), per_layer(wc), per_layer(ga), per_layer(gc),
                  pl.BlockSpec((tm, d), row),
                  per_layer(gn)],
        out_specs=[pl.BlockSpec((tm, d), row), pl.BlockSpec((tm, d), row)],
        out_shape=[jax.ShapeDtypeStruct((s, d), F32), jax.ShapeDtypeStruct((s, d), BF16)],
        compiler_params=_cparams(("parallel",)),
        name="out_proj",
    )(a, b, c, wa, wb, wc, ga, gc, x, gn)


def _mlp_kernel(h_ref, x_ref, w1_ref, w2_ref, gn_ref, *out_refs, final):
    acc_ref = out_refs[0]
    j = pl.program_id(1)

    @pl.when(j == 0)
    def _():
        acc_ref[...] = x_ref[...]

    u = jnp.dot(h_ref[...], w1_ref[...].astype(BF16), preferred_element_type=F32)
    act = jnp.square(jnp.maximum(u, 0.0)).astype(BF16)
    acc_ref[...] += jnp.dot(act, w2_ref[...].astype(BF16), preferred_element_type=F32)

    @pl.when(j == pl.num_programs(1) - 1)
    def _():
        y = _rms(acc_ref[...], gn_ref[...])
        if final:
            acc_ref[...] = y
        else:
            out_refs[1][...] = y.astype(out_refs[1].dtype)


def mlp_block(h, x, w1_all, w2_all, layer, gn, final, tm=1024, tf=512):
    s, d = x.shape
    ff = w1_all.shape[2]
    tm = min(tm, s)
    row = lambda i, j: (i, 0)
    once = pl.Buffered(1)
    out_specs = [pl.BlockSpec((tm, d), row)]
    out_shape = [jax.ShapeDtypeStruct((s, d), F32)]
    if not final:
        out_specs.append(pl.BlockSpec((tm, d), row))
        out_shape.append(jax.ShapeDtypeStruct((s, d), BF16))
    return pl.pallas_call(
        functools.partial(_mlp_kernel, final=final),
        grid=(s // tm, ff // tf),
        in_specs=[pl.BlockSpec((tm, d), row, pipeline_mode=once),
                  pl.BlockSpec((tm, d), row, pipeline_mode=once),
                  pl.BlockSpec((None, d, tf), lambda i, j: (layer, 0, j)),
                  pl.BlockSpec((None, tf, d), lambda i, j: (layer, j, 0)),
                  pl.BlockSpec((None, 1, d), lambda i, j: (layer, 0, 0))],
        out_specs=out_specs,
        out_shape=out_shape,
        compiler_params=_cparams(("parallel", "arbitrary")),
        name="mlp",
    )(h, x, w1_all, w2_all, gn)


def _swa_perm_cols(w):
    return jnp.concatenate([w[..., HEAD_DIM * h:HEAD_DIM * (h + 1)] for h in SWA_Q_PERM], axis=-1)


def kernel(x, norm_mix, w_in, swa_sinks, rel_bias, gla_gate_w, gla_gate_b, gla_norm,
           swa_out_norm, sb_out_norm, w_out, norm_mlp, w_mlp_in, w_mlp_out, norm_final):
    depth = w_in.shape[0]
    xs = x[0]
    gla_w = GLA_HEADS * GLA_DV
    w_in_p = prep_w_in(jnp.swapaxes(w_in, 1, 2))
    wa = jnp.concatenate([w_out[:, HEAD_DIM * h:HEAD_DIM * (h + 1)] for h in SWA_Q_PERM],
                         axis=1).astype(BF16)
    wb = w_out[:, SWA_WIDTH:SWA_WIDTH + gla_w].astype(BF16)
    wc = w_out[:, SWA_WIDTH + gla_w:].astype(BF16)
    ga = _swa_perm_cols(swa_out_norm)[:, None, :]
    gc = sb_out_norm[:, None, :]
    g_mlp = norm_mlp[:, None, :]
    g_next = jnp.concatenate([norm_mix[1:], norm_final[None]], axis=0)[:, None, :]
    sinks_p = jnp.concatenate([swa_sinks[:, h:h + 1] for h in SWA_Q_PERM], axis=1)
    bias_tab = swa_bias_table(rel_bias)
    gw = jnp.stack([jnp.pad(gla_gate_w[:, :, GLA_DK * h:GLA_DK * (h + 1)],
                            ((0, 0), (0, GLA_LR_PAD - GLA_LOWRANK), (0, GLA_DK_PAD - GLA_DK)))
                    for h in range(GLA_HEADS)], axis=1).astype(BF16)
    gb = jnp.stack([jnp.pad(gla_gate_b[:, GLA_DK * h:GLA_DK * (h + 1)],
                            ((0, 0), (0, GLA_DK_PAD - GLA_DK)))
                    for h in range(GLA_HEADS)], axis=1)[:, :, None, :]
    gn = jnp.pad(gla_norm, ((0, 0), (0, GLA_DV_PAD - GLA_DV)))[:, None, :]

    h = None
    for l in range(depth):
        if l == 0:
            proj = in_projection(xs, w_in_p, l, gain=norm_mix[0])
        else:
            proj = in_projection(h, w_in_p, l)
        a, b = swa_gla_attention(proj, sinks_p, bias_tab, gw, gb, gn, l)
        c = sb_attention(proj)
        xs, hm = out_projection(a, b, c, wa, wb, wc, ga, gc, xs, g_mlp, l)
        if l + 1 < depth:
            xs, h = mlp_block(hm, xs, w_mlp_in, w_mlp_out, l, g_next, final=False)
        else:
            (out,) = mlp_block(hm, xs, w_mlp_in, w_mlp_out, l, g_next, final=True)
    return out[None]
```

```python
import functools
import math

import numpy as np
import jax
import jax.numpy as jnp
from jax import lax
from jax.experimental import pallas as pl
from jax.experimental.pallas import tpu as pltpu

F32 = jnp.float32
BF16 = jnp.bfloat16

HEAD_DIM = 64
SWA_HEADS = 12
SWA_KV_HEADS = 4
SWA_WIDTH = SWA_HEADS * HEAD_DIM
SWA_KV_WIDTH = SWA_KV_HEADS * HEAD_DIM
SWA_BLOCK = 128
WINDOW = 128
REL_BUCKETS = 32
REL_MAX_DIST = 128
GLA_HEADS = 4
GLA_DV = 192
GLA_DK = 96
GLA_DK_PAD = 128
GLA_DV_PAD = 256
GLA_LOWRANK = 16
GLA_LR_LANE = GLA_DK
GLA_CHUNK = 64
GATE_NORMALIZER = 16.0
GATE_LOG_MIN = -1.0
SB_HEADS = 8
SB_WIDTH = SB_HEADS * HEAD_DIM
RMS_EPS = 1e-6
LOG2E = 1.4426950408889634
NEG_INF = -1e30

LANES = 128
VMEM_LIMIT = 56 * 1024 * 1024

GLA_VR = 2 * GLA_DV
OFF_BVR = 0
OFF_BQ = 1536
OFF_BK = 2048
OFF_CQ = 2560
OFF_CK = 3072
OFF_CV = 3584
OFF_AK = 4096
OFF_AV = 4352
OFF_AQ = 4608
PROJ_PAD_WIDTH = 5376

SWA_Q_PERM = (0, 3, 1, 4, 2, 5, 6, 9, 7, 10, 8, 11)

_O_AQ, _O_AK, _O_AV = 0, 768, 1024
_O_BQ, _O_BK, _O_BV, _O_BR, _O_BLR = 1280, 1664, 2048, 2816, 3584
_O_CQ, _O_CK, _O_CV = 3600, 4112, 4624


def _cparams(sem):
    return pltpu.CompilerParams(dimension_semantics=sem, vmem_limit_bytes=VMEM_LIMIT)


def _rms(x, gain):
    ms = jnp.mean(x * x, axis=-1, keepdims=True)
    return x * lax.rsqrt(ms + RMS_EPS) * gain


def _split_bf16(x):
    hi = x.astype(BF16)
    lo = (x - hi.astype(F32)).astype(BF16)
    return hi, lo


_NT_DIMS = (((1,), (1,)), ((), ()))


def _matmul_kernel(h_ref, w_ref, o_ref):
    o_ref[...] = lax.dot_general(h_ref[...], w_ref[...], _NT_DIMS,
                                 preferred_element_type=F32).astype(o_ref.dtype)


def _norm_matmul_kernel(x_ref, g_ref, w_ref, o_ref):
    h = _rms(x_ref[...], g_ref[...]).astype(BF16)
    o_ref[...] = lax.dot_general(h, w_ref[...], _NT_DIMS,
                                 preferred_element_type=F32).astype(o_ref.dtype)


def in_projection(h, w_all, layer, gain=None, tm=1024, tn=PROJ_PAD_WIDTH // 3):
    s, d = h.shape
    n = w_all.shape[1]
    tm = min(tm, s)
    w_spec = pl.BlockSpec((None, tn, d), lambda i, j: (layer, j, 0))
    h_spec = pl.BlockSpec((tm, d), lambda i, j: (i, 0))
    common = dict(
        grid=(s // tm, n // tn),
        out_specs=pl.BlockSpec((tm, tn), lambda i, j: (i, j)),
        out_shape=jax.ShapeDtypeStruct((s, n), BF16),
        compiler_params=_cparams(("parallel", "arbitrary")),
        name="in_proj",
    )
    if gain is None:
        return pl.pallas_call(_matmul_kernel, in_specs=[h_spec, w_spec], **common)(h, w_all)
    return pl.pallas_call(
        _norm_matmul_kernel,
        in_specs=[h_spec, pl.BlockSpec((1, d), lambda i, j: (0, 0)), w_spec],
        **common,
    )(h, gain.reshape(1, d), w_all)


def _w_in_moves():
    scale = HEAD_DIM ** -0.5
    moves = []
    for h in range(GLA_HEADS):
        moves.append((_O_BV + GLA_DV * h, OFF_BVR + GLA_VR * h, GLA_DV, 1.0))
        moves.append((_O_BR + GLA_DV * h, OFF_BVR + GLA_VR * h + GLA_DV, GLA_DV, 1.0))
        moves.append((_O_BQ + GLA_DK * h, OFF_BQ + GLA_DK_PAD * h, GLA_DK, 1.0))
        moves.append((_O_BK + GLA_DK * h, OFF_BK + GLA_DK_PAD * h, GLA_DK, 1.0))
    moves += [(_O_CQ, OFF_CQ, SB_WIDTH, scale), (_O_CK, OFF_CK, SB_WIDTH, 1.0),
              (_O_CV, OFF_CV, SB_WIDTH, 1.0)]
    for j, h in enumerate(SWA_Q_PERM):
        moves.append((_O_AQ + HEAD_DIM * h, OFF_AQ + HEAD_DIM * j, HEAD_DIM, scale))
    moves += [(_O_AK, OFF_AK, SWA_KV_WIDTH, 1.0), (_O_AV, OFF_AV, SWA_KV_WIDTH, 1.0)]
    for h in range(GLA_HEADS):
        moves.append((_O_BLR, OFF_BQ + GLA_DK_PAD * h + GLA_LR_LANE, GLA_LOWRANK, 1.0))
    return moves


def _w_in_prep_kernel(w_ref, o_ref):
    o_ref[...] = jnp.zeros_like(o_ref)
    for src, dst, width, scale in _w_in_moves():
        piece = w_ref[src:src + width, :]
        if scale != 1.0:
            piece = piece * scale
        o_ref[dst:dst + width, :] = piece.astype(o_ref.dtype)


def prep_w_in(w_in_t, tc=256):
    layers, n, d = w_in_t.shape
    return pl.pallas_call(
        _w_in_prep_kernel,
        grid=(layers, d // tc),
        in_specs=[pl.BlockSpec((None, n, tc), lambda l, i: (l, 0, i))],
        out_specs=pl.BlockSpec((None, PROJ_PAD_WIDTH, tc), lambda l, i: (l, 0, i)),
        out_shape=jax.ShapeDtypeStruct((layers, PROJ_PAD_WIDTH, d), BF16),
        compiler_params=_cparams(("parallel", "parallel")),
        name="w_in_prep",
    )(w_in_t)


def _t5_causal_bucket(dist):
    max_exact = REL_BUCKETS // 2
    is_small = dist < max_exact
    ratio = (jnp.log(jnp.maximum(dist, 1).astype(F32) / max_exact)
             / math.log(REL_MAX_DIST / max_exact))
    large = max_exact + (ratio * (REL_BUCKETS - max_exact)).astype(jnp.int32)
    large = jnp.minimum(large, REL_BUCKETS - 1)
    return jnp.where(is_small, dist, large)


def swa_bias_table(rel_bias):
    blk = SWA_BLOCK
    qpos = jnp.arange(blk) + blk
    kpos = jnp.arange(2 * blk)
    dist = qpos[:, None] - kpos[None, :]
    in_window = (dist >= 0) & (dist < WINDOW)
    bucket = _t5_causal_bucket(jnp.maximum(dist, 0))
    table = rel_bias.astype(F32)
    table = jnp.stack([table[:, h] for h in SWA_Q_PERM])
    bias = jnp.full((SWA_HEADS, blk, 2 * blk), NEG_INF, F32)
    for bkt in range(REL_BUCKETS):
        hit = (in_window & (bucket == bkt))[None]
        bias = jnp.where(hit, table[:, bkt][:, None, None], bias)
    bias = bias.reshape(SWA_HEADS // 2, 2 * blk, 2 * blk)
    first = jnp.where((kpos < blk)[None, None, :], NEG_INF, bias)
    return jnp.stack([first, bias])


def _swa_gla_kernel(sink_ref, aq_ref, akp_ref, akc_ref, avp_ref, avc_ref, bias_ref,
                    q_ref, k_ref, vr_ref, gw_ref, gb_ref, gn_ref, lm_ref,
                    a_ref, b_ref, st_ref, *, layer, rows):
    i = pl.program_id(0)
    blk = SWA_BLOCK
    nb = SWA_HEADS // 2
    ch = GLA_CHUNK
    nch = rows // ch
    heads = range(GLA_HEADS)
    nt = (((1,), (1,)), ((), ()))
    tn = (((0,), (0,)), ((), ()))

    @pl.when(i == 0)
    def _():
        st_ref[...] = jnp.zeros_like(st_ref)

    lane = lax.broadcasted_iota(jnp.int32, (blk, LANES), 1)
    low = lane < HEAD_DIM
    top = lax.broadcasted_iota(jnp.int32, (2 * blk, 1), 0) < blk
    ones = jnp.ones((2 * blk, LANES), BF16)
    kc, vc = akc_ref[...], avc_ref[...]
    swa_k = (jnp.concatenate([akp_ref[...], kc[:blk]], axis=0), kc)
    swa_v = (jnp.concatenate([avp_ref[...], vc[:blk]], axis=0), vc)
    swa = [dict(), dict()]

    def swa_scores(j):
        variant = jnp.minimum(2 * i + j, 1)
        out = []
        for b in range(nb):
            qb = aq_ref[blk * j:blk * (j + 1), LANES * b:LANES * (b + 1)]
            zero = jnp.zeros_like(qb)
            lhs = jnp.concatenate([jnp.where(low, qb, zero), jnp.where(low, zero, qb)], axis=0)
            kb = swa_k[j][:, LANES * (b // 3):LANES * (b // 3 + 1)]
            out.append(lax.dot_general(lhs, kb, nt, preferred_element_type=F32)
                       + bias_ref[variant, b])
        swa[j]["scores"] = out

    def swa_softmax(j):
        probs, sink_term = [], []
        for b in range(nb):
            sink = jnp.where(top, sink_ref[layer, 2 * b], sink_ref[layer, 2 * b + 1])
            sc_b = swa[j]["scores"][b]
            m = jnp.maximum(jnp.max(sc_b, axis=1, keepdims=True), sink)
            probs.append(jnp.exp(sc_b - m).astype(BF16))
            sink_term.append(jnp.exp(sink - m))
        swa[j]["probs"], swa[j]["sink"] = probs, sink_term

    def swa_output(j):
        for b in range(nb):
            vb = swa_v[j][:, LANES * (b // 3):LANES * (b // 3 + 1)]
            pv = jnp.dot(swa[j]["probs"][b], jnp.concatenate([vb, ones], axis=1),
                         preferred_element_type=F32)
            o = pv[:, :LANES] / (pv[:, LANES:] + swa[j]["sink"][b])
            a_ref[blk * j:blk * (j + 1), LANES * b:LANES * (b + 1)] = (
                jnp.where(low, o[:blk], o[blk:]).astype(a_ref.dtype))

    rowi = lax.broadcasted_iota(jnp.int32, (rows, rows), 0)
    coli = lax.broadcasted_iota(jnp.int32, (rows, rows), 1)
    same_chunk_causal = jnp.logical_and(coli <= rowi, (rowi // ch) == (coli // ch))
    gla = dict()

    def gla_gate():
        lg = []
        for h in heads:
            gp = jnp.dot(q_ref[:, GLA_DK_PAD * h:GLA_DK_PAD * (h + 1)], gw_ref[h],
                         preferred_element_type=F32) + gb_ref[h]
            log_sig = jnp.minimum(gp, 0.0) - jnp.log(1.0 + jnp.exp(-jnp.abs(gp)))
            lg.append(jnp.maximum(log_sig * (1.0 / GATE_NORMALIZER), GATE_LOG_MIN))
        gla["lg"] = lg

    def gla_decay():
        lm = lm_ref[...]
        q_dec, k_inv, k_end, dec = [], [], [], []
        for h in heads:
            hi, lo = _split_bf16(gla["lg"][h])
            bb = (jnp.dot(lm, hi, preferred_element_type=F32)
                  + jnp.dot(lm, lo, preferred_element_type=F32))
            b, b_last = bb[:rows], bb[rows:]
            q = q_ref[:, GLA_DK_PAD * h:GLA_DK_PAD * (h + 1)].astype(F32) * (GLA_DK ** -0.5)
            kk = k_ref[:, GLA_DK_PAD * h:GLA_DK_PAD * (h + 1)].astype(F32)
            q_dec.append((q * jnp.exp(b)).astype(BF16))
            k_inv.append((kk * jnp.exp(-b)).astype(BF16))
            k_end.append((kk * jnp.exp(b_last - b)).astype(BF16))
            dec.append(jnp.exp(b_last))
        gla.update(q_dec=q_dec, k_inv=k_inv, k_end=k_end, dec=dec)

    def gla_intra():
        vlane = lax.broadcasted_iota(jnp.int32, (rows, GLA_DV_PAD), 1) < GLA_DV
        v = []
        for h in heads:
            win = vr_ref[:, GLA_VR * h:GLA_VR * h + GLA_DV_PAD]
            v.append(jnp.where(vlane, win, jnp.zeros_like(win)))
        sc = []
        for h in heads:
            s_h = lax.dot_general(gla["q_dec"][h], gla["k_inv"][h], nt,
                                  preferred_element_type=F32)
            sc.append(jnp.where(same_chunk_causal, s_h, 0.0).astype(BF16))
        gla["o_intra"] = [jnp.dot(sc[h], v[h], preferred_element_type=F32) for h in heads]
        gla["d_st"] = [[lax.dot_general(v[h][c * ch:(c + 1) * ch],
                                        gla["k_end"][h][c * ch:(c + 1) * ch], tn,
                                        preferred_element_type=F32)
                        for c in range(nch)] for h in heads]

    def gla_state():
        st_in = []
        for h in heads:
            st = st_ref[h]
            states = []
            for c in range(nch):
                states.append(st.astype(BF16))
                st = st * gla["dec"][h][c * ch:c * ch + 1, :] + gla["d_st"][h][c]
            st_ref[h] = st
            st_in.append(states)
        gla["st_in"] = st_in

    def gla_output():
        gn = gn_ref[...]
        for h in heads:
            o_inter = [lax.dot_general(gla["q_dec"][h][c * ch:(c + 1) * ch], gla["st_in"][h][c],
                                       nt, preferred_element_type=F32) for c in range(nch)]
            o = gla["o_intra"][h] + jnp.concatenate(o_inter, axis=0)
            ms = jnp.sum(o * o, axis=1, keepdims=True) * (1.0 / GLA_DV)
            y = o * lax.rsqrt(ms + RMS_EPS) * gn
            win = vr_ref[:, GLA_VR * h + LANES:GLA_VR * (h + 1)].astype(F32)
            rr = win[:, GLA_DV_PAD - GLA_DV:]
            out = y[:, :GLA_DV] * (rr / (1.0 + jnp.exp(-rr)))
            b_ref[:, GLA_DV_PAD * h:GLA_DV_PAD * h + GLA_DV] = out.astype(b_ref.dtype)
            b_ref[:, GLA_DV_PAD * h + GLA_DV:GLA_DV_PAD * (h + 1)] = jnp.zeros(
                (rows, GLA_DV_PAD - GLA_DV), b_ref.dtype)

    gla_gate()
    swa_scores(0)
    swa_scores(1)
    gla_decay()
    swa_softmax(0)
    gla_intra()
    swa_softmax(1)
    swa_output(0)
    gla_state()
    swa_output(1)
    gla_output()


def swa_gla_attention(proj, sinks_perm, bias_tab, gw_pad, gb_pad, gn_pad, layer, rows=256):
    s = proj.shape[0]
    rows = min(rows, s)
    blk, ch = SWA_BLOCK, GLA_CHUNK
    assert rows == 2 * blk
    aq_i = OFF_AQ // SWA_WIDTH
    kv_w = SWA_KV_WIDTH
    dkw, dvw = GLA_HEADS * GLA_DK_PAD, GLA_HEADS * GLA_DV_PAD
    idx = np.arange(rows)
    same = (idx[:, None] // ch) == (idx[None, :] // ch)
    prefix = same & (idx[None, :] <= idx[:, None])
    lm = jnp.asarray(np.concatenate([prefix, same], axis=0).astype(np.float32), dtype=BF16)
    prev = lambda i: jnp.maximum(2 * i - 1, 0)
    col = lambda off, width: (lambda i: (i, off // width))
    return pl.pallas_call(
        functools.partial(_swa_gla_kernel, layer=layer, rows=rows),
        grid=(s // rows,),
        in_specs=[pl.BlockSpec(memory_space=pltpu.SMEM),
                  pl.BlockSpec((rows, SWA_WIDTH), lambda i: (i, aq_i)),
                  pl.BlockSpec((blk, kv_w), lambda i: (prev(i), OFF_AK // kv_w)),
                  pl.BlockSpec((rows, kv_w), col(OFF_AK, kv_w)),
                  pl.BlockSpec((blk, kv_w), lambda i: (prev(i), OFF_AV // kv_w)),
                  pl.BlockSpec((rows, kv_w), col(OFF_AV, kv_w)),
                  pl.BlockSpec(bias_tab.shape, lambda i: (0, 0, 0, 0)),
                  pl.BlockSpec((rows, dkw), col(OFF_BQ, dkw)),
                  pl.BlockSpec((rows, dkw), col(OFF_BK, dkw)),
                  pl.BlockSpec((rows, GLA_HEADS * GLA_VR), col(OFF_BVR, GLA_HEADS * GLA_VR)),
                  pl.BlockSpec((None, GLA_HEADS, GLA_DK_PAD, GLA_DK_PAD),
                               lambda i: (layer, 0, 0, 0)),
                  pl.BlockSpec((None, GLA_HEADS, 1, GLA_DK_PAD), lambda i: (layer, 0, 0, 0)),
                  pl.BlockSpec((None, 1, GLA_DV_PAD), lambda i: (layer, 0, 0)),
                  pl.BlockSpec((2 * rows, rows), lambda i: (0, 0))],
        out_specs=[pl.BlockSpec((rows, SWA_WIDTH), lambda i: (i, 0)),
                   pl.BlockSpec((rows, dvw), lambda i: (i, 0))],
        out_shape=[jax.ShapeDtypeStruct((s, SWA_WIDTH), BF16),
                   jax.ShapeDtypeStruct((s, dvw), BF16)],
        scratch_shapes=[pltpu.VMEM((GLA_HEADS, GLA_DV_PAD, GLA_DK_PAD), F32)],
        compiler_params=_cparams(("arbitrary",)),
        name="swa_gla",
    )(sinks_perm, proj, proj, proj, proj, proj, bias_tab,
      proj, proj, proj, gw_pad, gb_pad, gn_pad, lm)


def _sb_kernel(q_ref, k_ref, v_ref, u_ref, o_ref, acc_ref, car_ref, qm_ref, vm_ref,
               z0_ref, z1_ref, z2_ref, z3_ref, nl0_ref, nl1_ref, w0_ref, w1_ref,
               *, bq, nblk, unroll):
    lane = lax.broadcasted_iota(jnp.int32, (bq, LANES), 1)
    low = lane < HEAD_DIM
    nt = (((1,), (1,)), ((), ()))
    z_ref = (z0_ref, z1_ref, z2_ref, z3_ref)
    nl_ref, w_ref = (nl0_ref, nl1_ref), (w0_ref, w1_ref)

    def rows(blk):
        if isinstance(blk, int):
            return pl.ds(blk * bq, bq)
        return pl.ds(pl.multiple_of(blk * bq, bq), bq)

    def strict_mask():
        rowi = lax.broadcasted_iota(jnp.int32, (2 * bq, bq), 0)
        coli = lax.broadcasted_iota(jnp.int32, (2 * bq, bq), 1)
        return coli < (rowi & (bq - 1))

    def s1a(item, tick4):
        i, c = item
        z_ref[tick4][...] = lax.dot_general(qm_ref[i], k_ref[rows(c), :], nt,
                                            preferred_element_type=F32)

    def s1b(tick4, masked):
        z = z_ref[tick4][...].astype(BF16)
        nl = jnp.maximum(z, 0.0) + jnp.log(1.0 + jnp.exp2(jnp.abs(z) * (-LOG2E)))
        if masked:
            nl = jnp.where(strict_mask(), nl, jnp.zeros_like(nl))
        nl_ref[tick4 % 2][...] = nl

    def s2(item, tick4, masked, first):
        i, _ = item
        cs = jnp.dot(nl_ref[tick4 % 2][...], u_ref[...], preferred_element_type=F32)
        tot = jnp.broadcast_to(cs[:, 0:1], (2 * bq, LANES))
        if first:
            arg = z_ref[tick4][...] - cs
            car_ref[i] = tot
        else:
            car = car_ref[i]
            arg = z_ref[tick4][...] - cs - jnp.concatenate([car, car], axis=1)
            car_ref[i] = car + tot
        w = jnp.exp(arg)
        if masked:
            w = jnp.where(strict_mask(), w, 0.0)
        w = w.astype(BF16)
        w_ref[tick4 % 2][:, :bq] = w[:bq]
        w_ref[tick4 % 2][:, bq:] = w[bq:]

    def s3(item, tick4, first):
        i, c = item
        pv = jnp.dot(w_ref[tick4 % 2][...], vm_ref[c], preferred_element_type=F32)
        if first:
            acc_ref[rows(i), :] = pv
        else:
            acc_ref[rows(i), :] += pv

    def run_pipeline(items, step, masked, first):
        n = len(items)

        def tick(t4, its, do):
            if do[1]:
                s1b((t4 - 1) % 4, masked)
            if do[0]:
                s1a(its[0], t4)
            if do[2]:
                s2(its[2], (t4 - 2) % 4, masked, first)
            if do[3]:
                s3(its[3], (t4 - 3) % 4, first)

        def static_tick(t):
            do = [0 <= t - j < n for j in range(4)]
            its = [items[t - j] if do[j] else None for j in range(4)]
            tick(t % 4, its, do)

        lo_t, hi_t = 3, n - 1
        groups = max(hi_t - lo_t + 1, 0) // unroll
        if groups < 2:
            groups = 0
        for t in range(min(lo_t, n + 3)):
            static_tick(t)
        if groups:
            def body(_, carry):
                its = [(carry[2 * j], carry[2 * j + 1]) for j in range(4)]
                for k in range(unroll):
                    tick((lo_t + k) % 4, its, [True] * 4)
                    its = [step(*its[0])] + its[:3]
                return tuple(x for it in its for x in it)

            init = tuple(jnp.int32(x) for j in range(4) for x in items[lo_t - j])
            lax.fori_loop(0, groups, body, init)
        for t in range(lo_t + unroll * groups, n + 3):
            static_tick(t)

    for blk in range(nblk):
        r = rows(blk)
        qb, vb = q_ref[r, :], v_ref[r, :]
        zero = jnp.zeros_like(qb)
        qm_ref[blk, :bq] = jnp.where(low, qb, zero)
        qm_ref[blk, bq:] = jnp.where(low, zero, qb)
        vm_ref[blk, :bq] = jnp.where(low, vb, zero)
        vm_ref[blk, bq:] = jnp.where(low, zero, vb)

    diag = [(i, i) for i in range(nblk)]
    run_pipeline(diag, lambda i, c: (i + 1, c + 1), masked=True, first=True)

    off = [(i, c) for i in range(1, nblk) for c in range(i - 1, -1, -1)]

    def off_step(i, c):
        wrap = c == 0
        i2 = jnp.where(wrap, i + 1, i)
        return i2, jnp.where(wrap, i2 - 1, c - 1)

    if off:
        run_pipeline(off, off_step, masked=False, first=False)

    o_ref[...] = acc_ref[...].astype(o_ref.dtype)


def sb_attention(proj, bq=256, unroll=32):
    s = proj.shape[0]
    bq = min(bq, s)
    nblk = s // bq
    assert bq & (bq - 1) == 0 and unroll % 4 == 0
    qi = OFF_CQ // LANES
    ki = OFF_CK // LANES
    vi = OFF_CV // LANES
    u = (np.arange(bq)[:, None] >= np.arange(bq)[None, :]).astype(np.float32)
    u = jnp.asarray(u, dtype=BF16)
    slot = lambda shape, dtype: pltpu.VMEM(shape, dtype)
    return pl.pallas_call(
        functools.partial(_sb_kernel, bq=bq, nblk=nblk, unroll=unroll),
        grid=(SB_HEADS // 2,),
        in_specs=[pl.BlockSpec((s, LANES), lambda p: (0, qi + p)),
                  pl.BlockSpec((s, LANES), lambda p: (0, ki + p)),
                  pl.BlockSpec((s, LANES), lambda p: (0, vi + p)),
                  pl.BlockSpec((bq, bq), lambda p: (0, 0))],
        out_specs=pl.BlockSpec((s, LANES), lambda p: (0, p)),
        out_shape=jax.ShapeDtypeStruct((s, SB_WIDTH), BF16),
        scratch_shapes=[pltpu.VMEM((s, LANES), F32),
                        pltpu.VMEM((nblk, 2 * bq, LANES), F32),
                        pltpu.VMEM((nblk, 2 * bq, LANES), BF16),
                        pltpu.VMEM((nblk, 2 * bq, LANES), BF16)]
                       + [slot((2 * bq, bq), F32)] * 4
                       + [slot((2 * bq, bq), BF16)] * 2
                       + [slot((bq, 2 * bq), BF16)] * 2,
        compiler_params=_cparams(("parallel",)),
        name="stickbreak",
    )(proj, proj, proj, u)


def _oproj_kernel(a_ref, b_ref, c_ref, wa_ref, wb_ref, wc_ref, ga_ref, gc_ref,
                  x_ref, gn_ref, xo_ref, ho_ref):
    an = _rms(a_ref[...].astype(F32), ga_ref[...]).astype(BF16)
    cn = _rms(c_ref[...].astype(F32), gc_ref[...]).astype(BF16)
    y = (jnp.dot(an, wa_ref[...], preferred_element_type=F32)
         + jnp.dot(b_ref[...], wb_ref[...], preferred_element_type=F32)
         + jnp.dot(cn, wc_ref[...], preferred_element_type=F32))
    xn = x_ref[...] + y
    xo_ref[...] = xn
    ho_ref[...] = _rms(xn, gn_ref[...]).astype(ho_ref.dtype)


def out_projection(a, b, c, wa, wb, wc, ga, gc, x, gn, layer, tm=512):
    s, d = x.shape
    tm = min(tm, s)
    row = lambda i: (i, 0)
    fixed = lambda i: (layer, 0, 0)
    per_layer = lambda arr: pl.BlockSpec((None,) + arr.shape[1:], fixed)
    return pl.pallas_call(
        _oproj_kernel,
        grid=(s // tm,),
        in_specs=[pl.BlockSpec((tm, a.shape[1]), row),
                  pl.BlockSpec((tm, b.shape[1]), row),
                  pl.BlockSpec((tm, c.shape[1]), row),
                  per_layer(wa), per_layer(wb), per_layer(wc), per_layer(ga), per_layer(gc),
                  pl.BlockSpec((tm, d), row),
                  per_layer(gn)],
        out_specs=[pl.BlockSpec((tm, d), row), pl.BlockSpec((tm, d), row)],
        out_shape=[jax.ShapeDtypeStruct((s, d), F32), jax.ShapeDtypeStruct((s, d), BF16)],
        compiler_params=_cparams(("parallel",)),
        name="out_proj",
    )(a, b, c, wa, wb, wc, ga, gc, x, gn)


def _mlp_kernel(h_ref, x_ref, w1_ref, w2_ref, gn_ref, *out_refs, final):
    acc_ref = out_refs[0]
    j = pl.program_id(1)

    @pl.when(j == 0)
    def _():
        acc_ref[...] = x_ref[...]

    u = jnp.dot(h_ref[...], w1_ref[...].astype(BF16), preferred_element_type=F32)
    act = jnp.square(jnp.maximum(u, 0.0)).astype(BF16)
    acc_ref[...] += jnp.dot(act, w2_ref[...].astype(BF16), preferred_element_type=F32)

    @pl.when(j == pl.num_programs(1) - 1)
    def _():
        y = _rms(acc_ref[...], gn_ref[...])
        if final:
            acc_ref[...] = y
        else:
            out_refs[1][...] = y.astype(out_refs[1].dtype)


def mlp_block(h, x, w1_all, w2_all, layer, gn, final, tm=1024, tf=512):
    s, d = x.shape
    ff = w1_all.shape[2]
    tm = min(tm, s)
    row = lambda i, j: (i, 0)
    once = pl.Buffered(1)
    out_specs = [pl.BlockSpec((tm, d), row)]
    out_shape = [jax.ShapeDtypeStruct((s, d), F32)]
    if not final:
        out_specs.append(pl.BlockSpec((tm, d), row))
        out_shape.append(jax.ShapeDtypeStruct((s, d), BF16))
    return pl.pallas_call(
        functools.partial(_mlp_kernel, final=final),
        grid=(s // tm, ff // tf),
        in_specs=[pl.BlockSpec((tm, d), row, pipeline_mode=once),
                  pl.BlockSpec((tm, d), row, pipeline_mode=once),
                  pl.BlockSpec((None, d, tf), lambda i, j: (layer, 0, j)),
                  pl.BlockSpec((None, tf, d), lambda i, j: (layer, j, 0)),
                  pl.BlockSpec((None, 1, d), lambda i, j: (layer, 0, 0))],
        out_specs=out_specs,
        out_shape=out_shape,
        compiler_params=_cparams(("parallel", "arbitrary")),
        name="mlp",
    )(h, x, w1_all, w2_all, gn)


def _head_rows(w, off, heads, width, padded):
    pieces = []
    for h in range(heads):
        pieces.append(w[..., off + h * width:off + (h + 1) * width, :])
        if padded > width:
            pieces.append(jnp.zeros(w.shape[:-2] + (padded - width, w.shape[-1]), w.dtype))
    return pieces


def _swa_perm_cols(w):
    return jnp.concatenate([w[..., HEAD_DIM * h:HEAD_DIM * (h + 1)] for h in SWA_Q_PERM], axis=-1)


def kernel(x, norm_mix, w_in, swa_sinks, rel_bias, gla_gate_w, gla_gate_b, gla_norm,
           swa_out_norm, sb_out_norm, w_out, norm_mlp, w_mlp_in, w_mlp_out, norm_final):
    depth = w_in.shape[0]
    xs = x[0]
    gla_w = GLA_HEADS * GLA_DV
    w_in_p = prep_w_in(jnp.swapaxes(w_in, 1, 2))
    wa = jnp.concatenate([w_out[:, HEAD_DIM * h:HEAD_DIM * (h + 1)] for h in SWA_Q_PERM],
                         axis=1).astype(BF16)
    wb = jnp.concatenate(_head_rows(w_out, SWA_WIDTH, GLA_HEADS, GLA_DV, GLA_DV_PAD),
                         axis=1).astype(BF16)
    wc = w_out[:, SWA_WIDTH + gla_w:].astype(BF16)
    ga = _swa_perm_cols(swa_out_norm)[:, None, :]
    gc = sb_out_norm[:, None, :]
    g_mlp = norm_mlp[:, None, :]
    g_next = jnp.concatenate([norm_mix[1:], norm_final[None]], axis=0)[:, None, :]
    sinks_p = jnp.concatenate([swa_sinks[:, h:h + 1] for h in SWA_Q_PERM], axis=1)
    bias_tab = swa_bias_table(rel_bias)
    gw = jnp.stack([jnp.pad(gla_gate_w[:, :, GLA_DK * h:GLA_DK * (h + 1)],
                            ((0, 0), (GLA_LR_LANE, GLA_DK_PAD - GLA_LR_LANE - GLA_LOWRANK),
                             (0, GLA_DK_PAD - GLA_DK)))
                    for h in range(GLA_HEADS)], axis=1).astype(BF16)
    gb = jnp.stack([jnp.pad(gla_gate_b[:, GLA_DK * h:GLA_DK * (h + 1)],
                            ((0, 0), (0, GLA_DK_PAD - GLA_DK)))
                    for h in range(GLA_HEADS)], axis=1)[:, :, None, :]
    gn = jnp.pad(gla_norm, ((0, 0), (0, GLA_DV_PAD - GLA_DV)))[:, None, :]

    h = None
    for l in range(depth):
        if l == 0:
            proj = in_projection(xs, w_in_p, l, gain=norm_mix[0])
        else:
            proj = in_projection(h, w_in_p, l)
        a, b = swa_gla_attention(proj, sinks_p, bias_tab, gw, gb, gn, l)
        c = sb_attention(proj)
        xs, hm = out_projection(a, b, c, wa, wb, wc, ga, gc, xs, g_mlp, l)
        if l + 1 < depth:
            xs, h = mlp_block(hm, xs, w_mlp_in, w_mlp_out, l, g_next, final=False)
        else:
            (out,) = mlp_block(hm, xs, w_mlp_in, w_mlp_out, l, g_next, final=True)
    return out[None]
```

```python
import functools
import math

import numpy as np
import jax
import jax.numpy as jnp
from jax import lax
from jax.experimental import pallas as pl
from jax.experimental.pallas import tpu as pltpu

F32 = jnp.float32
BF16 = jnp.bfloat16

HEAD_DIM = 64
SWA_HEADS = 12
SWA_KV_HEADS = 4
SWA_WIDTH = SWA_HEADS * HEAD_DIM
SWA_KV_WIDTH = SWA_KV_HEADS * HEAD_DIM
SWA_BLOCK = 128
WINDOW = 128
REL_BUCKETS = 32
REL_MAX_DIST = 128
GLA_HEADS = 4
GLA_DV = 192
GLA_DK = 96
GLA_DK_PAD = 128
GLA_DV_PAD = 256
GLA_LOWRANK = 16
GLA_LR_PAD = 128
GLA_CHUNK = 64
GATE_NORMALIZER = 16.0
GATE_LOG_MIN = -1.0
SB_HEADS = 8
SB_WIDTH = SB_HEADS * HEAD_DIM
RMS_EPS = 1e-6
LOG2E = 1.4426950408889634
NEG_INF = -1e30

LANES = 128
VMEM_LIMIT = 56 * 1024 * 1024

GLA_VR = 2 * GLA_DV
OFF_BVR = 0
OFF_BQ = 1536
OFF_BK = 2048
OFF_CQ = 2560
OFF_CK = 3072
OFF_CV = 3584
OFF_AK = 4096
OFF_AV = 4352
OFF_AQ = 4608
OFF_BLR = 5376
PROJ_PAD_WIDTH = 5632

SWA_Q_PERM = (0, 3, 1, 4, 2, 5, 6, 9, 7, 10, 8, 11)

_O_AQ, _O_AK, _O_AV = 0, 768, 1024
_O_BQ, _O_BK, _O_BV, _O_BR, _O_BLR = 1280, 1664, 2048, 2816, 3584
_O_CQ, _O_CK, _O_CV = 3600, 4112, 4624


def _cparams(sem):
    return pltpu.CompilerParams(dimension_semantics=sem, vmem_limit_bytes=VMEM_LIMIT)


def _rms(x, gain):
    ms = jnp.mean(x * x, axis=-1, keepdims=True)
    return x * lax.rsqrt(ms + RMS_EPS) * gain


def _split_bf16(x):
    hi = x.astype(BF16)
    lo = (x - hi.astype(F32)).astype(BF16)
    return hi, lo


_NT_DIMS = (((1,), (1,)), ((), ()))


def _matmul_kernel(h_ref, w_ref, o_ref):
    o_ref[...] = lax.dot_general(h_ref[...], w_ref[...], _NT_DIMS,
                                 preferred_element_type=F32).astype(o_ref.dtype)


def _norm_matmul_kernel(x_ref, g_ref, w_ref, o_ref):
    h = _rms(x_ref[...], g_ref[...]).astype(BF16)
    o_ref[...] = lax.dot_general(h, w_ref[...], _NT_DIMS,
                                 preferred_element_type=F32).astype(o_ref.dtype)


def in_projection(h, w_all, layer, gain=None, tm=1024, tn=PROJ_PAD_WIDTH // 2):
    s, d = h.shape
    n = w_all.shape[1]
    tm = min(tm, s)
    w_spec = pl.BlockSpec((None, tn, d), lambda i, j: (layer, j, 0))
    h_spec = pl.BlockSpec((tm, d), lambda i, j: (i, 0))
    common = dict(
        grid=(s // tm, n // tn),
        out_specs=pl.BlockSpec((tm, tn), lambda i, j: (i, j)),
        out_shape=jax.ShapeDtypeStruct((s, n), BF16),
        compiler_params=_cparams(("parallel", "arbitrary")),
        name="in_proj",
    )
    if gain is None:
        return pl.pallas_call(_matmul_kernel, in_specs=[h_spec, w_spec], **common)(h, w_all)
    return pl.pallas_call(
        _norm_matmul_kernel,
        in_specs=[h_spec, pl.BlockSpec((1, d), lambda i, j: (0, 0)), w_spec],
        **common,
    )(h, gain.reshape(1, d), w_all)


def _w_in_moves():
    scale = HEAD_DIM ** -0.5
    moves = []
    for h in range(GLA_HEADS):
        moves.append((_O_BV + GLA_DV * h, OFF_BVR + GLA_VR * h, GLA_DV, 1.0))
        moves.append((_O_BR + GLA_DV * h, OFF_BVR + GLA_VR * h + GLA_DV, GLA_DV, 1.0))
        moves.append((_O_BQ + GLA_DK * h, OFF_BQ + GLA_DK_PAD * h, GLA_DK, 1.0))
        moves.append((_O_BK + GLA_DK * h, OFF_BK + GLA_DK_PAD * h, GLA_DK, 1.0))
    moves += [(_O_CQ, OFF_CQ, SB_WIDTH, scale), (_O_CK, OFF_CK, SB_WIDTH, 1.0),
              (_O_CV, OFF_CV, SB_WIDTH, 1.0)]
    for j, h in enumerate(SWA_Q_PERM):
        moves.append((_O_AQ + HEAD_DIM * h, OFF_AQ + HEAD_DIM * j, HEAD_DIM, scale))
    moves += [(_O_AK, OFF_AK, SWA_KV_WIDTH, 1.0), (_O_AV, OFF_AV, SWA_KV_WIDTH, 1.0),
              (_O_BLR, OFF_BLR, GLA_LOWRANK, 1.0)]
    return moves


def _w_in_prep_kernel(w_ref, o_ref):
    o_ref[...] = jnp.zeros_like(o_ref)
    for src, dst, width, scale in _w_in_moves():
        piece = w_ref[src:src + width, :]
        if scale != 1.0:
            piece = piece * scale
        o_ref[dst:dst + width, :] = piece.astype(o_ref.dtype)


def prep_w_in(w_in_t, tc=256):
    layers, n, d = w_in_t.shape
    return pl.pallas_call(
        _w_in_prep_kernel,
        grid=(layers, d // tc),
        in_specs=[pl.BlockSpec((None, n, tc), lambda l, i: (l, 0, i))],
        out_specs=pl.BlockSpec((None, PROJ_PAD_WIDTH, tc), lambda l, i: (l, 0, i)),
        out_shape=jax.ShapeDtypeStruct((layers, PROJ_PAD_WIDTH, d), BF16),
        compiler_params=_cparams(("parallel", "parallel")),
        name="w_in_prep",
    )(w_in_t)


def _t5_causal_bucket(dist):
    max_exact = REL_BUCKETS // 2
    is_small = dist < max_exact
    ratio = (jnp.log(jnp.maximum(dist, 1).astype(F32) / max_exact)
             / math.log(REL_MAX_DIST / max_exact))
    large = max_exact + (ratio * (REL_BUCKETS - max_exact)).astype(jnp.int32)
    large = jnp.minimum(large, REL_BUCKETS - 1)
    return jnp.where(is_small, dist, large)


def swa_bias_table(rel_bias):
    blk = SWA_BLOCK
    qpos = jnp.arange(blk) + blk
    kpos = jnp.arange(2 * blk)
    dist = qpos[:, None] - kpos[None, :]
    in_window = (dist >= 0) & (dist < WINDOW)
    bucket = _t5_causal_bucket(jnp.maximum(dist, 0))
    table = rel_bias.astype(F32)
    table = jnp.stack([table[:, h] for h in SWA_Q_PERM])
    bias = jnp.full((SWA_HEADS, blk, 2 * blk), NEG_INF, F32)
    for bkt in range(REL_BUCKETS):
        hit = (in_window & (bucket == bkt))[None]
        bias = jnp.where(hit, table[:, bkt][:, None, None], bias)
    bias = bias.reshape(SWA_HEADS // 2, 2 * blk, 2 * blk)
    first = jnp.where((kpos < blk)[None, None, :], NEG_INF, bias)
    return jnp.stack([first, bias])


def _swa_gla_kernel(sink_ref, aq_ref, akp_ref, akc_ref, avp_ref, avc_ref, bias_ref,
                    q_ref, k_ref, vr_ref, lr_ref, gw_ref, gb_ref, gn_ref, lm_ref,
                    a_ref, b_ref, st_ref, *, layer, rows):
    i = pl.program_id(0)
    blk = SWA_BLOCK
    nb = SWA_HEADS // 2
    ch = GLA_CHUNK
    nch = rows // ch
    heads = range(GLA_HEADS)
    nt = (((1,), (1,)), ((), ()))
    tn = (((0,), (0,)), ((), ()))

    @pl.when(i == 0)
    def _():
        st_ref[...] = jnp.zeros_like(st_ref)

    lane = lax.broadcasted_iota(jnp.int32, (blk, LANES), 1)
    low = lane < HEAD_DIM
    top = lax.broadcasted_iota(jnp.int32, (2 * blk, 1), 0) < blk
    ones = jnp.ones((2 * blk, LANES), BF16)
    kc, vc = akc_ref[...], avc_ref[...]
    swa_k = (jnp.concatenate([akp_ref[...], kc[:blk]], axis=0), kc)
    swa_v = (jnp.concatenate([avp_ref[...], vc[:blk]], axis=0), vc)
    swa = [dict(), dict()]

    def swa_scores(j):
        variant = jnp.minimum(2 * i + j, 1)
        out = []
        for b in range(nb):
            qb = aq_ref[blk * j:blk * (j + 1), LANES * b:LANES * (b + 1)]
            zero = jnp.zeros_like(qb)
            lhs = jnp.concatenate([jnp.where(low, qb, zero), jnp.where(low, zero, qb)], axis=0)
            kb = swa_k[j][:, LANES * (b // 3):LANES * (b // 3 + 1)]
            out.append(lax.dot_general(lhs, kb, nt, preferred_element_type=F32)
                       + bias_ref[variant, b])
        swa[j]["scores"] = out

    def swa_softmax(j):
        probs, sink_term = [], []
        for b in range(nb):
            sink = jnp.where(top, sink_ref[layer, 2 * b], sink_ref[layer, 2 * b + 1])
            sc_b = swa[j]["scores"][b]
            m = jnp.maximum(jnp.max(sc_b, axis=1, keepdims=True), sink)
            probs.append(jnp.exp(sc_b - m).astype(BF16))
            sink_term.append(jnp.exp(sink - m))
        swa[j]["probs"], swa[j]["sink"] = probs, sink_term

    def swa_output(j):
        for b in range(nb):
            vb = swa_v[j][:, LANES * (b // 3):LANES * (b // 3 + 1)]
            pv = jnp.dot(swa[j]["probs"][b], jnp.concatenate([vb, ones], axis=1),
                         preferred_element_type=F32)
            o = pv[:, :LANES] / (pv[:, LANES:] + swa[j]["sink"][b])
            a_ref[blk * j:blk * (j + 1), LANES * b:LANES * (b + 1)] = (
                jnp.where(low, o[:blk], o[blk:]).astype(a_ref.dtype))

    rowi = lax.broadcasted_iota(jnp.int32, (rows, rows), 0)
    coli = lax.broadcasted_iota(jnp.int32, (rows, rows), 1)
    same_chunk_causal = jnp.logical_and(coli <= rowi, (rowi // ch) == (coli // ch))
    gla = dict()

    def gla_gate():
        lr = lr_ref[...]
        lg = []
        for h in heads:
            gp = jnp.dot(lr, gw_ref[h], preferred_element_type=F32) + gb_ref[h]
            log_sig = jnp.minimum(gp, 0.0) - jnp.log(1.0 + jnp.exp(-jnp.abs(gp)))
            lg.append(jnp.maximum(log_sig * (1.0 / GATE_NORMALIZER), GATE_LOG_MIN))
        gla["lg"] = lg

    def gla_decay():
        lm = lm_ref[...]
        q_dec, k_inv, k_end, dec = [], [], [], []
        for h in heads:
            hi, lo = _split_bf16(gla["lg"][h])
            bb = (jnp.dot(lm, hi, preferred_element_type=F32)
                  + jnp.dot(lm, lo, preferred_element_type=F32))
            b, b_last = bb[:rows], bb[rows:]
            q = q_ref[:, GLA_DK_PAD * h:GLA_DK_PAD * (h + 1)].astype(F32) * (GLA_DK ** -0.5)
            kk = k_ref[:, GLA_DK_PAD * h:GLA_DK_PAD * (h + 1)].astype(F32)
            q_dec.append((q * jnp.exp(b)).astype(BF16))
            k_inv.append((kk * jnp.exp(-b)).astype(BF16))
            k_end.append((kk * jnp.exp(b_last - b)).astype(BF16))
            dec.append(jnp.exp(b_last))
        gla.update(q_dec=q_dec, k_inv=k_inv, k_end=k_end, dec=dec)

    def gla_intra():
        vlane = lax.broadcasted_iota(jnp.int32, (rows, GLA_DV_PAD), 1) < GLA_DV
        v = []
        for h in heads:
            win = vr_ref[:, GLA_VR * h:GLA_VR * h + GLA_DV_PAD]
            v.append(jnp.where(vlane, win, jnp.zeros_like(win)))
        sc = []
        for h in heads:
            s_h = lax.dot_general(gla["q_dec"][h], gla["k_inv"][h], nt,
                                  preferred_element_type=F32)
            sc.append(jnp.where(same_chunk_causal, s_h, 0.0).astype(BF16))
        gla["o_intra"] = [jnp.dot(sc[h], v[h], preferred_element_type=F32) for h in heads]
        gla["d_st"] = [[lax.dot_general(v[h][c * ch:(c + 1) * ch],
                                        gla["k_end"][h][c * ch:(c + 1) * ch], tn,
                                        preferred_element_type=F32)
                        for c in range(nch)] for h in heads]

    def gla_state():
        st_in = []
        for h in heads:
            st = st_ref[h]
            states = []
            for c in range(nch):
                states.append(st.astype(BF16))
                st = st * gla["dec"][h][c * ch:c * ch + 1, :] + gla["d_st"][h][c]
            st_ref[h] = st
            st_in.append(states)
        gla["st_in"] = st_in

    def gla_output():
        gn = gn_ref[...]
        for h in heads:
            o_inter = [lax.dot_general(gla["q_dec"][h][c * ch:(c + 1) * ch], gla["st_in"][h][c],
                                       nt, preferred_element_type=F32) for c in range(nch)]
            o = gla["o_intra"][h] + jnp.concatenate(o_inter, axis=0)
            ms = jnp.sum(o * o, axis=1, keepdims=True) * (1.0 / GLA_DV)
            y = o * lax.rsqrt(ms + RMS_EPS) * gn
            win = vr_ref[:, GLA_VR * h + LANES:GLA_VR * (h + 1)].astype(F32)
            rr = win[:, GLA_DV_PAD - GLA_DV:]
            out = y[:, :GLA_DV] * (rr / (1.0 + jnp.exp(-rr)))
            b_ref[:, GLA_DV_PAD * h:GLA_DV_PAD * h + GLA_DV] = out.astype(b_ref.dtype)
            b_ref[:, GLA_DV_PAD * h + GLA_DV:GLA_DV_PAD * (h + 1)] = jnp.zeros(
                (rows, GLA_DV_PAD - GLA_DV), b_ref.dtype)

    gla_gate()
    swa_scores(0)
    swa_scores(1)
    gla_decay()
    swa_softmax(0)
    gla_intra()
    swa_softmax(1)
    swa_output(0)
    gla_state()
    swa_output(1)
    gla_output()


def swa_gla_attention(proj, sinks_perm, bias_tab, gw_pad, gb_pad, gn_pad, layer, rows=256):
    s = proj.shape[0]
    rows = min(rows, s)
    blk, ch = SWA_BLOCK, GLA_CHUNK
    assert rows == 2 * blk
    aq_i = OFF_AQ // SWA_WIDTH
    kv_w = SWA_KV_WIDTH
    dkw, dvw = GLA_HEADS * GLA_DK_PAD, GLA_HEADS * GLA_DV_PAD
    idx = np.arange(rows)
    same = (idx[:, None] // ch) == (idx[None, :] // ch)
    prefix = same & (idx[None, :] <= idx[:, None])
    lm = jnp.asarray(np.concatenate([prefix, same], axis=0).astype(np.float32), dtype=BF16)
    prev = lambda i: jnp.maximum(2 * i - 1, 0)
    col = lambda off, width: (lambda i: (i, off // width))
    return pl.pallas_call(
        functools.partial(_swa_gla_kernel, layer=layer, rows=rows),
        grid=(s // rows,),
        in_specs=[pl.BlockSpec(memory_space=pltpu.SMEM),
                  pl.BlockSpec((rows, SWA_WIDTH), lambda i: (i, aq_i)),
                  pl.BlockSpec((blk, kv_w), lambda i: (prev(i), OFF_AK // kv_w)),
                  pl.BlockSpec((rows, kv_w), col(OFF_AK, kv_w)),
                  pl.BlockSpec((blk, kv_w), lambda i: (prev(i), OFF_AV // kv_w)),
                  pl.BlockSpec((rows, kv_w), col(OFF_AV, kv_w)),
                  pl.BlockSpec(bias_tab.shape, lambda i: (0, 0, 0, 0)),
                  pl.BlockSpec((rows, dkw), col(OFF_BQ, dkw)),
                  pl.BlockSpec((rows, dkw), col(OFF_BK, dkw)),
                  pl.BlockSpec((rows, GLA_HEADS * GLA_VR), col(OFF_BVR, GLA_HEADS * GLA_VR)),
                  pl.BlockSpec((rows, GLA_LR_PAD), col(OFF_BLR, GLA_LR_PAD)),
                  pl.BlockSpec((None, GLA_HEADS, GLA_LR_PAD, GLA_DK_PAD),
                               lambda i: (layer, 0, 0, 0)),
                  pl.BlockSpec((None, GLA_HEADS, 1, GLA_DK_PAD), lambda i: (layer, 0, 0, 0)),
                  pl.BlockSpec((None, 1, GLA_DV_PAD), lambda i: (layer, 0, 0)),
                  pl.BlockSpec((2 * rows, rows), lambda i: (0, 0))],
        out_specs=[pl.BlockSpec((rows, SWA_WIDTH), lambda i: (i, 0)),
                   pl.BlockSpec((rows, dvw), lambda i: (i, 0))],
        out_shape=[jax.ShapeDtypeStruct((s, SWA_WIDTH), BF16),
                   jax.ShapeDtypeStruct((s, dvw), BF16)],
        scratch_shapes=[pltpu.VMEM((GLA_HEADS, GLA_DV_PAD, GLA_DK_PAD), F32)],
        compiler_params=_cparams(("arbitrary",)),
        name="swa_gla",
    )(sinks_perm, proj, proj, proj, proj, proj, bias_tab,
      proj, proj, proj, proj, gw_pad, gb_pad, gn_pad, lm)


def _sb_kernel(q_ref, k_ref, v_ref, u_ref, o_ref, acc_ref, car_ref, qm_ref, vm_ref,
               z0_ref, z1_ref, z2_ref, z3_ref, nl0_ref, nl1_ref, w0_ref, w1_ref,
               *, bq, nblk, unroll):
    lane = lax.broadcasted_iota(jnp.int32, (bq, LANES), 1)
    low = lane < HEAD_DIM
    nt = (((1,), (1,)), ((), ()))
    z_ref = (z0_ref, z1_ref, z2_ref, z3_ref)
    nl_ref, w_ref = (nl0_ref, nl1_ref), (w0_ref, w1_ref)

    def rows(blk):
        if isinstance(blk, int):
            return pl.ds(blk * bq, bq)
        return pl.ds(pl.multiple_of(blk * bq, bq), bq)

    def strict_mask():
        rowi = lax.broadcasted_iota(jnp.int32, (2 * bq, bq), 0)
        coli = lax.broadcasted_iota(jnp.int32, (2 * bq, bq), 1)
        return coli < (rowi & (bq - 1))

    def s1a(item, tick4):
        i, c = item
        z_ref[tick4][...] = lax.dot_general(qm_ref[i], k_ref[rows(c), :], nt,
                                            preferred_element_type=F32)

    def s1b(tick4, masked):
        z = z_ref[tick4][...].astype(BF16)
        nl = jnp.maximum(z, 0.0) + jnp.log(1.0 + jnp.exp2(jnp.abs(z) * (-LOG2E)))
        if masked:
            nl = jnp.where(strict_mask(), nl, jnp.zeros_like(nl))
        nl_ref[tick4 % 2][...] = nl

    def s2(item, tick4, masked, first):
        i, _ = item
        cs = jnp.dot(nl_ref[tick4 % 2][...], u_ref[...], preferred_element_type=F32)
        tot = jnp.broadcast_to(cs[:, 0:1], (2 * bq, LANES))
        if first:
            arg = z_ref[tick4][...] - cs
            car_ref[i] = tot
        else:
            car = car_ref[i]
            arg = z_ref[tick4][...] - cs - jnp.concatenate([car, car], axis=1)
            car_ref[i] = car + tot
        w = jnp.exp(arg)
        if masked:
            w = jnp.where(strict_mask(), w, 0.0)
        w = w.astype(BF16)
        w_ref[tick4 % 2][:, :bq] = w[:bq]
        w_ref[tick4 % 2][:, bq:] = w[bq:]

    def s3(item, tick4, first):
        i, c = item
        pv = jnp.dot(w_ref[tick4 % 2][...], vm_ref[c], preferred_element_type=F32)
        if first:
            acc_ref[rows(i), :] = pv
        else:
            acc_ref[rows(i), :] += pv

    def run_pipeline(items, ndiag, step):
        n = len(items)

        def tick(t4, its, do, diag):
            if do[1]:
                s1b((t4 - 1) % 4, diag[1])
            if do[0]:
                s1a(its[0], t4)
            if do[2]:
                s2(its[2], (t4 - 2) % 4, diag[2], diag[2])
            if do[3]:
                s3(its[3], (t4 - 3) % 4, diag[3])

        def static_tick(t):
            do = [0 <= t - j < n for j in range(4)]
            its = [items[t - j] if do[j] else None for j in range(4)]
            tick(t % 4, its, do, [t - j < ndiag for j in range(4)])

        lo_t, hi_t = ndiag + 3, n - 1
        groups = max(hi_t - lo_t + 1, 0) // unroll
        if groups < 2:
            groups = 0
            lo_t = n + 3
        for t in range(min(lo_t, n + 3)):
            static_tick(t)
        if groups:
            def body(_, carry):
                its = [(carry[2 * j], carry[2 * j + 1]) for j in range(4)]
                for k in range(unroll):
                    tick((lo_t + k) % 4, its, [True] * 4, [False] * 4)
                    its = [step(*its[0])] + its[:3]
                return tuple(x for it in its for x in it)

            init = tuple(jnp.int32(x) for j in range(4) for x in items[lo_t - j])
            lax.fori_loop(0, groups, body, init)
        for t in range(lo_t + unroll * groups, n + 3):
            static_tick(t)

    for blk in range(nblk):
        r = rows(blk)
        qb, vb = q_ref[r, :], v_ref[r, :]
        zero = jnp.zeros_like(qb)
        qm_ref[blk, :bq] = jnp.where(low, qb, zero)
        qm_ref[blk, bq:] = jnp.where(low, zero, qb)
        vm_ref[blk, :bq] = jnp.where(low, vb, zero)
        vm_ref[blk, bq:] = jnp.where(low, zero, vb)

    diag = [(i, i) for i in range(nblk)]
    off = [(i, c) for i in range(1, nblk) for c in range(i - 1, -1, -1)]

    def off_step(i, c):
        wrap = c == 0
        i2 = jnp.where(wrap, i + 1, i)
        return i2, jnp.where(wrap, i2 - 1, c - 1)

    run_pipeline(diag + off, nblk, off_step)

    o_ref[...] = acc_ref[...].astype(o_ref.dtype)


def sb_attention(proj, bq=256, unroll=32):
    s = proj.shape[0]
    bq = min(bq, s)
    nblk = s // bq
    assert bq & (bq - 1) == 0 and unroll % 4 == 0
    qi = OFF_CQ // LANES
    ki = OFF_CK // LANES
    vi = OFF_CV // LANES
    u = (np.arange(bq)[:, None] >= np.arange(bq)[None, :]).astype(np.float32)
    u = jnp.asarray(u, dtype=BF16)
    slot = lambda shape, dtype: pltpu.VMEM(shape, dtype)
    return pl.pallas_call(
        functools.partial(_sb_kernel, bq=bq, nblk=nblk, unroll=unroll),
        grid=(SB_HEADS // 2,),
        in_specs=[pl.BlockSpec((s, LANES), lambda p: (0, qi + p)),
                  pl.BlockSpec((s, LANES), lambda p: (0, ki + p)),
                  pl.BlockSpec((s, LANES), lambda p: (0, vi + p)),
                  pl.BlockSpec((bq, bq), lambda p: (0, 0))],
        out_specs=pl.BlockSpec((s, LANES), lambda p: (0, p)),
        out_shape=jax.ShapeDtypeStruct((s, SB_WIDTH), BF16),
        scratch_shapes=[pltpu.VMEM((s, LANES), F32),
                        pltpu.VMEM((nblk, 2 * bq, LANES), F32),
                        pltpu.VMEM((nblk, 2 * bq, LANES), BF16),
                        pltpu.VMEM((nblk, 2 * bq, LANES), BF16)]
                       + [slot((2 * bq, bq), F32)] * 4
                       + [slot((2 * bq, bq), BF16)] * 2
                       + [slot((bq, 2 * bq), BF16)] * 2,
        compiler_params=_cparams(("parallel",)),
        name="stickbreak",
    )(proj, proj, proj, u)


def _oproj_kernel(a_ref, b_ref, c_ref, wa_ref, wb_ref, wc_ref, ga_ref, gc_ref,
                  x_ref, gn_ref, xo_ref, ho_ref):
    an = _rms(a_ref[...].astype(F32), ga_ref[...]).astype(BF16)
    cn = _rms(c_ref[...].astype(F32), gc_ref[...]).astype(BF16)
    y = (jnp.dot(an, wa_ref[...], preferred_element_type=F32)
         + jnp.dot(b_ref[...], wb_ref[...], preferred_element_type=F32)
         + jnp.dot(cn, wc_ref[...], preferred_element_type=F32))
    xn = x_ref[...] + y
    xo_ref[...] = xn
    ho_ref[...] = _rms(xn, gn_ref[...]).astype(ho_ref.dtype)


def out_projection(a, b, c, wa, wb, wc, ga, gc, x, gn, layer, tm=512):
    s, d = x.shape
    tm = min(tm, s)
    row = lambda i: (i, 0)
    fixed = lambda i: (layer, 0, 0)
    per_layer = lambda arr: pl.BlockSpec((None,) + arr.shape[1:], fixed)
    return pl.pallas_call(
        _oproj_kernel,
        grid=(s // tm,),
        in_specs=[pl.BlockSpec((tm, a.shape[1]), row),
                  pl.BlockSpec((tm, b.shape[1]), row),
                  pl.BlockSpec((tm, c.shape[1]), row),
                  per_layer(wa), per_layer(wb), per_layer(wc), per_layer(ga), per_layer(gc),
                  pl.BlockSpec((tm, d), row),
                  per_layer(gn)],
        out_specs=[pl.BlockSpec((tm, d), row), pl.BlockSpec((tm, d), row)],
        out_shape=[jax.ShapeDtypeStruct((s, d), F32), jax.ShapeDtypeStruct((s, d), BF16)],
        compiler_params=_cparams(("parallel",)),
        name="out_proj",
    )(a, b, c, wa, wb, wc, ga, gc, x, gn)


def _mlp_kernel(h_ref, x_ref, w1_ref, w2_ref, gn_ref, *out_refs, final):
    acc_ref = out_refs[0]
    j = pl.program_id(1)

    @pl.when(j == 0)
    def _():
        acc_ref[...] = x_ref[...]

    u = jnp.dot(h_ref[...], w1_ref[...].astype(BF16), preferred_element_type=F32)
    act = jnp.square(jnp.maximum(u, 0.0)).astype(BF16)
    acc_ref[...] += jnp.dot(act, w2_ref[...].astype(BF16), preferred_element_type=F32)

    @pl.when(j == pl.num_programs(1) - 1)
    def _():
        y = _rms(acc_ref[...], gn_ref[...])
        if final:
            acc_ref[...] = y
        else:
            out_refs[1][...] = y.astype(out_refs[1].dtype)


def mlp_block(h, x, w1_all, w2_all, layer, gn, final, tm=1024, tf=512):
    s, d = x.shape
    ff = w1_all.shape[2]
    tm = min(tm, s)
    row = lambda i, j: (i, 0)
    once = pl.Buffered(1)
    out_specs = [pl.BlockSpec((tm, d), row)]
    out_shape = [jax.ShapeDtypeStruct((s, d), F32)]
    if not final:
        out_specs.append(pl.BlockSpec((tm, d), row))
        out_shape.append(jax.ShapeDtypeStruct((s, d), BF16))
    return pl.pallas_call(
        functools.partial(_mlp_kernel, final=final),
        grid=(s // tm, ff // tf),
        in_specs=[pl.BlockSpec((tm, d), row, pipeline_mode=once),
                  pl.BlockSpec((tm, d), row, pipeline_mode=once),
                  pl.BlockSpec((None, d, tf), lambda i, j: (layer, 0, j)),
                  pl.BlockSpec((None, tf, d), lambda i, j: (layer, j, 0)),
                  pl.BlockSpec((None, 1, d), lambda i, j: (layer, 0, 0))],
        out_specs=out_specs,
        out_shape=out_shape,
        compiler_params=_cparams(("parallel", "arbitrary")),
        name="mlp",
    )(h, x, w1_all, w2_all, gn)


def _head_rows(w, off, heads, width, padded):
    pieces = []
    for h in range(heads):
        pieces.append(w[..., off + h * width:off + (h + 1) * width, :])
        if padded > width:
            pieces.append(jnp.zeros(w.shape[:-2] + (padded - width, w.shape[-1]), w.dtype))
    return pieces


def _swa_perm_cols(w):
    return jnp.concatenate([w[..., HEAD_DIM * h:HEAD_DIM * (h + 1)] for h in SWA_Q_PERM], axis=-1)


def kernel(x, norm_mix, w_in, swa_sinks, rel_bias, gla_gate_w, gla_gate_b, gla_norm,
           swa_out_norm, sb_out_norm, w_out, norm_mlp, w_mlp_in, w_mlp_out, norm_final):
    depth = w_in.shape[0]
    xs = x[0]
    gla_w = GLA_HEADS * GLA_DV
    w_in_p = prep_w_in(jnp.swapaxes(w_in, 1, 2))
    wa = jnp.concatenate([w_out[:, HEAD_DIM * h:HEAD_DIM * (h + 1)] for h in SWA_Q_PERM],
                         axis=1).astype(BF16)
    wb = jnp.concatenate(_head_rows(w_out, SWA_WIDTH, GLA_HEADS, GLA_DV, GLA_DV_PAD),
                         axis=1).astype(BF16)
    wc = w_out[:, SWA_WIDTH + gla_w:].astype(BF16)
    ga = _swa_perm_cols(swa_out_norm)[:, None, :]
    gc = sb_out_norm[:, None, :]
    g_mlp = norm_mlp[:, None, :]
    g_next = jnp.concatenate([norm_mix[1:], norm_final[None]], axis=0)[:, None, :]
    sinks_p = jnp.concatenate([swa_sinks[:, h:h + 1] for h in SWA_Q_PERM], axis=1)
    bias_tab = swa_bias_table(rel_bias)
    gw = jnp.stack([jnp.pad(gla_gate_w[:, :, GLA_DK * h:GLA_DK * (h + 1)],
                            ((0, 0), (0, GLA_LR_PAD - GLA_LOWRANK), (0, GLA_DK_PAD - GLA_DK)))
                    for h in range(GLA_HEADS)], axis=1).astype(BF16)
    gb = jnp.stack([jnp.pad(gla_gate_b[:, GLA_DK * h:GLA_DK * (h + 1)],
                            ((0, 0), (0, GLA_DK_PAD - GLA_DK)))
                    for h in range(GLA_HEADS)], axis=1)[:, :, None, :]
    gn = jnp.pad(gla_norm, ((0, 0), (0, GLA_DV_PAD - GLA_DV)))[:, None, :]

    h = None
    for l in range(depth):
        if l == 0:
            proj = in_projection(xs, w_in_p, l, gain=norm_mix[0])
        else:
            proj = in_projection(h, w_in_p, l)
        a, b = swa_gla_attention(proj, sinks_p, bias_tab, gw, gb, gn, l)
        c = sb_attention(proj)
        xs, hm = out_projection(a, b, c, wa, wb, wc, ga, gc, xs, g_mlp, l)
        if l + 1 < depth:
            xs, h = mlp_block(hm, xs, w_mlp_in, w_mlp_out, l, g_next, final=False)
        else:
            (out,) = mlp_block(hm, xs, w_mlp_in, w_mlp_out, l, g_next, final=True)
    return out[None]
```
